```python
import math
import jax
import jax.numpy as jnp
from jax import lax
import numpy as np

D_MODEL = 2048
BATCH = 4
SEQ = 2048
DEPTH = 1
DEC_BATCH = 128
DEC_SEQ = 4
PAST_LEN = 16384
PAGE_SIZE = 128

D_SSD = D_MODEL
SSD_HEADDIM = 64
SSD_HEADS = D_SSD // SSD_HEADDIM
SSD_GROUPS = 4
SSD_STATE = 128
SSD_CONV = 4
SSD_CHUNK = 128
SSD_CONV_DIM = D_SSD + 2 * SSD_GROUPS * SSD_STATE
D_MLP = D_MODEL
MLP_GROUPS = 8
MLP_GROUP_DIM = D_MLP // MLP_GROUPS
MLP_CHUNK = 128
D_FF = ((8 * D_MODEL // 3) + 255) // 256 * 256
FFN_CONV = 3
N_IN = D_SSD + SSD_CONV_DIM + SSD_HEADS + 2 * D_MLP + 2 * D_MODEL
IN_SPLITS = [D_SSD,
             D_SSD + SSD_CONV_DIM,
             D_SSD + SSD_CONV_DIM + SSD_HEADS,
             D_SSD + SSD_CONV_DIM + SSD_HEADS + D_MLP,
             D_SSD + SSD_CONV_DIM + SSD_HEADS + 2 * D_MLP]
EPS = 1e-6
ADA_SCALE = 0.5

kernel_name = 'hybrid_ssd_chunkmlp_convffn_step'


def rms_norm(x, w):
    xf = x.astype(jnp.float32)
    y = xf * lax.rsqrt(jnp.mean(xf * xf, axis=-1, keepdims=True) + EPS)
    return (y * w.astype(jnp.float32)).astype(x.dtype)


def layer_norm(x, w, b):
    xf = x.astype(jnp.float32)
    xc = xf - jnp.mean(xf, axis=-1, keepdims=True)
    y = xc * lax.rsqrt(jnp.mean(xc * xc, axis=-1, keepdims=True) + EPS)
    return (y * w.astype(jnp.float32) + b.astype(jnp.float32)).astype(x.dtype)


def gated_group_rms_norm(y, z, w):
    g = y.astype(jnp.float32) * jax.nn.silu(z.astype(jnp.float32))
    shp = g.shape
    g = g.reshape(shp[:-1] + (SSD_GROUPS, shp[-1] // SSD_GROUPS))
    g = g * lax.rsqrt(jnp.mean(g * g, axis=-1, keepdims=True) + EPS)
    return (g.reshape(shp) * w.astype(jnp.float32)).astype(y.dtype)


def causal_dwconv(x, buf, w, b):
    K = w.shape[0]
    L = x.shape[1]
    xp = jnp.concatenate([buf.astype(x.dtype), x], axis=1)
    y = b + w[K - 1] * xp[:, K - 1:K - 1 + L]
    for k in range(K - 1):
        y = y + w[k] * xp[:, k:k + L]
    return y, xp[:, xp.shape[1] - (K - 1):]


def segsum(a):
    T = a.shape[-1]
    cs = jnp.cumsum(a, axis=-1)
    diff = cs[..., :, None] - cs[..., None, :]
    mask = jnp.tril(jnp.ones((T, T), dtype=bool))
    return jnp.where(mask, diff, -jnp.inf)


def ssd_scan(x, dt, a, bm, cm, h0):
    bsz, L, H, P = x.shape
    G, N = bm.shape[2], bm.shape[3]
    Hg = H // G
    T = math.gcd(L, SSD_CHUNK)
    nc = L // T
    f32 = jnp.float32
    xdt = (x.astype(f32) * dt[..., None]).reshape(bsz, nc, T, G, Hg, P)
    adt = (dt * a).reshape(bsz, nc, T, G, Hg).transpose(0, 1, 3, 4, 2)
    bc = bm.astype(f32).reshape(bsz, nc, T, G, N)
    cc = cm.astype(f32).reshape(bsz, nc, T, G, N)
    a_cs = jnp.cumsum(adt, axis=-1)
    decay = jnp.exp(segsum(adt))
    cb = jnp.einsum('bclgn,bcsgn->bcgls', cc, bc)
    scores = cb[:, :, :, None] * decay
    y_diag = jnp.einsum('bcghls,bcsghp->bclghp', scores, xdt)
    decay_states = jnp.exp(a_cs[..., -1:] - a_cs)
    xw = xdt * jnp.moveaxis(decay_states, -1, 2)[..., None]
    states = jnp.einsum('bcsgn,bcsghp->bcghpn', bc, xw)
    states = jnp.concatenate([h0.astype(f32).reshape(bsz, 1, G, Hg, P, N), states], axis=1)
    a_last = jnp.pad(a_cs[..., -1], ((0, 0), (1, 0), (0, 0), (0, 0)))
    chunk_decay = jnp.exp(segsum(jnp.moveaxis(a_last, 1, -1)))
    new_states = jnp.einsum('bghzc,bcghpn->bzghpn', chunk_decay, states)
    states_in, h_final = new_states[:, :-1], new_states[:, -1]
    y_off = jnp.einsum('bclgn,bcghpn->bclghp', cc, states_in) * jnp.moveaxis(jnp.exp(a_cs), -1, 2)[..., None]
    y = (y_diag + y_off).reshape(bsz, L, H, P).astype(x.dtype)
    return y, h_final.reshape(bsz, H, P, N)


def chunk_spatial_mix(v, w_s, b_s):
    bsz, L, G, dg = v.shape
    T = MLP_CHUNK if L >= MLP_CHUNK else L
    nc = -(-L // T)
    vp = jnp.pad(v, ((0, 0), (0, nc * T - L), (0, 0), (0, 0))).reshape(bsz, nc, T, G, dg)
    mask = jnp.tril(jnp.ones((T, T), dtype=bool))
    w = jnp.where(mask, w_s[:, :T, :T], 0.0)
    out = jnp.einsum('gts,bcsgd->bctgd', w, vp) + jnp.swapaxes(b_s[:, :T], 0, 1)[None, None, :, :, None]
    return out.reshape(bsz, nc * T, G, dg)[:, :L]


def layer(x, c, ssm0, ssd_buf0, ffn_buf0, norm1_w, w_ada, b_ada, w_in, ssd_conv_w, ssd_conv_b,
          dt_bias, a_log, d_skip, ssd_norm_w, mlp_ln_w, mlp_ln_b, w_spatial, b_spatial,
          w_ssd_o, w_mlp_o, w_out, norm2_w, w_up, ffn_conv_w, ffn_conv_b, w_down):
    bsz, L, _ = x.shape
    mod = jax.nn.silu(c) @ w_ada + b_ada
    shift1, scale1, gate1, shift2, scale2, gate2 = jnp.split(mod[:, None, :], 6, axis=-1)

    n1 = rms_norm(x, norm1_w) * (1 + scale1) + shift1
    proj = n1 @ w_in
    z, xbc, dt_raw, u, v, gates = jnp.split(proj, IN_SPLITS, axis=-1)

    xbc, ssd_buf = causal_dwconv(xbc, ssd_buf0, ssd_conv_w, ssd_conv_b)
    xbc = jax.nn.silu(xbc)
    xs, bm, cm = jnp.split(xbc, [D_SSD, D_SSD + SSD_GROUPS * SSD_STATE], axis=-1)
    xs = xs.reshape(bsz, L, SSD_HEADS, SSD_HEADDIM)
    bm = bm.reshape(bsz, L, SSD_GROUPS, SSD_STATE)
    cm = cm.reshape(bsz, L, SSD_GROUPS, SSD_STATE)
    dt = jax.nn.softplus(dt_raw.astype(jnp.float32) + dt_bias.astype(jnp.float32))
    a = -jnp.exp(a_log.astype(jnp.float32))
    y_ssd, ssm = ssd_scan(xs, dt, a, bm, cm, ssm0)
    y_ssd = y_ssd + d_skip[:, None] * xs
    y_ssd = gated_group_rms_norm(y_ssd.reshape(bsz, L, D_SSD), z, ssd_norm_w)

    u = jax.nn.gelu(u)
    v = layer_norm(jax.nn.gelu(v), mlp_ln_w, mlp_ln_b)
    sv = chunk_spatial_mix(v.reshape(bsz, L, MLP_GROUPS, MLP_GROUP_DIM), w_spatial, b_spatial)
    y_mlp = u * sv.reshape(bsz, L, D_MLP)

    g_a, g_b = jnp.split(jax.nn.sigmoid(gates), 2, axis=-1)
    mixed = (g_a * (y_ssd @ w_ssd_o) + g_b * (y_mlp @ w_mlp_o)) @ w_out
    x = x + gate1 * mixed

    n2 = rms_norm(x, norm2_w) * (1 + scale2) + shift2
    a_ff, v_ff = jnp.split(n2 @ w_up, 2, axis=-1)
    a_ff, ffn_buf = causal_dwconv(a_ff, ffn_buf0, ffn_conv_w, ffn_conv_b)
    x = x + gate2 * ((jax.nn.gelu(a_ff) * v_ff) @ w_down)
    return x, ssm.astype(ssm0.dtype), ssd_buf, ffn_buf, v


def setup_inputs(seed: int = 0) -> dict:
    key = jax.random.key(seed)
    ks = jax.random.split(key, 32)
    f32 = jnp.float32

    def nrm(k, shape, scale=1.0):
        return scale * jax.random.normal(k, shape, f32)

    dt0 = jnp.exp(jax.random.uniform(ks[13], (DEPTH, SSD_HEADS), f32, math.log(1e-3), math.log(1e-1)))
    return {
        'x_prompt': nrm(ks[0], (BATCH, SEQ, D_MODEL)),
        'x_sample': nrm(ks[1], (DEC_BATCH, DEC_SEQ, D_MODEL)),
        'state_ssm': nrm(ks[2], (DEPTH, DEC_BATCH, SSD_HEADS, SSD_HEADDIM, SSD_STATE), 0.5),
        'state_ssd_conv': nrm(ks[3], (DEPTH, DEC_BATCH, SSD_CONV - 1, SSD_CONV_DIM)),
        'state_ffn_conv': nrm(ks[4], (DEPTH, DEC_BATCH, FFN_CONV - 1, D_FF)),
        'c_prompt': nrm(ks[5], (BATCH, D_MODEL)),
        'c_sample': nrm(ks[6], (DEC_BATCH, D_MODEL)),
        'norm1_w': 1.0 + nrm(ks[7], (DEPTH, D_MODEL), 0.02),
        'w_ada': nrm(ks[8], (DEPTH, D_MODEL, 6 * D_MODEL), ADA_SCALE * D_MODEL ** -0.5),
        'b_ada': nrm(ks[9], (DEPTH, 6 * D_MODEL), 0.02),
        'w_in': nrm(ks[10], (DEPTH, D_MODEL, N_IN), D_MODEL ** -0.5),
        'ssd_conv_w': nrm(ks[11], (DEPTH, SSD_CONV, SSD_CONV_DIM), SSD_CONV ** -0.5),
        'ssd_conv_b': nrm(ks[12], (DEPTH, SSD_CONV_DIM), 0.02),
        'dt_bias': dt0 + jnp.log(-jnp.expm1(-dt0)),
        'a_log': jnp.log(jax.random.uniform(ks[14], (DEPTH, SSD_HEADS), f32, 1.0, 16.0)),
        'd_skip': 1.0 + nrm(ks[15], (DEPTH, SSD_HEADS), 0.02),
        'ssd_norm_w': 1.0 + nrm(ks[16], (DEPTH, D_SSD), 0.02),
        'mlp_ln_w': 1.0 + nrm(ks[17], (DEPTH, D_MLP), 0.02),
        'mlp_ln_b': nrm(ks[18], (DEPTH, D_MLP), 0.02),
        'w_spatial': nrm(ks[19], (DEPTH, MLP_GROUPS, MLP_CHUNK, MLP_CHUNK), MLP_CHUNK ** -0.5),
        'b_spatial': 1.0 + nrm(ks[20], (DEPTH, MLP_GROUPS, MLP_CHUNK), 0.02),
        'w_ssd_o': nrm(ks[21], (DEPTH, D_SSD, D_MODEL), D_SSD ** -0.5),
        'w_mlp_o': nrm(ks[22], (DEPTH, D_MLP, D_MODEL), D_MLP ** -0.5),
        'w_out': nrm(ks[23], (DEPTH, D_MODEL, D_MODEL), D_MODEL ** -0.5),
        'norm2_w': 1.0 + nrm(ks[24], (DEPTH, D_MODEL), 0.02),
        'w_up': nrm(ks[25], (DEPTH, D_MODEL, 2 * D_FF), D_MODEL ** -0.5),
        'ffn_conv_w': nrm(ks[26], (DEPTH, FFN_CONV, D_FF), FFN_CONV ** -0.5),
        'ffn_conv_b': nrm(ks[27], (DEPTH, D_FF), 0.02),
        'w_down': nrm(ks[28], (DEPTH, D_FF, D_MODEL), D_FF ** -0.5),
        'final_norm_w': 1.0 + nrm(ks[29], (D_MODEL,), 0.02),
    }


def reference(x_prompt, x_sample, state_ssm, state_ssd_conv, state_ffn_conv, c_prompt, c_sample,
              norm1_w, w_ada, b_ada, w_in, ssd_conv_w, ssd_conv_b, dt_bias, a_log, d_skip,
              ssd_norm_w, mlp_ln_w, mlp_ln_b, w_spatial, b_spatial, w_ssd_o, w_mlp_o, w_out,
              norm2_w, w_up, ffn_conv_w, ffn_conv_b, w_down, final_norm_w):
    hp, hs = x_prompt, x_sample
    bp = x_prompt.shape[0]
    ssm_p, ssm_s, cv_p, cv_s, ff_p, ff_s, v_s = [], [], [], [], [], [], []
    for l in range(DEPTH):
        lw = (norm1_w[l], w_ada[l], b_ada[l], w_in[l], ssd_conv_w[l], ssd_conv_b[l], dt_bias[l],
              a_log[l], d_skip[l], ssd_norm_w[l], mlp_ln_w[l], mlp_ln_b[l], w_spatial[l],
              b_spatial[l], w_ssd_o[l], w_mlp_o[l], w_out[l], norm2_w[l], w_up[l],
              ffn_conv_w[l], ffn_conv_b[l], w_down[l])
        zs = jnp.zeros((bp, SSD_HEADS, SSD_HEADDIM, SSD_STATE), x_prompt.dtype)
        zc = jnp.zeros((bp, SSD_CONV - 1, SSD_CONV_DIM), x_prompt.dtype)
        zf = jnp.zeros((bp, FFN_CONV - 1, D_FF), x_prompt.dtype)
        hp, s1, c1, f1, _ = layer(hp, c_prompt, zs, zc, zf, *lw)
        hs, s2, c2, f2, v2 = layer(hs, c_sample, state_ssm[l], state_ssd_conv[l], state_ffn_conv[l], *lw)
        ssm_p.append(s1)
        ssm_s.append(s2)
        cv_p.append(c1)
        cv_s.append(c2)
        ff_p.append(f1)
        ff_s.append(f2)
        v_s.append(v2)
    y_prompt = rms_norm(hp, final_norm_w)
    y_sample = rms_norm(hs, final_norm_w)
    return (y_prompt, y_sample, jnp.stack(ssm_p), jnp.stack(ssm_s), jnp.stack(cv_p), jnp.stack(cv_s),
            jnp.stack(ff_p), jnp.stack(ff_s), jnp.stack(v_s))
```

```python
import functools

import jax
import jax.numpy as jnp
from jax import lax
from jax.experimental import pallas as pl
from jax.experimental.pallas import tpu as pltpu

F32 = jnp.float32
BF16 = jnp.bfloat16

D = 2048
BATCH, SEQ = 4, 2048
DEC_BATCH, DEC_SEQ = 128, 4
NP_ROWS = BATCH * SEQ
NS_ROWS = DEC_BATCH * DEC_SEQ
M_ROWS = NP_ROWS + NS_ROWS
TM = 512
NP_TILES = NP_ROWS // TM
N_TILES = M_ROWS // TM
TILES_PER_SEQ = SEQ // TM
HEADS, HEADDIM, GROUPS, STATE = 32, 64, 4, 128
CHUNK = 128
N_CHUNKS = SEQ // CHUNK
CONV_DIM = D + 2 * GROUPS * STATE
MLP_GROUPS = 8
MLP_GROUP_DIM = D // MLP_GROUPS
D_FF = 5632
EPS = 1e-6
DT_COL = D + CONV_DIM
TN_IN = 1024
LANE = 128
VMEM_LIMIT = 56 * 1024 * 1024


def _params(*sem):
    return pltpu.CompilerParams(dimension_semantics=sem, vmem_limit_bytes=VMEM_LIMIT)


def _dot(a, b):
    return jnp.dot(a, b, preferred_element_type=F32)


def _split_bf16(v, terms):
    out = []
    r = v
    for _ in range(terms):
        p = r.astype(BF16)
        out.append(p)
        r = r - p.astype(F32)
    return out


def _dot_exact_rhs(v, e, terms=3):
    acc = None
    for p in _split_bf16(v, terms):
        d = _dot(p, e)
        acc = d if acc is None else acc + d
    return acc


def _dot_exact_lhs(t, v, terms=3):
    acc = None
    for p in _split_bf16(v, terms):
        d = _dot(t, p)
        acc = d if acc is None else acc + d
    return acc


def _silu(x):
    return x * jax.nn.sigmoid(x)


def _softplus(x):
    return jnp.maximum(x, 0.0) + jnp.log1p(jnp.exp(-jnp.abs(x)))


def _rms(x):
    return x * lax.rsqrt(jnp.mean(x * x, axis=-1, keepdims=True) + EPS)


def _ada_kernel(c_ref, w_ref, b_ref, o_ref):
    a = _silu(c_ref[...]).astype(BF16)
    o_ref[...] = _dot(a, w_ref[...].astype(BF16)) + b_ref[...]


def _ada_call(c_all, w, b):
    rows = c_all.shape[0]
    tn = 1024
    return pl.pallas_call(
        _ada_kernel,
        grid=(6 * D // tn,),
        in_specs=[pl.BlockSpec((rows, D), lambda j: (0, 0)),
                  pl.BlockSpec((D, tn), lambda j: (0, j)),
                  pl.BlockSpec((1, tn), lambda j: (0, j))],
        out_specs=pl.BlockSpec((rows, tn), lambda j: (0, j)),
        out_shape=jax.ShapeDtypeStruct((rows, 6 * D), F32),
        compiler_params=_params("arbitrary"),
        name="ada_mod",
    )(c_all, w, b)


def _norm_kernel(with_dt, xp_ref, xs_ref, nw_ref, scp_ref, shp_ref, scs_ref, shs_ref, *rest):
    if with_dt:
        wdt_ref, n_ref, dt_ref = rest
    else:
        (n_ref,) = rest
    i = pl.program_id(0)

    def emit(n):
        nb = n.astype(BF16)
        n_ref[...] = nb
        if with_dt:
            dt_ref[...] = _dot(nb, wdt_ref[...].astype(BF16))

    @pl.when(i < NP_TILES)
    def _():
        y = _rms(xp_ref[...]) * nw_ref[...]
        emit(y * (1.0 + scp_ref[0]) + shp_ref[0])

    @pl.when(i >= NP_TILES)
    def _():
        y = _rms(xs_ref[...]) * nw_ref[...]
        y3 = y.reshape(DEC_SEQ, DEC_BATCH, D)
        emit((y3 * (1.0 + scs_ref[...])[None] + shs_ref[...][None]).reshape(TM, D))


def _seq_of_tile(i):
    return jnp.minimum(i // TILES_PER_SEQ, BATCH - 1)


def _norm_call(xp, xp_spec, xs, xs_spec, nw, mod_p, mod_s, k_scale, k_shift, w_dt):
    with_dt = w_dt is not None
    in_specs = [
        xp_spec, xs_spec,
        pl.BlockSpec((1, D), lambda i: (0, 0)),
        pl.BlockSpec((1, 1, D), lambda i: (_seq_of_tile(i), 0, k_scale)),
        pl.BlockSpec((1, 1, D), lambda i: (_seq_of_tile(i), 0, k_shift)),
        pl.BlockSpec((DEC_BATCH, D), lambda i: (0, k_scale)),
        pl.BlockSpec((DEC_BATCH, D), lambda i: (0, k_shift)),
    ]
    args = [xp, xs, nw, mod_p, mod_p, mod_s, mod_s]
    out_specs = [pl.BlockSpec((TM, D), lambda i: (i, 0))]
    out_shape = [jax.ShapeDtypeStruct((M_ROWS, D), BF16)]
    if with_dt:
        in_specs.append(pl.BlockSpec((D, LANE), lambda i: (0, 0)))
        args.append(w_dt)
        out_specs.append(pl.BlockSpec((TM, LANE), lambda i: (i, 0)))
        out_shape.append(jax.ShapeDtypeStruct((M_ROWS, LANE), F32))
    return pl.pallas_call(
        functools.partial(_norm_kernel, with_dt),
        grid=(N_TILES,),
        in_specs=in_specs,
        out_specs=out_specs,
        out_shape=out_shape,
        compiler_params=_params("arbitrary"),
        name="norm_mod_dt" if with_dt else "norm_mod",
    )(*args)


DT_SHIFT = HEADS
N_IN_BLOCKS = 13


def _in_src_block(j):
    return jnp.where(j < 4, j, jnp.where(j < 12, j + 1, 4))


def _inproj_kernel(a_ref, w_ref, wx_ref, o_ref, wbf_ref):
    j = pl.program_id(0)
    i = pl.program_id(1)
    shifted = jnp.logical_and(j >= 4, j < 12)
    n_chunks = D // LANE

    @pl.when(jnp.logical_and(i == 0, jnp.logical_not(shifted)))
    def _():
        def body(r, carry):
            rows = pl.ds(pl.multiple_of(r * LANE, LANE), LANE)
            wbf_ref[rows, :] = w_ref[rows, :].astype(BF16)
            return carry
        lax.fori_loop(0, n_chunks, body, 0)

    @pl.when(jnp.logical_and(i == 0, shifted))
    def _():
        def body(r, carry):
            rows = pl.ds(pl.multiple_of(r * LANE, LANE), LANE)
            w = w_ref[rows, :]
            wx = wx_ref[rows, :]
            wbf_ref[rows, :] = jnp.concatenate(
                [w[:, DT_SHIFT:], wx[:, :DT_SHIFT]], axis=1).astype(BF16)
            return carry
        lax.fori_loop(0, n_chunks, body, 0)

    o_ref[...] = _dot(a_ref[...], wbf_ref[...]).astype(o_ref.dtype)


def _inproj_call(n1, w_in):
    sub = TN_IN // LANE
    return pl.pallas_call(
        _inproj_kernel,
        grid=(N_IN_BLOCKS, N_TILES),
        in_specs=[pl.BlockSpec((TM, D), lambda j, i: (i, 0)),
                  pl.BlockSpec((D, TN_IN), lambda j, i: (0, _in_src_block(j))),
                  pl.BlockSpec((D, LANE), lambda j, i: (0, (_in_src_block(j) + 1) * sub))],
        out_specs=pl.BlockSpec((TM, TN_IN), lambda j, i: (i, j)),
        out_shape=jax.ShapeDtypeStruct((M_ROWS, N_IN_BLOCKS * TN_IN), BF16),
        scratch_shapes=[pltpu.VMEM((D, TN_IN), BF16)],
        compiler_params=_params("arbitrary", "arbitrary"),
        name="in_proj",
    )(n1, w_in, w_in)


PZ, PX, PU, PV, PGA, PGB = 0, 1, 2, 3, 4, 5
PBC_1024 = 12


def _gated_group_norm(get_y, z_ref, nw_ref, o_ref):
    gw = D // GROUPS
    for g in range(GROUPS):
        cols = slice(g * gw, (g + 1) * gw)
        zv = z_ref[:, cols].astype(F32)
        gg = _rms(get_y(cols) * _silu(zv))
        o_ref[:, cols] = (gg * nw_ref[:, cols]).astype(o_ref.dtype)


def _ssd_prompt_kernel(z_ref, x_ref, bc_ref, dt_ref, cw_ref, cb_ref, hp_ref, nw_ref, tri_ref,
                       y_ref, ssm_ref, cst_ref,
                       ext_ref, act_ref, st_ref, yscr_ref):
    c = pl.program_id(1)
    T = CHUNK

    @pl.when(c == 0)
    def _():
        ext_ref[0:8, :] = jnp.zeros((8, CONV_DIM), F32)
        st_ref[...] = jnp.zeros(st_ref.shape, F32)

    ext_ref[8:8 + T, 0:D] = x_ref[...].astype(F32)
    ext_ref[8:8 + T, D:CONV_DIM] = bc_ref[...].astype(F32)

    cwid = 512
    for q in range(CONV_DIM // cwid):
        cols = slice(q * cwid, (q + 1) * cwid)
        cw = cw_ref[:, cols]
        acc = cb_ref[:, cols] + cw[3:4] * ext_ref[8:8 + T, cols]
        acc = acc + cw[0:1] * ext_ref[5:5 + T, cols]
        acc = acc + cw[1:2] * ext_ref[6:6 + T, cols]
        acc = acc + cw[2:3] * ext_ref[7:7 + T, cols]
        act_ref[:, cols] = _silu(acc)

    @pl.when(c == N_CHUNKS - 1)
    def _():
        cst_ref[0] = ext_ref[5 + T:8 + T, :]

    ext_ref[0:8, :] = ext_ref[T:T + 8, :]

    hp = hp_ref[...]
    dt = _softplus(dt_ref[...] + hp[0:1, :])
    adt = dt * (-jnp.exp(hp[1:2, :]))
    cs = _dot_exact_lhs(tri_ref[...], adt)
    cs_t = cs.T
    dt_t = dt.T

    row = lax.broadcasted_iota(jnp.int32, (T, T), 0)
    lane = lax.broadcasted_iota(jnp.int32, (T, T), 1)
    causal = row >= lane
    left = lane < HEADDIM
    nt = (((1,), (1,)), ((), ()))

    for g in range(GROUPS):
        c_g = act_ref[:, D + GROUPS * STATE + g * STATE:D + GROUPS * STATE + (g + 1) * STATE]
        b_g = act_ref[:, D + g * STATE:D + (g + 1) * STATE]
        c_b = c_g.astype(BF16)
        cb = lax.dot_general(c_b, b_g.astype(BF16), nt, preferred_element_type=F32)
        b_t = b_g.T
        for k4 in range(HEADS // GROUPS // 2):
            k = g * (HEADS // GROUPS // 2) + k4
            xp = act_ref[:, k * LANE:(k + 1) * LANE]
            xpb = xp.astype(BF16)
            st = st_ref[k]
            yraw = _dot(c_b, st.astype(BF16))
            halves_y, halves_s = [], []
            for e in range(2):
                h = 2 * k + e
                colb = cs[:, h:h + 1]
                rowb = cs_t[h:h + 1, :]
                dtrow = dt_t[h:h + 1, :]
                decay = jnp.exp(jnp.where(causal, colb - rowb, -jnp.inf))
                m_h = (cb * decay * dtrow).astype(BF16)
                alast = cs_t[h:h + 1, T - 1:T]
                wrow = jnp.exp(alast - rowb) * dtrow
                dskip = hp[2:3, h:h + 1]
                halves_y.append(_dot(m_h, xpb) + jnp.exp(colb) * yraw + dskip * xp)
                halves_s.append(jnp.exp(alast) * st + _dot((b_t * wrow).astype(BF16), xpb))
            yscr_ref[:, k * LANE:(k + 1) * LANE] = jnp.where(left, halves_y[0], halves_y[1])
            st_ref[k] = jnp.where(left, halves_s[0], halves_s[1])

    _gated_group_norm(lambda cols: yscr_ref[:, cols], z_ref, nw_ref, y_ref)

    @pl.when(c == N_CHUNKS - 1)
    def _():
        for k in range(HEADS // 2):
            ssm_ref[0, k] = st_ref[k].T


def _ssd_prompt_call(proj, dt_raw, cw, cb, hp, nw, tri):
    row = lambda b, c: b * N_CHUNKS + c
    const = lambda b, c: (0, 0)
    return pl.pallas_call(
        _ssd_prompt_kernel,
        grid=(BATCH, N_CHUNKS),
        in_specs=[pl.BlockSpec((CHUNK, D), lambda b, c: (row(b, c), PZ)),
                  pl.BlockSpec((CHUNK, D), lambda b, c: (row(b, c), PX)),
                  pl.BlockSpec((CHUNK, 2 * GROUPS * STATE), lambda b, c: (row(b, c), PBC_1024)),
                  pl.BlockSpec((CHUNK, LANE), lambda b, c: (row(b, c), 0)),
                  pl.BlockSpec((4, CONV_DIM), const),
                  pl.BlockSpec((1, CONV_DIM), const),
                  pl.BlockSpec((8, LANE), const),
                  pl.BlockSpec((1, D), const),
                  pl.BlockSpec((CHUNK, CHUNK), const)],
        out_specs=[pl.BlockSpec((CHUNK, D), lambda b, c: (row(b, c), 0)),
                   pl.BlockSpec((1, HEADS // 2, LANE, STATE), lambda b, c: (b, 0, 0, 0)),
                   pl.BlockSpec((1, 3, CONV_DIM), lambda b, c: (b, 0, 0))],
        out_shape=[jax.ShapeDtypeStruct((NP_ROWS, D), BF16),
                   jax.ShapeDtypeStruct((BATCH, HEADS // 2, LANE, STATE), F32),
                   jax.ShapeDtypeStruct((BATCH, 3, CONV_DIM), F32)],
        scratch_shapes=[pltpu.VMEM((CHUNK + 8, CONV_DIM), F32),
                        pltpu.VMEM((CHUNK, CONV_DIM), F32),
                        pltpu.VMEM((HEADS // 2, STATE, LANE), F32),
                        pltpu.VMEM((CHUNK, D), F32)],
        compiler_params=_params("arbitrary", "arbitrary"),
        name="ssd_prompt",
    )(proj, proj, proj, dt_raw, cw, cb, hp, nw, tri)


def _ssds_prep_step(tt, x_ref, bc_ref, dt_ref, cst_ref, cw_ref, cb_ref, hp_ref, dskx_ref,
                    ex_ref, seg_ref,
                    ypre_ref, efull_ref, xw_ref, bb_ref, cc_ref, e3_ref, ncs_ref, act_ref):
    B = DEC_BATCH
    cwid = 512
    cdim = GROUPS * STATE
    blk = lambda t: slice(t * B, (t + 1) * B)

    def tap(idx, q):
        if idx < 3:
            return cst_ref[idx, :, q * cwid:(q + 1) * cwid]
        if q < D // cwid:
            return x_ref[blk(idx - 3), q * cwid:(q + 1) * cwid].astype(F32)
        return bc_ref[blk(idx - 3), q * cwid - D:(q + 1) * cwid - D].astype(F32)

    for q in range(CONV_DIM // cwid):
        cols = slice(q * cwid, (q + 1) * cwid)
        cw = cw_ref[:, cols]
        newest = tap(tt + 3, q)
        acc = cb_ref[:, cols] + cw[3:4] * newest
        for kk in range(3):
            acc = acc + cw[kk:kk + 1] * tap(tt + kk, q)
        act_ref[blk(tt), cols] = _silu(acc)
        if tt >= 1:
            ncs_ref[0, :, cols] = newest

    bb_ref[...] = act_ref[blk(tt), D:D + cdim]
    cc_ref[...] = act_ref[blk(tt), D + cdim:CONV_DIM]

    hp = hp_ref[...]
    a_neg = -jnp.exp(hp[1:2, :])
    dts, css = [], []
    run = None
    for t in range(DEC_SEQ):
        dt = _softplus(dt_ref[blk(t), :] + hp[0:1, :])
        run = dt * a_neg if run is None else run + dt * a_neg
        dts.append(dt)
        css.append(run)

    ex = ex_ref[...]
    e3_ref[...] = jnp.exp(css[-1])
    efull_ref[...] = _dot_exact_rhs(jnp.exp(css[tt]), ex)
    w_t = jnp.exp(css[-1] - css[tt]) * dts[tt]
    xw_ref[...] = act_ref[blk(tt), 0:D] * _dot_exact_rhs(w_t, ex)

    seg = seg_ref[...]
    acc = dskx_ref[...] * act_ref[blk(tt), 0:D]
    c_t = act_ref[blk(tt), D + cdim:CONV_DIM]
    for s in range(tt + 1):
        b_s = act_ref[blk(s), D:D + cdim]
        cbh = _dot_exact_rhs(c_t * b_s, seg)
        g_ts = cbh * jnp.exp(css[tt] - css[s]) * dts[s]
        acc = acc + _dot_exact_rhs(g_ts, ex) * act_ref[blk(s), 0:D]
    ypre_ref[...] = acc


def _ssds_prep_kernel(*refs):
    t = pl.program_id(0)
    for tt in range(DEC_SEQ):
        pl.when(t == tt)(functools.partial(_ssds_prep_step, tt, *refs))


def _ssds_prep_call(proj, dt_raw, cst_t, cw, cb, hp, dskx, ex, seg):
    full = lambda shape: pl.BlockSpec(shape, lambda t: (0,) * len(shape))
    step = lambda width: pl.BlockSpec((DEC_BATCH, width), lambda t: (t, 0))
    cdim = GROUPS * STATE
    return pl.pallas_call(
        _ssds_prep_kernel,
        grid=(DEC_SEQ,),
        in_specs=[pl.BlockSpec((NS_ROWS, D), lambda t: (NP_TILES, PX)),
                  pl.BlockSpec((NS_ROWS, 2 * cdim), lambda t: (NP_TILES, PBC_1024)),
                  pl.BlockSpec((NS_ROWS, LANE), lambda t: (NP_TILES, 0)),
                  full((3, DEC_BATCH, CONV_DIM)),
                  full((4, CONV_DIM)), full((1, CONV_DIM)), full((8, LANE)), full((1, D)),
                  full((LANE, D)), full((cdim, LANE))],
        out_specs=[step(D), step(D), step(D), step(cdim), step(cdim),
                   full((DEC_BATCH, LANE)),
                   pl.BlockSpec((1, DEC_BATCH, CONV_DIM), lambda t: (jnp.maximum(t - 1, 0), 0, 0))],
        out_shape=[jax.ShapeDtypeStruct((NS_ROWS, D), F32),
                   jax.ShapeDtypeStruct((NS_ROWS, D), F32),
                   jax.ShapeDtypeStruct((NS_ROWS, D), F32),
                   jax.ShapeDtypeStruct((NS_ROWS, cdim), F32),
                   jax.ShapeDtypeStruct((NS_ROWS, cdim), F32),
                   jax.ShapeDtypeStruct((DEC_BATCH, LANE), F32),
                   jax.ShapeDtypeStruct((3, DEC_BATCH, CONV_DIM), F32)],
        scratch_shapes=[pltpu.VMEM((NS_ROWS, CONV_DIM), F32)],
        compiler_params=_params("arbitrary"),
        name="ssd_sample_prep",
    )(proj, proj, dt_raw, cst_t, cw, cb, hp, dskx, ex, seg)


SEQ_PER_STEP = 4
T_PAD = 8


def _ssds_state_kernel(e3_ref, st_ref, cc_ref, bb_ref, xw_ref, yo_ref, so_ref):
    blk = pl.program_id(0)
    nt = (((1,), (1,)), ((), ()))
    tn = (((0,), (0,)), ((), ()))
    hpg = HEADS // GROUPS
    gw = hpg * HEADDIM
    for s in range(SEQ_PER_STEP):
        b = blk * SEQ_PER_STEP + s
        for g in range(GROUPS):
            h0 = st_ref[s, g]
            c_g = cc_ref[s, :, g * STATE:(g + 1) * STATE].astype(BF16)
            yo_ref[s, :, g * gw:(g + 1) * gw] = lax.dot_general(
                c_g, h0.astype(BF16), nt, preferred_element_type=F32)
            x_g = xw_ref[s, :, g * gw:(g + 1) * gw].astype(BF16)
            b_g = bb_ref[s, :, g * STATE:(g + 1) * STATE].astype(BF16)
            dh = lax.dot_general(x_g, b_g, tn, preferred_element_type=F32)
            for hh in range(hpg):
                rows = slice(hh * HEADDIM, (hh + 1) * HEADDIM)
                so_ref[s, g, rows, :] = e3_ref[b, g * hpg + hh] * h0[rows] + dh[rows]


def _ssds_state_call(e3, state, cc_b, bb_b, xw_b):
    sb = SEQ_PER_STEP
    gw = (HEADS // GROUPS) * HEADDIM
    cdim = GROUPS * STATE
    return pl.pallas_call(
        _ssds_state_kernel,
        grid=(DEC_BATCH // sb,),
        in_specs=[pl.BlockSpec(memory_space=pltpu.SMEM),
                  pl.BlockSpec((sb, GROUPS, gw, STATE), lambda i: (i, 0, 0, 0)),
                  pl.BlockSpec((sb, T_PAD, cdim), lambda i: (i, 0, 0)),
                  pl.BlockSpec((sb, T_PAD, cdim), lambda i: (i, 0, 0)),
                  pl.BlockSpec((sb, T_PAD, D), lambda i: (i, 0, 0))],
        out_specs=[pl.BlockSpec((sb, T_PAD, D), lambda i: (i, 0, 0)),
                   pl.BlockSpec((sb, GROUPS, gw, STATE), lambda i: (i, 0, 0, 0))],
        out_shape=[jax.ShapeDtypeStruct((DEC_BATCH, T_PAD, D), F32),
                   jax.ShapeDtypeStruct((DEC_BATCH, GROUPS, gw, STATE), F32)],
        compiler_params=_params("arbitrary"),
        name="ssd_sample_state",
    )(e3, state, cc_b, bb_b, xw_b)


def _ssds_post_kernel(ypre_ref, efull_ref, yo_ref, z_ref, nw_ref, o_ref):
    get_y = lambda cols: ypre_ref[:, cols] + efull_ref[:, cols] * yo_ref[:, cols]
    _gated_group_norm(get_y, z_ref, nw_ref, o_ref)


def _ssds_post_call(ypre, efull, yo_t, proj, nw):
    step = pl.BlockSpec((DEC_BATCH, D), lambda t: (t, 0))
    return pl.pallas_call(
        _ssds_post_kernel,
        grid=(DEC_SEQ,),
        in_specs=[step, step, step,
                  pl.BlockSpec((DEC_BATCH, D), lambda t: (NP_ROWS // DEC_BATCH + t, PZ)),
                  pl.BlockSpec((1, D), lambda t: (0, 0))],
        out_specs=step,
        out_shape=jax.ShapeDtypeStruct((NS_ROWS, D), BF16),
        compiler_params=_params("arbitrary"),
        name="ssd_sample_post",
    )(ypre, efull, yo_t, proj, nw)


def _mlp_kernel(u_ref, v_ref, lnw_ref, lnb_ref, ws_ref, bsx_ref, wsx_ref, bsx4_ref,
                y_ref, cv_ref, wm_ref):
    i = pl.program_id(0)
    T = CHUNK

    @pl.when(i == 0)
    def _():
        row = lax.broadcasted_iota(jnp.int32, (T, T), 0)
        lane = lax.broadcasted_iota(jnp.int32, (T, T), 1)
        for g in range(MLP_GROUPS):
            wm_ref[g] = jnp.where(row >= lane, ws_ref[g], 0.0).astype(BF16)

    def vnorm(rows):
        vg = jax.nn.gelu(v_ref[rows, :].astype(F32))
        xc = vg - jnp.mean(vg, axis=-1, keepdims=True)
        y = xc * lax.rsqrt(jnp.mean(xc * xc, axis=-1, keepdims=True) + EPS)
        return y * lnw_ref[...] + lnb_ref[...]

    @pl.when(i < NP_TILES)
    def _():
        for cc in range(TM // T):
            rows = slice(cc * T, (cc + 1) * T)
            vnb = vnorm(rows).astype(BF16)
            for g in range(MLP_GROUPS):
                cols = slice(g * MLP_GROUP_DIM, (g + 1) * MLP_GROUP_DIM)
                sv = _dot(wm_ref[g], vnb[:, cols]) + bsx_ref[:, cols]
                y_ref[rows, cols] = (jax.nn.gelu(u_ref[rows, cols].astype(F32)) * sv).astype(BF16)

    @pl.when(i >= NP_TILES)
    def _():
        B = DEC_BATCH
        for t in range(DEC_SEQ):
            rows = slice(t * B, (t + 1) * B)
            cv_ref[rows, :] = vnorm(rows)
        for t in range(DEC_SEQ):
            rows = slice(t * B, (t + 1) * B)
            acc = bsx4_ref[t:t + 1, :]
            for s in range(t + 1):
                acc = acc + wsx_ref[4 * t + s:4 * t + s + 1, :] * cv_ref[s * B:(s + 1) * B, :]
            y_ref[rows, :] = (jax.nn.gelu(u_ref[rows, :].astype(F32)) * acc).astype(BF16)


def _mlp_call(proj, lnw, lnb, ws, bsx, wsx, bsx4):
    full = lambda shape: pl.BlockSpec(shape, lambda i: (0,) * len(shape))
    return pl.pallas_call(
        _mlp_kernel,
        grid=(N_TILES,),
        in_specs=[pl.BlockSpec((TM, D), lambda i: (i, PU)),
                  pl.BlockSpec((TM, D), lambda i: (i, PV)),
                  full((1, D)), full((1, D)),
                  full((MLP_GROUPS, CHUNK, CHUNK)),
                  full((CHUNK, D)), full((16, D)), full((8, D))],
        out_specs=[pl.BlockSpec((TM, D), lambda i: (i, 0)),
                   full((NS_ROWS, D))],
        out_shape=[jax.ShapeDtypeStruct((M_ROWS, D), BF16),
                   jax.ShapeDtypeStruct((NS_ROWS, D), F32)],
        scratch_shapes=[pltpu.VMEM((MLP_GROUPS, CHUNK, CHUNK), BF16)],
        compiler_params=_params("arbitrary"),
        name="gmlp",
    )(proj, proj, lnw, lnb, ws, bsx, wsx, bsx4)


def _cast_rows(src_ref, dst_ref, chunk=256):
    def body(r, carry):
        rows = pl.ds(pl.multiple_of(r * chunk, chunk), chunk)
        dst_ref[rows, :] = src_ref[rows, :].astype(dst_ref.dtype)
        return carry
    lax.fori_loop(0, src_ref.shape[0] // chunk, body, 0)


def _merge_kernel(ysp_ref, yss_ref, ym_ref, ga_ref, gb_ref, w1_ref, w2_ref, o_ref, w1b, w2b):
    i = pl.program_id(1)

    @pl.when(i == 0)
    def _():
        _cast_rows(w1_ref, w1b)
        _cast_rows(w2_ref, w2b)

    def emit(ys):
        a1 = _dot(ys, w1b[...])
        a2 = _dot(ym_ref[...], w2b[...])
        o_ref[...] = (jax.nn.sigmoid(ga_ref[...].astype(F32)) * a1
                      + jax.nn.sigmoid(gb_ref[...].astype(F32)) * a2).astype(BF16)

    @pl.when(i < NP_TILES)
    def _():
        emit(ysp_ref[...])

    @pl.when(i >= NP_TILES)
    def _():
        emit(yss_ref[...])


def _merge_call(ysp, yss, ym, proj, w1, w2):
    tn = 512
    nb = D // tn
    return pl.pallas_call(
        _merge_kernel,
        grid=(nb, N_TILES),
        in_specs=[pl.BlockSpec((TM, D), lambda j, i: (jnp.minimum(i, NP_TILES - 1), 0)),
                  pl.BlockSpec((TM, D), lambda j, i: (0, 0)),
                  pl.BlockSpec((TM, D), lambda j, i: (i, 0)),
                  pl.BlockSpec((TM, tn), lambda j, i: (i, PGA * nb + j)),
                  pl.BlockSpec((TM, tn), lambda j, i: (i, PGB * nb + j)),
                  pl.BlockSpec((D, tn), lambda j, i: (0, j)),
                  pl.BlockSpec((D, tn), lambda j, i: (0, j))],
        out_specs=pl.BlockSpec((TM, tn), lambda j, i: (i, j)),
        out_shape=jax.ShapeDtypeStruct((M_ROWS, D), BF16),
        scratch_shapes=[pltpu.VMEM((D, tn), BF16), pltpu.VMEM((D, tn), BF16)],
        compiler_params=_params("arbitrary", "arbitrary"),
        name="branch_merge",
    )(ysp, yss, ym, proj, proj, w1, w2)


def _resid_kernel(a_ref, w_ref, rp_ref, rs_ref, gp_ref, gs_ref, o_ref, wb):
    i = pl.program_id(1)

    @pl.when(i == 0)
    def _():
        _cast_rows(w_ref, wb)

    acc = _dot(a_ref[...], wb[...])

    @pl.when(i < NP_TILES)
    def _():
        o_ref[...] = rp_ref[...] + gp_ref[0] * acc

    @pl.when(i >= NP_TILES)
    def _():
        tn = acc.shape[-1]
        acc3 = acc.reshape(DEC_SEQ, DEC_BATCH, tn) * gs_ref[...][None]
        o_ref[...] = rs_ref[...] + acc3.reshape(TM, tn)


def _resid_call(a, w, rp, rp_spec_fn, rs, rs_spec_fn, mod_p, mod_s, k_gate, tn, name):
    kdim = a.shape[1]
    nb = D // tn
    return pl.pallas_call(
        _resid_kernel,
        grid=(nb, N_TILES),
        in_specs=[pl.BlockSpec((TM, kdim), lambda j, i: (i, 0)),
                  pl.BlockSpec((kdim, tn), lambda j, i: (0, j)),
                  rp_spec_fn(tn), rs_spec_fn(tn),
                  pl.BlockSpec((1, 1, tn), lambda j, i: (_seq_of_tile(i), 0, k_gate * nb + j)),
                  pl.BlockSpec((DEC_BATCH, tn), lambda j, i: (0, k_gate * nb + j))],
        out_specs=pl.BlockSpec((TM, tn), lambda j, i: (i, j)),
        out_shape=jax.ShapeDtypeStruct((M_ROWS, D), F32),
        scratch_shapes=[pltpu.VMEM((kdim, tn), BF16)],
        compiler_params=_params("arbitrary", "arbitrary"),
        name=name,
    )(a, w, rp, rs, mod_p, mod_s)


TN_FF = 512
N_FF_BLOCKS = D_FF // TN_FF


def _up_kernel(a_ref, wa_ref, wv_ref, cw_ref, cb_ref, fst_ref,
               h_ref, fcp_ref, fcs_ref, wab, wvb, ext_ref):
    i = pl.program_id(1)

    @pl.when(i == 0)
    def _():
        _cast_rows(wa_ref, wab)
        _cast_rows(wv_ref, wvb)

    a = _dot(a_ref[...], wab[...])
    v = _dot(a_ref[...], wvb[...])
    cw = cw_ref[...]
    cb = cb_ref[...]

    @pl.when(i < NP_TILES)
    def _():
        @pl.when(i % TILES_PER_SEQ == 0)
        def _():
            ext_ref[0:8, :] = jnp.zeros((8, TN_FF), F32)

        ext_ref[8:8 + TM, :] = a
        conv = cb + cw[2:3] * a + cw[1:2] * ext_ref[7:7 + TM, :] + cw[0:1] * ext_ref[6:6 + TM, :]
        h_ref[...] = (jax.nn.gelu(conv) * v).astype(BF16)
        fcp_ref[0] = ext_ref[6 + TM:8 + TM, :]
        ext_ref[0:8, :] = ext_ref[TM:TM + 8, :]

    @pl.when(i >= NP_TILES)
    def _():
        B = DEC_BATCH
        ext_ref[0:TM, :] = a
        blocks = [fst_ref[0], fst_ref[1]] + [ext_ref[t * B:(t + 1) * B, :] for t in range(DEC_SEQ)]
        for t in range(DEC_SEQ):
            conv = cb + cw[2:3] * blocks[t + 2] + cw[1:2] * blocks[t + 1] + cw[0:1] * blocks[t]
            h_ref[t * B:(t + 1) * B, :] = (jax.nn.gelu(conv) * v[t * B:(t + 1) * B]).astype(BF16)
        fcs_ref[0] = blocks[4]
        fcs_ref[1] = blocks[5]


def _up_call(n2, w_up, cw, cb, fst_t):
    return pl.pallas_call(
        _up_kernel,
        grid=(N_FF_BLOCKS, N_TILES),
        in_specs=[pl.BlockSpec((TM, D), lambda j, i: (i, 0)),
                  pl.BlockSpec((D, TN_FF), lambda j, i: (0, j)),
                  pl.BlockSpec((D, TN_FF), lambda j, i: (0, N_FF_BLOCKS + j)),
                  pl.BlockSpec((3, TN_FF), lambda j, i: (0, j)),
                  pl.BlockSpec((1, TN_FF), lambda j, i: (0, j)),
                  pl.BlockSpec((2, DEC_BATCH, TN_FF), lambda j, i: (0, 0, j))],
        out_specs=[pl.BlockSpec((TM, TN_FF), lambda j, i: (i, j)),
                   pl.BlockSpec((1, 2, TN_FF), lambda j, i: (_seq_of_tile(i), 0, j)),
                   pl.BlockSpec((2, DEC_BATCH, TN_FF), lambda j, i: (0, 0, j))],
        out_shape=[jax.ShapeDtypeStruct((M_ROWS, D_FF), BF16),
                   jax.ShapeDtypeStruct((BATCH, 2, D_FF), F32),
                   jax.ShapeDtypeStruct((2, DEC_BATCH, D_FF), F32)],
        scratch_shapes=[pltpu.VMEM((D, TN_FF), BF16), pltpu.VMEM((D, TN_FF), BF16),
                        pltpu.VMEM((TM + 8, TN_FF), F32)],
        compiler_params=_params("arbitrary", "arbitrary"),
        name="ffn_up",
    )(n2, w_up, w_up, cw, cb, fst_t)


def _final_kernel(x_ref, w_ref, yp_ref, ys_ref):
    i = pl.program_id(0)
    y = _rms(x_ref[...]) * w_ref[...]

    @pl.when(i < NP_TILES)
    def _():
        yp_ref[...] = y

    @pl.when(i >= NP_TILES)
    def _():
        ys_ref[...] = y


def _final_call(x3, w):
    return pl.pallas_call(
        _final_kernel,
        grid=(N_TILES,),
        in_specs=[pl.BlockSpec((TM, D), lambda i: (i, 0)),
                  pl.BlockSpec((1, D), lambda i: (0, 0))],
        out_specs=[pl.BlockSpec((TM, D), lambda i: (jnp.minimum(i, NP_TILES - 1), 0)),
                   pl.BlockSpec((TM, D), lambda i: (0, 0))],
        out_shape=[jax.ShapeDtypeStruct((NP_ROWS, D), F32),
                   jax.ShapeDtypeStruct((NS_ROWS, D), F32)],
        compiler_params=_params("arbitrary"),
        name="final_norm",
    )(x3, w)


def _to_time_major(a):
    return jnp.transpose(a, (1, 0, 2))


def _pad_time(a_t, width):
    a_b = jnp.transpose(a_t.reshape(DEC_SEQ, DEC_BATCH, width), (1, 0, 2))
    return jnp.pad(a_b, ((0, 0), (0, T_PAD - DEC_SEQ), (0, 0)))


def kernel(x_prompt, x_sample, state_ssm, state_ssd_conv, state_ffn_conv, c_prompt, c_sample,
           norm1_w, w_ada, b_ada, w_in, ssd_conv_w, ssd_conv_b, dt_bias, a_log, d_skip,
           ssd_norm_w, mlp_ln_w, mlp_ln_b, w_spatial, b_spatial, w_ssd_o, w_mlp_o, w_out,
           norm2_w, w_up, ffn_conv_w, ffn_conv_b, w_down, final_norm_w):
    assert w_in.shape[0] == 1, "single-layer trunk"
    row = lambda v: v.reshape(1, -1)

    xp = x_prompt.reshape(NP_ROWS, D)
    xs = _to_time_major(x_sample).reshape(NS_ROWS, D)

    c_all = jnp.concatenate([c_prompt, c_sample, jnp.zeros((4, D), F32)], axis=0)
    mod = _ada_call(c_all, w_ada[0], row(b_ada[0]))
    mod_p = mod[:BATCH].reshape(BATCH, 1, 6 * D)
    mod_s = mod[BATCH:BATCH + DEC_BATCH]
    K_SHIFT1, K_SCALE1, K_GATE1, K_SHIFT2, K_SCALE2, K_GATE2 = range(6)

    xp_spec = pl.BlockSpec((TM, D), lambda i: (jnp.minimum(i, NP_TILES - 1), 0))
    xs_spec = pl.BlockSpec((TM, D), lambda i: (0, 0))
    w_in0 = w_in[0]
    w_dt = w_in0[:, DT_COL:DT_COL + LANE]
    n1, dt_raw = _norm_call(xp, xp_spec, xs, xs_spec, row(norm1_w[0]), mod_p, mod_s,
                            K_SCALE1, K_SHIFT1, w_dt)
    proj = _inproj_call(n1, w_in0)

    hp = jnp.zeros((8, LANE), F32)
    hp = hp.at[0, :HEADS].set(dt_bias[0]).at[1, :HEADS].set(a_log[0]).at[2, :HEADS].set(d_skip[0])
    tri = jnp.tril(jnp.ones((CHUNK, CHUNK), F32)).astype(BF16)
    cw, cb = ssd_conv_w[0], row(ssd_conv_b[0])
    nw = row(ssd_norm_w[0])

    y_ssd_p, ssm_p, cst_p = _ssd_prompt_call(proj, dt_raw, cw, cb, hp, nw, tri)

    head_of_col = jnp.arange(D) // HEADDIM
    ex = (jnp.arange(LANE)[:, None] == head_of_col[None, :]).astype(BF16)
    grp_of_head = jnp.arange(LANE) // (HEADS // GROUPS)
    seg = ((jnp.arange(GROUPS * STATE)[:, None] // STATE == grp_of_head[None, :])
           & (jnp.arange(LANE)[None, :] < HEADS)).astype(BF16)
    dskx = row(jnp.repeat(d_skip[0], HEADDIM))
    cst_t = _to_time_major(state_ssd_conv[0])
    ypre, efull, xw_t, bb_t, cc_t, e3, ncs_t = _ssds_prep_call(
        proj, dt_raw, cst_t, cw, cb, hp, dskx, ex, seg)
    cdim = GROUPS * STATE
    gw = (HEADS // GROUPS) * HEADDIM
    yo_b, ssm_s = _ssds_state_call(
        e3[:, :HEADS], state_ssm[0].reshape(DEC_BATCH, GROUPS, gw, STATE),
        _pad_time(cc_t, cdim), _pad_time(bb_t, cdim), _pad_time(xw_t, D))
    yo_t = jnp.transpose(yo_b[:, :DEC_SEQ], (1, 0, 2)).reshape(NS_ROWS, D)
    y_ssd_s = _ssds_post_call(ypre, efull, yo_t, proj, nw)

    gidx = jnp.arange(D) // MLP_GROUP_DIM
    bsx = b_spatial[0][:, :CHUNK].T[:, gidx]
    ws4 = w_spatial[0][:, :DEC_SEQ, :DEC_SEQ]
    wsx = jnp.transpose(ws4, (1, 2, 0)).reshape(DEC_SEQ * DEC_SEQ, MLP_GROUPS)[:, gidx]
    bsx4 = jnp.pad(b_spatial[0][:, :DEC_SEQ].T[:, gidx], ((0, 8 - DEC_SEQ), (0, 0)))
    y_mlp, cv_t = _mlp_call(proj, row(mlp_ln_w[0]), row(mlp_ln_b[0]), w_spatial[0], bsx, wsx, bsx4)

    mixed = _merge_call(y_ssd_p, y_ssd_s, y_mlp, proj, w_ssd_o[0], w_mlp_o[0])
    x2 = _resid_call(
        mixed, w_out[0], xp,
        lambda tn: pl.BlockSpec((TM, tn), lambda j, i: (jnp.minimum(i, NP_TILES - 1), j)),
        xs, lambda tn: pl.BlockSpec((TM, tn), lambda j, i: (0, j)),
        mod_p, mod_s, K_GATE1, 1024, "out_proj")

    x2p_spec = pl.BlockSpec((TM, D), lambda i: (jnp.minimum(i, NP_TILES - 1), 0))
    x2s_spec = pl.BlockSpec((TM, D), lambda i: (NP_TILES, 0))
    (n2,) = _norm_call(x2, x2p_spec, x2, x2s_spec, row(norm2_w[0]), mod_p, mod_s,
                       K_SCALE2, K_SHIFT2, None)
    fst_t = _to_time_major(state_ffn_conv[0])
    h, ffn_p, ffn_s_t = _up_call(n2, w_up[0], ffn_conv_w[0], row(ffn_conv_b[0]), fst_t)
    x3 = _resid_call(
        h, w_down[0], x2,
        lambda tn: pl.BlockSpec((TM, tn), lambda j, i: (jnp.minimum(i, NP_TILES - 1), j)),
        x2, lambda tn: pl.BlockSpec((TM, tn), lambda j, i: (NP_TILES, j)),
        mod_p, mod_s, K_GATE2, 512, "ffn_down")

    y_p, y_s_t = _final_call(x3, row(final_norm_w))

    from_t = lambda a, t, c: jnp.transpose(a.reshape(t, DEC_BATCH, c), (1, 0, 2))
    return (y_p.reshape(BATCH, SEQ, D),
            from_t(y_s_t, DEC_SEQ, D),
            ssm_p.reshape(1, BATCH, HEADS, HEADDIM, STATE),
            ssm_s.reshape(1, DEC_BATCH, HEADS, HEADDIM, STATE),
            cst_p[None],
            from_t(ncs_t, 3, CONV_DIM)[None],
            ffn_p[None],
            from_t(ffn_s_t, 2, D_FF)[None],
            from_t(cv_t, DEC_SEQ, D)[None])
```

```python
import functools

import jax
import jax.numpy as jnp
from jax import lax
from jax.experimental import pallas as pl
from jax.experimental.pallas import tpu as pltpu

F32 = jnp.float32
BF16 = jnp.bfloat16

D = 2048
BATCH, SEQ = 4, 2048
DEC_BATCH, DEC_SEQ = 128, 4
NP_ROWS = BATCH * SEQ
NS_ROWS = DEC_BATCH * DEC_SEQ
M_ROWS = NP_ROWS + NS_ROWS
TM = 512
NP_TILES = NP_ROWS // TM
N_TILES = M_ROWS // TM
TILES_PER_SEQ = SEQ // TM
HEADS, HEADDIM, GROUPS, STATE = 32, 64, 4, 128
CHUNK = 128
N_CHUNKS = SEQ // CHUNK
CONV_DIM = D + 2 * GROUPS * STATE
MLP_GROUPS = 8
MLP_GROUP_DIM = D // MLP_GROUPS
D_FF = 5632
EPS = 1e-6
DT_COL = D + CONV_DIM
TN_IN = 1024
LANE = 128
VMEM_LIMIT = 56 * 1024 * 1024


def _params(*sem, flags=None):
    return pltpu.CompilerParams(dimension_semantics=sem, vmem_limit_bytes=VMEM_LIMIT, flags=flags)


def _dot(a, b):
    return jnp.dot(a, b, preferred_element_type=F32)


def _split_bf16(v, terms):
    out = []
    r = v
    for _ in range(terms):
        p = r.astype(BF16)
        out.append(p)
        r = r - p.astype(F32)
    return out


def _dot_exact_rhs(v, e, terms=3):
    acc = None
    for p in _split_bf16(v, terms):
        d = _dot(p, e)
        acc = d if acc is None else acc + d
    return acc


def _dot_exact_lhs(t, v, terms=3):
    acc = None
    for p in _split_bf16(v, terms):
        d = _dot(t, p)
        acc = d if acc is None else acc + d
    return acc


def _silu(x):
    return x * jax.nn.sigmoid(x)


def _softplus(x):
    return jnp.maximum(x, 0.0) + jnp.log1p(jnp.exp(-jnp.abs(x)))


def _rms(x):
    return x * lax.rsqrt(jnp.mean(x * x, axis=-1, keepdims=True) + EPS)


def _ada_kernel(c_ref, w_ref, b_ref, o_ref):
    a = _silu(c_ref[...]).astype(BF16)
    o_ref[...] = _dot(a, w_ref[...].astype(BF16)) + b_ref[...]


def _ada_call(c_all, w, b):
    rows = c_all.shape[0]
    tn = 1024
    return pl.pallas_call(
        _ada_kernel,
        grid=(6 * D // tn,),
        in_specs=[pl.BlockSpec((rows, D), lambda j: (0, 0)),
                  pl.BlockSpec((D, tn), lambda j: (0, j)),
                  pl.BlockSpec((1, tn), lambda j: (0, j))],
        out_specs=pl.BlockSpec((rows, tn), lambda j: (0, j)),
        out_shape=jax.ShapeDtypeStruct((rows, 6 * D), F32),
        compiler_params=_params("arbitrary"),
        name="ada_mod",
    )(c_all, w, b)


def _norm_kernel(with_dt, xp_ref, xs_ref, nw_ref, scp_ref, shp_ref, scs_ref, shs_ref, *rest):
    if with_dt:
        wdt_ref, n_ref, dt_ref = rest
    else:
        (n_ref,) = rest
    i = pl.program_id(0)

    def emit(n):
        nb = n.astype(BF16)
        n_ref[...] = nb
        if with_dt:
            dt_ref[...] = lax.dot_general(nb, wdt_ref[...].astype(BF16),
                                          (((1,), (1,)), ((), ())),
                                          preferred_element_type=F32)

    @pl.when(i < NP_TILES)
    def _():
        y = _rms(xp_ref[...]) * nw_ref[...]
        emit(y * (1.0 + scp_ref[0]) + shp_ref[0])

    @pl.when(i >= NP_TILES)
    def _():
        y = _rms(xs_ref[...]) * nw_ref[...]
        y3 = y.reshape(DEC_SEQ, DEC_BATCH, D)
        emit((y3 * (1.0 + scs_ref[...])[None] + shs_ref[...][None]).reshape(TM, D))


def _seq_of_tile(i):
    return jnp.minimum(i // TILES_PER_SEQ, BATCH - 1)


def _norm_call(xp, xp_spec, xs, xs_spec, nw, mod_p, mod_s, k_scale, k_shift, w_dt):
    with_dt = w_dt is not None
    in_specs = [
        xp_spec, xs_spec,
        pl.BlockSpec((1, D), lambda i: (0, 0)),
        pl.BlockSpec((1, 1, D), lambda i: (_seq_of_tile(i), 0, k_scale)),
        pl.BlockSpec((1, 1, D), lambda i: (_seq_of_tile(i), 0, k_shift)),
        pl.BlockSpec((DEC_BATCH, D), lambda i: (0, k_scale)),
        pl.BlockSpec((DEC_BATCH, D), lambda i: (0, k_shift)),
    ]
    args = [xp, xs, nw, mod_p, mod_p, mod_s, mod_s]
    out_specs = [pl.BlockSpec((TM, D), lambda i: (i, 0))]
    out_shape = [jax.ShapeDtypeStruct((M_ROWS, D), BF16)]
    if with_dt:
        in_specs.append(pl.BlockSpec((LANE, D), lambda i: (0, 0)))
        args.append(w_dt)
        out_specs.append(pl.BlockSpec((TM, LANE), lambda i: (i, 0)))
        out_shape.append(jax.ShapeDtypeStruct((M_ROWS, LANE), F32))
    return pl.pallas_call(
        functools.partial(_norm_kernel, with_dt),
        grid=(N_TILES,),
        in_specs=in_specs,
        out_specs=out_specs,
        out_shape=out_shape,
        compiler_params=_params("arbitrary"),
        name="norm_mod_dt" if with_dt else "norm_mod",
    )(*args)


N_IN_BLOCKS = 13
UVG_ROW = DT_COL + HEADS
BC_ROW = 2 * D


def _in_src_row(j):
    row = jnp.where(j < 4, j * TN_IN, jnp.where(j < 12, UVG_ROW + (j - 4) * TN_IN, BC_ROW))
    return pl.multiple_of(row, HEADS)


def _inproj_kernel(a_ref, wt_ref, o_ref, wbf_ref):
    i = pl.program_id(1)

    @pl.when(i == 0)
    def _():
        for r in range(TN_IN // LANE):
            rows = slice(r * LANE, (r + 1) * LANE)
            wbf_ref[:, rows] = wt_ref[rows, :].T.astype(BF16)

    o_ref[...] = _dot(a_ref[...], wbf_ref[...]).astype(o_ref.dtype)


def _inproj_call(n1, w_in_t):
    return pl.pallas_call(
        _inproj_kernel,
        grid=(N_IN_BLOCKS, N_TILES),
        in_specs=[pl.BlockSpec((TM, D), lambda j, i: (i, 0)),
                  pl.BlockSpec((pl.Element(TN_IN), pl.Element(D)),
                               lambda j, i: (_in_src_row(j), 0))],
        out_specs=pl.BlockSpec((TM, TN_IN), lambda j, i: (i, j)),
        out_shape=jax.ShapeDtypeStruct((M_ROWS, N_IN_BLOCKS * TN_IN), BF16),
        scratch_shapes=[pltpu.VMEM((D, TN_IN), BF16)],
        compiler_params=_params("arbitrary", "arbitrary"),
        name="in_proj",
    )(n1, w_in_t)


PZ, PX, PU, PV, PGA, PGB = 0, 1, 2, 3, 4, 5
PBC_1024 = 12


def _gated_group_norm(get_y, z_ref, nw_ref, o_ref):
    gw = D // GROUPS
    for g in range(GROUPS):
        cols = slice(g * gw, (g + 1) * gw)
        zv = z_ref[:, cols].astype(F32)
        gg = _rms(get_y(cols) * _silu(zv))
        o_ref[:, cols] = (gg * nw_ref[:, cols]).astype(o_ref.dtype)


def _ssd_prompt_kernel(z_ref, x_ref, bc_ref, dt_ref, cw_ref, cb_ref, hp_ref, nw_ref, tri_ref,
                       y_ref, ssm_ref, cst_ref,
                       ext_ref, act_ref, st_ref, yscr_ref):
    c = pl.program_id(1)
    T = CHUNK

    @pl.when(c == 0)
    def _():
        ext_ref[0:8, :] = jnp.zeros((8, CONV_DIM), F32)
        st_ref[...] = jnp.zeros(st_ref.shape, F32)

    ext_ref[8:8 + T, 0:D] = x_ref[...].astype(F32)
    ext_ref[8:8 + T, D:CONV_DIM] = bc_ref[...].astype(F32)

    cwid = 512
    for q in range(CONV_DIM // cwid):
        cols = slice(q * cwid, (q + 1) * cwid)
        cw = cw_ref[:, cols]
        acc = cb_ref[:, cols] + cw[3:4] * ext_ref[8:8 + T, cols]
        acc = acc + cw[0:1] * ext_ref[5:5 + T, cols]
        acc = acc + cw[1:2] * ext_ref[6:6 + T, cols]
        acc = acc + cw[2:3] * ext_ref[7:7 + T, cols]
        act_ref[:, cols] = _silu(acc)

    @pl.when(c == N_CHUNKS - 1)
    def _():
        cst_ref[0] = ext_ref[5 + T:8 + T, :]

    ext_ref[0:8, :] = ext_ref[T:T + 8, :]

    hp = hp_ref[...]
    dt = _softplus(dt_ref[...] + hp[0:1, :])
    adt = dt * (-jnp.exp(hp[1:2, :]))
    cs = _dot_exact_lhs(tri_ref[...], adt)
    cs_t = cs.T
    dt_t = dt.T

    row = lax.broadcasted_iota(jnp.int32, (T, T), 0)
    lane = lax.broadcasted_iota(jnp.int32, (T, T), 1)
    causal = row >= lane
    left = lane < HEADDIM
    nt = (((1,), (1,)), ((), ()))

    for g in range(GROUPS):
        c_g = act_ref[:, D + GROUPS * STATE + g * STATE:D + GROUPS * STATE + (g + 1) * STATE]
        b_g = act_ref[:, D + g * STATE:D + (g + 1) * STATE]
        c_b = c_g.astype(BF16)
        cb = lax.dot_general(c_b, b_g.astype(BF16), nt, preferred_element_type=F32)
        b_t = b_g.T
        for k4 in range(HEADS // GROUPS // 2):
            k = g * (HEADS // GROUPS // 2) + k4
            xp = act_ref[:, k * LANE:(k + 1) * LANE]
            xpb = xp.astype(BF16)
            st = st_ref[k]
            yraw = _dot(c_b, st.astype(BF16))
            halves_y, halves_s = [], []
            for e in range(2):
                h = 2 * k + e
                colb = cs[:, h:h + 1]
                rowb = cs_t[h:h + 1, :]
                dtrow = dt_t[h:h + 1, :]
                decay = jnp.exp(jnp.where(causal, colb - rowb, -jnp.inf))
                m_h = (cb * decay * dtrow).astype(BF16)
                alast = cs_t[h:h + 1, T - 1:T]
                wrow = jnp.exp(alast - rowb) * dtrow
                dskip = hp[2:3, h:h + 1]
                halves_y.append(_dot(m_h, xpb) + jnp.exp(colb) * yraw + dskip * xp)
                halves_s.append(jnp.exp(alast) * st + _dot((b_t * wrow).astype(BF16), xpb))
            yscr_ref[:, k * LANE:(k + 1) * LANE] = jnp.where(left, halves_y[0], halves_y[1])
            st_ref[k] = jnp.where(left, halves_s[0], halves_s[1])

    _gated_group_norm(lambda cols: yscr_ref[:, cols], z_ref, nw_ref, y_ref)

    @pl.when(c == N_CHUNKS - 1)
    def _():
        for k in range(HEADS // 2):
            ssm_ref[0, k] = st_ref[k].T


def _ssd_prompt_call(proj, dt_raw, cw, cb, hp, nw, tri):
    row = lambda b, c: b * N_CHUNKS + c
    const = lambda b, c: (0, 0)
    return pl.pallas_call(
        _ssd_prompt_kernel,
        grid=(BATCH, N_CHUNKS),
        in_specs=[pl.BlockSpec((CHUNK, D), lambda b, c: (row(b, c), PZ)),
                  pl.BlockSpec((CHUNK, D), lambda b, c: (row(b, c), PX)),
                  pl.BlockSpec((CHUNK, 2 * GROUPS * STATE), lambda b, c: (row(b, c), PBC_1024)),
                  pl.BlockSpec((CHUNK, LANE), lambda b, c: (row(b, c), 0)),
                  pl.BlockSpec((4, CONV_DIM), const),
                  pl.BlockSpec((1, CONV_DIM), const),
                  pl.BlockSpec((8, LANE), const),
                  pl.BlockSpec((1, D), const),
                  pl.BlockSpec((CHUNK, CHUNK), const)],
        out_specs=[pl.BlockSpec((CHUNK, D), lambda b, c: (row(b, c), 0)),
                   pl.BlockSpec((1, HEADS // 2, LANE, STATE), lambda b, c: (b, 0, 0, 0)),
                   pl.BlockSpec((1, 3, CONV_DIM), lambda b, c: (b, 0, 0))],
        out_shape=[jax.ShapeDtypeStruct((NP_ROWS, D), BF16),
                   jax.ShapeDtypeStruct((BATCH, HEADS // 2, LANE, STATE), F32),
                   jax.ShapeDtypeStruct((BATCH, 3, CONV_DIM), F32)],
        scratch_shapes=[pltpu.VMEM((CHUNK + 8, CONV_DIM), F32),
                        pltpu.VMEM((CHUNK, CONV_DIM), F32),
                        pltpu.VMEM((HEADS // 2, STATE, LANE), F32),
                        pltpu.VMEM((CHUNK, D), F32)],
        compiler_params=_params("arbitrary", "arbitrary"),
        name="ssd_prompt",
    )(proj, proj, proj, dt_raw, cw, cb, hp, nw, tri)


def _ssds_prep_step(tt, x_ref, bc_ref, dt_ref, cst_ref, cw_ref, cb_ref, hp_ref, dskx_ref,
                    ex_ref, seg_ref,
                    ypre_ref, efull_ref, xw_ref, bb_ref, cc_ref, e3_ref, ncs_ref, act_ref):
    B = DEC_BATCH
    cwid = 512
    cdim = GROUPS * STATE
    blk = lambda t: slice(t * B, (t + 1) * B)

    def tap(idx, q):
        if idx < 3:
            return cst_ref[idx, :, q * cwid:(q + 1) * cwid]
        if q < D // cwid:
            return x_ref[blk(idx - 3), q * cwid:(q + 1) * cwid].astype(F32)
        return bc_ref[blk(idx - 3), q * cwid - D:(q + 1) * cwid - D].astype(F32)

    for q in range(CONV_DIM // cwid):
        cols = slice(q * cwid, (q + 1) * cwid)
        cw = cw_ref[:, cols]
        newest = tap(tt + 3, q)
        acc = cb_ref[:, cols] + cw[3:4] * newest
        for kk in range(3):
            acc = acc + cw[kk:kk + 1] * tap(tt + kk, q)
        act_ref[blk(tt), cols] = _silu(acc)
        if tt >= 1:
            ncs_ref[0, :, cols] = newest

    bb_ref[...] = act_ref[blk(tt), D:D + cdim]
    cc_ref[...] = act_ref[blk(tt), D + cdim:CONV_DIM]

    hp = hp_ref[...]
    a_neg = -jnp.exp(hp[1:2, :])
    dts, css = [], []
    run = None
    for t in range(DEC_SEQ):
        dt = _softplus(dt_ref[blk(t), :] + hp[0:1, :])
        run = dt * a_neg if run is None else run + dt * a_neg
        dts.append(dt)
        css.append(run)

    ex = ex_ref[...]
    e3_ref[...] = jnp.exp(css[-1])
    efull_ref[...] = _dot_exact_rhs(jnp.exp(css[tt]), ex)
    w_t = jnp.exp(css[-1] - css[tt]) * dts[tt]
    xw_ref[...] = act_ref[blk(tt), 0:D] * _dot_exact_rhs(w_t, ex)

    seg = seg_ref[...]
    acc = dskx_ref[...] * act_ref[blk(tt), 0:D]
    c_t = act_ref[blk(tt), D + cdim:CONV_DIM]
    for s in range(tt + 1):
        b_s = act_ref[blk(s), D:D + cdim]
        cbh = _dot_exact_rhs(c_t * b_s, seg)
        g_ts = cbh * jnp.exp(css[tt] - css[s]) * dts[s]
        acc = acc + _dot_exact_rhs(g_ts, ex) * act_ref[blk(s), 0:D]
    ypre_ref[...] = acc


def _ssds_prep_kernel(*refs):
    t = pl.program_id(0)
    for tt in range(DEC_SEQ):
        pl.when(t == tt)(functools.partial(_ssds_prep_step, tt, *refs))


def _ssds_prep_call(proj, dt_raw, cst_t, cw, cb, hp, dskx, ex, seg):
    full = lambda shape: pl.BlockSpec(shape, lambda t: (0,) * len(shape))
    step = lambda width: pl.BlockSpec((DEC_BATCH, width), lambda t: (t, 0))
    cdim = GROUPS * STATE
    return pl.pallas_call(
        _ssds_prep_kernel,
        grid=(DEC_SEQ,),
        in_specs=[pl.BlockSpec((NS_ROWS, D), lambda t: (NP_TILES, PX)),
                  pl.BlockSpec((NS_ROWS, 2 * cdim), lambda t: (NP_TILES, PBC_1024)),
                  pl.BlockSpec((NS_ROWS, LANE), lambda t: (NP_TILES, 0)),
                  full((3, DEC_BATCH, CONV_DIM)),
                  full((4, CONV_DIM)), full((1, CONV_DIM)), full((8, LANE)), full((1, D)),
                  full((LANE, D)), full((cdim, LANE))],
        out_specs=[step(D), step(D), step(D), step(cdim), step(cdim),
                   full((DEC_BATCH, LANE)),
                   pl.BlockSpec((1, DEC_BATCH, CONV_DIM), lambda t: (jnp.maximum(t - 1, 0), 0, 0))],
        out_shape=[jax.ShapeDtypeStruct((NS_ROWS, D), F32),
                   jax.ShapeDtypeStruct((NS_ROWS, D), F32),
                   jax.ShapeDtypeStruct((NS_ROWS, D), F32),
                   jax.ShapeDtypeStruct((NS_ROWS, cdim), F32),
                   jax.ShapeDtypeStruct((NS_ROWS, cdim), F32),
                   jax.ShapeDtypeStruct((DEC_BATCH, LANE), F32),
                   jax.ShapeDtypeStruct((3, DEC_BATCH, CONV_DIM), F32)],
        scratch_shapes=[pltpu.VMEM((NS_ROWS, CONV_DIM), F32)],
        compiler_params=_params("arbitrary"),
        name="ssd_sample_prep",
    )(proj, proj, dt_raw, cst_t, cw, cb, hp, dskx, ex, seg)


SEQ_PER_STEP = 4
T_PAD = 8


def _ssds_state_kernel(e3_ref, st_ref, cc_ref, bb_ref, xw_ref, yo_ref, so_ref):
    blk = pl.program_id(0)
    nt = (((1,), (1,)), ((), ()))
    tn = (((0,), (0,)), ((), ()))
    hpg = HEADS // GROUPS
    gw = hpg * HEADDIM
    for s in range(SEQ_PER_STEP):
        b = blk * SEQ_PER_STEP + s
        for g in range(GROUPS):
            h0 = st_ref[s, g]
            c_g = cc_ref[s, :, g * STATE:(g + 1) * STATE].astype(BF16)
            yo_ref[s, :, g * gw:(g + 1) * gw] = lax.dot_general(
                c_g, h0.astype(BF16), nt, preferred_element_type=F32)
            x_g = xw_ref[s, :, g * gw:(g + 1) * gw].astype(BF16)
            b_g = bb_ref[s, :, g * STATE:(g + 1) * STATE].astype(BF16)
            dh = lax.dot_general(x_g, b_g, tn, preferred_element_type=F32)
            for hh in range(hpg):
                rows = slice(hh * HEADDIM, (hh + 1) * HEADDIM)
                so_ref[s, g, rows, :] = e3_ref[b, g * hpg + hh] * h0[rows] + dh[rows]


def _ssds_state_call(e3, state, cc_b, bb_b, xw_b):
    sb = SEQ_PER_STEP
    gw = (HEADS // GROUPS) * HEADDIM
    cdim = GROUPS * STATE
    return pl.pallas_call(
        _ssds_state_kernel,
        grid=(DEC_BATCH // sb,),
        in_specs=[pl.BlockSpec(memory_space=pltpu.SMEM),
                  pl.BlockSpec((sb, GROUPS, gw, STATE), lambda i: (i, 0, 0, 0)),
                  pl.BlockSpec((sb, T_PAD, cdim), lambda i: (i, 0, 0)),
                  pl.BlockSpec((sb, T_PAD, cdim), lambda i: (i, 0, 0)),
                  pl.BlockSpec((sb, T_PAD, D), lambda i: (i, 0, 0))],
        out_specs=[pl.BlockSpec((sb, T_PAD, D), lambda i: (i, 0, 0)),
                   pl.BlockSpec((sb, GROUPS, gw, STATE), lambda i: (i, 0, 0, 0))],
        out_shape=[jax.ShapeDtypeStruct((DEC_BATCH, T_PAD, D), F32),
                   jax.ShapeDtypeStruct((DEC_BATCH, GROUPS, gw, STATE), F32)],
        compiler_params=_params("arbitrary"),
        name="ssd_sample_state",
    )(e3, state, cc_b, bb_b, xw_b)


def _ssds_post_kernel(ypre_ref, efull_ref, yo_ref, z_ref, nw_ref, o_ref):
    get_y = lambda cols: ypre_ref[:, cols] + efull_ref[:, cols] * yo_ref[:, cols]
    _gated_group_norm(get_y, z_ref, nw_ref, o_ref)


def _ssds_post_call(ypre, efull, yo_t, proj, nw):
    step = pl.BlockSpec((DEC_BATCH, D), lambda t: (t, 0))
    return pl.pallas_call(
        _ssds_post_kernel,
        grid=(DEC_SEQ,),
        in_specs=[step, step, step,
                  pl.BlockSpec((DEC_BATCH, D), lambda t: (NP_ROWS // DEC_BATCH + t, PZ)),
                  pl.BlockSpec((1, D), lambda t: (0, 0))],
        out_specs=step,
        out_shape=jax.ShapeDtypeStruct((NS_ROWS, D), BF16),
        compiler_params=_params("arbitrary"),
        name="ssd_sample_post",
    )(ypre, efull, yo_t, proj, nw)


def _mlp_kernel(u_ref, v_ref, lnw_ref, lnb_ref, ws_ref, bsx_ref, wsx_ref, bsx4_ref,
                y_ref, cv_ref, wm_ref):
    i = pl.program_id(0)
    T = CHUNK

    @pl.when(i == 0)
    def _():
        row = lax.broadcasted_iota(jnp.int32, (T, T), 0)
        lane = lax.broadcasted_iota(jnp.int32, (T, T), 1)
        for g in range(MLP_GROUPS):
            wm_ref[g] = jnp.where(row >= lane, ws_ref[g], 0.0).astype(BF16)

    def vnorm(rows):
        vg = jax.nn.gelu(v_ref[rows, :].astype(F32))
        xc = vg - jnp.mean(vg, axis=-1, keepdims=True)
        y = xc * lax.rsqrt(jnp.mean(xc * xc, axis=-1, keepdims=True) + EPS)
        return y * lnw_ref[...] + lnb_ref[...]

    @pl.when(i < NP_TILES)
    def _():
        for cc in range(TM // T):
            rows = slice(cc * T, (cc + 1) * T)
            vnb = vnorm(rows).astype(BF16)
            for g in range(MLP_GROUPS):
                cols = slice(g * MLP_GROUP_DIM, (g + 1) * MLP_GROUP_DIM)
                sv = _dot(wm_ref[g], vnb[:, cols]) + bsx_ref[:, cols]
                y_ref[rows, cols] = (jax.nn.gelu(u_ref[rows, cols].astype(F32)) * sv).astype(BF16)

    @pl.when(i >= NP_TILES)
    def _():
        B = DEC_BATCH
        for t in range(DEC_SEQ):
            rows = slice(t * B, (t + 1) * B)
            cv_ref[rows, :] = vnorm(rows)
        for t in range(DEC_SEQ):
            rows = slice(t * B, (t + 1) * B)
            acc = bsx4_ref[t:t + 1, :]
            for s in range(t + 1):
                acc = acc + wsx_ref[4 * t + s:4 * t + s + 1, :] * cv_ref[s * B:(s + 1) * B, :]
            y_ref[rows, :] = (jax.nn.gelu(u_ref[rows, :].astype(F32)) * acc).astype(BF16)


def _mlp_call(proj, lnw, lnb, ws, bsx, wsx, bsx4):
    full = lambda shape: pl.BlockSpec(shape, lambda i: (0,) * len(shape))
    return pl.pallas_call(
        _mlp_kernel,
        grid=(N_TILES,),
        in_specs=[pl.BlockSpec((TM, D), lambda i: (i, PU)),
                  pl.BlockSpec((TM, D), lambda i: (i, PV)),
                  full((1, D)), full((1, D)),
                  full((MLP_GROUPS, CHUNK, CHUNK)),
                  full((CHUNK, D)), full((16, D)), full((8, D))],
        out_specs=[pl.BlockSpec((TM, D), lambda i: (i, 0)),
                   full((NS_ROWS, D))],
        out_shape=[jax.ShapeDtypeStruct((M_ROWS, D), BF16),
                   jax.ShapeDtypeStruct((NS_ROWS, D), F32)],
        scratch_shapes=[pltpu.VMEM((MLP_GROUPS, CHUNK, CHUNK), BF16)],
        compiler_params=_params("arbitrary"),
        name="gmlp",
    )(proj, proj, lnw, lnb, ws, bsx, wsx, bsx4)


def _cast_rows(src_ref, dst_ref, chunk=256):
    def body(r, carry):
        rows = pl.ds(pl.multiple_of(r * chunk, chunk), chunk)
        dst_ref[rows, :] = src_ref[rows, :].astype(dst_ref.dtype)
        return carry
    lax.fori_loop(0, src_ref.shape[0] // chunk, body, 0)


def _merge_kernel(ysp_ref, yss_ref, ym_ref, ga_ref, gb_ref, w1_ref, w2_ref, o_ref, w1b, w2b):
    i = pl.program_id(1)

    @pl.when(i == 0)
    def _():
        _cast_rows(w1_ref, w1b)
        _cast_rows(w2_ref, w2b)

    def emit(ys):
        a1 = _dot(ys, w1b[...])
        a2 = _dot(ym_ref[...], w2b[...])
        o_ref[...] = (jax.nn.sigmoid(ga_ref[...].astype(F32)) * a1
                      + jax.nn.sigmoid(gb_ref[...].astype(F32)) * a2).astype(BF16)

    @pl.when(i < NP_TILES)
    def _():
        emit(ysp_ref[...])

    @pl.when(i >= NP_TILES)
    def _():
        emit(yss_ref[...])


def _merge_call(ysp, yss, ym, proj, w1, w2):
    tn = 512
    nb = D // tn
    return pl.pallas_call(
        _merge_kernel,
        grid=(nb, N_TILES),
        in_specs=[pl.BlockSpec((TM, D), lambda j, i: (jnp.minimum(i, NP_TILES - 1), 0)),
                  pl.BlockSpec((TM, D), lambda j, i: (0, 0)),
                  pl.BlockSpec((TM, D), lambda j, i: (i, 0)),
                  pl.BlockSpec((TM, tn), lambda j, i: (i, PGA * nb + j)),
                  pl.BlockSpec((TM, tn), lambda j, i: (i, PGB * nb + j)),
                  pl.BlockSpec((D, tn), lambda j, i: (0, j)),
                  pl.BlockSpec((D, tn), lambda j, i: (0, j))],
        out_specs=pl.BlockSpec((TM, tn), lambda j, i: (i, j)),
        out_shape=jax.ShapeDtypeStruct((M_ROWS, D), BF16),
        scratch_shapes=[pltpu.VMEM((D, tn), BF16), pltpu.VMEM((D, tn), BF16)],
        compiler_params=_params("arbitrary", "arbitrary"),
        name="branch_merge",
    )(ysp, yss, ym, proj, proj, w1, w2)


def _resid_kernel(a_ref, w_ref, rp_ref, rs_ref, gp_ref, gs_ref, o_ref, wb):
    i = pl.program_id(1)

    @pl.when(i == 0)
    def _():
        _cast_rows(w_ref, wb)

    acc = _dot(a_ref[...], wb[...])

    @pl.when(i < NP_TILES)
    def _():
        o_ref[...] = rp_ref[...] + gp_ref[0] * acc

    @pl.when(i >= NP_TILES)
    def _():
        tn = acc.shape[-1]
        acc3 = acc.reshape(DEC_SEQ, DEC_BATCH, tn) * gs_ref[...][None]
        o_ref[...] = rs_ref[...] + acc3.reshape(TM, tn)


def _resid_call(a, w, rp, rp_spec_fn, rs, rs_spec_fn, mod_p, mod_s, k_gate, tn, name):
    kdim = a.shape[1]
    nb = D // tn
    return pl.pallas_call(
        _resid_kernel,
        grid=(nb, N_TILES),
        in_specs=[pl.BlockSpec((TM, kdim), lambda j, i: (i, 0)),
                  pl.BlockSpec((kdim, tn), lambda j, i: (0, j)),
                  rp_spec_fn(tn), rs_spec_fn(tn),
                  pl.BlockSpec((1, 1, tn), lambda j, i: (_seq_of_tile(i), 0, k_gate * nb + j)),
                  pl.BlockSpec((DEC_BATCH, tn), lambda j, i: (0, k_gate * nb + j))],
        out_specs=pl.BlockSpec((TM, tn), lambda j, i: (i, j)),
        out_shape=jax.ShapeDtypeStruct((M_ROWS, D), F32),
        scratch_shapes=[pltpu.VMEM((kdim, tn), BF16)],
        compiler_params=_params("arbitrary", "arbitrary"),
        name=name,
    )(a, w, rp, rs, mod_p, mod_s)


TN_FF = 512
N_FF_BLOCKS = D_FF // TN_FF


def _up_kernel(a_ref, wa_ref, wv_ref, cw_ref, cb_ref, fst_ref,
               h_ref, fcp_ref, fcs_ref,
               wab, wvb, acc_a0, acc_v0, acc_a1, acc_v1, carry_ref, hs_ref):
    i = pl.program_id(1)
    accs = ((acc_a0, acc_v0), (acc_a1, acc_v1))
    rc = 64

    nw = 256

    def matmul_pieces(slot):
        acc_a, acc_v = accs[slot]
        x = a_ref[...]
        for c0 in range(0, TN_FF, nw):
            yield lambda c0=c0: acc_a.__setitem__(
                (slice(8, 8 + TM), slice(c0, c0 + nw)), _dot(x, wab[:, c0:c0 + nw]))
        for c0 in range(0, TN_FF, nw):
            yield lambda c0=c0: acc_v.__setitem__(
                (slice(None), slice(c0, c0 + nw)), _dot(x, wvb[:, c0:c0 + nw]))

    def matmuls(slot):
        for piece in matmul_pieces(slot):
            piece()

    def epilogue_rows(slot, cw, cb, r0, r1):
        acc_a, acc_v = accs[slot]
        for r in range(r0, r1, rc):
            conv = (cb + cw[2:3] * acc_a[8 + r:8 + r + rc, :]
                    + cw[1:2] * acc_a[7 + r:7 + r + rc, :]
                    + cw[0:1] * acc_a[6 + r:6 + r + rc, :])
            hs_ref[r:r + rc, :] = (jax.nn.gelu(conv) * acc_v[r:r + rc, :]).astype(BF16)

    def fused_step(slot):
        prev = 1 - slot
        acc_a_prev, _ = accs[prev]
        cw, cb = cw_ref[...], cb_ref[...]
        acc_a_prev[0:8, :] = carry_ref[...]
        pieces = list(matmul_pieces(slot))
        rows_per_piece = TM // len(pieces)
        for p, piece in enumerate(pieces):
            piece()
            epilogue_rows(prev, cw, cb, p * rows_per_piece, (p + 1) * rows_per_piece)
        starts_sequence = i % TILES_PER_SEQ == 0
        carry_ref[...] = jnp.where(starts_sequence, 0.0, acc_a_prev[TM:TM + 8, :])
        h_ref[...] = hs_ref[...]
        fcp_ref[0] = acc_a_prev[TM + 6:TM + 8, :]

    def sample_epilogue(slot):
        acc_a, acc_v = accs[slot]
        B = DEC_BATCH
        cw = cw_ref[...]
        cb = cb_ref[...]

        def pre(t, r):
            if t < 0:
                return fst_ref[t + 2, r:r + rc, :]
            return acc_a[8 + t * B + r:8 + t * B + r + rc, :]

        for t in range(DEC_SEQ):
            for r in range(0, B, rc):
                conv = cb + cw[2:3] * pre(t, r) + cw[1:2] * pre(t - 1, r) + cw[0:1] * pre(t - 2, r)
                h_ref[t * B + r:t * B + r + rc, :] = (
                    jax.nn.gelu(conv) * acc_v[t * B + r:t * B + r + rc, :]).astype(BF16)
        fcs_ref[0] = acc_a[8 + 2 * B:8 + 3 * B, :]
        fcs_ref[1] = acc_a[8 + 3 * B:8 + 4 * B, :]

    @pl.when(i == 0)
    def _():
        _cast_rows(wa_ref, wab)
        _cast_rows(wv_ref, wvb)
        carry_ref[...] = jnp.zeros(carry_ref.shape, F32)
        matmuls(0)

    @pl.when(jnp.logical_and(i % 2 == 1, i < N_TILES))
    def _():
        fused_step(1)

    @pl.when(jnp.logical_and(i % 2 == 0, i > 0))
    def _():
        fused_step(0)

    @pl.when(i == N_TILES)
    def _():
        sample_epilogue((N_TILES - 1) % 2)


def _up_call(n2, w_up, cw, cb, fst_t):
    prev = lambda i: jnp.maximum(i - 1, 0)
    return pl.pallas_call(
        _up_kernel,
        grid=(N_FF_BLOCKS, N_TILES + 1),
        in_specs=[pl.BlockSpec((TM, D), lambda j, i: (jnp.minimum(i, N_TILES - 1), 0)),
                  pl.BlockSpec((D, TN_FF), lambda j, i: (0, j)),
                  pl.BlockSpec((D, TN_FF), lambda j, i: (0, N_FF_BLOCKS + j)),
                  pl.BlockSpec((3, TN_FF), lambda j, i: (0, j)),
                  pl.BlockSpec((1, TN_FF), lambda j, i: (0, j)),
                  pl.BlockSpec((2, DEC_BATCH, TN_FF), lambda j, i: (0, 0, j))],
        out_specs=[pl.BlockSpec((TM, TN_FF), lambda j, i: (prev(i), j)),
                   pl.BlockSpec((1, 2, TN_FF), lambda j, i: (_seq_of_tile(prev(i)), 0, j)),
                   pl.BlockSpec((2, DEC_BATCH, TN_FF), lambda j, i: (0, 0, j))],
        out_shape=[jax.ShapeDtypeStruct((M_ROWS, D_FF), BF16),
                   jax.ShapeDtypeStruct((BATCH, 2, D_FF), F32),
                   jax.ShapeDtypeStruct((2, DEC_BATCH, D_FF), F32)],
        scratch_shapes=[pltpu.VMEM((D, TN_FF), BF16), pltpu.VMEM((D, TN_FF), BF16),
                        pltpu.VMEM((TM + 8, TN_FF), F32), pltpu.VMEM((TM, TN_FF), F32),
                        pltpu.VMEM((TM + 8, TN_FF), F32), pltpu.VMEM((TM, TN_FF), F32),
                        pltpu.VMEM((8, TN_FF), F32), pltpu.VMEM((TM, TN_FF), BF16)],
        compiler_params=_params("arbitrary", "arbitrary"),
        name="ffn_up",
    )(n2, w_up, w_up, cw, cb, fst_t)


def _final_kernel(x_ref, w_ref, yp_ref, ys_ref):
    i = pl.program_id(0)
    y = _rms(x_ref[...]) * w_ref[...]

    @pl.when(i < NP_TILES)
    def _():
        yp_ref[...] = y

    @pl.when(i >= NP_TILES)
    def _():
        ys_ref[...] = y


def _final_call(x3, w):
    return pl.pallas_call(
        _final_kernel,
        grid=(N_TILES,),
        in_specs=[pl.BlockSpec((TM, D), lambda i: (i, 0)),
                  pl.BlockSpec((1, D), lambda i: (0, 0))],
        out_specs=[pl.BlockSpec((TM, D), lambda i: (jnp.minimum(i, NP_TILES - 1), 0)),
                   pl.BlockSpec((TM, D), lambda i: (0, 0))],
        out_shape=[jax.ShapeDtypeStruct((NP_ROWS, D), F32),
                   jax.ShapeDtypeStruct((NS_ROWS, D), F32)],
        compiler_params=_params("arbitrary"),
        name="final_norm",
    )(x3, w)


def _to_time_major(a):
    return jnp.transpose(a, (1, 0, 2))


def _pad_time(a_t, width):
    a_b = jnp.transpose(a_t.reshape(DEC_SEQ, DEC_BATCH, width), (1, 0, 2))
    return jnp.pad(a_b, ((0, 0), (0, T_PAD - DEC_SEQ), (0, 0)))


def kernel(x_prompt, x_sample, state_ssm, state_ssd_conv, state_ffn_conv, c_prompt, c_sample,
           norm1_w, w_ada, b_ada, w_in, ssd_conv_w, ssd_conv_b, dt_bias, a_log, d_skip,
           ssd_norm_w, mlp_ln_w, mlp_ln_b, w_spatial, b_spatial, w_ssd_o, w_mlp_o, w_out,
           norm2_w, w_up, ffn_conv_w, ffn_conv_b, w_down, final_norm_w):
    assert w_in.shape[0] == 1, "single-layer trunk"
    row = lambda v: v.reshape(1, -1)

    xp = x_prompt.reshape(NP_ROWS, D)
    xs = _to_time_major(x_sample).reshape(NS_ROWS, D)

    c_all = jnp.concatenate([c_prompt, c_sample, jnp.zeros((4, D), F32)], axis=0)
    mod = _ada_call(c_all, w_ada[0], row(b_ada[0]))
    mod_p = mod[:BATCH].reshape(BATCH, 1, 6 * D)
    mod_s = mod[BATCH:BATCH + DEC_BATCH]
    K_SHIFT1, K_SCALE1, K_GATE1, K_SHIFT2, K_SCALE2, K_GATE2 = range(6)

    xp_spec = pl.BlockSpec((TM, D), lambda i: (jnp.minimum(i, NP_TILES - 1), 0))
    xs_spec = pl.BlockSpec((TM, D), lambda i: (0, 0))
    w_in_t = w_in[0].T
    w_dt_t = w_in_t[DT_COL:DT_COL + LANE]
    n1, dt_raw = _norm_call(xp, xp_spec, xs, xs_spec, row(norm1_w[0]), mod_p, mod_s,
                            K_SCALE1, K_SHIFT1, w_dt_t)
    proj = _inproj_call(n1, w_in_t)

    hp = jnp.zeros((8, LANE), F32)
    hp = hp.at[0, :HEADS].set(dt_bias[0]).at[1, :HEADS].set(a_log[0]).at[2, :HEADS].set(d_skip[0])
    tri = jnp.tril(jnp.ones((CHUNK, CHUNK), F32)).astype(BF16)
    cw, cb = ssd_conv_w[0], row(ssd_conv_b[0])
    nw = row(ssd_norm_w[0])

    y_ssd_p, ssm_p, cst_p = _ssd_prompt_call(proj, dt_raw, cw, cb, hp, nw, tri)

    head_of_col = jnp.arange(D) // HEADDIM
    ex = (jnp.arange(LANE)[:, None] == head_of_col[None, :]).astype(BF16)
    grp_of_head = jnp.arange(LANE) // (HEADS // GROUPS)
    seg = ((jnp.arange(GROUPS * STATE)[:, None] // STATE == grp_of_head[None, :])
           & (jnp.arange(LANE)[None, :] < HEADS)).astype(BF16)
    dskx = row(jnp.repeat(d_skip[0], HEADDIM))
    cst_t = _to_time_major(state_ssd_conv[0])
    ypre, efull, xw_t, bb_t, cc_t, e3, ncs_t = _ssds_prep_call(
        proj, dt_raw, cst_t, cw, cb, hp, dskx, ex, seg)
    cdim = GROUPS * STATE
    gw = (HEADS // GROUPS) * HEADDIM
    yo_b, ssm_s = _ssds_state_call(
        e3[:, :HEADS], state_ssm[0].reshape(DEC_BATCH, GROUPS, gw, STATE),
        _pad_time(cc_t, cdim), _pad_time(bb_t, cdim), _pad_time(xw_t, D))
    yo_t = jnp.transpose(yo_b[:, :DEC_SEQ], (1, 0, 2)).reshape(NS_ROWS, D)
    y_ssd_s = _ssds_post_call(ypre, efull, yo_t, proj, nw)

    gidx = jnp.arange(D) // MLP_GROUP_DIM
    bsx = b_spatial[0][:, :CHUNK].T[:, gidx]
    ws4 = w_spatial[0][:, :DEC_SEQ, :DEC_SEQ]
    wsx = jnp.transpose(ws4, (1, 2, 0)).reshape(DEC_SEQ * DEC_SEQ, MLP_GROUPS)[:, gidx]
    bsx4 = jnp.pad(b_spatial[0][:, :DEC_SEQ].T[:, gidx], ((0, 8 - DEC_SEQ), (0, 0)))
    y_mlp, cv_t = _mlp_call(proj, row(mlp_ln_w[0]), row(mlp_ln_b[0]), w_spatial[0], bsx, wsx, bsx4)

    mixed = _merge_call(y_ssd_p, y_ssd_s, y_mlp, proj, w_ssd_o[0], w_mlp_o[0])
    x2 = _resid_call(
        mixed, w_out[0], xp,
        lambda tn: pl.BlockSpec((TM, tn), lambda j, i: (jnp.minimum(i, NP_TILES - 1), j)),
        xs, lambda tn: pl.BlockSpec((TM, tn), lambda j, i: (0, j)),
        mod_p, mod_s, K_GATE1, 1024, "out_proj")

    x2p_spec = pl.BlockSpec((TM, D), lambda i: (jnp.minimum(i, NP_TILES - 1), 0))
    x2s_spec = pl.BlockSpec((TM, D), lambda i: (NP_TILES, 0))
    (n2,) = _norm_call(x2, x2p_spec, x2, x2s_spec, row(norm2_w[0]), mod_p, mod_s,
                       K_SCALE2, K_SHIFT2, None)
    fst_t = _to_time_major(state_ffn_conv[0])
    h, ffn_p, ffn_s_t = _up_call(n2, w_up[0], ffn_conv_w[0], row(ffn_conv_b[0]), fst_t)
    x3 = _resid_call(
        h, w_down[0], x2,
        lambda tn: pl.BlockSpec((TM, tn), lambda j, i: (jnp.minimum(i, NP_TILES - 1), j)),
        x2, lambda tn: pl.BlockSpec((TM, tn), lambda j, i: (NP_TILES, j)),
        mod_p, mod_s, K_GATE2, 512, "ffn_down")

    y_p, y_s_t = _final_call(x3, row(final_norm_w))

    from_t = lambda a, t, c: jnp.transpose(a.reshape(t, DEC_BATCH, c), (1, 0, 2))
    return (y_p.reshape(BATCH, SEQ, D),
            from_t(y_s_t, DEC_SEQ, D),
            ssm_p.reshape(1, BATCH, HEADS, HEADDIM, STATE),
            ssm_s.reshape(1, DEC_BATCH, HEADS, HEADDIM, STATE),
            cst_p[None],
            from_t(ncs_t, 3, CONV_DIM)[None],
            ffn_p[None],
            from_t(ffn_s_t, 2, D_FF)[None],
            from_t(cv_t, DEC_SEQ, D)[None])
```

```python
import functools

import jax
import jax.numpy as jnp
from jax import lax
from jax.experimental import pallas as pl
from jax.experimental.pallas import tpu as pltpu

F32 = jnp.float32
BF16 = jnp.bfloat16

D = 2048
BATCH, SEQ = 4, 2048
DEC_BATCH, DEC_SEQ = 128, 4
NP_ROWS = BATCH * SEQ
NS_ROWS = DEC_BATCH * DEC_SEQ
M_ROWS = NP_ROWS + NS_ROWS
TM = 512
NP_TILES = NP_ROWS // TM
N_TILES = M_ROWS // TM
TILES_PER_SEQ = SEQ // TM
TMM = 1024
NP_MT = NP_ROWS // TMM
N_MT = NP_MT + 1
MT_PER_SEQ = SEQ // TMM
HEADS, HEADDIM, GROUPS, STATE = 32, 64, 4, 128
CHUNK = 128
N_CHUNKS = SEQ // CHUNK
CONV_DIM = D + 2 * GROUPS * STATE
MLP_GROUPS = 8
MLP_GROUP_DIM = D // MLP_GROUPS
D_FF = 5632
EPS = 1e-6
DT_COL = D + CONV_DIM
TN_IN = 1024
LANE = 128
VMEM_LIMIT = 56 * 1024 * 1024


def _params(*sem, flags=None):
    return pltpu.CompilerParams(dimension_semantics=sem, vmem_limit_bytes=VMEM_LIMIT, flags=flags)


def _dot(a, b):
    return jnp.dot(a, b, preferred_element_type=F32)


def _split_bf16(v, terms):
    out = []
    r = v
    for _ in range(terms):
        p = r.astype(BF16)
        out.append(p)
        r = r - p.astype(F32)
    return out


def _dot_exact_rhs(v, e, terms=3):
    acc = None
    for p in _split_bf16(v, terms):
        d = _dot(p, e)
        acc = d if acc is None else acc + d
    return acc


def _dot_exact_lhs(t, v, terms=3):
    acc = None
    for p in _split_bf16(v, terms):
        d = _dot(t, p)
        acc = d if acc is None else acc + d
    return acc


def _silu(x):
    return x * jax.nn.sigmoid(x)


def _softplus(x):
    return jnp.maximum(x, 0.0) + jnp.log1p(jnp.exp(-jnp.abs(x)))


def _rms(x):
    return x * lax.rsqrt(jnp.mean(x * x, axis=-1, keepdims=True) + EPS)


def _ada_kernel(c_ref, w_ref, b_ref, o_ref):
    a = _silu(c_ref[...]).astype(BF16)
    o_ref[...] = _dot(a, w_ref[...].astype(BF16)) + b_ref[...]


def _ada_call(c_all, w, b):
    rows = c_all.shape[0]
    tn = 1024
    return pl.pallas_call(
        _ada_kernel,
        grid=(6 * D // tn,),
        in_specs=[pl.BlockSpec((rows, D), lambda j: (0, 0)),
                  pl.BlockSpec((D, tn), lambda j: (0, j)),
                  pl.BlockSpec((1, tn), lambda j: (0, j))],
        out_specs=pl.BlockSpec((rows, tn), lambda j: (0, j)),
        out_shape=jax.ShapeDtypeStruct((rows, 6 * D), F32),
        compiler_params=_params("arbitrary"),
        name="ada_mod",
    )(c_all, w, b)


def _norm_kernel(with_dt, xp_ref, xs_ref, nw_ref, scp_ref, shp_ref, scs_ref, shs_ref, *rest):
    if with_dt:
        wdt_ref, n_ref, dt_ref = rest
    else:
        (n_ref,) = rest
    i = pl.program_id(0)

    def emit(n):
        nb = n.astype(BF16)
        n_ref[...] = nb
        if with_dt:
            dt_ref[...] = lax.dot_general(nb, wdt_ref[...].astype(BF16),
                                          (((1,), (1,)), ((), ())),
                                          preferred_element_type=F32)

    @pl.when(i < NP_TILES)
    def _():
        y = _rms(xp_ref[...]) * nw_ref[...]
        emit(y * (1.0 + scp_ref[0]) + shp_ref[0])

    @pl.when(i >= NP_TILES)
    def _():
        y = _rms(xs_ref[...]) * nw_ref[...]
        y3 = y.reshape(DEC_SEQ, DEC_BATCH, D)
        emit((y3 * (1.0 + scs_ref[...])[None] + shs_ref[...][None]).reshape(TM, D))


def _seq_of_tile(i):
    return jnp.minimum(i // TILES_PER_SEQ, BATCH - 1)


def _seq_of_mtile(i):
    return jnp.minimum(i // MT_PER_SEQ, BATCH - 1)


def _norm_call(xp, xp_spec, xs, xs_spec, nw, mod_p, mod_s, k_scale, k_shift, w_dt):
    with_dt = w_dt is not None
    in_specs = [
        xp_spec, xs_spec,
        pl.BlockSpec((1, D), lambda i: (0, 0)),
        pl.BlockSpec((1, 1, D), lambda i: (_seq_of_tile(i), 0, k_scale)),
        pl.BlockSpec((1, 1, D), lambda i: (_seq_of_tile(i), 0, k_shift)),
        pl.BlockSpec((DEC_BATCH, D), lambda i: (0, k_scale)),
        pl.BlockSpec((DEC_BATCH, D), lambda i: (0, k_shift)),
    ]
    args = [xp, xs, nw, mod_p, mod_p, mod_s, mod_s]
    out_specs = [pl.BlockSpec((TM, D), lambda i: (i, 0))]
    out_shape = [jax.ShapeDtypeStruct((M_ROWS, D), BF16)]
    if with_dt:
        in_specs.append(pl.BlockSpec((LANE, D), lambda i: (0, 0)))
        args.append(w_dt)
        out_specs.append(pl.BlockSpec((TM, LANE), lambda i: (i, 0)))
        out_shape.append(jax.ShapeDtypeStruct((M_ROWS, LANE), F32))
    return pl.pallas_call(
        functools.partial(_norm_kernel, with_dt),
        grid=(N_TILES,),
        in_specs=in_specs,
        out_specs=out_specs,
        out_shape=out_shape,
        compiler_params=_params("arbitrary"),
        name="norm_mod_dt" if with_dt else "norm_mod",
    )(*args)


N_IN_BLOCKS = 13
UVG_ROW = DT_COL + HEADS
BC_ROW = 2 * D


def _in_src_row(j):
    row = jnp.where(j < 4, j * TN_IN, jnp.where(j < 12, UVG_ROW + (j - 4) * TN_IN, BC_ROW))
    return pl.multiple_of(row, HEADS)


def _inproj_kernel(a_ref, wt_ref, o_ref, wbf_ref):
    i = pl.program_id(1)

    @pl.when(i == 0)
    def _():
        for r in range(TN_IN // LANE):
            rows = slice(r * LANE, (r + 1) * LANE)
            wbf_ref[:, rows] = wt_ref[rows, :].T.astype(BF16)

    @pl.when(i < NP_MT)
    def _():
        o_ref[...] = _dot(a_ref[...], wbf_ref[...]).astype(o_ref.dtype)

    @pl.when(i == NP_MT)
    def _():
        o_ref[0:NS_ROWS, :] = _dot(a_ref[0:NS_ROWS, :], wbf_ref[...]).astype(o_ref.dtype)


def _inproj_call(n1, w_in_t):
    return pl.pallas_call(
        _inproj_kernel,
        grid=(N_IN_BLOCKS, N_MT),
        in_specs=[pl.BlockSpec((TMM, D), lambda j, i: (i, 0)),
                  pl.BlockSpec((pl.Element(TN_IN), pl.Element(D)),
                               lambda j, i: (_in_src_row(j), 0))],
        out_specs=pl.BlockSpec((TMM, TN_IN), lambda j, i: (i, j)),
        out_shape=jax.ShapeDtypeStruct((M_ROWS, N_IN_BLOCKS * TN_IN), BF16),
        scratch_shapes=[pltpu.VMEM((D, TN_IN), BF16)],
        compiler_params=_params("arbitrary", "arbitrary"),
        name="in_proj",
    )(n1, w_in_t)


PZ, PX, PU, PV, PGA, PGB = 0, 1, 2, 3, 4, 5
PBC_1024 = 12


def _gated_group_norm(get_y, z_ref, nw_ref, o_ref):
    gw = D // GROUPS
    for g in range(GROUPS):
        cols = slice(g * gw, (g + 1) * gw)
        zv = z_ref[:, cols].astype(F32)
        gg = _rms(get_y(cols) * _silu(zv))
        o_ref[:, cols] = (gg * nw_ref[:, cols]).astype(o_ref.dtype)


def _ssd_prompt_kernel(z_ref, x_ref, bc_ref, dt_ref, cw_ref, cb_ref, hp_ref, nw_ref, tri_ref,
                       y_ref, ssm_ref, cst_ref,
                       ext_ref, act_ref, st_ref, yscr_ref):
    c = pl.program_id(1)
    T = CHUNK

    @pl.when(c == 0)
    def _():
        ext_ref[0:8, :] = jnp.zeros((8, CONV_DIM), F32)
        st_ref[...] = jnp.zeros(st_ref.shape, F32)

    ext_ref[8:8 + T, 0:D] = x_ref[...].astype(F32)
    ext_ref[8:8 + T, D:CONV_DIM] = bc_ref[...].astype(F32)

    cwid = 512
    for q in range(CONV_DIM // cwid):
        cols = slice(q * cwid, (q + 1) * cwid)
        cw = cw_ref[:, cols]
        acc = cb_ref[:, cols] + cw[3:4] * ext_ref[8:8 + T, cols]
        acc = acc + cw[0:1] * ext_ref[5:5 + T, cols]
        acc = acc + cw[1:2] * ext_ref[6:6 + T, cols]
        acc = acc + cw[2:3] * ext_ref[7:7 + T, cols]
        act_ref[:, cols] = _silu(acc)

    @pl.when(c == N_CHUNKS - 1)
    def _():
        cst_ref[0] = ext_ref[5 + T:8 + T, :]

    ext_ref[0:8, :] = ext_ref[T:T + 8, :]

    hp = hp_ref[...]
    dt = _softplus(dt_ref[...] + hp[0:1, :])
    adt = dt * (-jnp.exp(hp[1:2, :]))
    cs = _dot_exact_lhs(tri_ref[...], adt)
    cs_t = cs.T
    dt_t = dt.T

    row = lax.broadcasted_iota(jnp.int32, (T, T), 0)
    lane = lax.broadcasted_iota(jnp.int32, (T, T), 1)
    causal = row >= lane
    left = lane < HEADDIM
    nt = (((1,), (1,)), ((), ()))

    for g in range(GROUPS):
        c_g = act_ref[:, D + GROUPS * STATE + g * STATE:D + GROUPS * STATE + (g + 1) * STATE]
        b_g = act_ref[:, D + g * STATE:D + (g + 1) * STATE]
        c_b = c_g.astype(BF16)
        cb = lax.dot_general(c_b, b_g.astype(BF16), nt, preferred_element_type=F32)
        b_t = b_g.T
        for k4 in range(HEADS // GROUPS // 2):
            k = g * (HEADS // GROUPS // 2) + k4
            xp = act_ref[:, k * LANE:(k + 1) * LANE]
            xpb = xp.astype(BF16)
            st = st_ref[k]
            yraw = _dot(c_b, st.astype(BF16))
            halves_y, halves_s = [], []
            for e in range(2):
                h = 2 * k + e
                colb = cs[:, h:h + 1]
                rowb = cs_t[h:h + 1, :]
                dtrow = dt_t[h:h + 1, :]
                decay = jnp.exp(jnp.where(causal, colb - rowb, -jnp.inf))
                m_h = (cb * decay * dtrow).astype(BF16)
                alast = cs_t[h:h + 1, T - 1:T]
                wrow = jnp.exp(alast - rowb) * dtrow
                dskip = hp[2:3, h:h + 1]
                halves_y.append(_dot(m_h, xpb) + jnp.exp(colb) * yraw + dskip * xp)
                halves_s.append(jnp.exp(alast) * st + _dot((b_t * wrow).astype(BF16), xpb))
            yscr_ref[:, k * LANE:(k + 1) * LANE] = jnp.where(left, halves_y[0], halves_y[1])
            st_ref[k] = jnp.where(left, halves_s[0], halves_s[1])

    _gated_group_norm(lambda cols: yscr_ref[:, cols], z_ref, nw_ref, y_ref)

    @pl.when(c == N_CHUNKS - 1)
    def _():
        for k in range(HEADS // 2):
            ssm_ref[0, k] = st_ref[k].T


def _ssd_prompt_call(proj, dt_raw, cw, cb, hp, nw, tri):
    row = lambda b, c: b * N_CHUNKS + c
    const = lambda b, c: (0, 0)
    return pl.pallas_call(
        _ssd_prompt_kernel,
        grid=(BATCH, N_CHUNKS),
        in_specs=[pl.BlockSpec((CHUNK, D), lambda b, c: (row(b, c), PZ)),
                  pl.BlockSpec((CHUNK, D), lambda b, c: (row(b, c), PX)),
                  pl.BlockSpec((CHUNK, 2 * GROUPS * STATE), lambda b, c: (row(b, c), PBC_1024)),
                  pl.BlockSpec((CHUNK, LANE), lambda b, c: (row(b, c), 0)),
                  pl.BlockSpec((4, CONV_DIM), const),
                  pl.BlockSpec((1, CONV_DIM), const),
                  pl.BlockSpec((8, LANE), const),
                  pl.BlockSpec((1, D), const),
                  pl.BlockSpec((CHUNK, CHUNK), const)],
        out_specs=[pl.BlockSpec((CHUNK, D), lambda b, c: (row(b, c), 0)),
                   pl.BlockSpec((1, HEADS // 2, LANE, STATE), lambda b, c: (b, 0, 0, 0)),
                   pl.BlockSpec((1, 3, CONV_DIM), lambda b, c: (b, 0, 0))],
        out_shape=[jax.ShapeDtypeStruct((NP_ROWS, D), BF16),
                   jax.ShapeDtypeStruct((BATCH, HEADS // 2, LANE, STATE), F32),
                   jax.ShapeDtypeStruct((BATCH, 3, CONV_DIM), F32)],
        scratch_shapes=[pltpu.VMEM((CHUNK + 8, CONV_DIM), F32),
                        pltpu.VMEM((CHUNK, CONV_DIM), F32),
                        pltpu.VMEM((HEADS // 2, STATE, LANE), F32),
                        pltpu.VMEM((CHUNK, D), F32)],
        compiler_params=_params("arbitrary", "arbitrary"),
        name="ssd_prompt",
    )(proj, proj, proj, dt_raw, cw, cb, hp, nw, tri)


def _ssds_prep_step(tt, x_ref, bc_ref, dt_ref, cst_ref, cw_ref, cb_ref, hp_ref, dskx_ref,
                    ex_ref, seg_ref,
                    ypre_ref, efull_ref, xw_ref, bb_ref, cc_ref, e3_ref, ncs_ref, act_ref):
    B = DEC_BATCH
    cwid = 512
    cdim = GROUPS * STATE
    blk = lambda t: slice(t * B, (t + 1) * B)

    def tap(idx, q):
        if idx < 3:
            return cst_ref[idx, :, q * cwid:(q + 1) * cwid]
        if q < D // cwid:
            return x_ref[blk(idx - 3), q * cwid:(q + 1) * cwid].astype(F32)
        return bc_ref[blk(idx - 3), q * cwid - D:(q + 1) * cwid - D].astype(F32)

    for q in range(CONV_DIM // cwid):
        cols = slice(q * cwid, (q + 1) * cwid)
        cw = cw_ref[:, cols]
        newest = tap(tt + 3, q)
        acc = cb_ref[:, cols] + cw[3:4] * newest
        for kk in range(3):
            acc = acc + cw[kk:kk + 1] * tap(tt + kk, q)
        act_ref[blk(tt), cols] = _silu(acc)
        if tt >= 1:
            ncs_ref[0, :, cols] = newest

    bb_ref[...] = act_ref[blk(tt), D:D + cdim]
    cc_ref[...] = act_ref[blk(tt), D + cdim:CONV_DIM]

    hp = hp_ref[...]
    a_neg = -jnp.exp(hp[1:2, :])
    dts, css = [], []
    run = None
    for t in range(DEC_SEQ):
        dt = _softplus(dt_ref[blk(t), :] + hp[0:1, :])
        run = dt * a_neg if run is None else run + dt * a_neg
        dts.append(dt)
        css.append(run)

    ex = ex_ref[...]
    e3_ref[...] = jnp.exp(css[-1])
    efull_ref[...] = _dot_exact_rhs(jnp.exp(css[tt]), ex)
    w_t = jnp.exp(css[-1] - css[tt]) * dts[tt]
    xw_ref[...] = act_ref[blk(tt), 0:D] * _dot_exact_rhs(w_t, ex)

    seg = seg_ref[...]
    acc = dskx_ref[...] * act_ref[blk(tt), 0:D]
    c_t = act_ref[blk(tt), D + cdim:CONV_DIM]
    for s in range(tt + 1):
        b_s = act_ref[blk(s), D:D + cdim]
        cbh = _dot_exact_rhs(c_t * b_s, seg)
        g_ts = cbh * jnp.exp(css[tt] - css[s]) * dts[s]
        acc = acc + _dot_exact_rhs(g_ts, ex) * act_ref[blk(s), 0:D]
    ypre_ref[...] = acc


def _ssds_prep_kernel(*refs):
    t = pl.program_id(0)
    for tt in range(DEC_SEQ):
        pl.when(t == tt)(functools.partial(_ssds_prep_step, tt, *refs))


def _ssds_prep_call(proj, dt_raw, cst_t, cw, cb, hp, dskx, ex, seg):
    full = lambda shape: pl.BlockSpec(shape, lambda t: (0,) * len(shape))
    step = lambda width: pl.BlockSpec((DEC_BATCH, width), lambda t: (t, 0))
    cdim = GROUPS * STATE
    return pl.pallas_call(
        _ssds_prep_kernel,
        grid=(DEC_SEQ,),
        in_specs=[pl.BlockSpec((NS_ROWS, D), lambda t: (NP_TILES, PX)),
                  pl.BlockSpec((NS_ROWS, 2 * cdim), lambda t: (NP_TILES, PBC_1024)),
                  pl.BlockSpec((NS_ROWS, LANE), lambda t: (NP_TILES, 0)),
                  full((3, DEC_BATCH, CONV_DIM)),
                  full((4, CONV_DIM)), full((1, CONV_DIM)), full((8, LANE)), full((1, D)),
                  full((LANE, D)), full((cdim, LANE))],
        out_specs=[step(D), step(D), step(D), step(cdim), step(cdim),
                   full((DEC_BATCH, LANE)),
                   pl.BlockSpec((1, DEC_BATCH, CONV_DIM), lambda t: (jnp.maximum(t - 1, 0), 0, 0))],
        out_shape=[jax.ShapeDtypeStruct((NS_ROWS, D), F32),
                   jax.ShapeDtypeStruct((NS_ROWS, D), F32),
                   jax.ShapeDtypeStruct((NS_ROWS, D), F32),
                   jax.ShapeDtypeStruct((NS_ROWS, cdim), F32),
                   jax.ShapeDtypeStruct((NS_ROWS, cdim), F32),
                   jax.ShapeDtypeStruct((DEC_BATCH, LANE), F32),
                   jax.ShapeDtypeStruct((3, DEC_BATCH, CONV_DIM), F32)],
        scratch_shapes=[pltpu.VMEM((NS_ROWS, CONV_DIM), F32)],
        compiler_params=_params("arbitrary"),
        name="ssd_sample_prep",
    )(proj, proj, dt_raw, cst_t, cw, cb, hp, dskx, ex, seg)


SEQ_PER_STEP = 4
T_PAD = 8


def _ssds_state_kernel(e3_ref, st_ref, cc_ref, bb_ref, xw_ref, yo_ref, so_ref):
    blk = pl.program_id(0)
    nt = (((1,), (1,)), ((), ()))
    tn = (((0,), (0,)), ((), ()))
    hpg = HEADS // GROUPS
    gw = hpg * HEADDIM
    for s in range(SEQ_PER_STEP):
        b = blk * SEQ_PER_STEP + s
        for g in range(GROUPS):
            h0 = st_ref[s, g]
            c_g = cc_ref[s, :, g * STATE:(g + 1) * STATE].astype(BF16)
            yo_ref[s, :, g * gw:(g + 1) * gw] = lax.dot_general(
                c_g, h0.astype(BF16), nt, preferred_element_type=F32)
            x_g = xw_ref[s, :, g * gw:(g + 1) * gw].astype(BF16)
            b_g = bb_ref[s, :, g * STATE:(g + 1) * STATE].astype(BF16)
            dh = lax.dot_general(x_g, b_g, tn, preferred_element_type=F32)
            for hh in range(hpg):
                rows = slice(hh * HEADDIM, (hh + 1) * HEADDIM)
                so_ref[s, g, rows, :] = e3_ref[b, g * hpg + hh] * h0[rows] + dh[rows]


def _ssds_state_call(e3, state, cc_b, bb_b, xw_b):
    sb = SEQ_PER_STEP
    gw = (HEADS // GROUPS) * HEADDIM
    cdim = GROUPS * STATE
    return pl.pallas_call(
        _ssds_state_kernel,
        grid=(DEC_BATCH // sb,),
        in_specs=[pl.BlockSpec(memory_space=pltpu.SMEM),
                  pl.BlockSpec((sb, GROUPS, gw, STATE), lambda i: (i, 0, 0, 0)),
                  pl.BlockSpec((sb, T_PAD, cdim), lambda i: (i, 0, 0)),
                  pl.BlockSpec((sb, T_PAD, cdim), lambda i: (i, 0, 0)),
                  pl.BlockSpec((sb, T_PAD, D), lambda i: (i, 0, 0))],
        out_specs=[pl.BlockSpec((sb, T_PAD, D), lambda i: (i, 0, 0)),
                   pl.BlockSpec((sb, GROUPS, gw, STATE), lambda i: (i, 0, 0, 0))],
        out_shape=[jax.ShapeDtypeStruct((DEC_BATCH, T_PAD, D), F32),
                   jax.ShapeDtypeStruct((DEC_BATCH, GROUPS, gw, STATE), F32)],
        compiler_params=_params("arbitrary"),
        name="ssd_sample_state",
    )(e3, state, cc_b, bb_b, xw_b)


def _ssds_post_kernel(ypre_ref, efull_ref, yo_ref, z_ref, nw_ref, o_ref):
    get_y = lambda cols: ypre_ref[:, cols] + efull_ref[:, cols] * yo_ref[:, cols]
    _gated_group_norm(get_y, z_ref, nw_ref, o_ref)


def _ssds_post_call(ypre, efull, yo_t, proj, nw):
    step = pl.BlockSpec((DEC_BATCH, D), lambda t: (t, 0))
    return pl.pallas_call(
        _ssds_post_kernel,
        grid=(DEC_SEQ,),
        in_specs=[step, step, step,
                  pl.BlockSpec((DEC_BATCH, D), lambda t: (NP_ROWS // DEC_BATCH + t, PZ)),
                  pl.BlockSpec((1, D), lambda t: (0, 0))],
        out_specs=step,
        out_shape=jax.ShapeDtypeStruct((NS_ROWS, D), BF16),
        compiler_params=_params("arbitrary"),
        name="ssd_sample_post",
    )(ypre, efull, yo_t, proj, nw)


def _mlp_kernel(u_ref, v_ref, lnw_ref, lnb_ref, ws_ref, bsx_ref, wsx_ref, bsx4_ref,
                y_ref, cv_ref, wm_ref):
    i = pl.program_id(0)
    T = CHUNK

    @pl.when(i == 0)
    def _():
        row = lax.broadcasted_iota(jnp.int32, (T, T), 0)
        lane = lax.broadcasted_iota(jnp.int32, (T, T), 1)
        for g in range(MLP_GROUPS):
            wm_ref[g] = jnp.where(row >= lane, ws_ref[g], 0.0).astype(BF16)

    def vnorm(rows):
        vg = jax.nn.gelu(v_ref[rows, :].astype(F32))
        xc = vg - jnp.mean(vg, axis=-1, keepdims=True)
        y = xc * lax.rsqrt(jnp.mean(xc * xc, axis=-1, keepdims=True) + EPS)
        return y * lnw_ref[...] + lnb_ref[...]

    @pl.when(i < NP_TILES)
    def _():
        for cc in range(TM // T):
            rows = slice(cc * T, (cc + 1) * T)
            vnb = vnorm(rows).astype(BF16)
            for g in range(MLP_GROUPS):
                cols = slice(g * MLP_GROUP_DIM, (g + 1) * MLP_GROUP_DIM)
                sv = _dot(wm_ref[g], vnb[:, cols]) + bsx_ref[:, cols]
                y_ref[rows, cols] = (jax.nn.gelu(u_ref[rows, cols].astype(F32)) * sv).astype(BF16)

    @pl.when(i >= NP_TILES)
    def _():
        B = DEC_BATCH
        for t in range(DEC_SEQ):
            rows = slice(t * B, (t + 1) * B)
            cv_ref[rows, :] = vnorm(rows)
        for t in range(DEC_SEQ):
            rows = slice(t * B, (t + 1) * B)
            acc = bsx4_ref[t:t + 1, :]
            for s in range(t + 1):
                acc = acc + wsx_ref[4 * t + s:4 * t + s + 1, :] * cv_ref[s * B:(s + 1) * B, :]
            y_ref[rows, :] = (jax.nn.gelu(u_ref[rows, :].astype(F32)) * acc).astype(BF16)


def _mlp_call(proj, lnw, lnb, ws, bsx, wsx, bsx4):
    full = lambda shape: pl.BlockSpec(shape, lambda i: (0,) * len(shape))
    return pl.pallas_call(
        _mlp_kernel,
        grid=(N_TILES,),
        in_specs=[pl.BlockSpec((TM, D), lambda i: (i, PU)),
                  pl.BlockSpec((TM, D), lambda i: (i, PV)),
                  full((1, D)), full((1, D)),
                  full((MLP_GROUPS, CHUNK, CHUNK)),
                  full((CHUNK, D)), full((16, D)), full((8, D))],
        out_specs=[pl.BlockSpec((TM, D), lambda i: (i, 0)),
                   full((NS_ROWS, D))],
        out_shape=[jax.ShapeDtypeStruct((M_ROWS, D), BF16),
                   jax.ShapeDtypeStruct((NS_ROWS, D), F32)],
        scratch_shapes=[pltpu.VMEM((MLP_GROUPS, CHUNK, CHUNK), BF16)],
        compiler_params=_params("arbitrary"),
        name="gmlp",
    )(proj, proj, lnw, lnb, ws, bsx, wsx, bsx4)


def _cast_rows(src_ref, dst_ref, chunk=256):
    def body(r, carry):
        rows = pl.ds(pl.multiple_of(r * chunk, chunk), chunk)
        dst_ref[rows, :] = src_ref[rows, :].astype(dst_ref.dtype)
        return carry
    lax.fori_loop(0, src_ref.shape[0] // chunk, body, 0)


def _per_tile(i, prompt_fn, sample_fn):
    pl.when(i < NP_MT)(prompt_fn)
    pl.when(i == NP_MT)(sample_fn)


def _merge_kernel(ysp_ref, yss_ref, ym_ref, ga_ref, gb_ref, w1_ref, w2_ref, o_ref, w1b, w2b):
    i = pl.program_id(1)

    @pl.when(i == 0)
    def _():
        _cast_rows(w1_ref, w1b)
        _cast_rows(w2_ref, w2b)

    def emit(ys, rows):
        a1 = _dot(ys, w1b[...])
        a2 = _dot(ym_ref[rows, :], w2b[...])
        o_ref[rows, :] = (jax.nn.sigmoid(ga_ref[rows, :].astype(F32)) * a1
                          + jax.nn.sigmoid(gb_ref[rows, :].astype(F32)) * a2).astype(BF16)

    _per_tile(i,
              lambda: emit(ysp_ref[...], slice(None)),
              lambda: emit(yss_ref[...], slice(0, NS_ROWS)))


def _merge_call(ysp, yss, ym, proj, w1, w2):
    tn = 512
    nb = D // tn
    return pl.pallas_call(
        _merge_kernel,
        grid=(nb, N_MT),
        in_specs=[pl.BlockSpec((TMM, D), lambda j, i: (jnp.minimum(i, NP_MT - 1), 0)),
                  pl.BlockSpec((NS_ROWS, D), lambda j, i: (0, 0)),
                  pl.BlockSpec((TMM, D), lambda j, i: (i, 0)),
                  pl.BlockSpec((TMM, tn), lambda j, i: (i, PGA * nb + j)),
                  pl.BlockSpec((TMM, tn), lambda j, i: (i, PGB * nb + j)),
                  pl.BlockSpec((D, tn), lambda j, i: (0, j)),
                  pl.BlockSpec((D, tn), lambda j, i: (0, j))],
        out_specs=pl.BlockSpec((TMM, tn), lambda j, i: (i, j)),
        out_shape=jax.ShapeDtypeStruct((M_ROWS, D), BF16),
        scratch_shapes=[pltpu.VMEM((D, tn), BF16), pltpu.VMEM((D, tn), BF16)],
        compiler_params=_params("arbitrary", "arbitrary"),
        name="branch_merge",
    )(ysp, yss, ym, proj, proj, w1, w2)


def _resid_kernel(a_ref, w_ref, rp_ref, rs_ref, gp_ref, gs_ref, o_ref, wb):
    i = pl.program_id(1)

    @pl.when(i == 0)
    def _():
        _cast_rows(w_ref, wb)

    def prompt():
        o_ref[...] = rp_ref[...] + gp_ref[0] * _dot(a_ref[...], wb[...])

    def sample():
        acc = _dot(a_ref[0:NS_ROWS, :], wb[...])
        tn = acc.shape[-1]
        acc3 = acc.reshape(DEC_SEQ, DEC_BATCH, tn) * gs_ref[...][None]
        o_ref[0:NS_ROWS, :] = rs_ref[...] + acc3.reshape(NS_ROWS, tn)

    _per_tile(i, prompt, sample)


def _resid_call(a, w, rp, rs, rs_block, mod_p, mod_s, k_gate, tn, name, single_buffer_w=False):
    kdim = a.shape[1]
    nb = D // tn
    w_mode = dict(pipeline_mode=pl.Buffered(1)) if single_buffer_w else {}
    return pl.pallas_call(
        _resid_kernel,
        grid=(nb, N_MT),
        in_specs=[pl.BlockSpec((TMM, kdim), lambda j, i: (i, 0)),
                  pl.BlockSpec((kdim, tn), lambda j, i: (0, j), **w_mode),
                  pl.BlockSpec((TMM, tn), lambda j, i: (jnp.minimum(i, NP_MT - 1), j)),
                  pl.BlockSpec((NS_ROWS, tn), lambda j, i: (rs_block, j)),
                  pl.BlockSpec((1, 1, tn), lambda j, i: (_seq_of_mtile(i), 0, k_gate * nb + j)),
                  pl.BlockSpec((DEC_BATCH, tn), lambda j, i: (0, k_gate * nb + j))],
        out_specs=pl.BlockSpec((TMM, tn), lambda j, i: (i, j)),
        out_shape=jax.ShapeDtypeStruct((M_ROWS, D), F32),
        scratch_shapes=[pltpu.VMEM((kdim, tn), BF16)],
        compiler_params=_params("arbitrary", "arbitrary"),
        name=name,
    )(a, w, rp, rs, mod_p, mod_s)


TN_FF = 512
N_FF_BLOCKS = D_FF // TN_FF


def _up_kernel(a_ref, wa_ref, wv_ref, cw_ref, cb_ref, fst_ref,
               h_ref, fcp_ref, fcs_ref, wab, wvb, acc_a, acc_v):
    i = pl.program_id(1)
    rc = 64

    @pl.when(i == 0)
    def _():
        _cast_rows(wa_ref, wab)
        _cast_rows(wv_ref, wvb)

    def prompt():
        starts_sequence = i % MT_PER_SEQ == 0
        acc_a[0:8, :] = jnp.where(starts_sequence, 0.0, acc_a[TMM:TMM + 8, :])
        x = a_ref[...]
        acc_a[8:8 + TMM, :] = _dot(x, wab[...])
        acc_v[...] = _dot(x, wvb[...])
        cw, cb = cw_ref[...], cb_ref[...]
        for r in range(0, TMM, rc):
            conv = (cb + cw[2:3] * acc_a[8 + r:8 + r + rc, :]
                    + cw[1:2] * acc_a[7 + r:7 + r + rc, :]
                    + cw[0:1] * acc_a[6 + r:6 + r + rc, :])
            h_ref[r:r + rc, :] = (jax.nn.gelu(conv) * acc_v[r:r + rc, :]).astype(BF16)
        fcp_ref[0] = acc_a[TMM + 6:TMM + 8, :]

    def sample():
        B = DEC_BATCH
        x = a_ref[0:NS_ROWS, :]
        acc_a[8:8 + NS_ROWS, :] = _dot(x, wab[...])
        acc_v[0:NS_ROWS, :] = _dot(x, wvb[...])
        cw, cb = cw_ref[...], cb_ref[...]

        def pre(t, r):
            if t < 0:
                return fst_ref[t + 2, r:r + rc, :]
            return acc_a[8 + t * B + r:8 + t * B + r + rc, :]

        for t in range(DEC_SEQ):
            for r in range(0, B, rc):
                conv = cb + cw[2:3] * pre(t, r) + cw[1:2] * pre(t - 1, r) + cw[0:1] * pre(t - 2, r)
                h_ref[t * B + r:t * B + r + rc, :] = (
                    jax.nn.gelu(conv) * acc_v[t * B + r:t * B + r + rc, :]).astype(BF16)
        fcs_ref[0] = acc_a[8 + 2 * B:8 + 3 * B, :]
        fcs_ref[1] = acc_a[8 + 3 * B:8 + 4 * B, :]

    _per_tile(i, prompt, sample)


def _up_call(n2, w_up, cw, cb, fst_t):
    return pl.pallas_call(
        _up_kernel,
        grid=(N_FF_BLOCKS, N_MT),
        in_specs=[pl.BlockSpec((TMM, D), lambda j, i: (i, 0)),
                  pl.BlockSpec((D, TN_FF), lambda j, i: (0, j)),
                  pl.BlockSpec((D, TN_FF), lambda j, i: (0, N_FF_BLOCKS + j)),
                  pl.BlockSpec((3, TN_FF), lambda j, i: (0, j)),
                  pl.BlockSpec((1, TN_FF), lambda j, i: (0, j)),
                  pl.BlockSpec((2, DEC_BATCH, TN_FF), lambda j, i: (0, 0, j))],
        out_specs=[pl.BlockSpec((TMM, TN_FF), lambda j, i: (i, j)),
                   pl.BlockSpec((1, 2, TN_FF), lambda j, i: (_seq_of_mtile(i), 0, j)),
                   pl.BlockSpec((2, DEC_BATCH, TN_FF), lambda j, i: (0, 0, j))],
        out_shape=[jax.ShapeDtypeStruct((M_ROWS, D_FF), BF16),
                   jax.ShapeDtypeStruct((BATCH, 2, D_FF), F32),
                   jax.ShapeDtypeStruct((2, DEC_BATCH, D_FF), F32)],
        scratch_shapes=[pltpu.VMEM((D, TN_FF), BF16), pltpu.VMEM((D, TN_FF), BF16),
                        pltpu.VMEM((TMM + 8, TN_FF), F32), pltpu.VMEM((TMM, TN_FF), F32)],
        compiler_params=_params("arbitrary", "arbitrary"),
        name="ffn_up",
    )(n2, w_up, w_up, cw, cb, fst_t)


def _final_kernel(x_ref, w_ref, yp_ref, ys_ref):
    i = pl.program_id(0)
    y = _rms(x_ref[...]) * w_ref[...]

    @pl.when(i < NP_TILES)
    def _():
        yp_ref[...] = y

    @pl.when(i >= NP_TILES)
    def _():
        ys_ref[...] = y


def _final_call(x3, w):
    return pl.pallas_call(
        _final_kernel,
        grid=(N_TILES,),
        in_specs=[pl.BlockSpec((TM, D), lambda i: (i, 0)),
                  pl.BlockSpec((1, D), lambda i: (0, 0))],
        out_specs=[pl.BlockSpec((TM, D), lambda i: (jnp.minimum(i, NP_TILES - 1), 0)),
                   pl.BlockSpec((TM, D), lambda i: (0, 0))],
        out_shape=[jax.ShapeDtypeStruct((NP_ROWS, D), F32),
                   jax.ShapeDtypeStruct((NS_ROWS, D), F32)],
        compiler_params=_params("arbitrary"),
        name="final_norm",
    )(x3, w)


def _to_time_major(a):
    return jnp.transpose(a, (1, 0, 2))


def _pad_time(a_t, width):
    a_b = jnp.transpose(a_t.reshape(DEC_SEQ, DEC_BATCH, width), (1, 0, 2))
    return jnp.pad(a_b, ((0, 0), (0, T_PAD - DEC_SEQ), (0, 0)))


def kernel(x_prompt, x_sample, state_ssm, state_ssd_conv, state_ffn_conv, c_prompt, c_sample,
           norm1_w, w_ada, b_ada, w_in, ssd_conv_w, ssd_conv_b, dt_bias, a_log, d_skip,
           ssd_norm_w, mlp_ln_w, mlp_ln_b, w_spatial, b_spatial, w_ssd_o, w_mlp_o, w_out,
           norm2_w, w_up, ffn_conv_w, ffn_conv_b, w_down, final_norm_w):
    assert w_in.shape[0] == 1, "single-layer trunk"
    row = lambda v: v.reshape(1, -1)

    xp = x_prompt.reshape(NP_ROWS, D)
    xs = _to_time_major(x_sample).reshape(NS_ROWS, D)

    c_all = jnp.concatenate([c_prompt, c_sample, jnp.zeros((4, D), F32)], axis=0)
    mod = _ada_call(c_all, w_ada[0], row(b_ada[0]))
    mod_p = mod[:BATCH].reshape(BATCH, 1, 6 * D)
    mod_s = mod[BATCH:BATCH + DEC_BATCH]
    K_SHIFT1, K_SCALE1, K_GATE1, K_SHIFT2, K_SCALE2, K_GATE2 = range(6)

    xp_spec = pl.BlockSpec((TM, D), lambda i: (jnp.minimum(i, NP_TILES - 1), 0))
    xs_spec = pl.BlockSpec((TM, D), lambda i: (0, 0))
    w_in_t = w_in[0].T
    w_dt_t = w_in_t[DT_COL:DT_COL + LANE]
    n1, dt_raw = _norm_call(xp, xp_spec, xs, xs_spec, row(norm1_w[0]), mod_p, mod_s,
                            K_SCALE1, K_SHIFT1, w_dt_t)
    proj = _inproj_call(n1, w_in_t)

    hp = jnp.zeros((8, LANE), F32)
    hp = hp.at[0, :HEADS].set(dt_bias[0]).at[1, :HEADS].set(a_log[0]).at[2, :HEADS].set(d_skip[0])
    tri = jnp.tril(jnp.ones((CHUNK, CHUNK), F32)).astype(BF16)
    cw, cb = ssd_conv_w[0], row(ssd_conv_b[0])
    nw = row(ssd_norm_w[0])

    y_ssd_p, ssm_p, cst_p = _ssd_prompt_call(proj, dt_raw, cw, cb, hp, nw, tri)

    head_of_col = jnp.arange(D) // HEADDIM
    ex = (jnp.arange(LANE)[:, None] == head_of_col[None, :]).astype(BF16)
    grp_of_head = jnp.arange(LANE) // (HEADS // GROUPS)
    seg = ((jnp.arange(GROUPS * STATE)[:, None] // STATE == grp_of_head[None, :])
           & (jnp.arange(LANE)[None, :] < HEADS)).astype(BF16)
    dskx = row(jnp.repeat(d_skip[0], HEADDIM))
    cst_t = _to_time_major(state_ssd_conv[0])
    ypre, efull, xw_t, bb_t, cc_t, e3, ncs_t = _ssds_prep_call(
        proj, dt_raw, cst_t, cw, cb, hp, dskx, ex, seg)
    cdim = GROUPS * STATE
    gw = (HEADS // GROUPS) * HEADDIM
    yo_b, ssm_s = _ssds_state_call(
        e3[:, :HEADS], state_ssm[0].reshape(DEC_BATCH, GROUPS, gw, STATE),
        _pad_time(cc_t, cdim), _pad_time(bb_t, cdim), _pad_time(xw_t, D))
    yo_t = jnp.transpose(yo_b[:, :DEC_SEQ], (1, 0, 2)).reshape(NS_ROWS, D)
    y_ssd_s = _ssds_post_call(ypre, efull, yo_t, proj, nw)

    gidx = jnp.arange(D) // MLP_GROUP_DIM
    bsx = b_spatial[0][:, :CHUNK].T[:, gidx]
    ws4 = w_spatial[0][:, :DEC_SEQ, :DEC_SEQ]
    wsx = jnp.transpose(ws4, (1, 2, 0)).reshape(DEC_SEQ * DEC_SEQ, MLP_GROUPS)[:, gidx]
    bsx4 = jnp.pad(b_spatial[0][:, :DEC_SEQ].T[:, gidx], ((0, 8 - DEC_SEQ), (0, 0)))
    y_mlp, cv_t = _mlp_call(proj, row(mlp_ln_w[0]), row(mlp_ln_b[0]), w_spatial[0], bsx, wsx, bsx4)

    mixed = _merge_call(y_ssd_p, y_ssd_s, y_mlp, proj, w_ssd_o[0], w_mlp_o[0])
    x2 = _resid_call(mixed, w_out[0], xp, xs, 0, mod_p, mod_s, K_GATE1, 1024, "out_proj")

    x2p_spec = pl.BlockSpec((TM, D), lambda i: (jnp.minimum(i, NP_TILES - 1), 0))
    x2s_spec = pl.BlockSpec((TM, D), lambda i: (NP_TILES, 0))
    (n2,) = _norm_call(x2, x2p_spec, x2, x2s_spec, row(norm2_w[0]), mod_p, mod_s,
                       K_SCALE2, K_SHIFT2, None)
    fst_t = _to_time_major(state_ffn_conv[0])
    h, ffn_p, ffn_s_t = _up_call(n2, w_up[0], ffn_conv_w[0], row(ffn_conv_b[0]), fst_t)
    x3 = _resid_call(h, w_down[0], x2, x2, NP_TILES, mod_p, mod_s, K_GATE2, 512, "ffn_down",
                     single_buffer_w=True)

    y_p, y_s_t = _final_call(x3, row(final_norm_w))

    from_t = lambda a, t, c: jnp.transpose(a.reshape(t, DEC_BATCH, c), (1, 0, 2))
    return (y_p.reshape(BATCH, SEQ, D),
            from_t(y_s_t, DEC_SEQ, D),
            ssm_p.reshape(1, BATCH, HEADS, HEADDIM, STATE),
            ssm_s.reshape(1, DEC_BATCH, HEADS, HEADDIM, STATE),
            cst_p[None],
            from_t(ncs_t, 3, CONV_DIM)[None],
            ffn_p[None],
            from_t(ffn_s_t, 2, D_FF)[None],
            from_t(cv_t, DEC_SEQ, D)[None])
```

```python
import functools

import jax
import jax.numpy as jnp
from jax import lax
from jax.experimental import pallas as pl
from jax.experimental.pallas import tpu as pltpu

F32 = jnp.float32
BF16 = jnp.bfloat16

D = 2048
BATCH, SEQ = 4, 2048
DEC_BATCH, DEC_SEQ = 128, 4
NP_ROWS = BATCH * SEQ
NS_ROWS = DEC_BATCH * DEC_SEQ
M_ROWS = NP_ROWS + NS_ROWS
TM = 512
NP_TILES = NP_ROWS // TM
N_TILES = M_ROWS // TM
TILES_PER_SEQ = SEQ // TM
TMM = 1024
NP_MT = NP_ROWS // TMM
N_MT = NP_MT + 1
MT_PER_SEQ = SEQ // TMM
HEADS, HEADDIM, GROUPS, STATE = 32, 64, 4, 128
CHUNK = 128
N_CHUNKS = SEQ // CHUNK
CONV_DIM = D + 2 * GROUPS * STATE
MLP_GROUPS = 8
MLP_GROUP_DIM = D // MLP_GROUPS
D_FF = 5632
EPS = 1e-6
DT_COL = D + CONV_DIM
TN_IN = 1024
LANE = 128
VMEM_LIMIT = 56 * 1024 * 1024


def _params(*sem, flags=None):
    return pltpu.CompilerParams(dimension_semantics=sem, vmem_limit_bytes=VMEM_LIMIT, flags=flags)


def _dot(a, b):
    return jnp.dot(a, b, preferred_element_type=F32)


def _split_bf16(v, terms):
    out = []
    r = v
    for _ in range(terms):
        p = r.astype(BF16)
        out.append(p)
        r = r - p.astype(F32)
    return out


def _dot_exact_rhs(v, e, terms=3):
    acc = None
    for p in _split_bf16(v, terms):
        d = _dot(p, e)
        acc = d if acc is None else acc + d
    return acc


def _dot_exact_lhs(t, v, terms=3):
    acc = None
    for p in _split_bf16(v, terms):
        d = _dot(t, p)
        acc = d if acc is None else acc + d
    return acc


def _silu(x):
    return x * jax.nn.sigmoid(x)


def _softplus(x):
    return jnp.maximum(x, 0.0) + jnp.log1p(jnp.exp(-jnp.abs(x)))


def _rms(x):
    return x * lax.rsqrt(jnp.mean(x * x, axis=-1, keepdims=True) + EPS)


def _ada_kernel(c_ref, w_ref, b_ref, o_ref):
    a = _silu(c_ref[...]).astype(BF16)
    o_ref[...] = _dot(a, w_ref[...].astype(BF16)) + b_ref[...]


def _ada_call(c_all, w, b):
    rows = c_all.shape[0]
    tn = 1024
    return pl.pallas_call(
        _ada_kernel,
        grid=(6 * D // tn,),
        in_specs=[pl.BlockSpec((rows, D), lambda j: (0, 0)),
                  pl.BlockSpec((D, tn), lambda j: (0, j)),
                  pl.BlockSpec((1, tn), lambda j: (0, j))],
        out_specs=pl.BlockSpec((rows, tn), lambda j: (0, j)),
        out_shape=jax.ShapeDtypeStruct((rows, 6 * D), F32),
        compiler_params=_params("arbitrary"),
        name="ada_mod",
    )(c_all, w, b)


def _norm_kernel(with_dt, xp_ref, xs_ref, nw_ref, scp_ref, shp_ref, scs_ref, shs_ref, *rest):
    if with_dt:
        wdt_ref, n_ref, dt_ref = rest
    else:
        (n_ref,) = rest
    i = pl.program_id(0)

    def emit(n):
        nb = n.astype(BF16)
        n_ref[...] = nb
        if with_dt:
            dt_ref[...] = lax.dot_general(nb, wdt_ref[...].astype(BF16),
                                          (((1,), (1,)), ((), ())),
                                          preferred_element_type=F32)

    @pl.when(i < NP_TILES)
    def _():
        y = _rms(xp_ref[...]) * nw_ref[...]
        emit(y * (1.0 + scp_ref[0]) + shp_ref[0])

    @pl.when(i >= NP_TILES)
    def _():
        y = _rms(xs_ref[...]) * nw_ref[...]
        y3 = y.reshape(DEC_SEQ, DEC_BATCH, D)
        emit((y3 * (1.0 + scs_ref[...])[None] + shs_ref[...][None]).reshape(TM, D))


def _seq_of_tile(i):
    return jnp.minimum(i // TILES_PER_SEQ, BATCH - 1)


def _seq_of_mtile(i):
    return jnp.minimum(i // MT_PER_SEQ, BATCH - 1)


def _norm_call(xp, xp_spec, xs, xs_spec, nw, mod_p, mod_s, k_scale, k_shift, w_dt):
    with_dt = w_dt is not None
    in_specs = [
        xp_spec, xs_spec,
        pl.BlockSpec((1, D), lambda i: (0, 0)),
        pl.BlockSpec((1, 1, D), lambda i: (_seq_of_tile(i), 0, k_scale)),
        pl.BlockSpec((1, 1, D), lambda i: (_seq_of_tile(i), 0, k_shift)),
        pl.BlockSpec((DEC_BATCH, D), lambda i: (0, k_scale)),
        pl.BlockSpec((DEC_BATCH, D), lambda i: (0, k_shift)),
    ]
    args = [xp, xs, nw, mod_p, mod_p, mod_s, mod_s]
    out_specs = [pl.BlockSpec((TM, D), lambda i: (i, 0))]
    out_shape = [jax.ShapeDtypeStruct((M_ROWS, D), BF16)]
    if with_dt:
        in_specs.append(pl.BlockSpec((LANE, D), lambda i: (0, 0)))
        args.append(w_dt)
        out_specs.append(pl.BlockSpec((TM, LANE), lambda i: (i, 0)))
        out_shape.append(jax.ShapeDtypeStruct((M_ROWS, LANE), F32))
    return pl.pallas_call(
        functools.partial(_norm_kernel, with_dt),
        grid=(N_TILES,),
        in_specs=in_specs,
        out_specs=out_specs,
        out_shape=out_shape,
        compiler_params=_params("arbitrary"),
        name="norm_mod_dt" if with_dt else "norm_mod",
    )(*args)


N_IN_BLOCKS = 13
UVG_ROW = DT_COL + HEADS
FIRST_CONV_BLOCK = 10


def _in_src_row(j):
    row = jnp.where(j < 2, j * TN_IN,
                    jnp.where(j < FIRST_CONV_BLOCK, UVG_ROW + (j - 2) * TN_IN,
                              D + (j - FIRST_CONV_BLOCK) * TN_IN))
    return pl.multiple_of(row, HEADS)


def _in_conv_block(j):
    return jnp.maximum(j - FIRST_CONV_BLOCK, 0)


def _inproj_kernel(a_ref, wt_ref, cw_ref, cb_ref, cst_ref, o_ref, csp_ref, css_ref, wbf_ref, acc_ref):
    j = pl.program_id(0)
    i = pl.program_id(1)
    rc = 64
    B = DEC_BATCH

    @pl.when(i == 0)
    def _():
        for r in range(TN_IN // LANE):
            rows = slice(r * LANE, (r + 1) * LANE)
            wbf_ref[:, rows] = wt_ref[rows, :].T.astype(BF16)

    def elementwise(fn, rows):
        def body():
            acc_ref[8:8 + rows, :] = _dot(a_ref[0:rows, :], wbf_ref[...])
            for r in range(0, rows, rc):
                o_ref[r:r + rc, :] = fn(acc_ref[8 + r:8 + r + rc, :]).astype(o_ref.dtype)
        return body

    def conv_prompt():
        starts_sequence = i % MT_PER_SEQ == 0
        acc_ref[0:8, :] = jnp.where(starts_sequence, 0.0, acc_ref[TMM:TMM + 8, :])
        acc_ref[8:8 + TMM, :] = _dot(a_ref[...], wbf_ref[...])
        cw, cb = cw_ref[...], cb_ref[...]
        for r in range(0, TMM, rc):
            conv = cb + cw[3:4] * acc_ref[8 + r:8 + r + rc, :]
            for k in range(3):
                conv = conv + cw[k:k + 1] * acc_ref[5 + k + r:5 + k + r + rc, :]
            o_ref[r:r + rc, :] = _silu(conv).astype(o_ref.dtype)
        csp_ref[0] = acc_ref[TMM + 5:TMM + 8, :]

    def conv_sample():
        acc_ref[8:8 + NS_ROWS, :] = _dot(a_ref[0:NS_ROWS, :], wbf_ref[...])
        cw, cb = cw_ref[...], cb_ref[...]

        def pre(t, r):
            if t < 0:
                return cst_ref[t + 3, r:r + rc, :]
            return acc_ref[8 + t * B + r:8 + t * B + r + rc, :]

        for t in range(DEC_SEQ):
            for r in range(0, B, rc):
                conv = cb + cw[3:4] * pre(t, r)
                for k in range(3):
                    conv = conv + cw[k:k + 1] * pre(t - 3 + k, r)
                o_ref[t * B + r:t * B + r + rc, :] = _silu(conv).astype(o_ref.dtype)
        for t in range(1, DEC_SEQ):
            css_ref[t - 1] = acc_ref[8 + t * B:8 + (t + 1) * B, :]

    is_conv = j >= FIRST_CONV_BLOCK
    kinds = ((j < 2, _silu),
             (jnp.logical_and(j >= 2, j < 6), jax.nn.gelu),
             (jnp.logical_and(j >= 6, j < FIRST_CONV_BLOCK), jax.nn.sigmoid))
    for cond, fn in kinds:
        pl.when(jnp.logical_and(cond, i < NP_MT))(elementwise(fn, TMM))
        pl.when(jnp.logical_and(cond, i == NP_MT))(elementwise(fn, NS_ROWS))
    pl.when(jnp.logical_and(is_conv, i < NP_MT))(conv_prompt)
    pl.when(jnp.logical_and(is_conv, i == NP_MT))(conv_sample)


def _inproj_call(n1, w_in_t, cw, cb, cst_t):
    cblk = _in_conv_block
    seq = lambda j, i: jnp.where(j < FIRST_CONV_BLOCK, 0, _seq_of_mtile(i))
    return pl.pallas_call(
        _inproj_kernel,
        grid=(N_IN_BLOCKS, N_MT),
        in_specs=[pl.BlockSpec((TMM, D), lambda j, i: (i, 0)),
                  pl.BlockSpec((pl.Element(TN_IN), pl.Element(D)),
                               lambda j, i: (_in_src_row(j), 0)),
                  pl.BlockSpec((4, TN_IN), lambda j, i: (0, cblk(j))),
                  pl.BlockSpec((1, TN_IN), lambda j, i: (0, cblk(j))),
                  pl.BlockSpec((3, DEC_BATCH, TN_IN), lambda j, i: (0, 0, cblk(j)))],
        out_specs=[pl.BlockSpec((TMM, TN_IN), lambda j, i: (i, j)),
                   pl.BlockSpec((1, 3, TN_IN), lambda j, i: (seq(j, i), 0, cblk(j))),
                   pl.BlockSpec((3, DEC_BATCH, TN_IN), lambda j, i: (0, 0, cblk(j)))],
        out_shape=[jax.ShapeDtypeStruct((M_ROWS, N_IN_BLOCKS * TN_IN), BF16),
                   jax.ShapeDtypeStruct((BATCH, 3, CONV_DIM), F32),
                   jax.ShapeDtypeStruct((3, DEC_BATCH, CONV_DIM), F32)],
        scratch_shapes=[pltpu.VMEM((D, TN_IN), BF16), pltpu.VMEM((TMM + 8, TN_IN), F32)],
        compiler_params=_params("arbitrary", "arbitrary"),
        name="in_proj",
    )(n1, w_in_t, cw, cb, cst_t)


PZ, PU, PV, PGA, PGB, PX = 0, 1, 2, 3, 4, 5
PBC_1024 = 12


def _gated_group_norm(get_y, zact_ref, nw_ref, o_ref):
    gw = D // GROUPS
    for g in range(GROUPS):
        cols = slice(g * gw, (g + 1) * gw)
        gg = _rms(get_y(cols) * zact_ref[:, cols].astype(F32))
        o_ref[:, cols] = (gg * nw_ref[:, cols]).astype(o_ref.dtype)


def _ssd_prompt_kernel(z_ref, x_ref, bc_ref, dt_ref, hp_ref, hpc_ref, dskx_ref, nw_ref, triu_ref,
                       y_ref, ssm_ref, st_ref, yscr_ref):
    c = pl.program_id(1)
    T = CHUNK
    cdim = GROUPS * STATE

    @pl.when(c == 0)
    def _():
        st_ref[...] = jnp.zeros(st_ref.shape, F32)

    hp = hp_ref[...]
    dt_t = _softplus((dt_ref[...] + hp[0:1, :]).T[0:HEADS, :])
    adt_t = dt_t * (-jnp.exp(hpc_ref[0:HEADS, 1:2]))
    cs_t = _dot_exact_rhs(adt_t, triu_ref[...])
    rsub_t = cs_t - jnp.log(dt_t)
    cs = jnp.concatenate([cs_t, jnp.zeros((LANE - HEADS, T), F32)], axis=0).T
    ecs = jnp.exp(cs)

    row = lax.broadcasted_iota(jnp.int32, (T, T), 0)
    lane = lax.broadcasted_iota(jnp.int32, (T, T), 1)
    causal = row >= lane
    left = lane < HEADDIM
    mask_l = jnp.where(left, 1.0, 0.0).astype(BF16)
    nt = (((1,), (1,)), ((), ()))
    pairs_per_group = HEADS // GROUPS // 2

    for g in range(GROUPS):
        c_b = bc_ref[:, cdim + g * STATE:cdim + (g + 1) * STATE]
        b_b = bc_ref[:, g * STATE:(g + 1) * STATE]
        cb = lax.dot_general(c_b, b_b, nt, preferred_element_type=F32)
        b_t = b_b.astype(F32).T
        for k4 in range(pairs_per_group):
            k = g * pairs_per_group + k4
            cols = slice(k * LANE, (k + 1) * LANE)
            xpb = x_ref[:, cols]
            x_lo = xpb * mask_l
            xbd = jnp.concatenate([x_lo, xpb - x_lo], axis=0)
            st = st_ref[k]
            yraw = _dot(c_b, st.astype(BF16))
            scores, bws = [], []
            for h in (2 * k, 2 * k + 1):
                decay_dt = jnp.exp(jnp.where(causal, cs[:, h:h + 1] - rsub_t[h:h + 1, :], -jnp.inf))
                scores.append((cb * decay_dt).astype(BF16))
                wrow = jnp.exp(cs_t[h:h + 1, T - 1:T] - cs_t[h:h + 1, :]) * dt_t[h:h + 1, :]
                bws.append((b_t * wrow).astype(BF16))
            h0, h1 = 2 * k, 2 * k + 1
            ecol = jnp.where(left, ecs[:, h0:h0 + 1], ecs[:, h1:h1 + 1])
            elast = jnp.where(left[0:1], ecs[T - 1:T, h0:h0 + 1], ecs[T - 1:T, h1:h1 + 1])
            yscr_ref[:, cols] = (_dot(jnp.concatenate(scores, axis=1), xbd) + ecol * yraw
                                 + dskx_ref[:, cols] * xpb.astype(F32))
            st_ref[k] = elast * st + _dot(jnp.concatenate(bws, axis=1), xbd)

    _gated_group_norm(lambda cols: yscr_ref[:, cols], z_ref, nw_ref, y_ref)

    @pl.when(c == N_CHUNKS - 1)
    def _():
        for k in range(HEADS // 2):
            ssm_ref[0, k] = st_ref[k].T


def _ssd_prompt_call(proj, dt_raw, hp, hpc, dskx, nw, triu):
    row = lambda b, c: b * N_CHUNKS + c
    const = lambda b, c: (0, 0)
    return pl.pallas_call(
        _ssd_prompt_kernel,
        grid=(BATCH, N_CHUNKS),
        in_specs=[pl.BlockSpec((CHUNK, D), lambda b, c: (row(b, c), PZ)),
                  pl.BlockSpec((CHUNK, D), lambda b, c: (row(b, c), PX)),
                  pl.BlockSpec((CHUNK, 2 * GROUPS * STATE), lambda b, c: (row(b, c), PBC_1024)),
                  pl.BlockSpec((CHUNK, LANE), lambda b, c: (row(b, c), 0)),
                  pl.BlockSpec((8, LANE), const),
                  pl.BlockSpec((LANE, 8), const),
                  pl.BlockSpec((1, D), const),
                  pl.BlockSpec((1, D), const),
                  pl.BlockSpec((CHUNK, CHUNK), const)],
        out_specs=[pl.BlockSpec((CHUNK, D), lambda b, c: (row(b, c), 0)),
                   pl.BlockSpec((1, HEADS // 2, LANE, STATE), lambda b, c: (b, 0, 0, 0))],
        out_shape=[jax.ShapeDtypeStruct((NP_ROWS, D), BF16),
                   jax.ShapeDtypeStruct((BATCH, HEADS // 2, LANE, STATE), F32)],
        scratch_shapes=[pltpu.VMEM((HEADS // 2, STATE, LANE), F32),
                        pltpu.VMEM((CHUNK, D), F32)],
        compiler_params=_params("arbitrary", "arbitrary"),
        name="ssd_prompt",
    )(proj, proj, proj, dt_raw, hp, hpc, dskx, nw, triu)


def _ssds_prep_step(tt, x_ref, bc_ref, dt_ref, hp_ref, dskx_ref, ex_ref, seg_ref,
                    ypre_ref, efull_ref, xw_ref, bb_ref, cc_ref, e3_ref):
    B = DEC_BATCH
    cdim = GROUPS * STATE
    blk = lambda t: slice(t * B, (t + 1) * B)
    x_of = lambda t: x_ref[blk(t), :].astype(F32)
    b_of = lambda t: bc_ref[blk(t), 0:cdim].astype(F32)
    c_of = lambda t: bc_ref[blk(t), cdim:2 * cdim].astype(F32)

    bb_ref[...] = b_of(tt)
    cc_ref[...] = c_of(tt)

    hp = hp_ref[...]
    a_neg = -jnp.exp(hp[1:2, :])
    dts, css = [], []
    run = None
    for t in range(DEC_SEQ):
        dt = _softplus(dt_ref[blk(t), :] + hp[0:1, :])
        run = dt * a_neg if run is None else run + dt * a_neg
        dts.append(dt)
        css.append(run)

    ex = ex_ref[...]
    e3_ref[...] = jnp.exp(css[-1])
    efull_ref[...] = _dot_exact_rhs(jnp.exp(css[tt]), ex)
    w_t = jnp.exp(css[-1] - css[tt]) * dts[tt]
    xw_ref[...] = x_of(tt) * _dot_exact_rhs(w_t, ex)

    seg = seg_ref[...]
    acc = dskx_ref[...] * x_of(tt)
    c_t = c_of(tt)
    for s in range(tt + 1):
        cbh = _dot_exact_rhs(c_t * b_of(s), seg)
        g_ts = cbh * jnp.exp(css[tt] - css[s]) * dts[s]
        acc = acc + _dot_exact_rhs(g_ts, ex) * x_of(s)
    ypre_ref[...] = acc


def _ssds_prep_kernel(*refs):
    t = pl.program_id(0)
    for tt in range(DEC_SEQ):
        pl.when(t == tt)(functools.partial(_ssds_prep_step, tt, *refs))


def _ssds_prep_call(proj, dt_raw, hp, dskx, ex, seg):
    full = lambda shape: pl.BlockSpec(shape, lambda t: (0,) * len(shape))
    step = lambda width: pl.BlockSpec((DEC_BATCH, width), lambda t: (t, 0))
    cdim = GROUPS * STATE
    return pl.pallas_call(
        _ssds_prep_kernel,
        grid=(DEC_SEQ,),
        in_specs=[pl.BlockSpec((NS_ROWS, D), lambda t: (NP_TILES, PX)),
                  pl.BlockSpec((NS_ROWS, 2 * cdim), lambda t: (NP_TILES, PBC_1024)),
                  pl.BlockSpec((NS_ROWS, LANE), lambda t: (NP_TILES, 0)),
                  full((8, LANE)), full((1, D)), full((LANE, D)), full((cdim, LANE))],
        out_specs=[step(D), step(D), step(D), step(cdim), step(cdim),
                   full((DEC_BATCH, LANE))],
        out_shape=[jax.ShapeDtypeStruct((NS_ROWS, D), F32),
                   jax.ShapeDtypeStruct((NS_ROWS, D), F32),
                   jax.ShapeDtypeStruct((NS_ROWS, D), F32),
                   jax.ShapeDtypeStruct((NS_ROWS, cdim), F32),
                   jax.ShapeDtypeStruct((NS_ROWS, cdim), F32),
                   jax.ShapeDtypeStruct((DEC_BATCH, LANE), F32)],
        compiler_params=_params("arbitrary"),
        name="ssd_sample_prep",
    )(proj, proj, dt_raw, hp, dskx, ex, seg)


SEQ_PER_STEP = 4
T_PAD = 8


def _ssds_state_kernel(e3_ref, st_ref, cc_ref, bb_ref, xw_ref, yo_ref, so_ref):
    blk = pl.program_id(0)
    nt = (((1,), (1,)), ((), ()))
    tn = (((0,), (0,)), ((), ()))
    hpg = HEADS // GROUPS
    gw = hpg * HEADDIM
    for s in range(SEQ_PER_STEP):
        b = blk * SEQ_PER_STEP + s
        for g in range(GROUPS):
            h0 = st_ref[s, g]
            c_g = cc_ref[s, :, g * STATE:(g + 1) * STATE].astype(BF16)
            yo_ref[s, :, g * gw:(g + 1) * gw] = lax.dot_general(
                c_g, h0.astype(BF16), nt, preferred_element_type=F32)
            x_g = xw_ref[s, :, g * gw:(g + 1) * gw].astype(BF16)
            b_g = bb_ref[s, :, g * STATE:(g + 1) * STATE].astype(BF16)
            dh = lax.dot_general(x_g, b_g, tn, preferred_element_type=F32)
            for hh in range(hpg):
                rows = slice(hh * HEADDIM, (hh + 1) * HEADDIM)
                so_ref[s, g, rows, :] = e3_ref[b, g * hpg + hh] * h0[rows] + dh[rows]


def _ssds_state_call(e3, state, cc_b, bb_b, xw_b):
    sb = SEQ_PER_STEP
    gw = (HEADS // GROUPS) * HEADDIM
    cdim = GROUPS * STATE
    return pl.pallas_call(
        _ssds_state_kernel,
        grid=(DEC_BATCH // sb,),
        in_specs=[pl.BlockSpec(memory_space=pltpu.SMEM),
                  pl.BlockSpec((sb, GROUPS, gw, STATE), lambda i: (i, 0, 0, 0)),
                  pl.BlockSpec((sb, T_PAD, cdim), lambda i: (i, 0, 0)),
                  pl.BlockSpec((sb, T_PAD, cdim), lambda i: (i, 0, 0)),
                  pl.BlockSpec((sb, T_PAD, D), lambda i: (i, 0, 0))],
        out_specs=[pl.BlockSpec((sb, T_PAD, D), lambda i: (i, 0, 0)),
                   pl.BlockSpec((sb, GROUPS, gw, STATE), lambda i: (i, 0, 0, 0))],
        out_shape=[jax.ShapeDtypeStruct((DEC_BATCH, T_PAD, D), F32),
                   jax.ShapeDtypeStruct((DEC_BATCH, GROUPS, gw, STATE), F32)],
        compiler_params=_params("arbitrary"),
        name="ssd_sample_state",
    )(e3, state, cc_b, bb_b, xw_b)


def _ssds_post_kernel(ypre_ref, efull_ref, yo_ref, z_ref, nw_ref, o_ref):
    get_y = lambda cols: ypre_ref[:, cols] + efull_ref[:, cols] * yo_ref[:, cols]
    _gated_group_norm(get_y, z_ref, nw_ref, o_ref)


def _ssds_post_call(ypre, efull, yo_t, proj, nw):
    step = pl.BlockSpec((DEC_BATCH, D), lambda t: (t, 0))
    return pl.pallas_call(
        _ssds_post_kernel,
        grid=(DEC_SEQ,),
        in_specs=[step, step, step,
                  pl.BlockSpec((DEC_BATCH, D), lambda t: (NP_ROWS // DEC_BATCH + t, PZ)),
                  pl.BlockSpec((1, D), lambda t: (0, 0))],
        out_specs=step,
        out_shape=jax.ShapeDtypeStruct((NS_ROWS, D), BF16),
        compiler_params=_params("arbitrary"),
        name="ssd_sample_post",
    )(ypre, efull, yo_t, proj, nw)


def _mlp_kernel(u_ref, v_ref, lnw_ref, lnb_ref, ws_ref, bsx_ref, wsx_ref, bsx4_ref,
                y_ref, cv_ref, wm_ref):
    i = pl.program_id(0)
    T = CHUNK

    @pl.when(i == 0)
    def _():
        row = lax.broadcasted_iota(jnp.int32, (T, T), 0)
        lane = lax.broadcasted_iota(jnp.int32, (T, T), 1)
        for g in range(MLP_GROUPS):
            wm_ref[g] = jnp.where(row >= lane, ws_ref[g], 0.0).astype(BF16)

    def vnorm(rows):
        vg = v_ref[rows, :].astype(F32)
        xc = vg - jnp.mean(vg, axis=-1, keepdims=True)
        y = xc * lax.rsqrt(jnp.mean(xc * xc, axis=-1, keepdims=True) + EPS)
        return y * lnw_ref[...] + lnb_ref[...]

    @pl.when(i < NP_TILES)
    def _():
        for cc in range(TM // T):
            rows = slice(cc * T, (cc + 1) * T)
            vnb = vnorm(rows).astype(BF16)
            for g in range(MLP_GROUPS):
                cols = slice(g * MLP_GROUP_DIM, (g + 1) * MLP_GROUP_DIM)
                sv = _dot(wm_ref[g], vnb[:, cols]) + bsx_ref[:, cols]
                y_ref[rows, cols] = (u_ref[rows, cols].astype(F32) * sv).astype(BF16)

    @pl.when(i >= NP_TILES)
    def _():
        B = DEC_BATCH
        for t in range(DEC_SEQ):
            rows = slice(t * B, (t + 1) * B)
            cv_ref[rows, :] = vnorm(rows)
        for t in range(DEC_SEQ):
            rows = slice(t * B, (t + 1) * B)
            acc = bsx4_ref[t:t + 1, :]
            for s in range(t + 1):
                acc = acc + wsx_ref[4 * t + s:4 * t + s + 1, :] * cv_ref[s * B:(s + 1) * B, :]
            y_ref[rows, :] = (u_ref[rows, :].astype(F32) * acc).astype(BF16)


def _mlp_call(proj, lnw, lnb, ws, bsx, wsx, bsx4):
    full = lambda shape: pl.BlockSpec(shape, lambda i: (0,) * len(shape))
    return pl.pallas_call(
        _mlp_kernel,
        grid=(N_TILES,),
        in_specs=[pl.BlockSpec((TM, D), lambda i: (i, PU)),
                  pl.BlockSpec((TM, D), lambda i: (i, PV)),
                  full((1, D)), full((1, D)),
                  full((MLP_GROUPS, CHUNK, CHUNK)),
                  full((CHUNK, D)), full((16, D)), full((8, D))],
        out_specs=[pl.BlockSpec((TM, D), lambda i: (i, 0)),
                   full((NS_ROWS, D))],
        out_shape=[jax.ShapeDtypeStruct((M_ROWS, D), BF16),
                   jax.ShapeDtypeStruct((NS_ROWS, D), F32)],
        scratch_shapes=[pltpu.VMEM((MLP_GROUPS, CHUNK, CHUNK), BF16)],
        compiler_params=_params("arbitrary"),
        name="gmlp",
    )(proj, proj, lnw, lnb, ws, bsx, wsx, bsx4)


def _cast_rows(src_ref, dst_ref, chunk=256):
    def body(r, carry):
        rows = pl.ds(pl.multiple_of(r * chunk, chunk), chunk)
        dst_ref[rows, :] = src_ref[rows, :].astype(dst_ref.dtype)
        return carry
    lax.fori_loop(0, src_ref.shape[0] // chunk, body, 0)


def _per_tile(i, prompt_fn, sample_fn):
    pl.when(i < NP_MT)(prompt_fn)
    pl.when(i == NP_MT)(sample_fn)


def _merge_kernel(ysp_ref, yss_ref, ym_ref, ga_ref, gb_ref, w1_ref, w2_ref, o_ref, w1b, w2b):
    i = pl.program_id(1)

    @pl.when(i == 0)
    def _():
        _cast_rows(w1_ref, w1b)
        _cast_rows(w2_ref, w2b)

    def emit(ys, rows):
        a1 = _dot(ys, w1b[...])
        a2 = _dot(ym_ref[rows, :], w2b[...])
        o_ref[rows, :] = (ga_ref[rows, :].astype(F32) * a1
                          + gb_ref[rows, :].astype(F32) * a2).astype(BF16)

    _per_tile(i,
              lambda: emit(ysp_ref[...], slice(None)),
              lambda: emit(yss_ref[...], slice(0, NS_ROWS)))


def _merge_call(ysp, yss, ym, proj, w1, w2):
    tn = 512
    nb = D // tn
    return pl.pallas_call(
        _merge_kernel,
        grid=(nb, N_MT),
        in_specs=[pl.BlockSpec((TMM, D), lambda j, i: (jnp.minimum(i, NP_MT - 1), 0)),
                  pl.BlockSpec((NS_ROWS, D), lambda j, i: (0, 0)),
                  pl.BlockSpec((TMM, D), lambda j, i: (i, 0)),
                  pl.BlockSpec((TMM, tn), lambda j, i: (i, PGA * nb + j)),
                  pl.BlockSpec((TMM, tn), lambda j, i: (i, PGB * nb + j)),
                  pl.BlockSpec((D, tn), lambda j, i: (0, j)),
                  pl.BlockSpec((D, tn), lambda j, i: (0, j))],
        out_specs=pl.BlockSpec((TMM, tn), lambda j, i: (i, j)),
        out_shape=jax.ShapeDtypeStruct((M_ROWS, D), BF16),
        scratch_shapes=[pltpu.VMEM((D, tn), BF16), pltpu.VMEM((D, tn), BF16)],
        compiler_params=_params("arbitrary", "arbitrary"),
        name="branch_merge",
    )(ysp, yss, ym, proj, proj, w1, w2)


def _resid_kernel(a_ref, w_ref, rp_ref, rs_ref, gp_ref, gs_ref, o_ref, wb):
    i = pl.program_id(1)

    @pl.when(i == 0)
    def _():
        _cast_rows(w_ref, wb)

    def prompt():
        o_ref[...] = rp_ref[...] + gp_ref[0] * _dot(a_ref[...], wb[...])

    def sample():
        acc = _dot(a_ref[0:NS_ROWS, :], wb[...])
        tn = acc.shape[-1]
        acc3 = acc.reshape(DEC_SEQ, DEC_BATCH, tn) * gs_ref[...][None]
        o_ref[0:NS_ROWS, :] = rs_ref[...] + acc3.reshape(NS_ROWS, tn)

    _per_tile(i, prompt, sample)


def _resid_call(a, w, rp, rs, rs_block, mod_p, mod_s, k_gate, tn, name, single_buffer_w=False):
    kdim = a.shape[1]
    nb = D // tn
    w_mode = dict(pipeline_mode=pl.Buffered(1)) if single_buffer_w else {}
    return pl.pallas_call(
        _resid_kernel,
        grid=(nb, N_MT),
        in_specs=[pl.BlockSpec((TMM, kdim), lambda j, i: (i, 0)),
                  pl.BlockSpec((kdim, tn), lambda j, i: (0, j), **w_mode),
                  pl.BlockSpec((TMM, tn), lambda j, i: (jnp.minimum(i, NP_MT - 1), j)),
                  pl.BlockSpec((NS_ROWS, tn), lambda j, i: (rs_block, j)),
                  pl.BlockSpec((1, 1, tn), lambda j, i: (_seq_of_mtile(i), 0, k_gate * nb + j)),
                  pl.BlockSpec((DEC_BATCH, tn), lambda j, i: (0, k_gate * nb + j))],
        out_specs=pl.BlockSpec((TMM, tn), lambda j, i: (i, j)),
        out_shape=jax.ShapeDtypeStruct((M_ROWS, D), F32),
        scratch_shapes=[pltpu.VMEM((kdim, tn), BF16)],
        compiler_params=_params("arbitrary", "arbitrary"),
        name=name,
    )(a, w, rp, rs, mod_p, mod_s)


TN_FF = 512
N_FF_BLOCKS = D_FF // TN_FF


def _up_kernel(a_ref, wa_ref, wv_ref, cw_ref, cb_ref, fst_ref,
               h_ref, fcp_ref, fcs_ref, wab, wvb, acc_a, acc_v):
    i = pl.program_id(1)
    rc = 64

    @pl.when(i == 0)
    def _():
        _cast_rows(wa_ref, wab)
        _cast_rows(wv_ref, wvb)

    def prompt():
        starts_sequence = i % MT_PER_SEQ == 0
        acc_a[0:8, :] = jnp.where(starts_sequence, 0.0, acc_a[TMM:TMM + 8, :])
        x = a_ref[...]
        acc_a[8:8 + TMM, :] = _dot(x, wab[...])
        acc_v[...] = _dot(x, wvb[...])
        cw, cb = cw_ref[...], cb_ref[...]
        for r in range(0, TMM, rc):
            conv = (cb + cw[2:3] * acc_a[8 + r:8 + r + rc, :]
                    + cw[1:2] * acc_a[7 + r:7 + r + rc, :]
                    + cw[0:1] * acc_a[6 + r:6 + r + rc, :])
            h_ref[r:r + rc, :] = (jax.nn.gelu(conv) * acc_v[r:r + rc, :]).astype(BF16)
        fcp_ref[0] = acc_a[TMM + 6:TMM + 8, :]

    def sample():
        B = DEC_BATCH
        x = a_ref[0:NS_ROWS, :]
        acc_a[8:8 + NS_ROWS, :] = _dot(x, wab[...])
        acc_v[0:NS_ROWS, :] = _dot(x, wvb[...])
        cw, cb = cw_ref[...], cb_ref[...]

        def pre(t, r):
            if t < 0:
                return fst_ref[t + 2, r:r + rc, :]
            return acc_a[8 + t * B + r:8 + t * B + r + rc, :]

        for t in range(DEC_SEQ):
            for r in range(0, B, rc):
                conv = cb + cw[2:3] * pre(t, r) + cw[1:2] * pre(t - 1, r) + cw[0:1] * pre(t - 2, r)
                h_ref[t * B + r:t * B + r + rc, :] = (
                    jax.nn.gelu(conv) * acc_v[t * B + r:t * B + r + rc, :]).astype(BF16)
        fcs_ref[0] = acc_a[8 + 2 * B:8 + 3 * B, :]
        fcs_ref[1] = acc_a[8 + 3 * B:8 + 4 * B, :]

    _per_tile(i, prompt, sample)


def _up_call(n2, w_up, cw, cb, fst_t):
    return pl.pallas_call(
        _up_kernel,
        grid=(N_FF_BLOCKS, N_MT),
        in_specs=[pl.BlockSpec((TMM, D), lambda j, i: (i, 0)),
                  pl.BlockSpec((D, TN_FF), lambda j, i: (0, j)),
                  pl.BlockSpec((D, TN_FF), lambda j, i: (0, N_FF_BLOCKS + j)),
                  pl.BlockSpec((3, TN_FF), lambda j, i: (0, j)),
                  pl.BlockSpec((1, TN_FF), lambda j, i: (0, j)),
                  pl.BlockSpec((2, DEC_BATCH, TN_FF), lambda j, i: (0, 0, j))],
        out_specs=[pl.BlockSpec((TMM, TN_FF), lambda j, i: (i, j)),
                   pl.BlockSpec((1, 2, TN_FF), lambda j, i: (_seq_of_mtile(i), 0, j)),
                   pl.BlockSpec((2, DEC_BATCH, TN_FF), lambda j, i: (0, 0, j))],
        out_shape=[jax.ShapeDtypeStruct((M_ROWS, D_FF), BF16),
                   jax.ShapeDtypeStruct((BATCH, 2, D_FF), F32),
                   jax.ShapeDtypeStruct((2, DEC_BATCH, D_FF), F32)],
        scratch_shapes=[pltpu.VMEM((D, TN_FF), BF16), pltpu.VMEM((D, TN_FF), BF16),
                        pltpu.VMEM((TMM + 8, TN_FF), F32), pltpu.VMEM((TMM, TN_FF), F32)],
        compiler_params=_params("arbitrary", "arbitrary"),
        name="ffn_up",
    )(n2, w_up, w_up, cw, cb, fst_t)


def _final_kernel(x_ref, w_ref, yp_ref, ys_ref):
    i = pl.program_id(0)
    y = _rms(x_ref[...]) * w_ref[...]

    @pl.when(i < NP_TILES)
    def _():
        yp_ref[...] = y

    @pl.when(i >= NP_TILES)
    def _():
        ys_ref[...] = y


def _final_call(x3, w):
    return pl.pallas_call(
        _final_kernel,
        grid=(N_TILES,),
        in_specs=[pl.BlockSpec((TM, D), lambda i: (i, 0)),
                  pl.BlockSpec((1, D), lambda i: (0, 0))],
        out_specs=[pl.BlockSpec((TM, D), lambda i: (jnp.minimum(i, NP_TILES - 1), 0)),
                   pl.BlockSpec((TM, D), lambda i: (0, 0))],
        out_shape=[jax.ShapeDtypeStruct((NP_ROWS, D), F32),
                   jax.ShapeDtypeStruct((NS_ROWS, D), F32)],
        compiler_params=_params("arbitrary"),
        name="final_norm",
    )(x3, w)


def _to_time_major(a):
    return jnp.transpose(a, (1, 0, 2))


def _pad_time(a_t, width):
    a_b = jnp.transpose(a_t.reshape(DEC_SEQ, DEC_BATCH, width), (1, 0, 2))
    return jnp.pad(a_b, ((0, 0), (0, T_PAD - DEC_SEQ), (0, 0)))


def kernel(x_prompt, x_sample, state_ssm, state_ssd_conv, state_ffn_conv, c_prompt, c_sample,
           norm1_w, w_ada, b_ada, w_in, ssd_conv_w, ssd_conv_b, dt_bias, a_log, d_skip,
           ssd_norm_w, mlp_ln_w, mlp_ln_b, w_spatial, b_spatial, w_ssd_o, w_mlp_o, w_out,
           norm2_w, w_up, ffn_conv_w, ffn_conv_b, w_down, final_norm_w):
    assert w_in.shape[0] == 1, "single-layer trunk"
    row = lambda v: v.reshape(1, -1)

    xp = x_prompt.reshape(NP_ROWS, D)
    xs = _to_time_major(x_sample).reshape(NS_ROWS, D)

    c_all = jnp.concatenate([c_prompt, c_sample, jnp.zeros((4, D), F32)], axis=0)
    mod = _ada_call(c_all, w_ada[0], row(b_ada[0]))
    mod_p = mod[:BATCH].reshape(BATCH, 1, 6 * D)
    mod_s = mod[BATCH:BATCH + DEC_BATCH]
    K_SHIFT1, K_SCALE1, K_GATE1, K_SHIFT2, K_SCALE2, K_GATE2 = range(6)

    xp_spec = pl.BlockSpec((TM, D), lambda i: (jnp.minimum(i, NP_TILES - 1), 0))
    xs_spec = pl.BlockSpec((TM, D), lambda i: (0, 0))
    w_in_t = w_in[0].T
    w_dt_t = w_in_t[DT_COL:DT_COL + LANE]
    n1, dt_raw = _norm_call(xp, xp_spec, xs, xs_spec, row(norm1_w[0]), mod_p, mod_s,
                            K_SCALE1, K_SHIFT1, w_dt_t)
    cst_t = _to_time_major(state_ssd_conv[0])
    proj, cst_p, ncs_t = _inproj_call(n1, w_in_t, ssd_conv_w[0], row(ssd_conv_b[0]), cst_t)

    hp = jnp.zeros((8, LANE), F32)
    hp = hp.at[0, :HEADS].set(dt_bias[0]).at[1, :HEADS].set(a_log[0]).at[2, :HEADS].set(d_skip[0])
    hpc = hp.T
    triu = jnp.triu(jnp.ones((CHUNK, CHUNK), F32)).astype(BF16)
    dskx = row(jnp.repeat(d_skip[0], HEADDIM))
    nw = row(ssd_norm_w[0])

    y_ssd_p, ssm_p = _ssd_prompt_call(proj, dt_raw, hp, hpc, dskx, nw, triu)

    head_of_col = jnp.arange(D) // HEADDIM
    ex = (jnp.arange(LANE)[:, None] == head_of_col[None, :]).astype(BF16)
    grp_of_head = jnp.arange(LANE) // (HEADS // GROUPS)
    seg = ((jnp.arange(GROUPS * STATE)[:, None] // STATE == grp_of_head[None, :])
           & (jnp.arange(LANE)[None, :] < HEADS)).astype(BF16)
    ypre, efull, xw_t, bb_t, cc_t, e3 = _ssds_prep_call(proj, dt_raw, hp, dskx, ex, seg)
    cdim = GROUPS * STATE
    gw = (HEADS // GROUPS) * HEADDIM
    yo_b, ssm_s = _ssds_state_call(
        e3[:, :HEADS], state_ssm[0].reshape(DEC_BATCH, GROUPS, gw, STATE),
        _pad_time(cc_t, cdim), _pad_time(bb_t, cdim), _pad_time(xw_t, D))
    yo_t = jnp.transpose(yo_b[:, :DEC_SEQ], (1, 0, 2)).reshape(NS_ROWS, D)
    y_ssd_s = _ssds_post_call(ypre, efull, yo_t, proj, nw)

    gidx = jnp.arange(D) // MLP_GROUP_DIM
    bsx = b_spatial[0][:, :CHUNK].T[:, gidx]
    ws4 = w_spatial[0][:, :DEC_SEQ, :DEC_SEQ]
    wsx = jnp.transpose(ws4, (1, 2, 0)).reshape(DEC_SEQ * DEC_SEQ, MLP_GROUPS)[:, gidx]
    bsx4 = jnp.pad(b_spatial[0][:, :DEC_SEQ].T[:, gidx], ((0, 8 - DEC_SEQ), (0, 0)))
    y_mlp, cv_t = _mlp_call(proj, row(mlp_ln_w[0]), row(mlp_ln_b[0]), w_spatial[0], bsx, wsx, bsx4)

    mixed = _merge_call(y_ssd_p, y_ssd_s, y_mlp, proj, w_ssd_o[0], w_mlp_o[0])
    x2 = _resid_call(mixed, w_out[0], xp, xs, 0, mod_p, mod_s, K_GATE1, 1024, "out_proj")

    x2p_spec = pl.BlockSpec((TM, D), lambda i: (jnp.minimum(i, NP_TILES - 1), 0))
    x2s_spec = pl.BlockSpec((TM, D), lambda i: (NP_TILES, 0))
    (n2,) = _norm_call(x2, x2p_spec, x2, x2s_spec, row(norm2_w[0]), mod_p, mod_s,
                       K_SCALE2, K_SHIFT2, None)
    fst_t = _to_time_major(state_ffn_conv[0])
    h, ffn_p, ffn_s_t = _up_call(n2, w_up[0], ffn_conv_w[0], row(ffn_conv_b[0]), fst_t)
    x3 = _resid_call(h, w_down[0], x2, x2, NP_TILES, mod_p, mod_s, K_GATE2, 512, "ffn_down",
                     single_buffer_w=True)

    y_p, y_s_t = _final_call(x3, row(final_norm_w))

    from_t = lambda a, t, c: jnp.transpose(a.reshape(t, DEC_BATCH, c), (1, 0, 2))
    return (y_p.reshape(BATCH, SEQ, D),
            from_t(y_s_t, DEC_SEQ, D),
            ssm_p.reshape(1, BATCH, HEADS, HEADDIM, STATE),
            ssm_s.reshape(1, DEC_BATCH, HEADS, HEADDIM, STATE),
            cst_p[None],
            from_t(ncs_t, 3, CONV_DIM)[None],
            ffn_p[None],
            from_t(ffn_s_t, 2, D_FF)[None],
            from_t(cv_t, DEC_SEQ, D)[None])
```

```python
import functools

import jax
import jax.numpy as jnp
from jax import lax
from jax.experimental import pallas as pl
from jax.experimental.pallas import tpu as pltpu

F32 = jnp.float32
BF16 = jnp.bfloat16

D = 2048
BATCH, SEQ = 4, 2048
DEC_BATCH, DEC_SEQ = 128, 4
NP_ROWS = BATCH * SEQ
NS_ROWS = DEC_BATCH * DEC_SEQ
M_ROWS = NP_ROWS + NS_ROWS
TM = 512
NP_TILES = NP_ROWS // TM
N_TILES = M_ROWS // TM
TILES_PER_SEQ = SEQ // TM
TMM = 1024
NP_MT = NP_ROWS // TMM
N_MT = NP_MT + 1
MT_PER_SEQ = SEQ // TMM
HEADS, HEADDIM, GROUPS, STATE = 32, 64, 4, 128
CHUNK = 128
N_CHUNKS = SEQ // CHUNK
CONV_DIM = D + 2 * GROUPS * STATE
MLP_GROUPS = 8
MLP_GROUP_DIM = D // MLP_GROUPS
D_FF = 5632
EPS = 1e-6
DT_COL = D + CONV_DIM
TN_IN = 1024
LANE = 128
VMEM_LIMIT = 56 * 1024 * 1024


def _params(*sem, flags=None):
    return pltpu.CompilerParams(dimension_semantics=sem, vmem_limit_bytes=VMEM_LIMIT, flags=flags)


def _dot(a, b):
    return jnp.dot(a, b, preferred_element_type=F32)


def _split_bf16(v, terms):
    out = []
    r = v
    for _ in range(terms):
        p = r.astype(BF16)
        out.append(p)
        r = r - p.astype(F32)
    return out


def _dot_exact_rhs(v, e, terms=3):
    acc = None
    for p in _split_bf16(v, terms):
        d = _dot(p, e)
        acc = d if acc is None else acc + d
    return acc


def _dot_exact_lhs(t, v, terms=3):
    acc = None
    for p in _split_bf16(v, terms):
        d = _dot(t, p)
        acc = d if acc is None else acc + d
    return acc


def _silu(x):
    return x * jax.nn.sigmoid(x)


def _softplus(x):
    return jnp.maximum(x, 0.0) + jnp.log1p(jnp.exp(-jnp.abs(x)))


def _rms(x):
    return x * lax.rsqrt(jnp.mean(x * x, axis=-1, keepdims=True) + EPS)


def _ada_kernel(cp_ref, cs_ref, w_ref, b_ref, op_ref, os_ref):
    w = w_ref[...].astype(BF16)
    cp8 = jnp.concatenate([cp_ref[...], jnp.zeros((8 - BATCH, D), F32)], axis=0)
    op_ref[...] = (_dot(_silu(cp8).astype(BF16), w) + b_ref[...])[0:BATCH]
    os_ref[...] = _dot(_silu(cs_ref[...]).astype(BF16), w) + b_ref[...]


def _ada_call(c_prompt, c_sample, w, b):
    tn = 1024
    return pl.pallas_call(
        _ada_kernel,
        grid=(6 * D // tn,),
        in_specs=[pl.BlockSpec((BATCH, D), lambda j: (0, 0)),
                  pl.BlockSpec((DEC_BATCH, D), lambda j: (0, 0)),
                  pl.BlockSpec((D, tn), lambda j: (0, j)),
                  pl.BlockSpec((1, tn), lambda j: (0, j))],
        out_specs=[pl.BlockSpec((BATCH, tn), lambda j: (0, j)),
                   pl.BlockSpec((DEC_BATCH, tn), lambda j: (0, j))],
        out_shape=[jax.ShapeDtypeStruct((BATCH, 6 * D), F32),
                   jax.ShapeDtypeStruct((DEC_BATCH, 6 * D), F32)],
        compiler_params=_params("arbitrary"),
        name="ada_mod",
    )(c_prompt, c_sample, w, b)


def _norm_kernel(with_dt, xp_ref, xs_ref, nw_ref, scp_ref, shp_ref, scs_ref, shs_ref, *rest):
    if with_dt:
        wdt_ref, n_ref, dt_ref = rest
    else:
        (n_ref,) = rest
    i = pl.program_id(0)

    def emit(n):
        nb = n.astype(BF16)
        n_ref[...] = nb
        if with_dt:
            dt_ref[...] = lax.dot_general(nb, wdt_ref[...].astype(BF16),
                                          (((1,), (1,)), ((), ())),
                                          preferred_element_type=F32)

    @pl.when(i < NP_TILES)
    def _():
        y = _rms(xp_ref[...]) * nw_ref[...]
        emit(y * (1.0 + scp_ref[0]) + shp_ref[0])

    @pl.when(i >= NP_TILES)
    def _():
        y = _rms(xs_ref[...]) * nw_ref[...]
        y3 = y.reshape(DEC_SEQ, DEC_BATCH, D)
        emit((y3 * (1.0 + scs_ref[...])[None] + shs_ref[...][None]).reshape(TM, D))


def _seq_of_tile(i):
    return jnp.minimum(i // TILES_PER_SEQ, BATCH - 1)


def _seq_of_mtile(i):
    return jnp.minimum(i // MT_PER_SEQ, BATCH - 1)


def _norm_call(xp, xp_spec, xs, xs_spec, nw, mod_p, mod_s, k_scale, k_shift, w_dt):
    with_dt = w_dt is not None
    in_specs = [
        xp_spec, xs_spec,
        pl.BlockSpec((1, D), lambda i: (0, 0)),
        pl.BlockSpec((1, 1, D), lambda i: (_seq_of_tile(i), 0, k_scale)),
        pl.BlockSpec((1, 1, D), lambda i: (_seq_of_tile(i), 0, k_shift)),
        pl.BlockSpec((DEC_BATCH, D), lambda i: (0, k_scale)),
        pl.BlockSpec((DEC_BATCH, D), lambda i: (0, k_shift)),
    ]
    args = [xp, xs, nw, mod_p, mod_p, mod_s, mod_s]
    out_specs = [pl.BlockSpec((TM, D), lambda i: (i, 0))]
    out_shape = [jax.ShapeDtypeStruct((M_ROWS, D), BF16)]
    if with_dt:
        in_specs.append(pl.BlockSpec((LANE, D), lambda i: (0, 0)))
        args.append(w_dt)
        out_specs.append(pl.BlockSpec((TM, LANE), lambda i: (i, 0)))
        out_shape.append(jax.ShapeDtypeStruct((M_ROWS, LANE), F32))
    return pl.pallas_call(
        functools.partial(_norm_kernel, with_dt),
        grid=(N_TILES,),
        in_specs=in_specs,
        out_specs=out_specs,
        out_shape=out_shape,
        compiler_params=_params("arbitrary"),
        name="norm_mod_dt" if with_dt else "norm_mod",
    )(*args)


N_IN_BLOCKS = 13
UVG_ROW = DT_COL + HEADS
FIRST_CONV_BLOCK = 10


def _in_src_row(j):
    row = jnp.where(j < 2, j * TN_IN,
                    jnp.where(j < FIRST_CONV_BLOCK, UVG_ROW + (j - 2) * TN_IN,
                              D + (j - FIRST_CONV_BLOCK) * TN_IN))
    return pl.multiple_of(row, HEADS)


def _in_conv_block(j):
    return jnp.maximum(j - FIRST_CONV_BLOCK, 0)


def _inproj_kernel(a_ref, wt_ref, cw_ref, cb_ref, cst_ref, o_ref, csp_ref, css_ref, wbf_ref, acc_ref):
    j = pl.program_id(0)
    i = pl.program_id(1)
    rc = 64
    B = DEC_BATCH

    @pl.when(i == 0)
    def _():
        for r in range(TN_IN // LANE):
            rows = slice(r * LANE, (r + 1) * LANE)
            wbf_ref[:, rows] = wt_ref[rows, :].T.astype(BF16)

    mc = 256

    def elementwise(fn, rows):
        def body():
            for m in range(0, rows, mc):
                acc_ref[8 + m:8 + m + mc, :] = _dot(a_ref[m:m + mc, :], wbf_ref[...])
                for r in range(m, m + mc, rc):
                    o_ref[r:r + rc, :] = fn(acc_ref[8 + r:8 + r + rc, :]).astype(o_ref.dtype)
        return body

    def conv_prompt():
        starts_sequence = i % MT_PER_SEQ == 0
        acc_ref[0:8, :] = jnp.where(starts_sequence, 0.0, acc_ref[TMM:TMM + 8, :])
        cw, cb = cw_ref[...], cb_ref[...]
        for m in range(0, TMM, mc):
            acc_ref[8 + m:8 + m + mc, :] = _dot(a_ref[m:m + mc, :], wbf_ref[...])
            for r in range(m, m + mc, rc):
                conv = cb + cw[3:4] * acc_ref[8 + r:8 + r + rc, :]
                for k in range(3):
                    conv = conv + cw[k:k + 1] * acc_ref[5 + k + r:5 + k + r + rc, :]
                o_ref[r:r + rc, :] = _silu(conv).astype(o_ref.dtype)
        csp_ref[0] = acc_ref[TMM + 5:TMM + 8, :]

    def conv_sample():
        acc_ref[8:8 + NS_ROWS, :] = _dot(a_ref[0:NS_ROWS, :], wbf_ref[...])
        cw, cb = cw_ref[...], cb_ref[...]

        def pre(t, r):
            if t < 0:
                return cst_ref[t + 3, r:r + rc, :]
            return acc_ref[8 + t * B + r:8 + t * B + r + rc, :]

        for t in range(DEC_SEQ):
            for r in range(0, B, rc):
                conv = cb + cw[3:4] * pre(t, r)
                for k in range(3):
                    conv = conv + cw[k:k + 1] * pre(t - 3 + k, r)
                o_ref[t * B + r:t * B + r + rc, :] = _silu(conv).astype(o_ref.dtype)
        for t in range(1, DEC_SEQ):
            css_ref[t - 1] = acc_ref[8 + t * B:8 + (t + 1) * B, :]

    is_conv = j >= FIRST_CONV_BLOCK
    kinds = ((j < 2, _silu),
             (jnp.logical_and(j >= 2, j < 6), jax.nn.gelu),
             (jnp.logical_and(j >= 6, j < FIRST_CONV_BLOCK), jax.nn.sigmoid))
    for cond, fn in kinds:
        pl.when(jnp.logical_and(cond, i < NP_MT))(elementwise(fn, TMM))
        pl.when(jnp.logical_and(cond, i == NP_MT))(elementwise(fn, NS_ROWS))
    pl.when(jnp.logical_and(is_conv, i < NP_MT))(conv_prompt)
    pl.when(jnp.logical_and(is_conv, i == NP_MT))(conv_sample)


def _inproj_call(n1, w_in_t, cw, cb, cst_t):
    cblk = _in_conv_block
    seq = lambda j, i: jnp.where(j < FIRST_CONV_BLOCK, 0, _seq_of_mtile(i))
    return pl.pallas_call(
        _inproj_kernel,
        grid=(N_IN_BLOCKS, N_MT),
        in_specs=[pl.BlockSpec((TMM, D), lambda j, i: (i, 0)),
                  pl.BlockSpec((pl.Element(TN_IN), pl.Element(D)),
                               lambda j, i: (_in_src_row(j), 0)),
                  pl.BlockSpec((4, TN_IN), lambda j, i: (0, cblk(j))),
                  pl.BlockSpec((1, TN_IN), lambda j, i: (0, cblk(j))),
                  pl.BlockSpec((3, DEC_BATCH, TN_IN), lambda j, i: (0, 0, cblk(j)))],
        out_specs=[pl.BlockSpec((TMM, TN_IN), lambda j, i: (i, j)),
                   pl.BlockSpec((1, 3, TN_IN), lambda j, i: (seq(j, i), 0, cblk(j))),
                   pl.BlockSpec((3, DEC_BATCH, TN_IN), lambda j, i: (0, 0, cblk(j)))],
        out_shape=[jax.ShapeDtypeStruct((M_ROWS, N_IN_BLOCKS * TN_IN), BF16),
                   jax.ShapeDtypeStruct((BATCH, 3, CONV_DIM), F32),
                   jax.ShapeDtypeStruct((3, DEC_BATCH, CONV_DIM), F32)],
        scratch_shapes=[pltpu.VMEM((D, TN_IN), BF16), pltpu.VMEM((TMM + 8, TN_IN), F32)],
        compiler_params=_params("arbitrary", "arbitrary"),
        name="in_proj",
    )(n1, w_in_t, cw, cb, cst_t)


PZ, PU, PV, PGA, PGB, PX = 0, 1, 2, 3, 4, 5
PBC_1024 = 12


def _gated_group_norm(get_y, zact_ref, nw_ref, o_ref):
    gw = D // GROUPS
    for g in range(GROUPS):
        cols = slice(g * gw, (g + 1) * gw)
        gg = _rms(get_y(cols) * zact_ref[:, cols].astype(F32))
        o_ref[:, cols] = (gg * nw_ref[:, cols]).astype(o_ref.dtype)


def _ssd_prompt_kernel(z_ref, x_ref, bc_ref, dt_ref, hp_ref, hpc_ref, dskx_ref, nw_ref, triu_ref,
                       y_ref, ssm_ref, st_ref, yscr_ref):
    c = pl.program_id(1)
    T = CHUNK
    cdim = GROUPS * STATE

    @pl.when(c == 0)
    def _():
        st_ref[...] = jnp.zeros(st_ref.shape, F32)

    hp = hp_ref[...]
    dt_t = _softplus((dt_ref[...] + hp[0:1, :]).T[0:HEADS, :])
    adt_t = dt_t * (-jnp.exp(hpc_ref[0:HEADS, 1:2]))
    cs_t = _dot_exact_rhs(adt_t, triu_ref[...])
    rsub_t = cs_t - jnp.log(dt_t)
    cs = jnp.concatenate([cs_t, jnp.zeros((LANE - HEADS, T), F32)], axis=0).T
    ecs = jnp.exp(cs)

    row = lax.broadcasted_iota(jnp.int32, (T, T), 0)
    lane = lax.broadcasted_iota(jnp.int32, (T, T), 1)
    causal = row >= lane
    left = lane < HEADDIM
    mask_l = jnp.where(left, 1.0, 0.0).astype(BF16)
    nt = (((1,), (1,)), ((), ()))
    pairs_per_group = HEADS // GROUPS // 2

    for g in range(GROUPS):
        c_b = bc_ref[:, cdim + g * STATE:cdim + (g + 1) * STATE]
        b_b = bc_ref[:, g * STATE:(g + 1) * STATE]
        cb = lax.dot_general(c_b, b_b, nt, preferred_element_type=F32)
        b_t = b_b.astype(F32).T
        for k4 in range(pairs_per_group):
            k = g * pairs_per_group + k4
            cols = slice(k * LANE, (k + 1) * LANE)
            xpb = x_ref[:, cols]
            x_lo = xpb * mask_l
            xbd = jnp.concatenate([x_lo, xpb - x_lo], axis=0)
            st = st_ref[k]
            yraw = _dot(c_b, st.astype(BF16))
            scores, bws = [], []
            for h in (2 * k, 2 * k + 1):
                decay_dt = jnp.exp(jnp.where(causal, cs[:, h:h + 1] - rsub_t[h:h + 1, :], -jnp.inf))
                scores.append((cb * decay_dt).astype(BF16))
                wrow = jnp.exp(cs_t[h:h + 1, T - 1:T] - cs_t[h:h + 1, :]) * dt_t[h:h + 1, :]
                bws.append((b_t * wrow).astype(BF16))
            h0, h1 = 2 * k, 2 * k + 1
            ecol = jnp.where(left, ecs[:, h0:h0 + 1], ecs[:, h1:h1 + 1])
            elast = jnp.where(left[0:1], ecs[T - 1:T, h0:h0 + 1], ecs[T - 1:T, h1:h1 + 1])
            yscr_ref[:, cols] = (_dot(jnp.concatenate(scores, axis=1), xbd) + ecol * yraw
                                 + dskx_ref[:, cols] * xpb.astype(F32))
            st_ref[k] = elast * st + _dot(jnp.concatenate(bws, axis=1), xbd)

    _gated_group_norm(lambda cols: yscr_ref[:, cols], z_ref, nw_ref, y_ref)

    @pl.when(c == N_CHUNKS - 1)
    def _():
        for k in range(HEADS // 2):
            ssm_ref[0, k] = st_ref[k].T


def _ssd_prompt_call(proj, dt_raw, hp, hpc, dskx, nw, triu):
    row = lambda b, c: b * N_CHUNKS + c
    const = lambda b, c: (0, 0)
    return pl.pallas_call(
        _ssd_prompt_kernel,
        grid=(BATCH, N_CHUNKS),
        in_specs=[pl.BlockSpec((CHUNK, D), lambda b, c: (row(b, c), PZ)),
                  pl.BlockSpec((CHUNK, D), lambda b, c: (row(b, c), PX)),
                  pl.BlockSpec((CHUNK, 2 * GROUPS * STATE), lambda b, c: (row(b, c), PBC_1024)),
                  pl.BlockSpec((CHUNK, LANE), lambda b, c: (row(b, c), 0)),
                  pl.BlockSpec((8, LANE), const),
                  pl.BlockSpec((LANE, 8), const),
                  pl.BlockSpec((1, D), const),
                  pl.BlockSpec((1, D), const),
                  pl.BlockSpec((CHUNK, CHUNK), const)],
        out_specs=[pl.BlockSpec((CHUNK, D), lambda b, c: (row(b, c), 0)),
                   pl.BlockSpec((1, HEADS // 2, LANE, STATE), lambda b, c: (b, 0, 0, 0))],
        out_shape=[jax.ShapeDtypeStruct((NP_ROWS, D), BF16),
                   jax.ShapeDtypeStruct((BATCH, HEADS // 2, LANE, STATE), F32)],
        scratch_shapes=[pltpu.VMEM((HEADS // 2, STATE, LANE), F32),
                        pltpu.VMEM((CHUNK, D), F32)],
        compiler_params=_params("arbitrary", "arbitrary"),
        name="ssd_prompt",
    )(proj, proj, proj, dt_raw, hp, hpc, dskx, nw, triu)


def _ssds_prep_step(tt, x_ref, bc_ref, dt_ref, hp_ref, dskx_ref, ex_ref, seg_ref,
                    ypre_ref, efull_ref, xw_ref, bb_ref, cc_ref, e3_ref):
    B = DEC_BATCH
    cdim = GROUPS * STATE
    blk = lambda t: slice(t * B, (t + 1) * B)
    x_of = lambda t: x_ref[blk(t), :].astype(F32)
    b_of = lambda t: bc_ref[blk(t), 0:cdim].astype(F32)
    c_of = lambda t: bc_ref[blk(t), cdim:2 * cdim].astype(F32)

    bb_ref[...] = b_of(tt)
    cc_ref[...] = c_of(tt)

    hp = hp_ref[...]
    a_neg = -jnp.exp(hp[1:2, :])
    dts, css = [], []
    run = None
    for t in range(DEC_SEQ):
        dt = _softplus(dt_ref[blk(t), :] + hp[0:1, :])
        run = dt * a_neg if run is None else run + dt * a_neg
        dts.append(dt)
        css.append(run)

    ex = ex_ref[...]
    e3_ref[...] = jnp.exp(css[-1])
    efull_ref[...] = _dot_exact_rhs(jnp.exp(css[tt]), ex)
    w_t = jnp.exp(css[-1] - css[tt]) * dts[tt]
    xw_ref[...] = x_of(tt) * _dot_exact_rhs(w_t, ex)

    seg = seg_ref[...]
    acc = dskx_ref[...] * x_of(tt)
    c_t = c_of(tt)
    for s in range(tt + 1):
        cbh = _dot_exact_rhs(c_t * b_of(s), seg)
        g_ts = cbh * jnp.exp(css[tt] - css[s]) * dts[s]
        acc = acc + _dot_exact_rhs(g_ts, ex) * x_of(s)
    ypre_ref[...] = acc


def _ssds_prep_kernel(*refs):
    t = pl.program_id(0)
    for tt in range(DEC_SEQ):
        pl.when(t == tt)(functools.partial(_ssds_prep_step, tt, *refs))


def _ssds_prep_call(proj, dt_raw, hp, dskx, ex, seg):
    full = lambda shape: pl.BlockSpec(shape, lambda t: (0,) * len(shape))
    step = lambda width: pl.BlockSpec((DEC_BATCH, width), lambda t: (t, 0))
    cdim = GROUPS * STATE
    return pl.pallas_call(
        _ssds_prep_kernel,
        grid=(DEC_SEQ,),
        in_specs=[pl.BlockSpec((NS_ROWS, D), lambda t: (NP_TILES, PX)),
                  pl.BlockSpec((NS_ROWS, 2 * cdim), lambda t: (NP_TILES, PBC_1024)),
                  pl.BlockSpec((NS_ROWS, LANE), lambda t: (NP_TILES, 0)),
                  full((8, LANE)), full((1, D)), full((LANE, D)), full((cdim, LANE))],
        out_specs=[step(D), step(D), step(D), step(cdim), step(cdim),
                   full((DEC_BATCH, LANE))],
        out_shape=[jax.ShapeDtypeStruct((NS_ROWS, D), F32),
                   jax.ShapeDtypeStruct((NS_ROWS, D), F32),
                   jax.ShapeDtypeStruct((NS_ROWS, D), F32),
                   jax.ShapeDtypeStruct((NS_ROWS, cdim), F32),
                   jax.ShapeDtypeStruct((NS_ROWS, cdim), F32),
                   jax.ShapeDtypeStruct((DEC_BATCH, LANE), F32)],
        compiler_params=_params("arbitrary"),
        name="ssd_sample_prep",
    )(proj, proj, dt_raw, hp, dskx, ex, seg)


SEQ_PER_STEP = 8


def _ssds_state_kernel(e3_ref, st_ref, cc_ref, bb_ref, xw_ref, yo_ref, so_ref):
    blk = pl.program_id(0)
    nt = (((1,), (1,)), ((), ()))
    tn = (((0,), (0,)), ((), ()))
    hpg = HEADS // GROUPS
    gw = hpg * HEADDIM

    def rows_of(ref, s, cols):
        v = ref[:, s, cols]
        return jnp.concatenate([v, jnp.zeros((8 - DEC_SEQ, v.shape[-1]), F32)], axis=0).astype(BF16)

    for s in range(SEQ_PER_STEP):
        b = blk * SEQ_PER_STEP + s
        for g in range(GROUPS):
            h0 = st_ref[s, g]
            c_g = rows_of(cc_ref, s, slice(g * STATE, (g + 1) * STATE))
            yraw = lax.dot_general(c_g, h0.astype(BF16), nt, preferred_element_type=F32)
            yo_ref[:, s, g * gw:(g + 1) * gw] = yraw[0:DEC_SEQ]
            x_g = rows_of(xw_ref, s, slice(g * gw, (g + 1) * gw))
            b_g = rows_of(bb_ref, s, slice(g * STATE, (g + 1) * STATE))
            dh = lax.dot_general(x_g, b_g, tn, preferred_element_type=F32)
            for hh in range(hpg):
                rows = slice(hh * HEADDIM, (hh + 1) * HEADDIM)
                so_ref[s, g, rows, :] = e3_ref[b, g * hpg + hh] * h0[rows] + dh[rows]


def _ssds_state_call(e3, state, cc_t, bb_t, xw_t):
    sb = SEQ_PER_STEP
    gw = (HEADS // GROUPS) * HEADDIM
    cdim = GROUPS * STATE
    tmajor = lambda width: pl.BlockSpec((DEC_SEQ, sb, width), lambda i: (0, i, 0))
    return pl.pallas_call(
        _ssds_state_kernel,
        grid=(DEC_BATCH // sb,),
        in_specs=[pl.BlockSpec(memory_space=pltpu.SMEM),
                  pl.BlockSpec((sb, GROUPS, gw, STATE), lambda i: (i, 0, 0, 0)),
                  tmajor(cdim), tmajor(cdim), tmajor(D)],
        out_specs=[tmajor(D),
                   pl.BlockSpec((sb, GROUPS, gw, STATE), lambda i: (i, 0, 0, 0))],
        out_shape=[jax.ShapeDtypeStruct((DEC_SEQ, DEC_BATCH, D), F32),
                   jax.ShapeDtypeStruct((DEC_BATCH, GROUPS, gw, STATE), F32)],
        compiler_params=_params("arbitrary"),
        name="ssd_sample_state",
    )(e3, state, cc_t.reshape(DEC_SEQ, DEC_BATCH, cdim), bb_t.reshape(DEC_SEQ, DEC_BATCH, cdim),
      xw_t.reshape(DEC_SEQ, DEC_BATCH, D))


def _ssds_post_kernel(ypre_ref, efull_ref, yo_ref, z_ref, nw_ref, o_ref):
    get_y = lambda cols: ypre_ref[:, cols] + efull_ref[:, cols] * yo_ref[:, cols]
    _gated_group_norm(get_y, z_ref, nw_ref, o_ref)


def _ssds_post_call(ypre, efull, yo_t, proj, nw):
    step = pl.BlockSpec((DEC_BATCH, D), lambda t: (t, 0))
    return pl.pallas_call(
        _ssds_post_kernel,
        grid=(DEC_SEQ,),
        in_specs=[step, step, step,
                  pl.BlockSpec((DEC_BATCH, D), lambda t: (NP_ROWS // DEC_BATCH + t, PZ)),
                  pl.BlockSpec((1, D), lambda t: (0, 0))],
        out_specs=step,
        out_shape=jax.ShapeDtypeStruct((NS_ROWS, D), BF16),
        compiler_params=_params("arbitrary"),
        name="ssd_sample_post",
    )(ypre, efull, yo_t, proj, nw)


def _mlp_kernel(u_ref, v_ref, lnw_ref, lnb_ref, ws_ref, bsx_ref, wsx_ref, bsx4_ref,
                y_ref, cv_ref, wm_ref):
    i = pl.program_id(0)
    T = CHUNK

    @pl.when(i == 0)
    def _():
        row = lax.broadcasted_iota(jnp.int32, (T, T), 0)
        lane = lax.broadcasted_iota(jnp.int32, (T, T), 1)
        for g in range(MLP_GROUPS):
            wm_ref[g] = jnp.where(row >= lane, ws_ref[g], 0.0).astype(BF16)

    def vnorm(rows):
        vg = v_ref[rows, :].astype(F32)
        xc = vg - jnp.mean(vg, axis=-1, keepdims=True)
        y = xc * lax.rsqrt(jnp.mean(xc * xc, axis=-1, keepdims=True) + EPS)
        return y * lnw_ref[...] + lnb_ref[...]

    @pl.when(i < NP_TILES)
    def _():
        for cc in range(TM // T):
            rows = slice(cc * T, (cc + 1) * T)
            vnb = vnorm(rows).astype(BF16)
            for g in range(MLP_GROUPS):
                cols = slice(g * MLP_GROUP_DIM, (g + 1) * MLP_GROUP_DIM)
                sv = _dot(wm_ref[g], vnb[:, cols]) + bsx_ref[:, cols]
                y_ref[rows, cols] = (u_ref[rows, cols].astype(F32) * sv).astype(BF16)

    @pl.when(i >= NP_TILES)
    def _():
        B = DEC_BATCH
        for t in range(DEC_SEQ):
            rows = slice(t * B, (t + 1) * B)
            cv_ref[rows, :] = vnorm(rows)
        for t in range(DEC_SEQ):
            rows = slice(t * B, (t + 1) * B)
            acc = bsx4_ref[t:t + 1, :]
            for s in range(t + 1):
                acc = acc + wsx_ref[4 * t + s:4 * t + s + 1, :] * cv_ref[s * B:(s + 1) * B, :]
            y_ref[rows, :] = (u_ref[rows, :].astype(F32) * acc).astype(BF16)


def _mlp_call(proj, lnw, lnb, ws, bsx, wsx, bsx4):
    full = lambda shape: pl.BlockSpec(shape, lambda i: (0,) * len(shape))
    return pl.pallas_call(
        _mlp_kernel,
        grid=(N_TILES,),
        in_specs=[pl.BlockSpec((TM, D), lambda i: (i, PU)),
                  pl.BlockSpec((TM, D), lambda i: (i, PV)),
                  full((1, D)), full((1, D)),
                  full((MLP_GROUPS, CHUNK, CHUNK)),
                  full((CHUNK, D)), full((16, D)), full((8, D))],
        out_specs=[pl.BlockSpec((TM, D), lambda i: (i, 0)),
                   full((NS_ROWS, D))],
        out_shape=[jax.ShapeDtypeStruct((M_ROWS, D), BF16),
                   jax.ShapeDtypeStruct((NS_ROWS, D), F32)],
        scratch_shapes=[pltpu.VMEM((MLP_GROUPS, CHUNK, CHUNK), BF16)],
        compiler_params=_params("arbitrary"),
        name="gmlp",
    )(proj, proj, lnw, lnb, ws, bsx, wsx, bsx4)


def _cast_rows(src_ref, dst_ref, chunk=256):
    def body(r, carry):
        rows = pl.ds(pl.multiple_of(r * chunk, chunk), chunk)
        dst_ref[rows, :] = src_ref[rows, :].astype(dst_ref.dtype)
        return carry
    lax.fori_loop(0, src_ref.shape[0] // chunk, body, 0)


def _per_tile(i, prompt_fn, sample_fn):
    pl.when(i < NP_MT)(prompt_fn)
    pl.when(i == NP_MT)(sample_fn)


def _merge_kernel(ysp_ref, yss_ref, ym_ref, ga_ref, gb_ref, w1_ref, w2_ref, o_ref, w1b, w2b):
    i = pl.program_id(1)

    @pl.when(i == 0)
    def _():
        _cast_rows(w1_ref, w1b)
        _cast_rows(w2_ref, w2b)

    def emit(ys, rows):
        a1 = _dot(ys, w1b[...])
        a2 = _dot(ym_ref[rows, :], w2b[...])
        o_ref[rows, :] = (ga_ref[rows, :].astype(F32) * a1
                          + gb_ref[rows, :].astype(F32) * a2).astype(BF16)

    _per_tile(i,
              lambda: emit(ysp_ref[...], slice(None)),
              lambda: emit(yss_ref[...], slice(0, NS_ROWS)))


def _merge_call(ysp, yss, ym, proj, w1, w2):
    tn = 512
    nb = D // tn
    return pl.pallas_call(
        _merge_kernel,
        grid=(nb, N_MT),
        in_specs=[pl.BlockSpec((TMM, D), lambda j, i: (jnp.minimum(i, NP_MT - 1), 0)),
                  pl.BlockSpec((NS_ROWS, D), lambda j, i: (0, 0)),
                  pl.BlockSpec((TMM, D), lambda j, i: (i, 0)),
                  pl.BlockSpec((TMM, tn), lambda j, i: (i, PGA * nb + j)),
                  pl.BlockSpec((TMM, tn), lambda j, i: (i, PGB * nb + j)),
                  pl.BlockSpec((D, tn), lambda j, i: (0, j)),
                  pl.BlockSpec((D, tn), lambda j, i: (0, j))],
        out_specs=pl.BlockSpec((TMM, tn), lambda j, i: (i, j)),
        out_shape=jax.ShapeDtypeStruct((M_ROWS, D), BF16),
        scratch_shapes=[pltpu.VMEM((D, tn), BF16), pltpu.VMEM((D, tn), BF16)],
        compiler_params=_params("arbitrary", "arbitrary"),
        name="branch_merge",
    )(ysp, yss, ym, proj, proj, w1, w2)


def _resid_kernel(a_ref, w_ref, rp_ref, rs_ref, gp_ref, gs_ref, o_ref, wb):
    i = pl.program_id(1)

    @pl.when(i == 0)
    def _():
        _cast_rows(w_ref, wb)

    def prompt():
        o_ref[...] = rp_ref[...] + gp_ref[0] * _dot(a_ref[...], wb[...])

    def sample():
        acc = _dot(a_ref[0:NS_ROWS, :], wb[...])
        tn = acc.shape[-1]
        acc3 = acc.reshape(DEC_SEQ, DEC_BATCH, tn) * gs_ref[...][None]
        o_ref[0:NS_ROWS, :] = rs_ref[...] + acc3.reshape(NS_ROWS, tn)

    _per_tile(i, prompt, sample)


def _resid_call(a, w, rp, rs, rs_block, mod_p, mod_s, k_gate, tn, name, single_buffer_w=False):
    kdim = a.shape[1]
    nb = D // tn
    w_mode = dict(pipeline_mode=pl.Buffered(1)) if single_buffer_w else {}
    return pl.pallas_call(
        _resid_kernel,
        grid=(nb, N_MT),
        in_specs=[pl.BlockSpec((TMM, kdim), lambda j, i: (i, 0)),
                  pl.BlockSpec((kdim, tn), lambda j, i: (0, j), **w_mode),
                  pl.BlockSpec((TMM, tn), lambda j, i: (jnp.minimum(i, NP_MT - 1), j)),
                  pl.BlockSpec((NS_ROWS, tn), lambda j, i: (rs_block, j)),
                  pl.BlockSpec((1, 1, tn), lambda j, i: (_seq_of_mtile(i), 0, k_gate * nb + j)),
                  pl.BlockSpec((DEC_BATCH, tn), lambda j, i: (0, k_gate * nb + j))],
        out_specs=pl.BlockSpec((TMM, tn), lambda j, i: (i, j)),
        out_shape=jax.ShapeDtypeStruct((M_ROWS, D), F32),
        scratch_shapes=[pltpu.VMEM((kdim, tn), BF16)],
        compiler_params=_params("arbitrary", "arbitrary"),
        name=name,
    )(a, w, rp, rs, mod_p, mod_s)


TN_FF = 512
N_FF_BLOCKS = D_FF // TN_FF


def _up_kernel(a_ref, wa_ref, wv_ref, cw_ref, cb_ref, fst_ref,
               h_ref, fcp_ref, fcs_ref, wab, wvb, acc_a, acc_v):
    i = pl.program_id(1)
    rc = 64

    @pl.when(i == 0)
    def _():
        _cast_rows(wa_ref, wab)
        _cast_rows(wv_ref, wvb)

    def prompt():
        starts_sequence = i % MT_PER_SEQ == 0
        acc_a[0:8, :] = jnp.where(starts_sequence, 0.0, acc_a[TMM:TMM + 8, :])
        x = a_ref[...]
        acc_a[8:8 + TMM, :] = _dot(x, wab[...])
        acc_v[...] = _dot(x, wvb[...])
        cw, cb = cw_ref[...], cb_ref[...]
        for r in range(0, TMM, rc):
            conv = (cb + cw[2:3] * acc_a[8 + r:8 + r + rc, :]
                    + cw[1:2] * acc_a[7 + r:7 + r + rc, :]
                    + cw[0:1] * acc_a[6 + r:6 + r + rc, :])
            h_ref[r:r + rc, :] = (jax.nn.gelu(conv) * acc_v[r:r + rc, :]).astype(BF16)
        fcp_ref[0] = acc_a[TMM + 6:TMM + 8, :]

    def sample():
        B = DEC_BATCH
        x = a_ref[0:NS_ROWS, :]
        acc_a[8:8 + NS_ROWS, :] = _dot(x, wab[...])
        acc_v[0:NS_ROWS, :] = _dot(x, wvb[...])
        cw, cb = cw_ref[...], cb_ref[...]

        def pre(t, r):
            if t < 0:
                return fst_ref[t + 2, r:r + rc, :]
            return acc_a[8 + t * B + r:8 + t * B + r + rc, :]

        for t in range(DEC_SEQ):
            for r in range(0, B, rc):
                conv = cb + cw[2:3] * pre(t, r) + cw[1:2] * pre(t - 1, r) + cw[0:1] * pre(t - 2, r)
                h_ref[t * B + r:t * B + r + rc, :] = (
                    jax.nn.gelu(conv) * acc_v[t * B + r:t * B + r + rc, :]).astype(BF16)
        fcs_ref[0] = acc_a[8 + 2 * B:8 + 3 * B, :]
        fcs_ref[1] = acc_a[8 + 3 * B:8 + 4 * B, :]

    _per_tile(i, prompt, sample)


def _up_call(n2, w_up, cw, cb, fst_t):
    return pl.pallas_call(
        _up_kernel,
        grid=(N_FF_BLOCKS, N_MT),
        in_specs=[pl.BlockSpec((TMM, D), lambda j, i: (i, 0)),
                  pl.BlockSpec((D, TN_FF), lambda j, i: (0, j)),
                  pl.BlockSpec((D, TN_FF), lambda j, i: (0, N_FF_BLOCKS + j)),
                  pl.BlockSpec((3, TN_FF), lambda j, i: (0, j)),
                  pl.BlockSpec((1, TN_FF), lambda j, i: (0, j)),
                  pl.BlockSpec((2, DEC_BATCH, TN_FF), lambda j, i: (0, 0, j))],
        out_specs=[pl.BlockSpec((TMM, TN_FF), lambda j, i: (i, j)),
                   pl.BlockSpec((1, 2, TN_FF), lambda j, i: (_seq_of_mtile(i), 0, j)),
                   pl.BlockSpec((2, DEC_BATCH, TN_FF), lambda j, i: (0, 0, j))],
        out_shape=[jax.ShapeDtypeStruct((M_ROWS, D_FF), BF16),
                   jax.ShapeDtypeStruct((BATCH, 2, D_FF), F32),
                   jax.ShapeDtypeStruct((2, DEC_BATCH, D_FF), F32)],
        scratch_shapes=[pltpu.VMEM((D, TN_FF), BF16), pltpu.VMEM((D, TN_FF), BF16),
                        pltpu.VMEM((TMM + 8, TN_FF), F32), pltpu.VMEM((TMM, TN_FF), F32)],
        compiler_params=_params("arbitrary", "arbitrary"),
        name="ffn_up",
    )(n2, w_up, w_up, cw, cb, fst_t)


def _final_kernel(x_ref, w_ref, yp_ref, ys_ref):
    i = pl.program_id(0)
    y = _rms(x_ref[...]) * w_ref[...]

    @pl.when(i < NP_TILES)
    def _():
        yp_ref[...] = y

    @pl.when(i >= NP_TILES)
    def _():
        ys_ref[...] = y


def _final_call(x3, w):
    return pl.pallas_call(
        _final_kernel,
        grid=(N_TILES,),
        in_specs=[pl.BlockSpec((TM, D), lambda i: (i, 0)),
                  pl.BlockSpec((1, D), lambda i: (0, 0))],
        out_specs=[pl.BlockSpec((TM, D), lambda i: (jnp.minimum(i, NP_TILES - 1), 0)),
                   pl.BlockSpec((TM, D), lambda i: (0, 0))],
        out_shape=[jax.ShapeDtypeStruct((NP_ROWS, D), F32),
                   jax.ShapeDtypeStruct((NS_ROWS, D), F32)],
        compiler_params=_params("arbitrary"),
        name="final_norm",
    )(x3, w)


def _to_time_major(a):
    return jnp.transpose(a, (1, 0, 2))


def kernel(x_prompt, x_sample, state_ssm, state_ssd_conv, state_ffn_conv, c_prompt, c_sample,
           norm1_w, w_ada, b_ada, w_in, ssd_conv_w, ssd_conv_b, dt_bias, a_log, d_skip,
           ssd_norm_w, mlp_ln_w, mlp_ln_b, w_spatial, b_spatial, w_ssd_o, w_mlp_o, w_out,
           norm2_w, w_up, ffn_conv_w, ffn_conv_b, w_down, final_norm_w):
    assert w_in.shape[0] == 1, "single-layer trunk"
    row = lambda v: v.reshape(1, -1)

    xp = x_prompt.reshape(NP_ROWS, D)
    xs = _to_time_major(x_sample).reshape(NS_ROWS, D)

    mod_p, mod_s = _ada_call(c_prompt, c_sample, w_ada[0], row(b_ada[0]))
    mod_p = mod_p.reshape(BATCH, 1, 6 * D)
    K_SHIFT1, K_SCALE1, K_GATE1, K_SHIFT2, K_SCALE2, K_GATE2 = range(6)

    xp_spec = pl.BlockSpec((TM, D), lambda i: (jnp.minimum(i, NP_TILES - 1), 0))
    xs_spec = pl.BlockSpec((TM, D), lambda i: (0, 0))
    w_in_t = w_in[0].T
    w_dt_t = w_in_t[DT_COL:DT_COL + LANE]
    n1, dt_raw = _norm_call(xp, xp_spec, xs, xs_spec, row(norm1_w[0]), mod_p, mod_s,
                            K_SCALE1, K_SHIFT1, w_dt_t)
    cst_t = _to_time_major(state_ssd_conv[0])
    proj, cst_p, ncs_t = _inproj_call(n1, w_in_t, ssd_conv_w[0], row(ssd_conv_b[0]), cst_t)

    hp = jnp.zeros((8, LANE), F32)
    hp = hp.at[0, :HEADS].set(dt_bias[0]).at[1, :HEADS].set(a_log[0]).at[2, :HEADS].set(d_skip[0])
    hpc = hp.T
    triu = jnp.triu(jnp.ones((CHUNK, CHUNK), F32)).astype(BF16)
    dskx = row(jnp.repeat(d_skip[0], HEADDIM))
    nw = row(ssd_norm_w[0])

    y_ssd_p, ssm_p = _ssd_prompt_call(proj, dt_raw, hp, hpc, dskx, nw, triu)

    head_of_col = jnp.arange(D) // HEADDIM
    ex = (jnp.arange(LANE)[:, None] == head_of_col[None, :]).astype(BF16)
    grp_of_head = jnp.arange(LANE) // (HEADS // GROUPS)
    seg = ((jnp.arange(GROUPS * STATE)[:, None] // STATE == grp_of_head[None, :])
           & (jnp.arange(LANE)[None, :] < HEADS)).astype(BF16)
    ypre, efull, xw_t, bb_t, cc_t, e3 = _ssds_prep_call(proj, dt_raw, hp, dskx, ex, seg)
    gw = (HEADS // GROUPS) * HEADDIM
    yo_t, ssm_s = _ssds_state_call(
        e3[:, :HEADS], state_ssm[0].reshape(DEC_BATCH, GROUPS, gw, STATE), cc_t, bb_t, xw_t)
    y_ssd_s = _ssds_post_call(ypre, efull, yo_t.reshape(NS_ROWS, D), proj, nw)

    per_col = lambda a: jnp.repeat(a, MLP_GROUP_DIM, axis=1)
    bsx = per_col(b_spatial[0][:, :CHUNK].T)
    ws4 = w_spatial[0][:, :DEC_SEQ, :DEC_SEQ]
    wsx = per_col(jnp.transpose(ws4, (1, 2, 0)).reshape(DEC_SEQ * DEC_SEQ, MLP_GROUPS))
    bsx4 = jnp.pad(bsx[:DEC_SEQ], ((0, 8 - DEC_SEQ), (0, 0)))
    y_mlp, cv_t = _mlp_call(proj, row(mlp_ln_w[0]), row(mlp_ln_b[0]), w_spatial[0], bsx, wsx, bsx4)

    mixed = _merge_call(y_ssd_p, y_ssd_s, y_mlp, proj, w_ssd_o[0], w_mlp_o[0])
    x2 = _resid_call(mixed, w_out[0], xp, xs, 0, mod_p, mod_s, K_GATE1, 1024, "out_proj")

    x2p_spec = pl.BlockSpec((TM, D), lambda i: (jnp.minimum(i, NP_TILES - 1), 0))
    x2s_spec = pl.BlockSpec((TM, D), lambda i: (NP_TILES, 0))
    (n2,) = _norm_call(x2, x2p_spec, x2, x2s_spec, row(norm2_w[0]), mod_p, mod_s,
                       K_SCALE2, K_SHIFT2, None)
    fst_t = _to_time_major(state_ffn_conv[0])
    h, ffn_p, ffn_s_t = _up_call(n2, w_up[0], ffn_conv_w[0], row(ffn_conv_b[0]), fst_t)
    x3 = _resid_call(h, w_down[0], x2, x2, NP_TILES, mod_p, mod_s, K_GATE2, 512, "ffn_down",
                     single_buffer_w=True)

    y_p, y_s_t = _final_call(x3, row(final_norm_w))

    from_t = lambda a, t, c: jnp.transpose(a.reshape(t, DEC_BATCH, c), (1, 0, 2))
    return (y_p.reshape(BATCH, SEQ, D),
            from_t(y_s_t, DEC_SEQ, D),
            ssm_p.reshape(1, BATCH, HEADS, HEADDIM, STATE),
            ssm_s.reshape(1, DEC_BATCH, HEADS, HEADDIM, STATE),
            cst_p[None],
            from_t(ncs_t, 3, CONV_DIM)[None],
            ffn_p[None],
            from_t(ffn_s_t, 2, D_FF)[None],
            from_t(cv_t, DEC_SEQ, D)[None])
```

```python
import functools

import jax
import jax.numpy as jnp
from jax import lax
from jax.experimental import pallas as pl
from jax.experimental.pallas import tpu as pltpu

F32 = jnp.float32
BF16 = jnp.bfloat16

D = 2048
BATCH, SEQ = 4, 2048
DEC_BATCH, DEC_SEQ = 128, 4
NP_ROWS = BATCH * SEQ
NS_ROWS = DEC_BATCH * DEC_SEQ
M_ROWS = NP_ROWS + NS_ROWS
TM = 512
NP_TILES = NP_ROWS // TM
N_TILES = M_ROWS // TM
TILES_PER_SEQ = SEQ // TM
TMM = 1024
NP_MT = NP_ROWS // TMM
N_MT = NP_MT + 1
MT_PER_SEQ = SEQ // TMM
HEADS, HEADDIM, GROUPS, STATE = 32, 64, 4, 128
CHUNK = 128
N_CHUNKS = SEQ // CHUNK
CONV_DIM = D + 2 * GROUPS * STATE
MLP_GROUPS = 8
MLP_GROUP_DIM = D // MLP_GROUPS
D_FF = 5632
EPS = 1e-6
DT_COL = D + CONV_DIM
TN_IN = 1024
LANE = 128
VMEM_LIMIT = 56 * 1024 * 1024


def _params(*sem, flags=None):
    return pltpu.CompilerParams(dimension_semantics=sem, vmem_limit_bytes=VMEM_LIMIT, flags=flags)


def _dot(a, b):
    return jnp.dot(a, b, preferred_element_type=F32)


def _split_bf16(v, terms):
    out = []
    r = v
    for _ in range(terms):
        p = r.astype(BF16)
        out.append(p)
        r = r - p.astype(F32)
    return out


def _dot_exact_rhs(v, e, terms=3):
    acc = None
    for p in _split_bf16(v, terms):
        d = _dot(p, e)
        acc = d if acc is None else acc + d
    return acc


def _dot_exact_lhs(t, v, terms=3):
    acc = None
    for p in _split_bf16(v, terms):
        d = _dot(t, p)
        acc = d if acc is None else acc + d
    return acc


def _silu(x):
    return x * jax.nn.sigmoid(x)


def _softplus(x):
    return jnp.maximum(x, 0.0) + jnp.log1p(jnp.exp(-jnp.abs(x)))


def _rms(x):
    return x * lax.rsqrt(jnp.mean(x * x, axis=-1, keepdims=True) + EPS)


def _ada_kernel(cp_ref, cs_ref, w_ref, b_ref, op_ref, os_ref):
    w = w_ref[...].astype(BF16)
    cp8 = jnp.concatenate([cp_ref[...], jnp.zeros((8 - BATCH, D), F32)], axis=0)
    op_ref[...] = (_dot(_silu(cp8).astype(BF16), w) + b_ref[...])[0:BATCH]
    os_ref[...] = _dot(_silu(cs_ref[...]).astype(BF16), w) + b_ref[...]


def _ada_call(c_prompt, c_sample, w, b):
    tn = 1024
    return pl.pallas_call(
        _ada_kernel,
        grid=(6 * D // tn,),
        in_specs=[pl.BlockSpec((BATCH, D), lambda j: (0, 0)),
                  pl.BlockSpec((DEC_BATCH, D), lambda j: (0, 0)),
                  pl.BlockSpec((D, tn), lambda j: (0, j)),
                  pl.BlockSpec((1, tn), lambda j: (0, j))],
        out_specs=[pl.BlockSpec((BATCH, tn), lambda j: (0, j)),
                   pl.BlockSpec((DEC_BATCH, tn), lambda j: (0, j))],
        out_shape=[jax.ShapeDtypeStruct((BATCH, 6 * D), F32),
                   jax.ShapeDtypeStruct((DEC_BATCH, 6 * D), F32)],
        compiler_params=_params("arbitrary"),
        name="ada_mod",
    )(c_prompt, c_sample, w, b)


def _norm_kernel(with_dt, xp_ref, xs_ref, nw_ref, scp_ref, shp_ref, scs_ref, shs_ref, *rest):
    if with_dt:
        wdt_ref, n_ref, dt_ref = rest
    else:
        (n_ref,) = rest
    i = pl.program_id(0)

    def emit(n):
        nb = n.astype(BF16)
        n_ref[...] = nb
        if with_dt:
            dt_ref[...] = lax.dot_general(nb, wdt_ref[...].astype(BF16),
                                          (((1,), (1,)), ((), ())),
                                          preferred_element_type=F32)

    @pl.when(i < NP_TILES)
    def _():
        y = _rms(xp_ref[...]) * nw_ref[...]
        emit(y * (1.0 + scp_ref[0]) + shp_ref[0])

    @pl.when(i >= NP_TILES)
    def _():
        y = _rms(xs_ref[...]) * nw_ref[...]
        y3 = y.reshape(DEC_SEQ, DEC_BATCH, D)
        emit((y3 * (1.0 + scs_ref[...])[None] + shs_ref[...][None]).reshape(TM, D))


def _seq_of_tile(i):
    return jnp.minimum(i // TILES_PER_SEQ, BATCH - 1)


def _seq_of_mtile(i):
    return jnp.minimum(i // MT_PER_SEQ, BATCH - 1)


def _norm_call(xp, xp_spec, xs, xs_spec, nw, mod_p, mod_s, k_scale, k_shift, w_dt):
    with_dt = w_dt is not None
    in_specs = [
        xp_spec, xs_spec,
        pl.BlockSpec((1, D), lambda i: (0, 0)),
        pl.BlockSpec((1, 1, D), lambda i: (_seq_of_tile(i), 0, k_scale)),
        pl.BlockSpec((1, 1, D), lambda i: (_seq_of_tile(i), 0, k_shift)),
        pl.BlockSpec((DEC_BATCH, D), lambda i: (0, k_scale)),
        pl.BlockSpec((DEC_BATCH, D), lambda i: (0, k_shift)),
    ]
    args = [xp, xs, nw, mod_p, mod_p, mod_s, mod_s]
    out_specs = [pl.BlockSpec((TM, D), lambda i: (i, 0))]
    out_shape = [jax.ShapeDtypeStruct((M_ROWS, D), BF16)]
    if with_dt:
        in_specs.append(pl.BlockSpec((LANE, D), lambda i: (0, 0)))
        args.append(w_dt)
        out_specs.append(pl.BlockSpec((TM, LANE), lambda i: (i, 0)))
        out_shape.append(jax.ShapeDtypeStruct((M_ROWS, LANE), F32))
    return pl.pallas_call(
        functools.partial(_norm_kernel, with_dt),
        grid=(N_TILES,),
        in_specs=in_specs,
        out_specs=out_specs,
        out_shape=out_shape,
        compiler_params=_params("arbitrary"),
        name="norm_mod_dt" if with_dt else "norm_mod",
    )(*args)


N_IN_BLOCKS = 13
UVG_ROW = DT_COL + HEADS
FIRST_CONV_BLOCK = 10


def _in_src_row(j):
    row = jnp.where(j < 2, j * TN_IN,
                    jnp.where(j < FIRST_CONV_BLOCK, UVG_ROW + (j - 2) * TN_IN,
                              D + (j - FIRST_CONV_BLOCK) * TN_IN))
    return pl.multiple_of(row, HEADS)


def _in_conv_block(j):
    return jnp.maximum(j - FIRST_CONV_BLOCK, 0)


def _inproj_kernel(a_ref, wt_ref, cw_ref, cb_ref, cst_ref, o_ref, csp_ref, css_ref, wbf_ref, acc_ref):
    j = pl.program_id(0)
    i = pl.program_id(1)
    rc = 64
    B = DEC_BATCH

    @pl.when(i == 0)
    def _():
        for r in range(TN_IN // LANE):
            rows = slice(r * LANE, (r + 1) * LANE)
            wbf_ref[:, rows] = wt_ref[rows, :].T.astype(BF16)

    def elementwise(fn, rows):
        def body():
            acc_ref[8:8 + rows, :] = _dot(a_ref[0:rows, :], wbf_ref[...])
            for r in range(0, rows, rc):
                val = acc_ref[8 + r:8 + r + rc, :]
                o_ref[r:r + rc, :] = (val if fn is None else fn(val)).astype(o_ref.dtype)
        return body

    def conv_prompt():
        starts_sequence = i % MT_PER_SEQ == 0
        acc_ref[0:8, :] = jnp.where(starts_sequence, 0.0, acc_ref[TMM:TMM + 8, :])
        acc_ref[8:8 + TMM, :] = _dot(a_ref[...], wbf_ref[...])
        cw, cb = cw_ref[...], cb_ref[...]
        for r in range(0, TMM, rc):
            conv = cb + cw[3:4] * acc_ref[8 + r:8 + r + rc, :]
            for k in range(3):
                conv = conv + cw[k:k + 1] * acc_ref[5 + k + r:5 + k + r + rc, :]
            o_ref[r:r + rc, :] = _silu(conv).astype(o_ref.dtype)
        csp_ref[0] = acc_ref[TMM + 5:TMM + 8, :]

    def conv_sample():
        acc_ref[8:8 + NS_ROWS, :] = _dot(a_ref[0:NS_ROWS, :], wbf_ref[...])
        cw, cb = cw_ref[...], cb_ref[...]

        def pre(t, r):
            if t < 0:
                return cst_ref[t + 3, r:r + rc, :]
            return acc_ref[8 + t * B + r:8 + t * B + r + rc, :]

        for t in range(DEC_SEQ):
            for r in range(0, B, rc):
                conv = cb + cw[3:4] * pre(t, r)
                for k in range(3):
                    conv = conv + cw[k:k + 1] * pre(t - 3 + k, r)
                o_ref[t * B + r:t * B + r + rc, :] = _silu(conv).astype(o_ref.dtype)
        for t in range(1, DEC_SEQ):
            css_ref[t - 1] = acc_ref[8 + t * B:8 + (t + 1) * B, :]

    is_conv = j >= FIRST_CONV_BLOCK
    kinds = ((j < 2, _silu),
             (jnp.logical_and(j >= 2, j < 6), jax.nn.gelu),
             (jnp.logical_and(j >= 6, j < FIRST_CONV_BLOCK), None))
    for cond, fn in kinds:
        pl.when(jnp.logical_and(cond, i < NP_MT))(elementwise(fn, TMM))
        pl.when(jnp.logical_and(cond, i == NP_MT))(elementwise(fn, NS_ROWS))
    pl.when(jnp.logical_and(is_conv, i < NP_MT))(conv_prompt)
    pl.when(jnp.logical_and(is_conv, i == NP_MT))(conv_sample)


def _inproj_call(n1, w_in_t, cw, cb, cst_t):
    cblk = _in_conv_block
    seq = lambda j, i: jnp.where(j < FIRST_CONV_BLOCK, 0, _seq_of_mtile(i))
    return pl.pallas_call(
        _inproj_kernel,
        grid=(N_IN_BLOCKS, N_MT),
        in_specs=[pl.BlockSpec((TMM, D), lambda j, i: (i, 0)),
                  pl.BlockSpec((pl.Element(TN_IN), pl.Element(D)),
                               lambda j, i: (_in_src_row(j), 0)),
                  pl.BlockSpec((4, TN_IN), lambda j, i: (0, cblk(j))),
                  pl.BlockSpec((1, TN_IN), lambda j, i: (0, cblk(j))),
                  pl.BlockSpec((3, DEC_BATCH, TN_IN), lambda j, i: (0, 0, cblk(j)))],
        out_specs=[pl.BlockSpec((TMM, TN_IN), lambda j, i: (i, j)),
                   pl.BlockSpec((1, 3, TN_IN), lambda j, i: (seq(j, i), 0, cblk(j))),
                   pl.BlockSpec((3, DEC_BATCH, TN_IN), lambda j, i: (0, 0, cblk(j)))],
        out_shape=[jax.ShapeDtypeStruct((M_ROWS, N_IN_BLOCKS * TN_IN), BF16),
                   jax.ShapeDtypeStruct((BATCH, 3, CONV_DIM), F32),
                   jax.ShapeDtypeStruct((3, DEC_BATCH, CONV_DIM), F32)],
        scratch_shapes=[pltpu.VMEM((D, TN_IN), BF16), pltpu.VMEM((TMM + 8, TN_IN), F32)],
        compiler_params=_params("arbitrary", "arbitrary"),
        name="in_proj",
    )(n1, w_in_t, cw, cb, cst_t)


PZ, PU, PV, PGA, PGB, PX = 0, 1, 2, 3, 4, 5
PBC_1024 = 12


def _gated_group_norm(get_y, zact_ref, nw_ref, o_ref):
    gw = D // GROUPS
    for g in range(GROUPS):
        cols = slice(g * gw, (g + 1) * gw)
        gg = _rms(get_y(cols) * zact_ref[:, cols].astype(F32))
        o_ref[:, cols] = (gg * nw_ref[:, cols]).astype(o_ref.dtype)


def _ssd_prompt_kernel(z_ref, x_ref, bc_ref, dt_ref, hp_ref, hpc_ref, dskx_ref, nw_ref, triu_ref,
                       y_ref, ssm_ref, st_ref, yscr_ref):
    c = pl.program_id(1)
    T = CHUNK
    cdim = GROUPS * STATE

    @pl.when(c == 0)
    def _():
        st_ref[...] = jnp.zeros(st_ref.shape, F32)

    hp = hp_ref[...]
    dt_t = _softplus((dt_ref[...] + hp[0:1, :]).T[0:HEADS, :])
    adt_t = dt_t * (-jnp.exp(hpc_ref[0:HEADS, 1:2]))
    cs_t = _dot_exact_rhs(adt_t, triu_ref[...])
    rsub_t = cs_t - jnp.log(dt_t)
    cs = jnp.concatenate([cs_t, jnp.zeros((LANE - HEADS, T), F32)], axis=0).T
    ecs = jnp.exp(cs)

    row = lax.broadcasted_iota(jnp.int32, (T, T), 0)
    lane = lax.broadcasted_iota(jnp.int32, (T, T), 1)
    causal = row >= lane
    left = lane < HEADDIM
    mask_l = jnp.where(left, 1.0, 0.0).astype(BF16)
    nt = (((1,), (1,)), ((), ()))
    pairs_per_group = HEADS // GROUPS // 2

    for g in range(GROUPS):
        c_b = bc_ref[:, cdim + g * STATE:cdim + (g + 1) * STATE]
        b_b = bc_ref[:, g * STATE:(g + 1) * STATE]
        cb = lax.dot_general(c_b, b_b, nt, preferred_element_type=F32)
        b_t = b_b.astype(F32).T
        for k4 in range(pairs_per_group):
            k = g * pairs_per_group + k4
            cols = slice(k * LANE, (k + 1) * LANE)
            xpb = x_ref[:, cols]
            x_lo = xpb * mask_l
            xbd = jnp.concatenate([x_lo, xpb - x_lo], axis=0)
            st = st_ref[k]
            yraw = _dot(c_b, st.astype(BF16))
            scores, bws = [], []
            for h in (2 * k, 2 * k + 1):
                decay_dt = jnp.exp(jnp.where(causal, cs[:, h:h + 1] - rsub_t[h:h + 1, :], -jnp.inf))
                scores.append((cb * decay_dt).astype(BF16))
                wrow = jnp.exp(cs_t[h:h + 1, T - 1:T] - cs_t[h:h + 1, :]) * dt_t[h:h + 1, :]
                bws.append((b_t * wrow).astype(BF16))
            h0, h1 = 2 * k, 2 * k + 1
            ecol = jnp.where(left, ecs[:, h0:h0 + 1], ecs[:, h1:h1 + 1])
            elast = jnp.where(left[0:1], ecs[T - 1:T, h0:h0 + 1], ecs[T - 1:T, h1:h1 + 1])
            yscr_ref[:, cols] = (_dot(jnp.concatenate(scores, axis=1), xbd) + ecol * yraw
                                 + dskx_ref[:, cols] * xpb.astype(F32))
            st_ref[k] = elast * st + _dot(jnp.concatenate(bws, axis=1), xbd)

    _gated_group_norm(lambda cols: yscr_ref[:, cols], z_ref, nw_ref, y_ref)

    @pl.when(c == N_CHUNKS - 1)
    def _():
        for k in range(HEADS // 2):
            ssm_ref[0, k] = st_ref[k].T


def _ssd_prompt_call(proj, dt_raw, hp, hpc, dskx, nw, triu):
    row = lambda b, c: b * N_CHUNKS + c
    const = lambda b, c: (0, 0)
    return pl.pallas_call(
        _ssd_prompt_kernel,
        grid=(BATCH, N_CHUNKS),
        in_specs=[pl.BlockSpec((CHUNK, D), lambda b, c: (row(b, c), PZ)),
                  pl.BlockSpec((CHUNK, D), lambda b, c: (row(b, c), PX)),
                  pl.BlockSpec((CHUNK, 2 * GROUPS * STATE), lambda b, c: (row(b, c), PBC_1024)),
                  pl.BlockSpec((CHUNK, LANE), lambda b, c: (row(b, c), 0)),
                  pl.BlockSpec((8, LANE), const),
                  pl.BlockSpec((LANE, 8), const),
                  pl.BlockSpec((1, D), const),
                  pl.BlockSpec((1, D), const),
                  pl.BlockSpec((CHUNK, CHUNK), const)],
        out_specs=[pl.BlockSpec((CHUNK, D), lambda b, c: (row(b, c), 0)),
                   pl.BlockSpec((1, HEADS // 2, LANE, STATE), lambda b, c: (b, 0, 0, 0))],
        out_shape=[jax.ShapeDtypeStruct((NP_ROWS, D), BF16),
                   jax.ShapeDtypeStruct((BATCH, HEADS // 2, LANE, STATE), F32)],
        scratch_shapes=[pltpu.VMEM((HEADS // 2, STATE, LANE), F32),
                        pltpu.VMEM((CHUNK, D), F32)],
        compiler_params=_params("arbitrary", "arbitrary"),
        name="ssd_prompt",
    )(proj, proj, proj, dt_raw, hp, hpc, dskx, nw, triu)


def _ssds_prep_step(tt, x_ref, bc_ref, dt_ref, hp_ref, dskx_ref, ex_ref, seg_ref,
                    ypre_ref, efull_ref, xw_ref, bb_ref, cc_ref, e3_ref):
    B = DEC_BATCH
    cdim = GROUPS * STATE
    blk = lambda t: slice(t * B, (t + 1) * B)
    x_of = lambda t: x_ref[blk(t), :].astype(F32)
    b_of = lambda t: bc_ref[blk(t), 0:cdim].astype(F32)
    c_of = lambda t: bc_ref[blk(t), cdim:2 * cdim].astype(F32)

    bb_ref[...] = b_of(tt)
    cc_ref[...] = c_of(tt)

    hp = hp_ref[...]
    a_neg = -jnp.exp(hp[1:2, :])
    dts, css = [], []
    run = None
    for t in range(DEC_SEQ):
        dt = _softplus(dt_ref[blk(t), :] + hp[0:1, :])
        run = dt * a_neg if run is None else run + dt * a_neg
        dts.append(dt)
        css.append(run)

    ex = ex_ref[...]
    e3_ref[...] = jnp.exp(css[-1])
    efull_ref[...] = _dot_exact_rhs(jnp.exp(css[tt]), ex)
    w_t = jnp.exp(css[-1] - css[tt]) * dts[tt]
    xw_ref[...] = x_of(tt) * _dot_exact_rhs(w_t, ex)

    seg = seg_ref[...]
    acc = dskx_ref[...] * x_of(tt)
    c_t = c_of(tt)
    for s in range(tt + 1):
        cbh = _dot_exact_rhs(c_t * b_of(s), seg)
        g_ts = cbh * jnp.exp(css[tt] - css[s]) * dts[s]
        acc = acc + _dot_exact_rhs(g_ts, ex) * x_of(s)
    ypre_ref[...] = acc


def _ssds_prep_kernel(*refs):
    t = pl.program_id(0)
    for tt in range(DEC_SEQ):
        pl.when(t == tt)(functools.partial(_ssds_prep_step, tt, *refs))


def _ssds_prep_call(proj, dt_raw, hp, dskx, ex, seg):
    full = lambda shape: pl.BlockSpec(shape, lambda t: (0,) * len(shape))
    step = lambda width: pl.BlockSpec((DEC_BATCH, width), lambda t: (t, 0))
    cdim = GROUPS * STATE
    return pl.pallas_call(
        _ssds_prep_kernel,
        grid=(DEC_SEQ,),
        in_specs=[pl.BlockSpec((NS_ROWS, D), lambda t: (NP_TILES, PX)),
                  pl.BlockSpec((NS_ROWS, 2 * cdim), lambda t: (NP_TILES, PBC_1024)),
                  pl.BlockSpec((NS_ROWS, LANE), lambda t: (NP_TILES, 0)),
                  full((8, LANE)), full((1, D)), full((LANE, D)), full((cdim, LANE))],
        out_specs=[step(D), step(D), step(D), step(cdim), step(cdim),
                   full((DEC_BATCH, LANE))],
        out_shape=[jax.ShapeDtypeStruct((NS_ROWS, D), F32),
                   jax.ShapeDtypeStruct((NS_ROWS, D), F32),
                   jax.ShapeDtypeStruct((NS_ROWS, D), F32),
                   jax.ShapeDtypeStruct((NS_ROWS, cdim), F32),
                   jax.ShapeDtypeStruct((NS_ROWS, cdim), F32),
                   jax.ShapeDtypeStruct((DEC_BATCH, LANE), F32)],
        compiler_params=_params("arbitrary"),
        name="ssd_sample_prep",
    )(proj, proj, dt_raw, hp, dskx, ex, seg)


SEQ_PER_STEP = 8


def _ssds_state_kernel(e3_ref, st_ref, cc_ref, bb_ref, xw_ref, yo_ref, so_ref):
    blk = pl.program_id(0)
    nt = (((1,), (1,)), ((), ()))
    tn = (((0,), (0,)), ((), ()))
    hpg = HEADS // GROUPS
    gw = hpg * HEADDIM

    def rows_of(ref, s, cols):
        v = ref[:, s, cols]
        return jnp.concatenate([v, jnp.zeros((8 - DEC_SEQ, v.shape[-1]), F32)], axis=0).astype(BF16)

    for s in range(SEQ_PER_STEP):
        b = blk * SEQ_PER_STEP + s
        for g in range(GROUPS):
            h0 = st_ref[s, g]
            c_g = rows_of(cc_ref, s, slice(g * STATE, (g + 1) * STATE))
            yraw = lax.dot_general(c_g, h0.astype(BF16), nt, preferred_element_type=F32)
            yo_ref[:, s, g * gw:(g + 1) * gw] = yraw[0:DEC_SEQ]
            x_g = rows_of(xw_ref, s, slice(g * gw, (g + 1) * gw))
            b_g = rows_of(bb_ref, s, slice(g * STATE, (g + 1) * STATE))
            dh = lax.dot_general(x_g, b_g, tn, preferred_element_type=F32)
            for hh in range(hpg):
                rows = slice(hh * HEADDIM, (hh + 1) * HEADDIM)
                so_ref[s, g, rows, :] = e3_ref[b, g * hpg + hh] * h0[rows] + dh[rows]


def _ssds_state_call(e3, state, cc_t, bb_t, xw_t):
    sb = SEQ_PER_STEP
    gw = (HEADS // GROUPS) * HEADDIM
    cdim = GROUPS * STATE
    tmajor = lambda width: pl.BlockSpec((DEC_SEQ, sb, width), lambda i: (0, i, 0))
    return pl.pallas_call(
        _ssds_state_kernel,
        grid=(DEC_BATCH // sb,),
        in_specs=[pl.BlockSpec(memory_space=pltpu.SMEM),
                  pl.BlockSpec((sb, GROUPS, gw, STATE), lambda i: (i, 0, 0, 0)),
                  tmajor(cdim), tmajor(cdim), tmajor(D)],
        out_specs=[tmajor(D),
                   pl.BlockSpec((sb, GROUPS, gw, STATE), lambda i: (i, 0, 0, 0))],
        out_shape=[jax.ShapeDtypeStruct((DEC_SEQ, DEC_BATCH, D), F32),
                   jax.ShapeDtypeStruct((DEC_BATCH, GROUPS, gw, STATE), F32)],
        compiler_params=_params("arbitrary"),
        name="ssd_sample_state",
    )(e3, state, cc_t.reshape(DEC_SEQ, DEC_BATCH, cdim), bb_t.reshape(DEC_SEQ, DEC_BATCH, cdim),
      xw_t.reshape(DEC_SEQ, DEC_BATCH, D))


def _ssds_post_kernel(ypre_ref, efull_ref, yo_ref, z_ref, nw_ref, o_ref):
    get_y = lambda cols: ypre_ref[:, cols] + efull_ref[:, cols] * yo_ref[:, cols]
    _gated_group_norm(get_y, z_ref, nw_ref, o_ref)


def _ssds_post_call(ypre, efull, yo_t, proj, nw):
    step = pl.BlockSpec((DEC_BATCH, D), lambda t: (t, 0))
    return pl.pallas_call(
        _ssds_post_kernel,
        grid=(DEC_SEQ,),
        in_specs=[step, step, step,
                  pl.BlockSpec((DEC_BATCH, D), lambda t: (NP_ROWS // DEC_BATCH + t, PZ)),
                  pl.BlockSpec((1, D), lambda t: (0, 0))],
        out_specs=step,
        out_shape=jax.ShapeDtypeStruct((NS_ROWS, D), BF16),
        compiler_params=_params("arbitrary"),
        name="ssd_sample_post",
    )(ypre, efull, yo_t, proj, nw)


def _mlp_kernel(u_ref, v_ref, lnw_ref, lnb_ref, ws_ref, bsx_ref, wsx_ref, bsx4_ref,
                y_ref, cv_ref, wm_ref):
    i = pl.program_id(0)
    T = CHUNK

    @pl.when(i == 0)
    def _():
        row = lax.broadcasted_iota(jnp.int32, (T, T), 0)
        lane = lax.broadcasted_iota(jnp.int32, (T, T), 1)
        for g in range(MLP_GROUPS):
            wm_ref[g] = jnp.where(row >= lane, ws_ref[g], 0.0).astype(BF16)

    def vnorm(rows):
        vg = v_ref[rows, :].astype(F32)
        xc = vg - jnp.mean(vg, axis=-1, keepdims=True)
        y = xc * lax.rsqrt(jnp.mean(xc * xc, axis=-1, keepdims=True) + EPS)
        return y * lnw_ref[...] + lnb_ref[...]

    @pl.when(i < NP_TILES)
    def _():
        for cc in range(TM // T):
            rows = slice(cc * T, (cc + 1) * T)
            vnb = vnorm(rows).astype(BF16)
            for g in range(MLP_GROUPS):
                cols = slice(g * MLP_GROUP_DIM, (g + 1) * MLP_GROUP_DIM)
                sv = _dot(wm_ref[g], vnb[:, cols]) + bsx_ref[:, cols]
                y_ref[rows, cols] = (u_ref[rows, cols].astype(F32) * sv).astype(BF16)

    @pl.when(i >= NP_TILES)
    def _():
        B = DEC_BATCH
        for t in range(DEC_SEQ):
            rows = slice(t * B, (t + 1) * B)
            cv_ref[rows, :] = vnorm(rows)
        for t in range(DEC_SEQ):
            rows = slice(t * B, (t + 1) * B)
            acc = bsx4_ref[t:t + 1, :]
            for s in range(t + 1):
                acc = acc + wsx_ref[4 * t + s:4 * t + s + 1, :] * cv_ref[s * B:(s + 1) * B, :]
            y_ref[rows, :] = (u_ref[rows, :].astype(F32) * acc).astype(BF16)


def _mlp_call(proj, lnw, lnb, ws, bsx, wsx, bsx4):
    full = lambda shape: pl.BlockSpec(shape, lambda i: (0,) * len(shape))
    return pl.pallas_call(
        _mlp_kernel,
        grid=(N_TILES,),
        in_specs=[pl.BlockSpec((TM, D), lambda i: (i, PU)),
                  pl.BlockSpec((TM, D), lambda i: (i, PV)),
                  full((1, D)), full((1, D)),
                  full((MLP_GROUPS, CHUNK, CHUNK)),
                  full((CHUNK, D)), full((16, D)), full((8, D))],
        out_specs=[pl.BlockSpec((TM, D), lambda i: (i, 0)),
                   full((NS_ROWS, D))],
        out_shape=[jax.ShapeDtypeStruct((M_ROWS, D), BF16),
                   jax.ShapeDtypeStruct((NS_ROWS, D), F32)],
        scratch_shapes=[pltpu.VMEM((MLP_GROUPS, CHUNK, CHUNK), BF16)],
        compiler_params=_params("arbitrary"),
        name="gmlp",
    )(proj, proj, lnw, lnb, ws, bsx, wsx, bsx4)


def _cast_rows(src_ref, dst_ref, chunk=256):
    def body(r, carry):
        rows = pl.ds(pl.multiple_of(r * chunk, chunk), chunk)
        dst_ref[rows, :] = src_ref[rows, :].astype(dst_ref.dtype)
        return carry
    lax.fori_loop(0, src_ref.shape[0] // chunk, body, 0)


def _per_tile(i, prompt_fn, sample_fn):
    pl.when(i < NP_MT)(prompt_fn)
    pl.when(i == NP_MT)(sample_fn)


def _merge_kernel(ysp_ref, yss_ref, ym_ref, ga_ref, gb_ref, w1_ref, w2_ref, o_ref, w1b, w2b):
    i = pl.program_id(1)

    @pl.when(i == 0)
    def _():
        _cast_rows(w1_ref, w1b)
        _cast_rows(w2_ref, w2b)

    def emit(ys, rows):
        a1 = _dot(ys, w1b[...])
        a2 = _dot(ym_ref[rows, :], w2b[...])
        o_ref[rows, :] = (jax.nn.sigmoid(ga_ref[rows, :].astype(F32)) * a1
                          + jax.nn.sigmoid(gb_ref[rows, :].astype(F32)) * a2).astype(BF16)

    _per_tile(i,
              lambda: emit(ysp_ref[...], slice(None)),
              lambda: emit(yss_ref[...], slice(0, NS_ROWS)))


def _merge_call(ysp, yss, ym, proj, w1, w2):
    tn = 512
    nb = D // tn
    return pl.pallas_call(
        _merge_kernel,
        grid=(nb, N_MT),
        in_specs=[pl.BlockSpec((TMM, D), lambda j, i: (jnp.minimum(i, NP_MT - 1), 0)),
                  pl.BlockSpec((NS_ROWS, D), lambda j, i: (0, 0)),
                  pl.BlockSpec((TMM, D), lambda j, i: (i, 0)),
                  pl.BlockSpec((TMM, tn), lambda j, i: (i, PGA * nb + j)),
                  pl.BlockSpec((TMM, tn), lambda j, i: (i, PGB * nb + j)),
                  pl.BlockSpec((D, tn), lambda j, i: (0, j)),
                  pl.BlockSpec((D, tn), lambda j, i: (0, j))],
        out_specs=pl.BlockSpec((TMM, tn), lambda j, i: (i, j)),
        out_shape=jax.ShapeDtypeStruct((M_ROWS, D), BF16),
        scratch_shapes=[pltpu.VMEM((D, tn), BF16), pltpu.VMEM((D, tn), BF16)],
        compiler_params=_params("arbitrary", "arbitrary"),
        name="branch_merge",
    )(ysp, yss, ym, proj, proj, w1, w2)


def _resid_kernel(a_ref, w_ref, rp_ref, rs_ref, gp_ref, gs_ref, o_ref, wb):
    i = pl.program_id(1)

    @pl.when(i == 0)
    def _():
        _cast_rows(w_ref, wb)

    def prompt():
        o_ref[...] = rp_ref[...] + gp_ref[0] * _dot(a_ref[...], wb[...])

    def sample():
        acc = _dot(a_ref[0:NS_ROWS, :], wb[...])
        tn = acc.shape[-1]
        acc3 = acc.reshape(DEC_SEQ, DEC_BATCH, tn) * gs_ref[...][None]
        o_ref[0:NS_ROWS, :] = rs_ref[...] + acc3.reshape(NS_ROWS, tn)

    _per_tile(i, prompt, sample)


def _resid_call(a, w, rp, rs, rs_block, mod_p, mod_s, k_gate, tn, name, single_buffer_w=False):
    kdim = a.shape[1]
    nb = D // tn
    w_mode = dict(pipeline_mode=pl.Buffered(1)) if single_buffer_w else {}
    return pl.pallas_call(
        _resid_kernel,
        grid=(nb, N_MT),
        in_specs=[pl.BlockSpec((TMM, kdim), lambda j, i: (i, 0)),
                  pl.BlockSpec((kdim, tn), lambda j, i: (0, j), **w_mode),
                  pl.BlockSpec((TMM, tn), lambda j, i: (jnp.minimum(i, NP_MT - 1), j)),
                  pl.BlockSpec((NS_ROWS, tn), lambda j, i: (rs_block, j)),
                  pl.BlockSpec((1, 1, tn), lambda j, i: (_seq_of_mtile(i), 0, k_gate * nb + j)),
                  pl.BlockSpec((DEC_BATCH, tn), lambda j, i: (0, k_gate * nb + j))],
        out_specs=pl.BlockSpec((TMM, tn), lambda j, i: (i, j)),
        out_shape=jax.ShapeDtypeStruct((M_ROWS, D), F32),
        scratch_shapes=[pltpu.VMEM((kdim, tn), BF16)],
        compiler_params=_params("arbitrary", "arbitrary"),
        name=name,
    )(a, w, rp, rs, mod_p, mod_s)


TN_FF = 512
N_FF_BLOCKS = D_FF // TN_FF


def _up_kernel(a_ref, wa_ref, wv_ref, cw_ref, cb_ref, fst_ref,
               h_ref, fcp_ref, fcs_ref, wab, wvb, acc_a, acc_v):
    i = pl.program_id(1)
    rc = 64

    @pl.when(i == 0)
    def _():
        _cast_rows(wa_ref, wab)
        _cast_rows(wv_ref, wvb)

    def prompt():
        starts_sequence = i % MT_PER_SEQ == 0
        acc_a[0:8, :] = jnp.where(starts_sequence, 0.0, acc_a[TMM:TMM + 8, :])
        x = a_ref[...]
        acc_a[8:8 + TMM, :] = _dot(x, wab[...])
        acc_v[...] = _dot(x, wvb[...])
        cw, cb = cw_ref[...], cb_ref[...]
        for r in range(0, TMM, rc):
            conv = (cb + cw[2:3] * acc_a[8 + r:8 + r + rc, :]
                    + cw[1:2] * acc_a[7 + r:7 + r + rc, :]
                    + cw[0:1] * acc_a[6 + r:6 + r + rc, :])
            h_ref[r:r + rc, :] = (jax.nn.gelu(conv) * acc_v[r:r + rc, :]).astype(BF16)
        fcp_ref[0] = acc_a[TMM + 6:TMM + 8, :]

    def sample():
        B = DEC_BATCH
        x = a_ref[0:NS_ROWS, :]
        acc_a[8:8 + NS_ROWS, :] = _dot(x, wab[...])
        acc_v[0:NS_ROWS, :] = _dot(x, wvb[...])
        cw, cb = cw_ref[...], cb_ref[...]

        def pre(t, r):
            if t < 0:
                return fst_ref[t + 2, r:r + rc, :]
            return acc_a[8 + t * B + r:8 + t * B + r + rc, :]

        for t in range(DEC_SEQ):
            for r in range(0, B, rc):
                conv = cb + cw[2:3] * pre(t, r) + cw[1:2] * pre(t - 1, r) + cw[0:1] * pre(t - 2, r)
                h_ref[t * B + r:t * B + r + rc, :] = (
                    jax.nn.gelu(conv) * acc_v[t * B + r:t * B + r + rc, :]).astype(BF16)
        fcs_ref[0] = acc_a[8 + 2 * B:8 + 3 * B, :]
        fcs_ref[1] = acc_a[8 + 3 * B:8 + 4 * B, :]

    _per_tile(i, prompt, sample)


def _up_call(n2, w_up, cw, cb, fst_t):
    return pl.pallas_call(
        _up_kernel,
        grid=(N_FF_BLOCKS, N_MT),
        in_specs=[pl.BlockSpec((TMM, D), lambda j, i: (i, 0)),
                  pl.BlockSpec((D, TN_FF), lambda j, i: (0, j)),
                  pl.BlockSpec((D, TN_FF), lambda j, i: (0, N_FF_BLOCKS + j)),
                  pl.BlockSpec((3, TN_FF), lambda j, i: (0, j)),
                  pl.BlockSpec((1, TN_FF), lambda j, i: (0, j)),
                  pl.BlockSpec((2, DEC_BATCH, TN_FF), lambda j, i: (0, 0, j))],
        out_specs=[pl.BlockSpec((TMM, TN_FF), lambda j, i: (i, j)),
                   pl.BlockSpec((1, 2, TN_FF), lambda j, i: (_seq_of_mtile(i), 0, j)),
                   pl.BlockSpec((2, DEC_BATCH, TN_FF), lambda j, i: (0, 0, j))],
        out_shape=[jax.ShapeDtypeStruct((M_ROWS, D_FF), BF16),
                   jax.ShapeDtypeStruct((BATCH, 2, D_FF), F32),
                   jax.ShapeDtypeStruct((2, DEC_BATCH, D_FF), F32)],
        scratch_shapes=[pltpu.VMEM((D, TN_FF), BF16), pltpu.VMEM((D, TN_FF), BF16),
                        pltpu.VMEM((TMM + 8, TN_FF), F32), pltpu.VMEM((TMM, TN_FF), F32)],
        compiler_params=_params("arbitrary", "arbitrary"),
        name="ffn_up",
    )(n2, w_up, w_up, cw, cb, fst_t)


def _final_kernel(x_ref, w_ref, yp_ref, ys_ref):
    i = pl.program_id(0)
    y = _rms(x_ref[...]) * w_ref[...]

    @pl.when(i < NP_TILES)
    def _():
        yp_ref[...] = y

    @pl.when(i >= NP_TILES)
    def _():
        ys_ref[...] = y


def _final_call(x3, w):
    return pl.pallas_call(
        _final_kernel,
        grid=(N_TILES,),
        in_specs=[pl.BlockSpec((TM, D), lambda i: (i, 0)),
                  pl.BlockSpec((1, D), lambda i: (0, 0))],
        out_specs=[pl.BlockSpec((TM, D), lambda i: (jnp.minimum(i, NP_TILES - 1), 0)),
                   pl.BlockSpec((TM, D), lambda i: (0, 0))],
        out_shape=[jax.ShapeDtypeStruct((NP_ROWS, D), F32),
                   jax.ShapeDtypeStruct((NS_ROWS, D), F32)],
        compiler_params=_params("arbitrary"),
        name="final_norm",
    )(x3, w)


def _to_time_major(a):
    return jnp.transpose(a, (1, 0, 2))


def kernel(x_prompt, x_sample, state_ssm, state_ssd_conv, state_ffn_conv, c_prompt, c_sample,
           norm1_w, w_ada, b_ada, w_in, ssd_conv_w, ssd_conv_b, dt_bias, a_log, d_skip,
           ssd_norm_w, mlp_ln_w, mlp_ln_b, w_spatial, b_spatial, w_ssd_o, w_mlp_o, w_out,
           norm2_w, w_up, ffn_conv_w, ffn_conv_b, w_down, final_norm_w):
    assert w_in.shape[0] == 1, "single-layer trunk"
    row = lambda v: v.reshape(1, -1)

    xp = x_prompt.reshape(NP_ROWS, D)
    xs = _to_time_major(x_sample).reshape(NS_ROWS, D)

    mod_p, mod_s = _ada_call(c_prompt, c_sample, w_ada[0], row(b_ada[0]))
    mod_p = mod_p.reshape(BATCH, 1, 6 * D)
    K_SHIFT1, K_SCALE1, K_GATE1, K_SHIFT2, K_SCALE2, K_GATE2 = range(6)

    xp_spec = pl.BlockSpec((TM, D), lambda i: (jnp.minimum(i, NP_TILES - 1), 0))
    xs_spec = pl.BlockSpec((TM, D), lambda i: (0, 0))
    w_in_t = w_in[0].T
    w_dt_t = w_in_t[DT_COL:DT_COL + LANE]
    n1, dt_raw = _norm_call(xp, xp_spec, xs, xs_spec, row(norm1_w[0]), mod_p, mod_s,
                            K_SCALE1, K_SHIFT1, w_dt_t)
    cst_t = _to_time_major(state_ssd_conv[0])
    proj, cst_p, ncs_t = _inproj_call(n1, w_in_t, ssd_conv_w[0], row(ssd_conv_b[0]), cst_t)

    hp = jnp.zeros((8, LANE), F32)
    hp = hp.at[0, :HEADS].set(dt_bias[0]).at[1, :HEADS].set(a_log[0]).at[2, :HEADS].set(d_skip[0])
    hpc = hp.T
    triu = jnp.triu(jnp.ones((CHUNK, CHUNK), F32)).astype(BF16)
    dskx = row(jnp.repeat(d_skip[0], HEADDIM))
    nw = row(ssd_norm_w[0])

    y_ssd_p, ssm_p = _ssd_prompt_call(proj, dt_raw, hp, hpc, dskx, nw, triu)

    head_of_col = jnp.arange(D) // HEADDIM
    ex = (jnp.arange(LANE)[:, None] == head_of_col[None, :]).astype(BF16)
    grp_of_head = jnp.arange(LANE) // (HEADS // GROUPS)
    seg = ((jnp.arange(GROUPS * STATE)[:, None] // STATE == grp_of_head[None, :])
           & (jnp.arange(LANE)[None, :] < HEADS)).astype(BF16)
    ypre, efull, xw_t, bb_t, cc_t, e3 = _ssds_prep_call(proj, dt_raw, hp, dskx, ex, seg)
    gw = (HEADS // GROUPS) * HEADDIM
    yo_t, ssm_s = _ssds_state_call(
        e3[:, :HEADS], state_ssm[0].reshape(DEC_BATCH, GROUPS, gw, STATE), cc_t, bb_t, xw_t)
    y_ssd_s = _ssds_post_call(ypre, efull, yo_t.reshape(NS_ROWS, D), proj, nw)

    per_col = lambda a: jnp.repeat(a, MLP_GROUP_DIM, axis=1)
    bsx = per_col(b_spatial[0][:, :CHUNK].T)
    ws4 = w_spatial[0][:, :DEC_SEQ, :DEC_SEQ]
    wsx = per_col(jnp.transpose(ws4, (1, 2, 0)).reshape(DEC_SEQ * DEC_SEQ, MLP_GROUPS))
    bsx4 = jnp.pad(bsx[:DEC_SEQ], ((0, 8 - DEC_SEQ), (0, 0)))
    y_mlp, cv_t = _mlp_call(proj, row(mlp_ln_w[0]), row(mlp_ln_b[0]), w_spatial[0], bsx, wsx, bsx4)

    mixed = _merge_call(y_ssd_p, y_ssd_s, y_mlp, proj, w_ssd_o[0], w_mlp_o[0])
    x2 = _resid_call(mixed, w_out[0], xp, xs, 0, mod_p, mod_s, K_GATE1, 1024, "out_proj")

    x2p_spec = pl.BlockSpec((TM, D), lambda i: (jnp.minimum(i, NP_TILES - 1), 0))
    x2s_spec = pl.BlockSpec((TM, D), lambda i: (NP_TILES, 0))
    (n2,) = _norm_call(x2, x2p_spec, x2, x2s_spec, row(norm2_w[0]), mod_p, mod_s,
                       K_SCALE2, K_SHIFT2, None)
    fst_t = _to_time_major(state_ffn_conv[0])
    h, ffn_p, ffn_s_t = _up_call(n2, w_up[0], ffn_conv_w[0], row(ffn_conv_b[0]), fst_t)
    x3 = _resid_call(h, w_down[0], x2, x2, NP_TILES, mod_p, mod_s, K_GATE2, 512, "ffn_down",
                     single_buffer_w=True)

    y_p, y_s_t = _final_call(x3, row(final_norm_w))

    from_t = lambda a, t, c: jnp.transpose(a.reshape(t, DEC_BATCH, c), (1, 0, 2))
    return (y_p.reshape(BATCH, SEQ, D),
            from_t(y_s_t, DEC_SEQ, D),
            ssm_p.reshape(1, BATCH, HEADS, HEADDIM, STATE),
            ssm_s.reshape(1, DEC_BATCH, HEADS, HEADDIM, STATE),
            cst_p[None],
            from_t(ncs_t, 3, CONV_DIM)[None],
            ffn_p[None],
            from_t(ffn_s_t, 2, D_FF)[None],
            from_t(cv_t, DEC_SEQ, D)[None])
```

```python
import functools

import jax
import jax.numpy as jnp
from jax import lax
from jax.experimental import pallas as pl
from jax.experimental.pallas import tpu as pltpu

F32 = jnp.float32
BF16 = jnp.bfloat16

D = 2048
BATCH, SEQ = 4, 2048
DEC_BATCH, DEC_SEQ = 128, 4
NP_ROWS = BATCH * SEQ
NS_ROWS = DEC_BATCH * DEC_SEQ
M_ROWS = NP_ROWS + NS_ROWS
TM = 512
NP_TILES = NP_ROWS // TM
N_TILES = M_ROWS // TM
TILES_PER_SEQ = SEQ // TM
TMM = 1024
NP_MT = NP_ROWS // TMM
N_MT = NP_MT + 1
MT_PER_SEQ = SEQ // TMM
HEADS, HEADDIM, GROUPS, STATE = 32, 64, 4, 128
CHUNK = 128
N_CHUNKS = SEQ // CHUNK
CONV_DIM = D + 2 * GROUPS * STATE
MLP_GROUPS = 8
MLP_GROUP_DIM = D // MLP_GROUPS
D_FF = 5632
EPS = 1e-6
DT_COL = D + CONV_DIM
TN_IN = 1024
LANE = 128
VMEM_LIMIT = 56 * 1024 * 1024


def _params(*sem, flags=None):
    return pltpu.CompilerParams(dimension_semantics=sem, vmem_limit_bytes=VMEM_LIMIT, flags=flags)


def _dot(a, b):
    return jnp.dot(a, b, preferred_element_type=F32)


def _split_bf16(v, terms):
    out = []
    r = v
    for _ in range(terms):
        p = r.astype(BF16)
        out.append(p)
        r = r - p.astype(F32)
    return out


def _dot_exact_rhs(v, e, terms=3):
    acc = None
    for p in _split_bf16(v, terms):
        d = _dot(p, e)
        acc = d if acc is None else acc + d
    return acc


def _dot_exact_lhs(t, v, terms=3):
    acc = None
    for p in _split_bf16(v, terms):
        d = _dot(t, p)
        acc = d if acc is None else acc + d
    return acc


def _silu(x):
    return x * jax.nn.sigmoid(x)


def _softplus(x):
    return jnp.maximum(x, 0.0) + jnp.log1p(jnp.exp(-jnp.abs(x)))


def _rms(x):
    return x * lax.rsqrt(jnp.mean(x * x, axis=-1, keepdims=True) + EPS)


def _ada_kernel(cp_ref, cs_ref, w_ref, b_ref, op_ref, os_ref):
    w = w_ref[...].astype(BF16)
    cp8 = jnp.concatenate([cp_ref[...], jnp.zeros((8 - BATCH, D), F32)], axis=0)
    op_ref[...] = (_dot(_silu(cp8).astype(BF16), w) + b_ref[...])[0:BATCH]
    os_ref[...] = _dot(_silu(cs_ref[...]).astype(BF16), w) + b_ref[...]


def _ada_call(c_prompt, c_sample, w, b):
    tn = 2048
    return pl.pallas_call(
        _ada_kernel,
        grid=(6 * D // tn,),
        in_specs=[pl.BlockSpec((BATCH, D), lambda j: (0, 0)),
                  pl.BlockSpec((DEC_BATCH, D), lambda j: (0, 0)),
                  pl.BlockSpec((D, tn), lambda j: (0, j)),
                  pl.BlockSpec((1, tn), lambda j: (0, j))],
        out_specs=[pl.BlockSpec((BATCH, tn), lambda j: (0, j)),
                   pl.BlockSpec((DEC_BATCH, tn), lambda j: (0, j))],
        out_shape=[jax.ShapeDtypeStruct((BATCH, 6 * D), F32),
                   jax.ShapeDtypeStruct((DEC_BATCH, 6 * D), F32)],
        compiler_params=_params("arbitrary"),
        name="ada_mod",
    )(c_prompt, c_sample, w, b)


def _norm_kernel(with_dt, xp_ref, xs_ref, nw_ref, scp_ref, shp_ref, scs_ref, shs_ref, *rest):
    if with_dt:
        wdt_ref, n_ref, dt_ref = rest
    else:
        (n_ref,) = rest
    i = pl.program_id(0)

    def emit(n):
        nb = n.astype(BF16)
        n_ref[...] = nb
        if with_dt:
            dt_ref[...] = lax.dot_general(nb, wdt_ref[...].astype(BF16),
                                          (((1,), (1,)), ((), ())),
                                          preferred_element_type=F32)

    @pl.when(i < NP_TILES)
    def _():
        y = _rms(xp_ref[...]) * nw_ref[...]
        emit(y * (1.0 + scp_ref[0]) + shp_ref[0])

    @pl.when(i >= NP_TILES)
    def _():
        y = _rms(xs_ref[...]) * nw_ref[...]
        y3 = y.reshape(DEC_SEQ, DEC_BATCH, D)
        emit((y3 * (1.0 + scs_ref[...])[None] + shs_ref[...][None]).reshape(TM, D))


def _seq_of_tile(i):
    return jnp.minimum(i // TILES_PER_SEQ, BATCH - 1)


def _seq_of_mtile(i):
    return jnp.minimum(i // MT_PER_SEQ, BATCH - 1)


def _norm_call(xp, xp_spec, xs, xs_spec, nw, mod_p, mod_s, k_scale, k_shift, w_dt):
    with_dt = w_dt is not None
    in_specs = [
        xp_spec, xs_spec,
        pl.BlockSpec((1, D), lambda i: (0, 0)),
        pl.BlockSpec((1, 1, D), lambda i: (_seq_of_tile(i), 0, k_scale)),
        pl.BlockSpec((1, 1, D), lambda i: (_seq_of_tile(i), 0, k_shift)),
        pl.BlockSpec((DEC_BATCH, D), lambda i: (0, k_scale)),
        pl.BlockSpec((DEC_BATCH, D), lambda i: (0, k_shift)),
    ]
    args = [xp, xs, nw, mod_p, mod_p, mod_s, mod_s]
    out_specs = [pl.BlockSpec((TM, D), lambda i: (i, 0))]
    out_shape = [jax.ShapeDtypeStruct((M_ROWS, D), BF16)]
    if with_dt:
        in_specs.append(pl.BlockSpec((LANE, D), lambda i: (0, 0)))
        args.append(w_dt)
        out_specs.append(pl.BlockSpec((TM, LANE), lambda i: (i, 0)))
        out_shape.append(jax.ShapeDtypeStruct((M_ROWS, LANE), F32))
    return pl.pallas_call(
        functools.partial(_norm_kernel, with_dt),
        grid=(N_TILES,),
        in_specs=in_specs,
        out_specs=out_specs,
        out_shape=out_shape,
        compiler_params=_params("arbitrary"),
        name="norm_mod_dt" if with_dt else "norm_mod",
    )(*args)


N_IN_BLOCKS = 13
UVG_ROW = DT_COL + HEADS
FIRST_CONV_BLOCK = 10


def _in_src_row(j):
    row = jnp.where(j < 2, j * TN_IN,
                    jnp.where(j < FIRST_CONV_BLOCK, UVG_ROW + (j - 2) * TN_IN,
                              D + (j - FIRST_CONV_BLOCK) * TN_IN))
    return pl.multiple_of(row, HEADS)


def _in_conv_block(j):
    return jnp.maximum(j - FIRST_CONV_BLOCK, 0)


def _inproj_kernel(a_ref, wt_ref, cw_ref, cb_ref, cst_ref, o_ref, csp_ref, css_ref, wbf_ref, acc_ref):
    j = pl.program_id(0)
    i = pl.program_id(1)
    rc = 64
    B = DEC_BATCH

    @pl.when(i == 0)
    def _():
        for r in range(TN_IN // LANE):
            rows = slice(r * LANE, (r + 1) * LANE)
            wbf_ref[:, rows] = wt_ref[rows, :].T.astype(BF16)

    def elementwise(fn, rows):
        def body():
            acc_ref[8:8 + rows, :] = _dot(a_ref[0:rows, :], wbf_ref[...])
            for r in range(0, rows, rc):
                val = acc_ref[8 + r:8 + r + rc, :]
                o_ref[r:r + rc, :] = (val if fn is None else fn(val)).astype(o_ref.dtype)
        return body

    def conv_prompt():
        starts_sequence = i % MT_PER_SEQ == 0
        acc_ref[0:8, :] = jnp.where(starts_sequence, 0.0, acc_ref[TMM:TMM + 8, :])
        acc_ref[8:8 + TMM, :] = _dot(a_ref[...], wbf_ref[...])
        cw, cb = cw_ref[...], cb_ref[...]
        for r in range(0, TMM, rc):
            conv = cb + cw[3:4] * acc_ref[8 + r:8 + r + rc, :]
            for k in range(3):
                conv = conv + cw[k:k + 1] * acc_ref[5 + k + r:5 + k + r + rc, :]
            o_ref[r:r + rc, :] = _silu(conv).astype(o_ref.dtype)
        csp_ref[0] = acc_ref[TMM + 5:TMM + 8, :]

    def conv_sample():
        acc_ref[8:8 + NS_ROWS, :] = _dot(a_ref[0:NS_ROWS, :], wbf_ref[...])
        cw, cb = cw_ref[...], cb_ref[...]

        def pre(t, r):
            if t < 0:
                return cst_ref[t + 3, r:r + rc, :]
            return acc_ref[8 + t * B + r:8 + t * B + r + rc, :]

        for t in range(DEC_SEQ):
            for r in range(0, B, rc):
                conv = cb + cw[3:4] * pre(t, r)
                for k in range(3):
                    conv = conv + cw[k:k + 1] * pre(t - 3 + k, r)
                o_ref[t * B + r:t * B + r + rc, :] = _silu(conv).astype(o_ref.dtype)
        for t in range(1, DEC_SEQ):
            css_ref[t - 1] = acc_ref[8 + t * B:8 + (t + 1) * B, :]

    is_conv = j >= FIRST_CONV_BLOCK
    kinds = ((j < 2, _silu),
             (jnp.logical_and(j >= 2, j < 6), jax.nn.gelu),
             (jnp.logical_and(j >= 6, j < FIRST_CONV_BLOCK), None))
    for cond, fn in kinds:
        pl.when(jnp.logical_and(cond, i < NP_MT))(elementwise(fn, TMM))
        pl.when(jnp.logical_and(cond, i == NP_MT))(elementwise(fn, NS_ROWS))
    pl.when(jnp.logical_and(is_conv, i < NP_MT))(conv_prompt)
    pl.when(jnp.logical_and(is_conv, i == NP_MT))(conv_sample)


def _inproj_call(n1, w_in_t, cw, cb, cst_t):
    cblk = _in_conv_block
    seq = lambda j, i: jnp.where(j < FIRST_CONV_BLOCK, 0, _seq_of_mtile(i))
    return pl.pallas_call(
        _inproj_kernel,
        grid=(N_IN_BLOCKS, N_MT),
        in_specs=[pl.BlockSpec((TMM, D), lambda j, i: (i, 0)),
                  pl.BlockSpec((pl.Element(TN_IN), pl.Element(D)),
                               lambda j, i: (_in_src_row(j), 0)),
                  pl.BlockSpec((4, TN_IN), lambda j, i: (0, cblk(j))),
                  pl.BlockSpec((1, TN_IN), lambda j, i: (0, cblk(j))),
                  pl.BlockSpec((3, DEC_BATCH, TN_IN), lambda j, i: (0, 0, cblk(j)))],
        out_specs=[pl.BlockSpec((TMM, TN_IN), lambda j, i: (i, j)),
                   pl.BlockSpec((1, 3, TN_IN), lambda j, i: (seq(j, i), 0, cblk(j))),
                   pl.BlockSpec((3, DEC_BATCH, TN_IN), lambda j, i: (0, 0, cblk(j)))],
        out_shape=[jax.ShapeDtypeStruct((M_ROWS, N_IN_BLOCKS * TN_IN), BF16),
                   jax.ShapeDtypeStruct((BATCH, 3, CONV_DIM), F32),
                   jax.ShapeDtypeStruct((3, DEC_BATCH, CONV_DIM), F32)],
        scratch_shapes=[pltpu.VMEM((D, TN_IN), BF16), pltpu.VMEM((TMM + 8, TN_IN), F32)],
        compiler_params=_params("arbitrary", "arbitrary"),
        name="in_proj",
    )(n1, w_in_t, cw, cb, cst_t)


PZ, PU, PV, PGA, PGB, PX = 0, 1, 2, 3, 4, 5
PBC_1024 = 12


def _gated_group_norm(get_y, zact_ref, nw_ref, o_ref):
    gw = D // GROUPS
    for g in range(GROUPS):
        cols = slice(g * gw, (g + 1) * gw)
        gg = _rms(get_y(cols) * zact_ref[:, cols].astype(F32))
        o_ref[:, cols] = (gg * nw_ref[:, cols]).astype(o_ref.dtype)


def _ssd_prompt_kernel(z_ref, x_ref, bc_ref, dt_ref, hp_ref, hpc_ref, dskx_ref, nw_ref, triu_ref,
                       y_ref, ssm_ref, st_ref, yscr_ref):
    c = pl.program_id(1)
    T = CHUNK
    cdim = GROUPS * STATE

    @pl.when(c == 0)
    def _():
        st_ref[...] = jnp.zeros(st_ref.shape, F32)

    hp = hp_ref[...]
    dt_t = _softplus((dt_ref[...] + hp[0:1, :]).T[0:HEADS, :])
    adt_t = dt_t * (-jnp.exp(hpc_ref[0:HEADS, 1:2]))
    cs_t = _dot_exact_rhs(adt_t, triu_ref[...])
    rsub_t = cs_t - jnp.log(dt_t)
    cs = jnp.concatenate([cs_t, jnp.zeros((LANE - HEADS, T), F32)], axis=0).T

    row = lax.broadcasted_iota(jnp.int32, (T, T), 0)
    lane = lax.broadcasted_iota(jnp.int32, (T, T), 1)
    causal = row >= lane
    left = lane < HEADDIM
    mask_l = jnp.where(left, 1.0, 0.0).astype(BF16)
    nt = (((1,), (1,)), ((), ()))
    pairs_per_group = HEADS // GROUPS // 2

    for g in range(GROUPS):
        c_b = bc_ref[:, cdim + g * STATE:cdim + (g + 1) * STATE]
        b_b = bc_ref[:, g * STATE:(g + 1) * STATE]
        cb = lax.dot_general(c_b, b_b, nt, preferred_element_type=F32)
        b_t = b_b.astype(F32).T
        for k4 in range(pairs_per_group):
            k = g * pairs_per_group + k4
            cols = slice(k * LANE, (k + 1) * LANE)
            xpb = x_ref[:, cols]
            x_lo = xpb * mask_l
            xbd = jnp.concatenate([x_lo, xpb - x_lo], axis=0)
            st = st_ref[k]
            yraw = _dot(c_b, st.astype(BF16))
            scores, bws, colbs, alasts = [], [], [], []
            for h in (2 * k, 2 * k + 1):
                colb = jnp.broadcast_to(cs[:, h:h + 1], (T, T))
                alast = cs_t[h:h + 1, T - 1:T]
                decay_dt = jnp.exp(jnp.where(causal, colb - rsub_t[h:h + 1, :], -jnp.inf))
                scores.append((cb * decay_dt).astype(BF16))
                wrow = jnp.exp(alast - cs_t[h:h + 1, :]) * dt_t[h:h + 1, :]
                bws.append((b_t * wrow).astype(BF16))
                colbs.append(colb)
                alasts.append(alast)
            ecol = jnp.exp(jnp.where(left, colbs[0], colbs[1]))
            elast = jnp.exp(jnp.where(left[0:1], alasts[0], alasts[1]))
            yscr_ref[:, cols] = (_dot(jnp.concatenate(scores, axis=1), xbd) + ecol * yraw
                                 + dskx_ref[:, cols] * xpb.astype(F32))
            st_ref[k] = elast * st + _dot(jnp.concatenate(bws, axis=1), xbd)

    _gated_group_norm(lambda cols: yscr_ref[:, cols], z_ref, nw_ref, y_ref)

    @pl.when(c == N_CHUNKS - 1)
    def _():
        for k in range(HEADS // 2):
            ssm_ref[0, k] = st_ref[k].T


def _ssd_prompt_call(proj, dt_raw, hp, hpc, dskx, nw, triu):
    row = lambda b, c: b * N_CHUNKS + c
    const = lambda b, c: (0, 0)
    return pl.pallas_call(
        _ssd_prompt_kernel,
        grid=(BATCH, N_CHUNKS),
        in_specs=[pl.BlockSpec((CHUNK, D), lambda b, c: (row(b, c), PZ)),
                  pl.BlockSpec((CHUNK, D), lambda b, c: (row(b, c), PX)),
                  pl.BlockSpec((CHUNK, 2 * GROUPS * STATE), lambda b, c: (row(b, c), PBC_1024)),
                  pl.BlockSpec((CHUNK, LANE), lambda b, c: (row(b, c), 0)),
                  pl.BlockSpec((8, LANE), const),
                  pl.BlockSpec((LANE, 8), const),
                  pl.BlockSpec((1, D), const),
                  pl.BlockSpec((1, D), const),
                  pl.BlockSpec((CHUNK, CHUNK), const)],
        out_specs=[pl.BlockSpec((CHUNK, D), lambda b, c: (row(b, c), 0)),
                   pl.BlockSpec((1, HEADS // 2, LANE, STATE), lambda b, c: (b, 0, 0, 0))],
        out_shape=[jax.ShapeDtypeStruct((NP_ROWS, D), BF16),
                   jax.ShapeDtypeStruct((BATCH, HEADS // 2, LANE, STATE), F32)],
        scratch_shapes=[pltpu.VMEM((HEADS // 2, STATE, LANE), F32),
                        pltpu.VMEM((CHUNK, D), F32)],
        compiler_params=_params("arbitrary", "arbitrary"),
        name="ssd_prompt",
    )(proj, proj, proj, dt_raw, hp, hpc, dskx, nw, triu)


def _ssds_prep_step(tt, x_ref, bc_ref, dt_ref, hp_ref, dskx_ref, ex_ref, seg_ref,
                    ypre_ref, efull_ref, xw_ref, bb_ref, cc_ref, e3_ref):
    B = DEC_BATCH
    cdim = GROUPS * STATE
    blk = lambda t: slice(t * B, (t + 1) * B)
    x_of = lambda t: x_ref[blk(t), :].astype(F32)
    b_of = lambda t: bc_ref[blk(t), 0:cdim].astype(F32)
    c_of = lambda t: bc_ref[blk(t), cdim:2 * cdim].astype(F32)

    bb_ref[...] = b_of(tt)
    cc_ref[...] = c_of(tt)

    hp = hp_ref[...]
    a_neg = -jnp.exp(hp[1:2, :])
    dts, css = [], []
    run = None
    for t in range(DEC_SEQ):
        dt = _softplus(dt_ref[blk(t), :] + hp[0:1, :])
        run = dt * a_neg if run is None else run + dt * a_neg
        dts.append(dt)
        css.append(run)

    ex = ex_ref[...]
    e3_ref[...] = jnp.exp(css[-1])
    efull_ref[...] = _dot_exact_rhs(jnp.exp(css[tt]), ex)
    w_t = jnp.exp(css[-1] - css[tt]) * dts[tt]
    xw_ref[...] = x_of(tt) * _dot_exact_rhs(w_t, ex)

    seg = seg_ref[...]
    acc = dskx_ref[...] * x_of(tt)
    c_t = c_of(tt)
    for s in range(tt + 1):
        cbh = _dot_exact_rhs(c_t * b_of(s), seg)
        g_ts = cbh * jnp.exp(css[tt] - css[s]) * dts[s]
        acc = acc + _dot_exact_rhs(g_ts, ex) * x_of(s)
    ypre_ref[...] = acc


def _ssds_prep_kernel(*refs):
    t = pl.program_id(0)
    for tt in range(DEC_SEQ):
        pl.when(t == tt)(functools.partial(_ssds_prep_step, tt, *refs))


def _ssds_prep_call(proj, dt_raw, hp, dskx, ex, seg):
    full = lambda shape: pl.BlockSpec(shape, lambda t: (0,) * len(shape))
    step = lambda width: pl.BlockSpec((DEC_BATCH, width), lambda t: (t, 0))
    cdim = GROUPS * STATE
    return pl.pallas_call(
        _ssds_prep_kernel,
        grid=(DEC_SEQ,),
        in_specs=[pl.BlockSpec((NS_ROWS, D), lambda t: (NP_TILES, PX)),
                  pl.BlockSpec((NS_ROWS, 2 * cdim), lambda t: (NP_TILES, PBC_1024)),
                  pl.BlockSpec((NS_ROWS, LANE), lambda t: (NP_TILES, 0)),
                  full((8, LANE)), full((1, D)), full((LANE, D)), full((cdim, LANE))],
        out_specs=[step(D), step(D), step(D), step(cdim), step(cdim),
                   full((DEC_BATCH, LANE))],
        out_shape=[jax.ShapeDtypeStruct((NS_ROWS, D), F32),
                   jax.ShapeDtypeStruct((NS_ROWS, D), F32),
                   jax.ShapeDtypeStruct((NS_ROWS, D), F32),
                   jax.ShapeDtypeStruct((NS_ROWS, cdim), F32),
                   jax.ShapeDtypeStruct((NS_ROWS, cdim), F32),
                   jax.ShapeDtypeStruct((DEC_BATCH, LANE), F32)],
        compiler_params=_params("arbitrary"),
        name="ssd_sample_prep",
    )(proj, proj, dt_raw, hp, dskx, ex, seg)


SEQ_PER_STEP = 8


def _ssds_state_kernel(e3_ref, st_ref, cc_ref, bb_ref, xw_ref, yo_ref, so_ref):
    blk = pl.program_id(0)
    nt = (((1,), (1,)), ((), ()))
    tn = (((0,), (0,)), ((), ()))
    hpg = HEADS // GROUPS
    gw = hpg * HEADDIM

    def rows_of(ref, s, cols):
        v = ref[:, s, cols]
        return jnp.concatenate([v, jnp.zeros((8 - DEC_SEQ, v.shape[-1]), F32)], axis=0).astype(BF16)

    for s in range(SEQ_PER_STEP):
        b = blk * SEQ_PER_STEP + s
        for g in range(GROUPS):
            h0 = st_ref[s, g]
            c_g = rows_of(cc_ref, s, slice(g * STATE, (g + 1) * STATE))
            yraw = lax.dot_general(c_g, h0.astype(BF16), nt, preferred_element_type=F32)
            yo_ref[:, s, g * gw:(g + 1) * gw] = yraw[0:DEC_SEQ]
            x_g = rows_of(xw_ref, s, slice(g * gw, (g + 1) * gw))
            b_g = rows_of(bb_ref, s, slice(g * STATE, (g + 1) * STATE))
            dh = lax.dot_general(x_g, b_g, tn, preferred_element_type=F32)
            for hh in range(hpg):
                rows = slice(hh * HEADDIM, (hh + 1) * HEADDIM)
                so_ref[s, g, rows, :] = e3_ref[b, g * hpg + hh] * h0[rows] + dh[rows]


def _ssds_state_call(e3, state, cc_t, bb_t, xw_t):
    sb = SEQ_PER_STEP
    gw = (HEADS // GROUPS) * HEADDIM
    cdim = GROUPS * STATE
    tmajor = lambda width: pl.BlockSpec((DEC_SEQ, sb, width), lambda i: (0, i, 0))
    return pl.pallas_call(
        _ssds_state_kernel,
        grid=(DEC_BATCH // sb,),
        in_specs=[pl.BlockSpec(memory_space=pltpu.SMEM),
                  pl.BlockSpec((sb, GROUPS, gw, STATE), lambda i: (i, 0, 0, 0)),
                  tmajor(cdim), tmajor(cdim), tmajor(D)],
        out_specs=[tmajor(D),
                   pl.BlockSpec((sb, GROUPS, gw, STATE), lambda i: (i, 0, 0, 0))],
        out_shape=[jax.ShapeDtypeStruct((DEC_SEQ, DEC_BATCH, D), F32),
                   jax.ShapeDtypeStruct((DEC_BATCH, GROUPS, gw, STATE), F32)],
        compiler_params=_params("arbitrary"),
        name="ssd_sample_state",
    )(e3, state, cc_t.reshape(DEC_SEQ, DEC_BATCH, cdim), bb_t.reshape(DEC_SEQ, DEC_BATCH, cdim),
      xw_t.reshape(DEC_SEQ, DEC_BATCH, D))


def _ssds_post_kernel(ypre_ref, efull_ref, yo_ref, z_ref, nw_ref, o_ref):
    get_y = lambda cols: ypre_ref[:, cols] + efull_ref[:, cols] * yo_ref[:, cols]
    _gated_group_norm(get_y, z_ref, nw_ref, o_ref)


def _ssds_post_call(ypre, efull, yo_t, proj, nw):
    step = pl.BlockSpec((DEC_BATCH, D), lambda t: (t, 0))
    return pl.pallas_call(
        _ssds_post_kernel,
        grid=(DEC_SEQ,),
        in_specs=[step, step, step,
                  pl.BlockSpec((DEC_BATCH, D), lambda t: (NP_ROWS // DEC_BATCH + t, PZ)),
                  pl.BlockSpec((1, D), lambda t: (0, 0))],
        out_specs=step,
        out_shape=jax.ShapeDtypeStruct((NS_ROWS, D), BF16),
        compiler_params=_params("arbitrary"),
        name="ssd_sample_post",
    )(ypre, efull, yo_t, proj, nw)


def _mlp_kernel(u_ref, v_ref, lnw_ref, lnb_ref, ws_ref, bsx_ref, wsx_ref, bsx4_ref,
                y_ref, cv_ref, wm_ref):
    i = pl.program_id(0)
    T = CHUNK

    @pl.when(i == 0)
    def _():
        row = lax.broadcasted_iota(jnp.int32, (T, T), 0)
        lane = lax.broadcasted_iota(jnp.int32, (T, T), 1)
        for g in range(MLP_GROUPS):
            wm_ref[g] = jnp.where(row >= lane, ws_ref[g], 0.0).astype(BF16)

    def vnorm(rows):
        vg = v_ref[rows, :].astype(F32)
        xc = vg - jnp.mean(vg, axis=-1, keepdims=True)
        y = xc * lax.rsqrt(jnp.mean(xc * xc, axis=-1, keepdims=True) + EPS)
        return y * lnw_ref[...] + lnb_ref[...]

    @pl.when(i < NP_TILES)
    def _():
        for cc in range(TM // T):
            rows = slice(cc * T, (cc + 1) * T)
            vnb = vnorm(rows).astype(BF16)
            for g in range(MLP_GROUPS):
                cols = slice(g * MLP_GROUP_DIM, (g + 1) * MLP_GROUP_DIM)
                sv = _dot(wm_ref[g], vnb[:, cols]) + bsx_ref[:, cols]
                y_ref[rows, cols] = (u_ref[rows, cols].astype(F32) * sv).astype(BF16)

    @pl.when(i >= NP_TILES)
    def _():
        B = DEC_BATCH
        for t in range(DEC_SEQ):
            rows = slice(t * B, (t + 1) * B)
            cv_ref[rows, :] = vnorm(rows)
        for t in range(DEC_SEQ):
            rows = slice(t * B, (t + 1) * B)
            acc = bsx4_ref[t:t + 1, :]
            for s in range(t + 1):
                acc = acc + wsx_ref[4 * t + s:4 * t + s + 1, :] * cv_ref[s * B:(s + 1) * B, :]
            y_ref[rows, :] = (u_ref[rows, :].astype(F32) * acc).astype(BF16)


def _mlp_call(proj, lnw, lnb, ws, bsx, wsx, bsx4):
    full = lambda shape: pl.BlockSpec(shape, lambda i: (0,) * len(shape))
    return pl.pallas_call(
        _mlp_kernel,
        grid=(N_TILES,),
        in_specs=[pl.BlockSpec((TM, D), lambda i: (i, PU)),
                  pl.BlockSpec((TM, D), lambda i: (i, PV)),
                  full((1, D)), full((1, D)),
                  full((MLP_GROUPS, CHUNK, CHUNK)),
                  full((CHUNK, D)), full((16, D)), full((8, D))],
        out_specs=[pl.BlockSpec((TM, D), lambda i: (i, 0)),
                   full((NS_ROWS, D))],
        out_shape=[jax.ShapeDtypeStruct((M_ROWS, D), BF16),
                   jax.ShapeDtypeStruct((NS_ROWS, D), F32)],
        scratch_shapes=[pltpu.VMEM((MLP_GROUPS, CHUNK, CHUNK), BF16)],
        compiler_params=_params("arbitrary"),
        name="gmlp",
    )(proj, proj, lnw, lnb, ws, bsx, wsx, bsx4)


def _cast_rows(src_ref, dst_ref, chunk=256):
    def body(r, carry):
        rows = pl.ds(pl.multiple_of(r * chunk, chunk), chunk)
        dst_ref[rows, :] = src_ref[rows, :].astype(dst_ref.dtype)
        return carry
    lax.fori_loop(0, src_ref.shape[0] // chunk, body, 0)


def _per_tile(i, prompt_fn, sample_fn):
    pl.when(i < NP_MT)(prompt_fn)
    pl.when(i == NP_MT)(sample_fn)


def _merge_kernel(ysp_ref, yss_ref, ym_ref, ga_ref, gb_ref, w1_ref, w2_ref, o_ref, w1b, w2b):
    i = pl.program_id(1)

    @pl.when(i == 0)
    def _():
        _cast_rows(w1_ref, w1b)
        _cast_rows(w2_ref, w2b)

    def emit(ys, rows):
        a1 = _dot(ys, w1b[...])
        a2 = _dot(ym_ref[rows, :], w2b[...])
        o_ref[rows, :] = (jax.nn.sigmoid(ga_ref[rows, :].astype(F32)) * a1
                          + jax.nn.sigmoid(gb_ref[rows, :].astype(F32)) * a2).astype(BF16)

    _per_tile(i,
              lambda: emit(ysp_ref[...], slice(None)),
              lambda: emit(yss_ref[...], slice(0, NS_ROWS)))


def _merge_call(ysp, yss, ym, proj, w1, w2):
    tn = 512
    nb = D // tn
    return pl.pallas_call(
        _merge_kernel,
        grid=(nb, N_MT),
        in_specs=[pl.BlockSpec((TMM, D), lambda j, i: (jnp.minimum(i, NP_MT - 1), 0)),
                  pl.BlockSpec((NS_ROWS, D), lambda j, i: (0, 0)),
                  pl.BlockSpec((TMM, D), lambda j, i: (i, 0)),
                  pl.BlockSpec((TMM, tn), lambda j, i: (i, PGA * nb + j)),
                  pl.BlockSpec((TMM, tn), lambda j, i: (i, PGB * nb + j)),
                  pl.BlockSpec((D, tn), lambda j, i: (0, j)),
                  pl.BlockSpec((D, tn), lambda j, i: (0, j))],
        out_specs=pl.BlockSpec((TMM, tn), lambda j, i: (i, j)),
        out_shape=jax.ShapeDtypeStruct((M_ROWS, D), BF16),
        scratch_shapes=[pltpu.VMEM((D, tn), BF16), pltpu.VMEM((D, tn), BF16)],
        compiler_params=_params("arbitrary", "arbitrary"),
        name="branch_merge",
    )(ysp, yss, ym, proj, proj, w1, w2)


def _resid_kernel(a_ref, w_ref, rp_ref, rs_ref, gp_ref, gs_ref, o_ref, wb):
    i = pl.program_id(1)

    @pl.when(i == 0)
    def _():
        _cast_rows(w_ref, wb)

    def prompt():
        o_ref[...] = rp_ref[...] + gp_ref[0] * _dot(a_ref[...], wb[...])

    def sample():
        acc = _dot(a_ref[0:NS_ROWS, :], wb[...])
        tn = acc.shape[-1]
        acc3 = acc.reshape(DEC_SEQ, DEC_BATCH, tn) * gs_ref[...][None]
        o_ref[0:NS_ROWS, :] = rs_ref[...] + acc3.reshape(NS_ROWS, tn)

    _per_tile(i, prompt, sample)


def _resid_call(a, w, rp, rs, rs_block, mod_p, mod_s, k_gate, tn, name, single_buffer_w=False):
    kdim = a.shape[1]
    nb = D // tn
    w_mode = dict(pipeline_mode=pl.Buffered(1)) if single_buffer_w else {}
    return pl.pallas_call(
        _resid_kernel,
        grid=(nb, N_MT),
        in_specs=[pl.BlockSpec((TMM, kdim), lambda j, i: (i, 0)),
                  pl.BlockSpec((kdim, tn), lambda j, i: (0, j), **w_mode),
                  pl.BlockSpec((TMM, tn), lambda j, i: (jnp.minimum(i, NP_MT - 1), j)),
                  pl.BlockSpec((NS_ROWS, tn), lambda j, i: (rs_block, j)),
                  pl.BlockSpec((1, 1, tn), lambda j, i: (_seq_of_mtile(i), 0, k_gate * nb + j)),
                  pl.BlockSpec((DEC_BATCH, tn), lambda j, i: (0, k_gate * nb + j))],
        out_specs=pl.BlockSpec((TMM, tn), lambda j, i: (i, j)),
        out_shape=jax.ShapeDtypeStruct((M_ROWS, D), F32),
        scratch_shapes=[pltpu.VMEM((kdim, tn), BF16)],
        compiler_params=_params("arbitrary", "arbitrary"),
        name=name,
    )(a, w, rp, rs, mod_p, mod_s)


TN_FF = 512
N_FF_BLOCKS = D_FF // TN_FF


def _up_kernel(a_ref, wa_ref, wv_ref, cw_ref, cb_ref, fst_ref,
               h_ref, fcp_ref, fcs_ref, wab, wvb, acc_a, acc_v):
    i = pl.program_id(1)
    rc = 64

    @pl.when(i == 0)
    def _():
        _cast_rows(wa_ref, wab)
        _cast_rows(wv_ref, wvb)

    def prompt():
        starts_sequence = i % MT_PER_SEQ == 0
        acc_a[0:8, :] = jnp.where(starts_sequence, 0.0, acc_a[TMM:TMM + 8, :])
        x = a_ref[...]
        acc_a[8:8 + TMM, :] = _dot(x, wab[...])
        acc_v[...] = _dot(x, wvb[...])
        cw, cb = cw_ref[...], cb_ref[...]
        for r in range(0, TMM, rc):
            conv = (cb + cw[2:3] * acc_a[8 + r:8 + r + rc, :]
                    + cw[1:2] * acc_a[7 + r:7 + r + rc, :]
                    + cw[0:1] * acc_a[6 + r:6 + r + rc, :])
            h_ref[r:r + rc, :] = (jax.nn.gelu(conv) * acc_v[r:r + rc, :]).astype(BF16)
        fcp_ref[0] = acc_a[TMM + 6:TMM + 8, :]

    def sample():
        B = DEC_BATCH
        x = a_ref[0:NS_ROWS, :]
        acc_a[8:8 + NS_ROWS, :] = _dot(x, wab[...])
        acc_v[0:NS_ROWS, :] = _dot(x, wvb[...])
        cw, cb = cw_ref[...], cb_ref[...]

        def pre(t, r):
            if t < 0:
                return fst_ref[t + 2, r:r + rc, :]
            return acc_a[8 + t * B + r:8 + t * B + r + rc, :]

        for t in range(DEC_SEQ):
            for r in range(0, B, rc):
                conv = cb + cw[2:3] * pre(t, r) + cw[1:2] * pre(t - 1, r) + cw[0:1] * pre(t - 2, r)
                h_ref[t * B + r:t * B + r + rc, :] = (
                    jax.nn.gelu(conv) * acc_v[t * B + r:t * B + r + rc, :]).astype(BF16)
        fcs_ref[0] = acc_a[8 + 2 * B:8 + 3 * B, :]
        fcs_ref[1] = acc_a[8 + 3 * B:8 + 4 * B, :]

    _per_tile(i, prompt, sample)


def _up_call(n2, w_up, cw, cb, fst_t):
    return pl.pallas_call(
        _up_kernel,
        grid=(N_FF_BLOCKS, N_MT),
        in_specs=[pl.BlockSpec((TMM, D), lambda j, i: (i, 0)),
                  pl.BlockSpec((D, TN_FF), lambda j, i: (0, j)),
                  pl.BlockSpec((D, TN_FF), lambda j, i: (0, N_FF_BLOCKS + j)),
                  pl.BlockSpec((3, TN_FF), lambda j, i: (0, j)),
                  pl.BlockSpec((1, TN_FF), lambda j, i: (0, j)),
                  pl.BlockSpec((2, DEC_BATCH, TN_FF), lambda j, i: (0, 0, j))],
        out_specs=[pl.BlockSpec((TMM, TN_FF), lambda j, i: (i, j)),
                   pl.BlockSpec((1, 2, TN_FF), lambda j, i: (_seq_of_mtile(i), 0, j)),
                   pl.BlockSpec((2, DEC_BATCH, TN_FF), lambda j, i: (0, 0, j))],
        out_shape=[jax.ShapeDtypeStruct((M_ROWS, D_FF), BF16),
                   jax.ShapeDtypeStruct((BATCH, 2, D_FF), F32),
                   jax.ShapeDtypeStruct((2, DEC_BATCH, D_FF), F32)],
        scratch_shapes=[pltpu.VMEM((D, TN_FF), BF16), pltpu.VMEM((D, TN_FF), BF16),
                        pltpu.VMEM((TMM + 8, TN_FF), F32), pltpu.VMEM((TMM, TN_FF), F32)],
        compiler_params=_params("arbitrary", "arbitrary"),
        name="ffn_up",
    )(n2, w_up, w_up, cw, cb, fst_t)


def _final_kernel(x_ref, w_ref, yp_ref, ys_ref):
    i = pl.program_id(0)
    y = _rms(x_ref[...]) * w_ref[...]

    @pl.when(i < NP_TILES)
    def _():
        yp_ref[...] = y

    @pl.when(i >= NP_TILES)
    def _():
        ys_ref[...] = y


def _final_call(x3, w):
    return pl.pallas_call(
        _final_kernel,
        grid=(N_TILES,),
        in_specs=[pl.BlockSpec((TM, D), lambda i: (i, 0)),
                  pl.BlockSpec((1, D), lambda i: (0, 0))],
        out_specs=[pl.BlockSpec((TM, D), lambda i: (jnp.minimum(i, NP_TILES - 1), 0)),
                   pl.BlockSpec((TM, D), lambda i: (0, 0))],
        out_shape=[jax.ShapeDtypeStruct((NP_ROWS, D), F32),
                   jax.ShapeDtypeStruct((NS_ROWS, D), F32)],
        compiler_params=_params("arbitrary"),
        name="final_norm",
    )(x3, w)


def _to_time_major(a):
    return jnp.transpose(a, (1, 0, 2))


def kernel(x_prompt, x_sample, state_ssm, state_ssd_conv, state_ffn_conv, c_prompt, c_sample,
           norm1_w, w_ada, b_ada, w_in, ssd_conv_w, ssd_conv_b, dt_bias, a_log, d_skip,
           ssd_norm_w, mlp_ln_w, mlp_ln_b, w_spatial, b_spatial, w_ssd_o, w_mlp_o, w_out,
           norm2_w, w_up, ffn_conv_w, ffn_conv_b, w_down, final_norm_w):
    assert w_in.shape[0] == 1, "single-layer trunk"
    row = lambda v: v.reshape(1, -1)

    xp = x_prompt.reshape(NP_ROWS, D)
    xs = _to_time_major(x_sample).reshape(NS_ROWS, D)

    mod_p, mod_s = _ada_call(c_prompt, c_sample, w_ada[0], row(b_ada[0]))
    mod_p = mod_p.reshape(BATCH, 1, 6 * D)
    K_SHIFT1, K_SCALE1, K_GATE1, K_SHIFT2, K_SCALE2, K_GATE2 = range(6)

    xp_spec = pl.BlockSpec((TM, D), lambda i: (jnp.minimum(i, NP_TILES - 1), 0))
    xs_spec = pl.BlockSpec((TM, D), lambda i: (0, 0))
    w_in_t = w_in[0].T
    w_dt_t = w_in_t[DT_COL:DT_COL + LANE]
    n1, dt_raw = _norm_call(xp, xp_spec, xs, xs_spec, row(norm1_w[0]), mod_p, mod_s,
                            K_SCALE1, K_SHIFT1, w_dt_t)
    cst_t = _to_time_major(state_ssd_conv[0])
    proj, cst_p, ncs_t = _inproj_call(n1, w_in_t, ssd_conv_w[0], row(ssd_conv_b[0]), cst_t)

    hp = jnp.zeros((8, LANE), F32)
    hp = hp.at[0, :HEADS].set(dt_bias[0]).at[1, :HEADS].set(a_log[0]).at[2, :HEADS].set(d_skip[0])
    hpc = hp.T
    triu = jnp.triu(jnp.ones((CHUNK, CHUNK), F32)).astype(BF16)
    dskx = row(jnp.repeat(d_skip[0], HEADDIM))
    nw = row(ssd_norm_w[0])

    y_ssd_p, ssm_p = _ssd_prompt_call(proj, dt_raw, hp, hpc, dskx, nw, triu)

    head_of_col = jnp.arange(D) // HEADDIM
    ex = (jnp.arange(LANE)[:, None] == head_of_col[None, :]).astype(BF16)
    grp_of_head = jnp.arange(LANE) // (HEADS // GROUPS)
    seg = ((jnp.arange(GROUPS * STATE)[:, None] // STATE == grp_of_head[None, :])
           & (jnp.arange(LANE)[None, :] < HEADS)).astype(BF16)
    ypre, efull, xw_t, bb_t, cc_t, e3 = _ssds_prep_call(proj, dt_raw, hp, dskx, ex, seg)
    gw = (HEADS // GROUPS) * HEADDIM
    yo_t, ssm_s = _ssds_state_call(
        e3[:, :HEADS], state_ssm[0].reshape(DEC_BATCH, GROUPS, gw, STATE), cc_t, bb_t, xw_t)
    y_ssd_s = _ssds_post_call(ypre, efull, yo_t.reshape(NS_ROWS, D), proj, nw)

    per_col = lambda a: jnp.repeat(a, MLP_GROUP_DIM, axis=1)
    bsx = per_col(b_spatial[0][:, :CHUNK].T)
    ws4 = w_spatial[0][:, :DEC_SEQ, :DEC_SEQ]
    wsx = per_col(jnp.transpose(ws4, (1, 2, 0)).reshape(DEC_SEQ * DEC_SEQ, MLP_GROUPS))
    bsx4 = jnp.pad(bsx[:DEC_SEQ], ((0, 8 - DEC_SEQ), (0, 0)))
    y_mlp, cv_t = _mlp_call(proj, row(mlp_ln_w[0]), row(mlp_ln_b[0]), w_spatial[0], bsx, wsx, bsx4)

    mixed = _merge_call(y_ssd_p, y_ssd_s, y_mlp, proj, w_ssd_o[0], w_mlp_o[0])
    x2 = _resid_call(mixed, w_out[0], xp, xs, 0, mod_p, mod_s, K_GATE1, 1024, "out_proj")

    x2p_spec = pl.BlockSpec((TM, D), lambda i: (jnp.minimum(i, NP_TILES - 1), 0))
    x2s_spec = pl.BlockSpec((TM, D), lambda i: (NP_TILES, 0))
    (n2,) = _norm_call(x2, x2p_spec, x2, x2s_spec, row(norm2_w[0]), mod_p, mod_s,
                       K_SCALE2, K_SHIFT2, None)
    fst_t = _to_time_major(state_ffn_conv[0])
    h, ffn_p, ffn_s_t = _up_call(n2, w_up[0], ffn_conv_w[0], row(ffn_conv_b[0]), fst_t)
    x3 = _resid_call(h, w_down[0], x2, x2, NP_TILES, mod_p, mod_s, K_GATE2, 512, "ffn_down",
                     single_buffer_w=True)

    y_p, y_s_t = _final_call(x3, row(final_norm_w))

    from_t = lambda a, t, c: jnp.transpose(a.reshape(t, DEC_BATCH, c), (1, 0, 2))
    return (y_p.reshape(BATCH, SEQ, D),
            from_t(y_s_t, DEC_SEQ, D),
            ssm_p.reshape(1, BATCH, HEADS, HEADDIM, STATE),
            ssm_s.reshape(1, DEC_BATCH, HEADS, HEADDIM, STATE),
            cst_p[None],
            from_t(ncs_t, 3, CONV_DIM)[None],
            ffn_p[None],
            from_t(ffn_s_t, 2, D_FF)[None],
            from_t(cv_t, DEC_SEQ, D)[None])
```

```python
import functools

import jax
import jax.numpy as jnp
from jax import lax
from jax.experimental import pallas as pl
from jax.experimental.pallas import tpu as pltpu

F32 = jnp.float32
BF16 = jnp.bfloat16

D = 2048
BATCH, SEQ = 4, 2048
DEC_BATCH, DEC_SEQ = 128, 4
NP_ROWS = BATCH * SEQ
NS_ROWS = DEC_BATCH * DEC_SEQ
M_ROWS = NP_ROWS + NS_ROWS
TM = 512
NP_TILES = NP_ROWS // TM
N_TILES = M_ROWS // TM
TILES_PER_SEQ = SEQ // TM
TMM = 1024
NP_MT = NP_ROWS // TMM
N_MT = NP_MT + 1
MT_PER_SEQ = SEQ // TMM
HEADS, HEADDIM, GROUPS, STATE = 32, 64, 4, 128
CHUNK = 128
N_CHUNKS = SEQ // CHUNK
CONV_DIM = D + 2 * GROUPS * STATE
MLP_GROUPS = 8
MLP_GROUP_DIM = D // MLP_GROUPS
D_FF = 5632
EPS = 1e-6
DT_COL = D + CONV_DIM
TN_IN = 1024
LANE = 128
VMEM_LIMIT = 56 * 1024 * 1024


def _params(*sem, flags=None):
    return pltpu.CompilerParams(dimension_semantics=sem, vmem_limit_bytes=VMEM_LIMIT, flags=flags)


def _dot(a, b):
    return jnp.dot(a, b, preferred_element_type=F32)


def _split_bf16(v, terms):
    out = []
    r = v
    for _ in range(terms):
        p = r.astype(BF16)
        out.append(p)
        r = r - p.astype(F32)
    return out


def _dot_exact_rhs(v, e, terms=3):
    acc = None
    for p in _split_bf16(v, terms):
        d = _dot(p, e)
        acc = d if acc is None else acc + d
    return acc


def _dot_exact_lhs(t, v, terms=3):
    acc = None
    for p in _split_bf16(v, terms):
        d = _dot(t, p)
        acc = d if acc is None else acc + d
    return acc


def _silu(x):
    return x * jax.nn.sigmoid(x)


def _softplus(x):
    return jnp.maximum(x, 0.0) + jnp.log1p(jnp.exp(-jnp.abs(x)))


def _rms(x):
    return x * lax.rsqrt(jnp.mean(x * x, axis=-1, keepdims=True) + EPS)


def _ada_kernel(cp_ref, cs_ref, w_ref, b_ref, op_ref, os_ref):
    w = w_ref[...].astype(BF16)
    cp8 = jnp.concatenate([cp_ref[...], jnp.zeros((8 - BATCH, D), F32)], axis=0)
    op_ref[...] = (_dot(_silu(cp8).astype(BF16), w) + b_ref[...])[0:BATCH]
    os_ref[...] = _dot(_silu(cs_ref[...]).astype(BF16), w) + b_ref[...]


def _ada_call(c_prompt, c_sample, w, b):
    tn = 1024
    return pl.pallas_call(
        _ada_kernel,
        grid=(6 * D // tn,),
        in_specs=[pl.BlockSpec((BATCH, D), lambda j: (0, 0)),
                  pl.BlockSpec((DEC_BATCH, D), lambda j: (0, 0)),
                  pl.BlockSpec((D, tn), lambda j: (0, j)),
                  pl.BlockSpec((1, tn), lambda j: (0, j))],
        out_specs=[pl.BlockSpec((BATCH, tn), lambda j: (0, j)),
                   pl.BlockSpec((DEC_BATCH, tn), lambda j: (0, j))],
        out_shape=[jax.ShapeDtypeStruct((BATCH, 6 * D), F32),
                   jax.ShapeDtypeStruct((DEC_BATCH, 6 * D), F32)],
        compiler_params=_params("arbitrary"),
        name="ada_mod",
    )(c_prompt, c_sample, w, b)


def _norm_kernel(with_dt, xp_ref, xs_ref, nw_ref, scp_ref, shp_ref, scs_ref, shs_ref, *rest):
    if with_dt:
        wdt_ref, n_ref, dt_ref = rest
    else:
        (n_ref,) = rest
    i = pl.program_id(0)

    def emit(n):
        nb = n.astype(BF16)
        n_ref[...] = nb
        if with_dt:
            dt_ref[...] = lax.dot_general(nb, wdt_ref[...].astype(BF16),
                                          (((1,), (1,)), ((), ())),
                                          preferred_element_type=F32)

    @pl.when(i < NP_TILES)
    def _():
        y = _rms(xp_ref[...]) * nw_ref[...]
        emit(y * (1.0 + scp_ref[0]) + shp_ref[0])

    @pl.when(i >= NP_TILES)
    def _():
        y = _rms(xs_ref[...]) * nw_ref[...]
        y3 = y.reshape(DEC_SEQ, DEC_BATCH, D)
        emit((y3 * (1.0 + scs_ref[...])[None] + shs_ref[...][None]).reshape(TM, D))


def _seq_of_tile(i):
    return jnp.minimum(i // TILES_PER_SEQ, BATCH - 1)


def _seq_of_mtile(i):
    return jnp.minimum(i // MT_PER_SEQ, BATCH - 1)


def _norm_call(xp, xp_spec, xs, xs_spec, nw, mod_p, mod_s, k_scale, k_shift, w_dt):
    with_dt = w_dt is not None
    in_specs = [
        xp_spec, xs_spec,
        pl.BlockSpec((1, D), lambda i: (0, 0)),
        pl.BlockSpec((1, 1, D), lambda i: (_seq_of_tile(i), 0, k_scale)),
        pl.BlockSpec((1, 1, D), lambda i: (_seq_of_tile(i), 0, k_shift)),
        pl.BlockSpec((DEC_BATCH, D), lambda i: (0, k_scale)),
        pl.BlockSpec((DEC_BATCH, D), lambda i: (0, k_shift)),
    ]
    args = [xp, xs, nw, mod_p, mod_p, mod_s, mod_s]
    out_specs = [pl.BlockSpec((TM, D), lambda i: (i, 0))]
    out_shape = [jax.ShapeDtypeStruct((M_ROWS, D), BF16)]
    if with_dt:
        in_specs.append(pl.BlockSpec((LANE, D), lambda i: (0, 0)))
        args.append(w_dt)
        out_specs.append(pl.BlockSpec((TM, LANE), lambda i: (i, 0)))
        out_shape.append(jax.ShapeDtypeStruct((M_ROWS, LANE), F32))
    return pl.pallas_call(
        functools.partial(_norm_kernel, with_dt),
        grid=(N_TILES,),
        in_specs=in_specs,
        out_specs=out_specs,
        out_shape=out_shape,
        compiler_params=_params("arbitrary"),
        name="norm_mod_dt" if with_dt else "norm_mod",
    )(*args)


N_IN_BLOCKS = 13
UVG_ROW = DT_COL + HEADS
FIRST_CONV_BLOCK = 10


def _in_src_row(j):
    row = jnp.where(j < 2, j * TN_IN,
                    jnp.where(j < FIRST_CONV_BLOCK, UVG_ROW + (j - 2) * TN_IN,
                              D + (j - FIRST_CONV_BLOCK) * TN_IN))
    return pl.multiple_of(row, HEADS)


def _in_conv_block(j):
    return jnp.maximum(j - FIRST_CONV_BLOCK, 0)


def _inproj_kernel(a_ref, wt_ref, cw_ref, cb_ref, cst_ref, o_ref, csp_ref, css_ref, wbf_ref, acc_ref):
    j = pl.program_id(0)
    i = pl.program_id(1)
    rc = 64
    B = DEC_BATCH

    @pl.when(i == 0)
    def _():
        for r in range(TN_IN // LANE):
            rows = slice(r * LANE, (r + 1) * LANE)
            wbf_ref[:, rows] = wt_ref[rows, :].T.astype(BF16)

    def elementwise(fn, rows):
        def body():
            acc_ref[8:8 + rows, :] = _dot(a_ref[0:rows, :], wbf_ref[...])
            for r in range(0, rows, rc):
                val = acc_ref[8 + r:8 + r + rc, :]
                o_ref[r:r + rc, :] = (val if fn is None else fn(val)).astype(o_ref.dtype)
        return body

    def conv_prompt():
        starts_sequence = i % MT_PER_SEQ == 0
        acc_ref[0:8, :] = jnp.where(starts_sequence, 0.0, acc_ref[TMM:TMM + 8, :])
        acc_ref[8:8 + TMM, :] = _dot(a_ref[...], wbf_ref[...])
        cw, cb = cw_ref[...], cb_ref[...]
        for r in range(0, TMM, rc):
            conv = cb + cw[3:4] * acc_ref[8 + r:8 + r + rc, :]
            for k in range(3):
                conv = conv + cw[k:k + 1] * acc_ref[5 + k + r:5 + k + r + rc, :]
            o_ref[r:r + rc, :] = _silu(conv).astype(o_ref.dtype)
        csp_ref[0] = acc_ref[TMM + 5:TMM + 8, :]

    def conv_sample():
        acc_ref[8:8 + NS_ROWS, :] = _dot(a_ref[0:NS_ROWS, :], wbf_ref[...])
        cw, cb = cw_ref[...], cb_ref[...]

        def pre(t, r):
            if t < 0:
                return cst_ref[t + 3, r:r + rc, :]
            return acc_ref[8 + t * B + r:8 + t * B + r + rc, :]

        for t in range(DEC_SEQ):
            for r in range(0, B, rc):
                conv = cb + cw[3:4] * pre(t, r)
                for k in range(3):
                    conv = conv + cw[k:k + 1] * pre(t - 3 + k, r)
                o_ref[t * B + r:t * B + r + rc, :] = _silu(conv).astype(o_ref.dtype)
        for t in range(1, DEC_SEQ):
            css_ref[t - 1] = acc_ref[8 + t * B:8 + (t + 1) * B, :]

    is_conv = j >= FIRST_CONV_BLOCK
    kinds = ((j < 2, _silu),
             (jnp.logical_and(j >= 2, j < 6), jax.nn.gelu),
             (jnp.logical_and(j >= 6, j < FIRST_CONV_BLOCK), None))
    for cond, fn in kinds:
        pl.when(jnp.logical_and(cond, i < NP_MT))(elementwise(fn, TMM))
        pl.when(jnp.logical_and(cond, i == NP_MT))(elementwise(fn, NS_ROWS))
    pl.when(jnp.logical_and(is_conv, i < NP_MT))(conv_prompt)
    pl.when(jnp.logical_and(is_conv, i == NP_MT))(conv_sample)


def _inproj_call(n1, w_in_t, cw, cb, cst_t):
    cblk = _in_conv_block
    seq = lambda j, i: jnp.where(j < FIRST_CONV_BLOCK, 0, _seq_of_mtile(i))
    return pl.pallas_call(
        _inproj_kernel,
        grid=(N_IN_BLOCKS, N_MT),
        in_specs=[pl.BlockSpec((TMM, D), lambda j, i: (i, 0)),
                  pl.BlockSpec((pl.Element(TN_IN), pl.Element(D)),
                               lambda j, i: (_in_src_row(j), 0)),
                  pl.BlockSpec((4, TN_IN), lambda j, i: (0, cblk(j))),
                  pl.BlockSpec((1, TN_IN), lambda j, i: (0, cblk(j))),
                  pl.BlockSpec((3, DEC_BATCH, TN_IN), lambda j, i: (0, 0, cblk(j)))],
        out_specs=[pl.BlockSpec((TMM, TN_IN), lambda j, i: (i, j)),
                   pl.BlockSpec((1, 3, TN_IN), lambda j, i: (seq(j, i), 0, cblk(j))),
                   pl.BlockSpec((3, DEC_BATCH, TN_IN), lambda j, i: (0, 0, cblk(j)))],
        out_shape=[jax.ShapeDtypeStruct((M_ROWS, N_IN_BLOCKS * TN_IN), BF16),
                   jax.ShapeDtypeStruct((BATCH, 3, CONV_DIM), F32),
                   jax.ShapeDtypeStruct((3, DEC_BATCH, CONV_DIM), F32)],
        scratch_shapes=[pltpu.VMEM((D, TN_IN), BF16), pltpu.VMEM((TMM + 8, TN_IN), F32)],
        compiler_params=_params("arbitrary", "arbitrary"),
        name="in_proj",
    )(n1, w_in_t, cw, cb, cst_t)


PZ, PU, PV, PGA, PGB, PX = 0, 1, 2, 3, 4, 5
PBC_1024 = 12


def _gated_group_norm(get_y, zact_ref, nw_ref, o_ref):
    gw = D // GROUPS
    for g in range(GROUPS):
        cols = slice(g * gw, (g + 1) * gw)
        gg = _rms(get_y(cols) * zact_ref[:, cols].astype(F32))
        o_ref[:, cols] = (gg * nw_ref[:, cols]).astype(o_ref.dtype)


def _ssd_prompt_kernel(z_ref, x_ref, bc_ref, dt_ref, hp_ref, hpc_ref, dskx_ref, nw_ref, triu_ref,
                       y_ref, ssm_ref, st_ref, yscr_ref):
    c = pl.program_id(1)
    T = CHUNK
    cdim = GROUPS * STATE

    @pl.when(c == 0)
    def _():
        st_ref[...] = jnp.zeros(st_ref.shape, F32)

    hp = hp_ref[...]
    dt_t = _softplus((dt_ref[...] + hp[0:1, :]).T[0:HEADS, :])
    adt_t = dt_t * (-jnp.exp(hpc_ref[0:HEADS, 1:2]))
    cs_t = _dot_exact_rhs(adt_t, triu_ref[...])
    rsub_t = cs_t - jnp.log(dt_t)
    cs = jnp.concatenate([cs_t, jnp.zeros((LANE - HEADS, T), F32)], axis=0).T

    row = lax.broadcasted_iota(jnp.int32, (T, T), 0)
    lane = lax.broadcasted_iota(jnp.int32, (T, T), 1)
    causal = row >= lane
    left = lane < HEADDIM
    mask_l = jnp.where(left, 1.0, 0.0).astype(BF16)
    nt = (((1,), (1,)), ((), ()))
    pairs_per_group = HEADS // GROUPS // 2

    for g in range(GROUPS):
        c_b = bc_ref[:, cdim + g * STATE:cdim + (g + 1) * STATE]
        b_b = bc_ref[:, g * STATE:(g + 1) * STATE]
        cb = lax.dot_general(c_b, b_b, nt, preferred_element_type=F32)
        b_t = b_b.astype(F32).T
        for k4 in range(pairs_per_group):
            k = g * pairs_per_group + k4
            cols = slice(k * LANE, (k + 1) * LANE)
            xpb = x_ref[:, cols]
            x_lo = xpb * mask_l
            xbd = jnp.concatenate([x_lo, xpb - x_lo], axis=0)
            st = st_ref[k]
            yraw = _dot(c_b, st.astype(BF16))
            scores, bws, colbs, alasts = [], [], [], []
            for h in (2 * k, 2 * k + 1):
                colb = jnp.broadcast_to(cs[:, h:h + 1], (T, T))
                alast = cs_t[h:h + 1, T - 1:T]
                decay_dt = jnp.exp(jnp.where(causal, colb - rsub_t[h:h + 1, :], -jnp.inf))
                scores.append((cb * decay_dt).astype(BF16))
                wrow = jnp.exp(alast - cs_t[h:h + 1, :]) * dt_t[h:h + 1, :]
                bws.append((b_t * wrow).astype(BF16))
                colbs.append(colb)
                alasts.append(alast)
            ecol = jnp.exp(jnp.where(left, colbs[0], colbs[1]))
            elast = jnp.exp(jnp.where(left[0:1], alasts[0], alasts[1]))
            yscr_ref[:, cols] = (_dot(jnp.concatenate(scores, axis=1), xbd) + ecol * yraw
                                 + dskx_ref[:, cols] * xpb.astype(F32))
            st_ref[k] = elast * st + _dot(jnp.concatenate(bws, axis=1), xbd)

    _gated_group_norm(lambda cols: yscr_ref[:, cols], z_ref, nw_ref, y_ref)

    @pl.when(c == N_CHUNKS - 1)
    def _():
        for k in range(HEADS // 2):
            ssm_ref[0, k] = st_ref[k].T


def _ssd_prompt_call(proj, dt_raw, hp, hpc, dskx, nw, triu):
    row = lambda b, c: b * N_CHUNKS + c
    const = lambda b, c: (0, 0)
    return pl.pallas_call(
        _ssd_prompt_kernel,
        grid=(BATCH, N_CHUNKS),
        in_specs=[pl.BlockSpec((CHUNK, D), lambda b, c: (row(b, c), PZ)),
                  pl.BlockSpec((CHUNK, D), lambda b, c: (row(b, c), PX)),
                  pl.BlockSpec((CHUNK, 2 * GROUPS * STATE), lambda b, c: (row(b, c), PBC_1024)),
                  pl.BlockSpec((CHUNK, LANE), lambda b, c: (row(b, c), 0)),
                  pl.BlockSpec((8, LANE), const),
                  pl.BlockSpec((LANE, 8), const),
                  pl.BlockSpec((1, D), const),
                  pl.BlockSpec((1, D), const),
                  pl.BlockSpec((CHUNK, CHUNK), const)],
        out_specs=[pl.BlockSpec((CHUNK, D), lambda b, c: (row(b, c), 0)),
                   pl.BlockSpec((1, HEADS // 2, LANE, STATE), lambda b, c: (b, 0, 0, 0))],
        out_shape=[jax.ShapeDtypeStruct((NP_ROWS, D), BF16),
                   jax.ShapeDtypeStruct((BATCH, HEADS // 2, LANE, STATE), F32)],
        scratch_shapes=[pltpu.VMEM((HEADS // 2, STATE, LANE), F32),
                        pltpu.VMEM((CHUNK, D), F32)],
        compiler_params=_params("arbitrary", "arbitrary"),
        name="ssd_prompt",
    )(proj, proj, proj, dt_raw, hp, hpc, dskx, nw, triu)


def _ssds_prep_step(tt, x_ref, bc_ref, dt_ref, hp_ref, dskx_ref, ex_ref, seg_ref,
                    ypre_ref, efull_ref, xw_ref, bb_ref, cc_ref, e3_ref):
    B = DEC_BATCH
    cdim = GROUPS * STATE
    blk = lambda t: slice(t * B, (t + 1) * B)
    x_of = lambda t: x_ref[blk(t), :].astype(F32)
    b_of = lambda t: bc_ref[blk(t), 0:cdim].astype(F32)
    c_of = lambda t: bc_ref[blk(t), cdim:2 * cdim].astype(F32)

    bb_ref[...] = b_of(tt)
    cc_ref[...] = c_of(tt)

    hp = hp_ref[...]
    a_neg = -jnp.exp(hp[1:2, :])
    dts, css = [], []
    run = None
    for t in range(DEC_SEQ):
        dt = _softplus(dt_ref[blk(t), :] + hp[0:1, :])
        run = dt * a_neg if run is None else run + dt * a_neg
        dts.append(dt)
        css.append(run)

    ex = ex_ref[...]
    e3_ref[...] = jnp.exp(css[-1])
    efull_ref[...] = _dot_exact_rhs(jnp.exp(css[tt]), ex)
    w_t = jnp.exp(css[-1] - css[tt]) * dts[tt]
    xw_ref[...] = x_of(tt) * _dot_exact_rhs(w_t, ex)

    seg = seg_ref[...]
    acc = dskx_ref[...] * x_of(tt)
    c_t = c_of(tt)
    for s in range(tt + 1):
        cbh = _dot_exact_rhs(c_t * b_of(s), seg)
        g_ts = cbh * jnp.exp(css[tt] - css[s]) * dts[s]
        acc = acc + _dot_exact_rhs(g_ts, ex) * x_of(s)
    ypre_ref[...] = acc


def _ssds_prep_kernel(*refs):
    t = pl.program_id(0)
    for tt in range(DEC_SEQ):
        pl.when(t == tt)(functools.partial(_ssds_prep_step, tt, *refs))


def _ssds_prep_call(proj, dt_raw, hp, dskx, ex, seg):
    full = lambda shape: pl.BlockSpec(shape, lambda t: (0,) * len(shape))
    step = lambda width: pl.BlockSpec((DEC_BATCH, width), lambda t: (t, 0))
    cdim = GROUPS * STATE
    return pl.pallas_call(
        _ssds_prep_kernel,
        grid=(DEC_SEQ,),
        in_specs=[pl.BlockSpec((NS_ROWS, D), lambda t: (NP_TILES, PX)),
                  pl.BlockSpec((NS_ROWS, 2 * cdim), lambda t: (NP_TILES, PBC_1024)),
                  pl.BlockSpec((NS_ROWS, LANE), lambda t: (NP_TILES, 0)),
                  full((8, LANE)), full((1, D)), full((LANE, D)), full((cdim, LANE))],
        out_specs=[step(D), step(D), step(D), step(cdim), step(cdim),
                   full((DEC_BATCH, LANE))],
        out_shape=[jax.ShapeDtypeStruct((NS_ROWS, D), F32),
                   jax.ShapeDtypeStruct((NS_ROWS, D), F32),
                   jax.ShapeDtypeStruct((NS_ROWS, D), F32),
                   jax.ShapeDtypeStruct((NS_ROWS, cdim), F32),
                   jax.ShapeDtypeStruct((NS_ROWS, cdim), F32),
                   jax.ShapeDtypeStruct((DEC_BATCH, LANE), F32)],
        compiler_params=_params("arbitrary"),
        name="ssd_sample_prep",
    )(proj, proj, dt_raw, hp, dskx, ex, seg)


SEQ_PER_STEP = 8


def _ssds_state_kernel(e3_ref, st_ref, cc_ref, bb_ref, xw_ref, yo_ref, so_ref):
    blk = pl.program_id(0)
    nt = (((1,), (1,)), ((), ()))
    tn = (((0,), (0,)), ((), ()))
    hpg = HEADS // GROUPS
    gw = hpg * HEADDIM

    def rows_of(ref, s, cols):
        v = ref[:, s, cols]
        return jnp.concatenate([v, jnp.zeros((8 - DEC_SEQ, v.shape[-1]), F32)], axis=0).astype(BF16)

    for s in range(SEQ_PER_STEP):
        b = blk * SEQ_PER_STEP + s
        for g in range(GROUPS):
            h0 = st_ref[s, g]
            c_g = rows_of(cc_ref, s, slice(g * STATE, (g + 1) * STATE))
            yraw = lax.dot_general(c_g, h0.astype(BF16), nt, preferred_element_type=F32)
            yo_ref[:, s, g * gw:(g + 1) * gw] = yraw[0:DEC_SEQ]
            x_g = rows_of(xw_ref, s, slice(g * gw, (g + 1) * gw))
            b_g = rows_of(bb_ref, s, slice(g * STATE, (g + 1) * STATE))
            dh = lax.dot_general(x_g, b_g, tn, preferred_element_type=F32)
            for hh in range(hpg):
                rows = slice(hh * HEADDIM, (hh + 1) * HEADDIM)
                so_ref[s, g, rows, :] = e3_ref[b, g * hpg + hh] * h0[rows] + dh[rows]


def _ssds_state_call(e3, state, cc_t, bb_t, xw_t):
    sb = SEQ_PER_STEP
    gw = (HEADS // GROUPS) * HEADDIM
    cdim = GROUPS * STATE
    tmajor = lambda width: pl.BlockSpec((DEC_SEQ, sb, width), lambda i: (0, i, 0))
    return pl.pallas_call(
        _ssds_state_kernel,
        grid=(DEC_BATCH // sb,),
        in_specs=[pl.BlockSpec(memory_space=pltpu.SMEM),
                  pl.BlockSpec((sb, GROUPS, gw, STATE), lambda i: (i, 0, 0, 0)),
                  tmajor(cdim), tmajor(cdim), tmajor(D)],
        out_specs=[tmajor(D),
                   pl.BlockSpec((sb, GROUPS, gw, STATE), lambda i: (i, 0, 0, 0))],
        out_shape=[jax.ShapeDtypeStruct((DEC_SEQ, DEC_BATCH, D), F32),
                   jax.ShapeDtypeStruct((DEC_BATCH, GROUPS, gw, STATE), F32)],
        compiler_params=_params("arbitrary"),
        name="ssd_sample_state",
    )(e3, state, cc_t.reshape(DEC_SEQ, DEC_BATCH, cdim), bb_t.reshape(DEC_SEQ, DEC_BATCH, cdim),
      xw_t.reshape(DEC_SEQ, DEC_BATCH, D))


def _ssds_post_kernel(ypre_ref, efull_ref, yo_ref, z_ref, nw_ref, o_ref):
    get_y = lambda cols: ypre_ref[:, cols] + efull_ref[:, cols] * yo_ref[:, cols]
    _gated_group_norm(get_y, z_ref, nw_ref, o_ref)


def _ssds_post_call(ypre, efull, yo_t, proj, nw):
    step = pl.BlockSpec((DEC_BATCH, D), lambda t: (t, 0))
    return pl.pallas_call(
        _ssds_post_kernel,
        grid=(DEC_SEQ,),
        in_specs=[step, step, step,
                  pl.BlockSpec((DEC_BATCH, D), lambda t: (NP_ROWS // DEC_BATCH + t, PZ)),
                  pl.BlockSpec((1, D), lambda t: (0, 0))],
        out_specs=step,
        out_shape=jax.ShapeDtypeStruct((NS_ROWS, D), BF16),
        compiler_params=_params("arbitrary"),
        name="ssd_sample_post",
    )(ypre, efull, yo_t, proj, nw)


def _mlp_kernel(u_ref, v_ref, lnw_ref, lnb_ref, ws_ref, bsx_ref, wsx_ref, bsx4_ref,
                y_ref, cv_ref, wm_ref, cvs_ref):
    i = pl.program_id(0)
    T = CHUNK

    @pl.when(i == 0)
    def _():
        row = lax.broadcasted_iota(jnp.int32, (T, T), 0)
        lane = lax.broadcasted_iota(jnp.int32, (T, T), 1)
        for g in range(MLP_GROUPS):
            wm_ref[g] = jnp.where(row >= lane, ws_ref[g], 0.0).astype(BF16)

    def vnorm(rows):
        vg = v_ref[rows, :].astype(F32)
        xc = vg - jnp.mean(vg, axis=-1, keepdims=True)
        y = xc * lax.rsqrt(jnp.mean(xc * xc, axis=-1, keepdims=True) + EPS)
        return y * lnw_ref[...] + lnb_ref[...]

    @pl.when(i < NP_TILES)
    def _():
        for cc in range(TM // T):
            rows = slice(cc * T, (cc + 1) * T)
            vnb = vnorm(rows).astype(BF16)
            for g in range(MLP_GROUPS):
                cols = slice(g * MLP_GROUP_DIM, (g + 1) * MLP_GROUP_DIM)
                sv = _dot(wm_ref[g], vnb[:, cols]) + bsx_ref[:, cols]
                y_ref[rows, cols] = (u_ref[rows, cols].astype(F32) * sv).astype(BF16)

    @pl.when(i >= NP_TILES)
    def _():
        B = DEC_BATCH
        for t in range(DEC_SEQ):
            rows = slice(t * B, (t + 1) * B)
            vn = vnorm(rows)
            cvs_ref[rows, :] = vn
            cv_ref[:, t, :] = vn
        for t in range(DEC_SEQ):
            rows = slice(t * B, (t + 1) * B)
            acc = bsx4_ref[t:t + 1, :]
            for s in range(t + 1):
                acc = acc + wsx_ref[4 * t + s:4 * t + s + 1, :] * cvs_ref[s * B:(s + 1) * B, :]
            y_ref[rows, :] = (u_ref[rows, :].astype(F32) * acc).astype(BF16)


def _mlp_call(proj, lnw, lnb, ws, bsx, wsx, bsx4):
    full = lambda shape: pl.BlockSpec(shape, lambda i: (0,) * len(shape))
    return pl.pallas_call(
        _mlp_kernel,
        grid=(N_TILES,),
        in_specs=[pl.BlockSpec((TM, D), lambda i: (i, PU)),
                  pl.BlockSpec((TM, D), lambda i: (i, PV)),
                  full((1, D)), full((1, D)),
                  full((MLP_GROUPS, CHUNK, CHUNK)),
                  full((CHUNK, D)), full((16, D)), full((8, D))],
        out_specs=[pl.BlockSpec((TM, D), lambda i: (i, 0)),
                   full((DEC_BATCH, DEC_SEQ, D))],
        out_shape=[jax.ShapeDtypeStruct((M_ROWS, D), BF16),
                   jax.ShapeDtypeStruct((DEC_BATCH, DEC_SEQ, D), F32)],
        scratch_shapes=[pltpu.VMEM((MLP_GROUPS, CHUNK, CHUNK), BF16),
                        pltpu.VMEM((NS_ROWS, D), F32)],
        compiler_params=_params("arbitrary"),
        name="gmlp",
    )(proj, proj, lnw, lnb, ws, bsx, wsx, bsx4)


def _cast_rows(src_ref, dst_ref, chunk=256):
    def body(r, carry):
        rows = pl.ds(pl.multiple_of(r * chunk, chunk), chunk)
        dst_ref[rows, :] = src_ref[rows, :].astype(dst_ref.dtype)
        return carry
    lax.fori_loop(0, src_ref.shape[0] // chunk, body, 0)


def _per_tile(i, prompt_fn, sample_fn):
    pl.when(i < NP_MT)(prompt_fn)
    pl.when(i == NP_MT)(sample_fn)


def _merge_kernel(ysp_ref, yss_ref, ym_ref, ga_ref, gb_ref, w1_ref, w2_ref, o_ref, w1b, w2b):
    i = pl.program_id(1)

    @pl.when(i == 0)
    def _():
        _cast_rows(w1_ref, w1b)
        _cast_rows(w2_ref, w2b)

    def emit(ys, rows):
        a1 = _dot(ys, w1b[...])
        a2 = _dot(ym_ref[rows, :], w2b[...])
        o_ref[rows, :] = (jax.nn.sigmoid(ga_ref[rows, :].astype(F32)) * a1
                          + jax.nn.sigmoid(gb_ref[rows, :].astype(F32)) * a2).astype(BF16)

    _per_tile(i,
              lambda: emit(ysp_ref[...], slice(None)),
              lambda: emit(yss_ref[...], slice(0, NS_ROWS)))


def _merge_call(ysp, yss, ym, proj, w1, w2):
    tn = 512
    nb = D // tn
    return pl.pallas_call(
        _merge_kernel,
        grid=(nb, N_MT),
        in_specs=[pl.BlockSpec((TMM, D), lambda j, i: (jnp.minimum(i, NP_MT - 1), 0)),
                  pl.BlockSpec((NS_ROWS, D), lambda j, i: (0, 0)),
                  pl.BlockSpec((TMM, D), lambda j, i: (i, 0)),
                  pl.BlockSpec((TMM, tn), lambda j, i: (i, PGA * nb + j)),
                  pl.BlockSpec((TMM, tn), lambda j, i: (i, PGB * nb + j)),
                  pl.BlockSpec((D, tn), lambda j, i: (0, j)),
                  pl.BlockSpec((D, tn), lambda j, i: (0, j))],
        out_specs=pl.BlockSpec((TMM, tn), lambda j, i: (i, j)),
        out_shape=jax.ShapeDtypeStruct((M_ROWS, D), BF16),
        scratch_shapes=[pltpu.VMEM((D, tn), BF16), pltpu.VMEM((D, tn), BF16)],
        compiler_params=_params("arbitrary", "arbitrary"),
        name="branch_merge",
    )(ysp, yss, ym, proj, proj, w1, w2)


def _resid_kernel(a_ref, w_ref, rp_ref, rs_ref, gp_ref, gs_ref, o_ref, wb):
    i = pl.program_id(1)

    @pl.when(i == 0)
    def _():
        _cast_rows(w_ref, wb)

    def prompt():
        o_ref[...] = rp_ref[...] + gp_ref[0] * _dot(a_ref[...], wb[...])

    def sample():
        acc = _dot(a_ref[0:NS_ROWS, :], wb[...])
        tn = acc.shape[-1]
        acc3 = acc.reshape(DEC_SEQ, DEC_BATCH, tn) * gs_ref[...][None]
        o_ref[0:NS_ROWS, :] = rs_ref[...] + acc3.reshape(NS_ROWS, tn)

    _per_tile(i, prompt, sample)


def _resid_call(a, w, rp, rs, rs_block, mod_p, mod_s, k_gate, tn, name, single_buffer_w=False):
    kdim = a.shape[1]
    nb = D // tn
    w_mode = dict(pipeline_mode=pl.Buffered(1)) if single_buffer_w else {}
    return pl.pallas_call(
        _resid_kernel,
        grid=(nb, N_MT),
        in_specs=[pl.BlockSpec((TMM, kdim), lambda j, i: (i, 0)),
                  pl.BlockSpec((kdim, tn), lambda j, i: (0, j), **w_mode),
                  pl.BlockSpec((TMM, tn), lambda j, i: (jnp.minimum(i, NP_MT - 1), j)),
                  pl.BlockSpec((NS_ROWS, tn), lambda j, i: (rs_block, j)),
                  pl.BlockSpec((1, 1, tn), lambda j, i: (_seq_of_mtile(i), 0, k_gate * nb + j)),
                  pl.BlockSpec((DEC_BATCH, tn), lambda j, i: (0, k_gate * nb + j))],
        out_specs=pl.BlockSpec((TMM, tn), lambda j, i: (i, j)),
        out_shape=jax.ShapeDtypeStruct((M_ROWS, D), F32),
        scratch_shapes=[pltpu.VMEM((kdim, tn), BF16)],
        compiler_params=_params("arbitrary", "arbitrary"),
        name=name,
    )(a, w, rp, rs, mod_p, mod_s)


TN_FF = 512
N_FF_BLOCKS = D_FF // TN_FF


def _up_kernel(a_ref, wa_ref, wv_ref, cw_ref, cb_ref, fst_ref,
               h_ref, fcp_ref, fcs_ref, wab, wvb, acc_a, acc_v):
    i = pl.program_id(1)
    rc = 64

    @pl.when(i == 0)
    def _():
        _cast_rows(wa_ref, wab)
        _cast_rows(wv_ref, wvb)

    def prompt():
        starts_sequence = i % MT_PER_SEQ == 0
        acc_a[0:8, :] = jnp.where(starts_sequence, 0.0, acc_a[TMM:TMM + 8, :])
        x = a_ref[...]
        acc_a[8:8 + TMM, :] = _dot(x, wab[...])
        acc_v[...] = _dot(x, wvb[...])
        cw, cb = cw_ref[...], cb_ref[...]
        for r in range(0, TMM, rc):
            conv = (cb + cw[2:3] * acc_a[8 + r:8 + r + rc, :]
                    + cw[1:2] * acc_a[7 + r:7 + r + rc, :]
                    + cw[0:1] * acc_a[6 + r:6 + r + rc, :])
            h_ref[r:r + rc, :] = (jax.nn.gelu(conv) * acc_v[r:r + rc, :]).astype(BF16)
        fcp_ref[0] = acc_a[TMM + 6:TMM + 8, :]

    def sample():
        B = DEC_BATCH
        x = a_ref[0:NS_ROWS, :]
        acc_a[8:8 + NS_ROWS, :] = _dot(x, wab[...])
        acc_v[0:NS_ROWS, :] = _dot(x, wvb[...])
        cw, cb = cw_ref[...], cb_ref[...]

        def pre(t, r):
            if t < 0:
                return fst_ref[r:r + rc, t + 2, :]
            return acc_a[8 + t * B + r:8 + t * B + r + rc, :]

        for t in range(DEC_SEQ):
            for r in range(0, B, rc):
                conv = cb + cw[2:3] * pre(t, r) + cw[1:2] * pre(t - 1, r) + cw[0:1] * pre(t - 2, r)
                h_ref[t * B + r:t * B + r + rc, :] = (
                    jax.nn.gelu(conv) * acc_v[t * B + r:t * B + r + rc, :]).astype(BF16)
        fcs_ref[:, 0, :] = acc_a[8 + 2 * B:8 + 3 * B, :]
        fcs_ref[:, 1, :] = acc_a[8 + 3 * B:8 + 4 * B, :]

    _per_tile(i, prompt, sample)


def _up_call(n2, w_up, cw, cb, fst):
    return pl.pallas_call(
        _up_kernel,
        grid=(N_FF_BLOCKS, N_MT),
        in_specs=[pl.BlockSpec((TMM, D), lambda j, i: (i, 0)),
                  pl.BlockSpec((D, TN_FF), lambda j, i: (0, j)),
                  pl.BlockSpec((D, TN_FF), lambda j, i: (0, N_FF_BLOCKS + j)),
                  pl.BlockSpec((3, TN_FF), lambda j, i: (0, j)),
                  pl.BlockSpec((1, TN_FF), lambda j, i: (0, j)),
                  pl.BlockSpec((DEC_BATCH, 2, TN_FF), lambda j, i: (0, 0, j))],
        out_specs=[pl.BlockSpec((TMM, TN_FF), lambda j, i: (i, j)),
                   pl.BlockSpec((1, 2, TN_FF), lambda j, i: (_seq_of_mtile(i), 0, j)),
                   pl.BlockSpec((DEC_BATCH, 2, TN_FF), lambda j, i: (0, 0, j))],
        out_shape=[jax.ShapeDtypeStruct((M_ROWS, D_FF), BF16),
                   jax.ShapeDtypeStruct((BATCH, 2, D_FF), F32),
                   jax.ShapeDtypeStruct((DEC_BATCH, 2, D_FF), F32)],
        scratch_shapes=[pltpu.VMEM((D, TN_FF), BF16), pltpu.VMEM((D, TN_FF), BF16),
                        pltpu.VMEM((TMM + 8, TN_FF), F32), pltpu.VMEM((TMM, TN_FF), F32)],
        compiler_params=_params("arbitrary", "arbitrary"),
        name="ffn_up",
    )(n2, w_up, w_up, cw, cb, fst)


def _final_kernel(x_ref, w_ref, yp_ref, ys_ref):
    i = pl.program_id(0)
    y = _rms(x_ref[...]) * w_ref[...]

    @pl.when(i < NP_TILES)
    def _():
        yp_ref[...] = y

    @pl.when(i >= NP_TILES)
    def _():
        for t in range(DEC_SEQ):
            ys_ref[:, t, :] = y[t * DEC_BATCH:(t + 1) * DEC_BATCH]


def _final_call(x3, w):
    return pl.pallas_call(
        _final_kernel,
        grid=(N_TILES,),
        in_specs=[pl.BlockSpec((TM, D), lambda i: (i, 0)),
                  pl.BlockSpec((1, D), lambda i: (0, 0))],
        out_specs=[pl.BlockSpec((TM, D), lambda i: (jnp.minimum(i, NP_TILES - 1), 0)),
                   pl.BlockSpec((DEC_BATCH, DEC_SEQ, D), lambda i: (0, 0, 0))],
        out_shape=[jax.ShapeDtypeStruct((NP_ROWS, D), F32),
                   jax.ShapeDtypeStruct((DEC_BATCH, DEC_SEQ, D), F32)],
        compiler_params=_params("arbitrary"),
        name="final_norm",
    )(x3, w)


def _to_time_major(a):
    return jnp.transpose(a, (1, 0, 2))


def kernel(x_prompt, x_sample, state_ssm, state_ssd_conv, state_ffn_conv, c_prompt, c_sample,
           norm1_w, w_ada, b_ada, w_in, ssd_conv_w, ssd_conv_b, dt_bias, a_log, d_skip,
           ssd_norm_w, mlp_ln_w, mlp_ln_b, w_spatial, b_spatial, w_ssd_o, w_mlp_o, w_out,
           norm2_w, w_up, ffn_conv_w, ffn_conv_b, w_down, final_norm_w):
    assert w_in.shape[0] == 1, "single-layer trunk"
    row = lambda v: v.reshape(1, -1)

    xp = x_prompt.reshape(NP_ROWS, D)
    xs = _to_time_major(x_sample).reshape(NS_ROWS, D)

    mod_p, mod_s = _ada_call(c_prompt, c_sample, w_ada[0], row(b_ada[0]))
    mod_p = mod_p.reshape(BATCH, 1, 6 * D)
    K_SHIFT1, K_SCALE1, K_GATE1, K_SHIFT2, K_SCALE2, K_GATE2 = range(6)

    xp_spec = pl.BlockSpec((TM, D), lambda i: (jnp.minimum(i, NP_TILES - 1), 0))
    xs_spec = pl.BlockSpec((TM, D), lambda i: (0, 0))
    w_in_t = w_in[0].T
    w_dt_t = w_in_t[DT_COL:DT_COL + LANE]
    n1, dt_raw = _norm_call(xp, xp_spec, xs, xs_spec, row(norm1_w[0]), mod_p, mod_s,
                            K_SCALE1, K_SHIFT1, w_dt_t)
    cst_t = _to_time_major(state_ssd_conv[0])
    proj, cst_p, ncs_t = _inproj_call(n1, w_in_t, ssd_conv_w[0], row(ssd_conv_b[0]), cst_t)

    hp = jnp.zeros((8, LANE), F32)
    hp = hp.at[0, :HEADS].set(dt_bias[0]).at[1, :HEADS].set(a_log[0]).at[2, :HEADS].set(d_skip[0])
    hpc = hp.T
    triu = jnp.triu(jnp.ones((CHUNK, CHUNK), F32)).astype(BF16)
    dskx = row(jnp.repeat(d_skip[0], HEADDIM))
    nw = row(ssd_norm_w[0])

    y_ssd_p, ssm_p = _ssd_prompt_call(proj, dt_raw, hp, hpc, dskx, nw, triu)

    head_of_col = jnp.arange(D) // HEADDIM
    ex = (jnp.arange(LANE)[:, None] == head_of_col[None, :]).astype(BF16)
    grp_of_head = jnp.arange(LANE) // (HEADS // GROUPS)
    seg = ((jnp.arange(GROUPS * STATE)[:, None] // STATE == grp_of_head[None, :])
           & (jnp.arange(LANE)[None, :] < HEADS)).astype(BF16)
    ypre, efull, xw_t, bb_t, cc_t, e3 = _ssds_prep_call(proj, dt_raw, hp, dskx, ex, seg)
    gw = (HEADS // GROUPS) * HEADDIM
    yo_t, ssm_s = _ssds_state_call(
        e3[:, :HEADS], state_ssm[0].reshape(DEC_BATCH, GROUPS, gw, STATE), cc_t, bb_t, xw_t)
    y_ssd_s = _ssds_post_call(ypre, efull, yo_t.reshape(NS_ROWS, D), proj, nw)

    per_col = lambda a: jnp.repeat(a, MLP_GROUP_DIM, axis=1)
    bsx = per_col(b_spatial[0][:, :CHUNK].T)
    ws4 = w_spatial[0][:, :DEC_SEQ, :DEC_SEQ]
    wsx = per_col(jnp.transpose(ws4, (1, 2, 0)).reshape(DEC_SEQ * DEC_SEQ, MLP_GROUPS))
    bsx4 = jnp.pad(bsx[:DEC_SEQ], ((0, 8 - DEC_SEQ), (0, 0)))
    y_mlp, cv_s = _mlp_call(proj, row(mlp_ln_w[0]), row(mlp_ln_b[0]), w_spatial[0], bsx, wsx, bsx4)

    mixed = _merge_call(y_ssd_p, y_ssd_s, y_mlp, proj, w_ssd_o[0], w_mlp_o[0])
    x2 = _resid_call(mixed, w_out[0], xp, xs, 0, mod_p, mod_s, K_GATE1, 1024, "out_proj")

    x2p_spec = pl.BlockSpec((TM, D), lambda i: (jnp.minimum(i, NP_TILES - 1), 0))
    x2s_spec = pl.BlockSpec((TM, D), lambda i: (NP_TILES, 0))
    (n2,) = _norm_call(x2, x2p_spec, x2, x2s_spec, row(norm2_w[0]), mod_p, mod_s,
                       K_SCALE2, K_SHIFT2, None)
    h, ffn_p, ffn_s = _up_call(n2, w_up[0], ffn_conv_w[0], row(ffn_conv_b[0]), state_ffn_conv[0])
    x3 = _resid_call(h, w_down[0], x2, x2, NP_TILES, mod_p, mod_s, K_GATE2, 512, "ffn_down",
                     single_buffer_w=True)

    y_p, y_s = _final_call(x3, row(final_norm_w))

    from_t = lambda a, t, c: jnp.transpose(a.reshape(t, DEC_BATCH, c), (1, 0, 2))
    return (y_p.reshape(BATCH, SEQ, D),
            y_s,
            ssm_p.reshape(1, BATCH, HEADS, HEADDIM, STATE),
            ssm_s.reshape(1, DEC_BATCH, HEADS, HEADDIM, STATE),
            cst_p[None],
            from_t(ncs_t, 3, CONV_DIM)[None],
            ffn_p[None],
            ffn_s[None],
            cv_s[None])
```

```python
import functools

import jax
import jax.numpy as jnp
from jax import lax
from jax.experimental import pallas as pl
from jax.experimental.pallas import tpu as pltpu

F32 = jnp.float32
BF16 = jnp.bfloat16

D = 2048
BATCH, SEQ = 4, 2048
DEC_BATCH, DEC_SEQ = 128, 4
NP_ROWS = BATCH * SEQ
NS_ROWS = DEC_BATCH * DEC_SEQ
M_ROWS = NP_ROWS + NS_ROWS
TM = 512
NP_TILES = NP_ROWS // TM
N_TILES = M_ROWS // TM
TILES_PER_SEQ = SEQ // TM
TMM = 1024
NP_MT = NP_ROWS // TMM
N_MT = NP_MT + 1
MT_PER_SEQ = SEQ // TMM
HEADS, HEADDIM, GROUPS, STATE = 32, 64, 4, 128
CHUNK = 128
N_CHUNKS = SEQ // CHUNK
CONV_DIM = D + 2 * GROUPS * STATE
MLP_GROUPS = 8
MLP_GROUP_DIM = D // MLP_GROUPS
D_FF = 5632
EPS = 1e-6
DT_COL = D + CONV_DIM
TN_IN = 1024
LANE = 128
VMEM_LIMIT = 56 * 1024 * 1024


def _params(*sem, flags=None):
    return pltpu.CompilerParams(dimension_semantics=sem, vmem_limit_bytes=VMEM_LIMIT, flags=flags)


def _dot(a, b):
    return jnp.dot(a, b, preferred_element_type=F32)


def _split_bf16(v, terms):
    out = []
    r = v
    for _ in range(terms):
        p = r.astype(BF16)
        out.append(p)
        r = r - p.astype(F32)
    return out


def _dot_exact_rhs(v, e, terms=3):
    acc = None
    for p in _split_bf16(v, terms):
        d = _dot(p, e)
        acc = d if acc is None else acc + d
    return acc


def _dot_exact_lhs(t, v, terms=3):
    acc = None
    for p in _split_bf16(v, terms):
        d = _dot(t, p)
        acc = d if acc is None else acc + d
    return acc


def _silu(x):
    return x * jax.nn.sigmoid(x)


def _softplus(x):
    return jnp.maximum(x, 0.0) + jnp.log1p(jnp.exp(-jnp.abs(x)))


def _rms(x):
    return x * lax.rsqrt(jnp.mean(x * x, axis=-1, keepdims=True) + EPS)


def _ada_kernel(cp_ref, cs_ref, w_ref, b_ref, op_ref, os_ref):
    w = w_ref[...].astype(BF16)
    cp8 = jnp.concatenate([cp_ref[...], jnp.zeros((8 - BATCH, D), F32)], axis=0)
    op_ref[...] = (_dot(_silu(cp8).astype(BF16), w) + b_ref[...])[0:BATCH]
    os_ref[...] = _dot(_silu(cs_ref[...]).astype(BF16), w) + b_ref[...]


def _ada_call(c_prompt, c_sample, w, b):
    tn = 1024
    return pl.pallas_call(
        _ada_kernel,
        grid=(6 * D // tn,),
        in_specs=[pl.BlockSpec((BATCH, D), lambda j: (0, 0)),
                  pl.BlockSpec((DEC_BATCH, D), lambda j: (0, 0)),
                  pl.BlockSpec((D, tn), lambda j: (0, j)),
                  pl.BlockSpec((1, tn), lambda j: (0, j))],
        out_specs=[pl.BlockSpec((BATCH, tn), lambda j: (0, j)),
                   pl.BlockSpec((DEC_BATCH, tn), lambda j: (0, j))],
        out_shape=[jax.ShapeDtypeStruct((BATCH, 6 * D), F32),
                   jax.ShapeDtypeStruct((DEC_BATCH, 6 * D), F32)],
        compiler_params=_params("arbitrary"),
        name="ada_mod",
    )(c_prompt, c_sample, w, b)


def _norm_kernel(with_dt, xp_ref, xs_ref, nw_ref, scp_ref, shp_ref, scs_ref, shs_ref, *rest):
    if with_dt:
        wdt_ref, n_ref, dt_ref = rest
    else:
        (n_ref,) = rest
    i = pl.program_id(0)

    def emit(n):
        nb = n.astype(BF16)
        n_ref[...] = nb
        if with_dt:
            dt_ref[...] = lax.dot_general(nb, wdt_ref[...].astype(BF16),
                                          (((1,), (1,)), ((), ())),
                                          preferred_element_type=F32)

    @pl.when(i < NP_TILES)
    def _():
        y = _rms(xp_ref[...]) * nw_ref[...]
        emit(y * (1.0 + scp_ref[0]) + shp_ref[0])

    @pl.when(i >= NP_TILES)
    def _():
        y = _rms(xs_ref[...]) * nw_ref[...]
        y3 = y.reshape(DEC_SEQ, DEC_BATCH, D)
        emit((y3 * (1.0 + scs_ref[...])[None] + shs_ref[...][None]).reshape(TM, D))


def _seq_of_tile(i):
    return jnp.minimum(i // TILES_PER_SEQ, BATCH - 1)


def _seq_of_mtile(i):
    return jnp.minimum(i // MT_PER_SEQ, BATCH - 1)


def _norm_call(xp, xp_spec, xs, xs_spec, nw, mod_p, mod_s, k_scale, k_shift, w_dt):
    with_dt = w_dt is not None
    in_specs = [
        xp_spec, xs_spec,
        pl.BlockSpec((1, D), lambda i: (0, 0)),
        pl.BlockSpec((1, 1, D), lambda i: (_seq_of_tile(i), 0, k_scale)),
        pl.BlockSpec((1, 1, D), lambda i: (_seq_of_tile(i), 0, k_shift)),
        pl.BlockSpec((DEC_BATCH, D), lambda i: (0, k_scale)),
        pl.BlockSpec((DEC_BATCH, D), lambda i: (0, k_shift)),
    ]
    args = [xp, xs, nw, mod_p, mod_p, mod_s, mod_s]
    out_specs = [pl.BlockSpec((TM, D), lambda i: (i, 0))]
    out_shape = [jax.ShapeDtypeStruct((M_ROWS, D), BF16)]
    if with_dt:
        in_specs.append(pl.BlockSpec((LANE, D), lambda i: (0, 0)))
        args.append(w_dt)
        out_specs.append(pl.BlockSpec((TM, LANE), lambda i: (i, 0)))
        out_shape.append(jax.ShapeDtypeStruct((M_ROWS, LANE), F32))
    return pl.pallas_call(
        functools.partial(_norm_kernel, with_dt),
        grid=(N_TILES,),
        in_specs=in_specs,
        out_specs=out_specs,
        out_shape=out_shape,
        compiler_params=_params("arbitrary"),
        name="norm_mod_dt" if with_dt else "norm_mod",
    )(*args)


N_IN_BLOCKS = 13
UVG_ROW = DT_COL + HEADS
FIRST_CONV_BLOCK = 10


def _in_src_row(j):
    row = jnp.where(j < 2, j * TN_IN,
                    jnp.where(j < FIRST_CONV_BLOCK, UVG_ROW + (j - 2) * TN_IN,
                              D + (j - FIRST_CONV_BLOCK) * TN_IN))
    return pl.multiple_of(row, HEADS)


def _in_conv_block(j):
    return jnp.maximum(j - FIRST_CONV_BLOCK, 0)


def _inproj_kernel(a_ref, wt_ref, cw_ref, cb_ref, cst_ref, o_ref, csp_ref, css_ref, wbf_ref, acc_ref):
    j = pl.program_id(0)
    i = pl.program_id(1)
    rc = 64
    B = DEC_BATCH

    @pl.when(i == 0)
    def _():
        for r in range(TN_IN // LANE):
            rows = slice(r * LANE, (r + 1) * LANE)
            wbf_ref[:, rows] = wt_ref[rows, :].T.astype(BF16)

    def elementwise(fn, rows):
        def body():
            acc_ref[8:8 + rows, :] = _dot(a_ref[0:rows, :], wbf_ref[...])
            for r in range(0, rows, rc):
                val = acc_ref[8 + r:8 + r + rc, :]
                o_ref[r:r + rc, :] = (val if fn is None else fn(val)).astype(o_ref.dtype)
        return body

    def conv_prompt():
        starts_sequence = i % MT_PER_SEQ == 0
        acc_ref[0:8, :] = jnp.where(starts_sequence, 0.0, acc_ref[TMM:TMM + 8, :])
        acc_ref[8:8 + TMM, :] = _dot(a_ref[...], wbf_ref[...])
        cw, cb = cw_ref[...], cb_ref[...]
        for r in range(0, TMM, rc):
            conv = cb + cw[3:4] * acc_ref[8 + r:8 + r + rc, :]
            for k in range(3):
                conv = conv + cw[k:k + 1] * acc_ref[5 + k + r:5 + k + r + rc, :]
            o_ref[r:r + rc, :] = _silu(conv).astype(o_ref.dtype)
        csp_ref[0] = acc_ref[TMM + 5:TMM + 8, :]

    def conv_sample():
        acc_ref[8:8 + NS_ROWS, :] = _dot(a_ref[0:NS_ROWS, :], wbf_ref[...])
        cw, cb = cw_ref[...], cb_ref[...]

        def pre(t, r):
            if t < 0:
                return cst_ref[t + 3, r:r + rc, :]
            return acc_ref[8 + t * B + r:8 + t * B + r + rc, :]

        for t in range(DEC_SEQ):
            for r in range(0, B, rc):
                conv = cb + cw[3:4] * pre(t, r)
                for k in range(3):
                    conv = conv + cw[k:k + 1] * pre(t - 3 + k, r)
                o_ref[t * B + r:t * B + r + rc, :] = _silu(conv).astype(o_ref.dtype)
        for t in range(1, DEC_SEQ):
            css_ref[t - 1] = acc_ref[8 + t * B:8 + (t + 1) * B, :]

    is_conv = j >= FIRST_CONV_BLOCK
    kinds = ((j < 2, _silu),
             (jnp.logical_and(j >= 2, j < 6), jax.nn.gelu),
             (jnp.logical_and(j >= 6, j < FIRST_CONV_BLOCK), None))
    for cond, fn in kinds:
        pl.when(jnp.logical_and(cond, i < NP_MT))(elementwise(fn, TMM))
        pl.when(jnp.logical_and(cond, i == NP_MT))(elementwise(fn, NS_ROWS))
    pl.when(jnp.logical_and(is_conv, i < NP_MT))(conv_prompt)
    pl.when(jnp.logical_and(is_conv, i == NP_MT))(conv_sample)


def _inproj_call(n1, w_in_t, cw, cb, cst_t):
    cblk = _in_conv_block
    seq = lambda j, i: jnp.where(j < FIRST_CONV_BLOCK, 0, _seq_of_mtile(i))
    return pl.pallas_call(
        _inproj_kernel,
        grid=(N_IN_BLOCKS, N_MT),
        in_specs=[pl.BlockSpec((TMM, D), lambda j, i: (i, 0)),
                  pl.BlockSpec((pl.Element(TN_IN), pl.Element(D)),
                               lambda j, i: (_in_src_row(j), 0)),
                  pl.BlockSpec((4, TN_IN), lambda j, i: (0, cblk(j))),
                  pl.BlockSpec((1, TN_IN), lambda j, i: (0, cblk(j))),
                  pl.BlockSpec((3, DEC_BATCH, TN_IN), lambda j, i: (0, 0, cblk(j)))],
        out_specs=[pl.BlockSpec((TMM, TN_IN), lambda j, i: (i, j)),
                   pl.BlockSpec((1, 3, TN_IN), lambda j, i: (seq(j, i), 0, cblk(j))),
                   pl.BlockSpec((3, DEC_BATCH, TN_IN), lambda j, i: (0, 0, cblk(j)))],
        out_shape=[jax.ShapeDtypeStruct((M_ROWS, N_IN_BLOCKS * TN_IN), BF16),
                   jax.ShapeDtypeStruct((BATCH, 3, CONV_DIM), F32),
                   jax.ShapeDtypeStruct((3, DEC_BATCH, CONV_DIM), F32)],
        scratch_shapes=[pltpu.VMEM((D, TN_IN), BF16), pltpu.VMEM((TMM + 8, TN_IN), F32)],
        compiler_params=_params("arbitrary", "arbitrary"),
        name="in_proj",
    )(n1, w_in_t, cw, cb, cst_t)


PZ, PU, PV, PGA, PGB, PX = 0, 1, 2, 3, 4, 5
PBC_1024 = 12


def _gated_group_norm(get_y, zact_ref, nw_ref, o_ref):
    gw = D // GROUPS
    for g in range(GROUPS):
        cols = slice(g * gw, (g + 1) * gw)
        gg = _rms(get_y(cols) * zact_ref[:, cols].astype(F32))
        o_ref[:, cols] = (gg * nw_ref[:, cols]).astype(o_ref.dtype)


CHUNKS_PER_STEP = 4


def _ssd_prompt_kernel(z_ref, x_ref, bc_ref, dt_ref, hp_ref, hpc_ref, dskx_ref, nw_ref, triu_ref,
                       y_ref, ssm_ref, st_ref, yscr_ref):
    c = pl.program_id(1)
    T = CHUNK
    cdim = GROUPS * STATE

    @pl.when(c == 0)
    def _():
        st_ref[...] = jnp.zeros(st_ref.shape, F32)

    hp = hp_ref[...]
    a_col = -jnp.exp(hpc_ref[0:HEADS, 1:2])
    triu = triu_ref[...]
    row = lax.broadcasted_iota(jnp.int32, (T, T), 0)
    lane = lax.broadcasted_iota(jnp.int32, (T, T), 1)
    causal = row >= lane
    left = lane < HEADDIM
    mask_l = jnp.where(left, 1.0, 0.0).astype(BF16)
    nt = (((1,), (1,)), ((), ()))
    pairs_per_group = HEADS // GROUPS // 2

    for sub in range(CHUNKS_PER_STEP):
        rows = slice(sub * T, (sub + 1) * T)
        dt_t = _softplus((dt_ref[rows, :] + hp[0:1, :]).T[0:HEADS, :])
        cs_t = _dot_exact_rhs(dt_t * a_col, triu)
        rsub_t = cs_t - jnp.log(dt_t)
        cs = jnp.concatenate([cs_t, jnp.zeros((LANE - HEADS, T), F32)], axis=0).T

        for g in range(GROUPS):
            c_b = bc_ref[rows, cdim + g * STATE:cdim + (g + 1) * STATE]
            b_b = bc_ref[rows, g * STATE:(g + 1) * STATE]
            cb = lax.dot_general(c_b, b_b, nt, preferred_element_type=F32)
            b_t = b_b.astype(F32).T
            for k4 in range(pairs_per_group):
                k = g * pairs_per_group + k4
                cols = slice(k * LANE, (k + 1) * LANE)
                xpb = x_ref[rows, cols]
                x_lo = xpb * mask_l
                xbd = jnp.concatenate([x_lo, xpb - x_lo], axis=0)
                st = st_ref[k]
                yraw = _dot(c_b, st.astype(BF16))
                scores, bws, colbs, alasts = [], [], [], []
                for h in (2 * k, 2 * k + 1):
                    colb = jnp.broadcast_to(cs[:, h:h + 1], (T, T))
                    alast = cs_t[h:h + 1, T - 1:T]
                    decay_dt = jnp.exp(jnp.where(causal, colb - rsub_t[h:h + 1, :], -jnp.inf))
                    scores.append((cb * decay_dt).astype(BF16))
                    wrow = jnp.exp(alast - cs_t[h:h + 1, :]) * dt_t[h:h + 1, :]
                    bws.append((b_t * wrow).astype(BF16))
                    colbs.append(colb)
                    alasts.append(alast)
                ecol = jnp.exp(jnp.where(left, colbs[0], colbs[1]))
                elast = jnp.exp(jnp.where(left[0:1], alasts[0], alasts[1]))
                yscr_ref[rows, cols] = (_dot(jnp.concatenate(scores, axis=1), xbd) + ecol * yraw
                                        + dskx_ref[:, cols] * xpb.astype(F32))
                st_ref[k] = elast * st + _dot(jnp.concatenate(bws, axis=1), xbd)

    _gated_group_norm(lambda cols: yscr_ref[:, cols], z_ref, nw_ref, y_ref)

    @pl.when(c == N_CHUNKS // CHUNKS_PER_STEP - 1)
    def _():
        for k in range(HEADS // 2):
            ssm_ref[0, k] = st_ref[k].T


def _ssd_prompt_call(proj, dt_raw, hp, hpc, dskx, nw, triu):
    steps = N_CHUNKS // CHUNKS_PER_STEP
    rows = CHUNK * CHUNKS_PER_STEP
    row = lambda b, c: b * steps + c
    const = lambda b, c: (0, 0)
    return pl.pallas_call(
        _ssd_prompt_kernel,
        grid=(BATCH, steps),
        in_specs=[pl.BlockSpec((rows, D), lambda b, c: (row(b, c), PZ)),
                  pl.BlockSpec((rows, D), lambda b, c: (row(b, c), PX)),
                  pl.BlockSpec((rows, 2 * GROUPS * STATE),
                               lambda b, c: (row(b, c), PBC_1024)),
                  pl.BlockSpec((rows, LANE), lambda b, c: (row(b, c), 0)),
                  pl.BlockSpec((8, LANE), const),
                  pl.BlockSpec((LANE, 8), const),
                  pl.BlockSpec((1, D), const),
                  pl.BlockSpec((1, D), const),
                  pl.BlockSpec((CHUNK, CHUNK), const)],
        out_specs=[pl.BlockSpec((rows, D), lambda b, c: (row(b, c), 0)),
                   pl.BlockSpec((1, HEADS // 2, LANE, STATE), lambda b, c: (b, 0, 0, 0))],
        out_shape=[jax.ShapeDtypeStruct((NP_ROWS, D), BF16),
                   jax.ShapeDtypeStruct((BATCH, HEADS // 2, LANE, STATE), F32)],
        scratch_shapes=[pltpu.VMEM((HEADS // 2, STATE, LANE), F32),
                        pltpu.VMEM((rows, D), F32)],
        compiler_params=_params("arbitrary", "arbitrary"),
        name="ssd_prompt",
    )(proj, proj, proj, dt_raw, hp, hpc, dskx, nw, triu)


def _ssds_prep_step(tt, x_ref, bc_ref, dt_ref, hp_ref, dskx_ref, ex_ref, seg_ref,
                    ypre_ref, efull_ref, xw_ref, bb_ref, cc_ref, e3_ref):
    B = DEC_BATCH
    cdim = GROUPS * STATE
    blk = lambda t: slice(t * B, (t + 1) * B)
    x_of = lambda t: x_ref[blk(t), :].astype(F32)
    b_of = lambda t: bc_ref[blk(t), 0:cdim].astype(F32)
    c_of = lambda t: bc_ref[blk(t), cdim:2 * cdim].astype(F32)

    bb_ref[...] = b_of(tt)
    cc_ref[...] = c_of(tt)

    hp = hp_ref[...]
    a_neg = -jnp.exp(hp[1:2, :])
    dts, css = [], []
    run = None
    for t in range(DEC_SEQ):
        dt = _softplus(dt_ref[blk(t), :] + hp[0:1, :])
        run = dt * a_neg if run is None else run + dt * a_neg
        dts.append(dt)
        css.append(run)

    ex = ex_ref[...]
    e3_ref[...] = jnp.exp(css[-1])
    efull_ref[...] = _dot_exact_rhs(jnp.exp(css[tt]), ex)
    w_t = jnp.exp(css[-1] - css[tt]) * dts[tt]
    xw_ref[...] = x_of(tt) * _dot_exact_rhs(w_t, ex)

    seg = seg_ref[...]
    acc = dskx_ref[...] * x_of(tt)
    c_t = c_of(tt)
    for s in range(tt + 1):
        cbh = _dot_exact_rhs(c_t * b_of(s), seg)
        g_ts = cbh * jnp.exp(css[tt] - css[s]) * dts[s]
        acc = acc + _dot_exact_rhs(g_ts, ex) * x_of(s)
    ypre_ref[...] = acc


def _ssds_prep_kernel(*refs):
    t = pl.program_id(0)
    for tt in range(DEC_SEQ):
        pl.when(t == tt)(functools.partial(_ssds_prep_step, tt, *refs))


def _ssds_prep_call(proj, dt_raw, hp, dskx, ex, seg):
    full = lambda shape: pl.BlockSpec(shape, lambda t: (0,) * len(shape))
    step = lambda width: pl.BlockSpec((DEC_BATCH, width), lambda t: (t, 0))
    cdim = GROUPS * STATE
    return pl.pallas_call(
        _ssds_prep_kernel,
        grid=(DEC_SEQ,),
        in_specs=[pl.BlockSpec((NS_ROWS, D), lambda t: (NP_TILES, PX)),
                  pl.BlockSpec((NS_ROWS, 2 * cdim), lambda t: (NP_TILES, PBC_1024)),
                  pl.BlockSpec((NS_ROWS, LANE), lambda t: (NP_TILES, 0)),
                  full((8, LANE)), full((1, D)), full((LANE, D)), full((cdim, LANE))],
        out_specs=[step(D), step(D), step(D), step(cdim), step(cdim),
                   full((DEC_BATCH, LANE))],
        out_shape=[jax.ShapeDtypeStruct((NS_ROWS, D), F32),
                   jax.ShapeDtypeStruct((NS_ROWS, D), F32),
                   jax.ShapeDtypeStruct((NS_ROWS, D), F32),
                   jax.ShapeDtypeStruct((NS_ROWS, cdim), F32),
                   jax.ShapeDtypeStruct((NS_ROWS, cdim), F32),
                   jax.ShapeDtypeStruct((DEC_BATCH, LANE), F32)],
        compiler_params=_params("arbitrary"),
        name="ssd_sample_prep",
    )(proj, proj, dt_raw, hp, dskx, ex, seg)


SEQ_PER_STEP = 8


def _ssds_state_kernel(e3_ref, st_ref, cc_ref, bb_ref, xw_ref, yo_ref, so_ref):
    blk = pl.program_id(0)
    nt = (((1,), (1,)), ((), ()))
    tn = (((0,), (0,)), ((), ()))
    hpg = HEADS // GROUPS
    gw = hpg * HEADDIM

    def rows_of(ref, s, cols):
        v = ref[:, s, cols]
        return jnp.concatenate([v, jnp.zeros((8 - DEC_SEQ, v.shape[-1]), F32)], axis=0).astype(BF16)

    for s in range(SEQ_PER_STEP):
        b = blk * SEQ_PER_STEP + s
        for g in range(GROUPS):
            h0 = st_ref[s, g]
            c_g = rows_of(cc_ref, s, slice(g * STATE, (g + 1) * STATE))
            yraw = lax.dot_general(c_g, h0.astype(BF16), nt, preferred_element_type=F32)
            yo_ref[:, s, g * gw:(g + 1) * gw] = yraw[0:DEC_SEQ]
            x_g = rows_of(xw_ref, s, slice(g * gw, (g + 1) * gw))
            b_g = rows_of(bb_ref, s, slice(g * STATE, (g + 1) * STATE))
            dh = lax.dot_general(x_g, b_g, tn, preferred_element_type=F32)
            for hh in range(hpg):
                rows = slice(hh * HEADDIM, (hh + 1) * HEADDIM)
                so_ref[s, g, rows, :] = e3_ref[b, g * hpg + hh] * h0[rows] + dh[rows]


def _ssds_state_call(e3, state, cc_t, bb_t, xw_t):
    sb = SEQ_PER_STEP
    gw = (HEADS // GROUPS) * HEADDIM
    cdim = GROUPS * STATE
    tmajor = lambda width: pl.BlockSpec((DEC_SEQ, sb, width), lambda i: (0, i, 0))
    return pl.pallas_call(
        _ssds_state_kernel,
        grid=(DEC_BATCH // sb,),
        in_specs=[pl.BlockSpec(memory_space=pltpu.SMEM),
                  pl.BlockSpec((sb, GROUPS, gw, STATE), lambda i: (i, 0, 0, 0)),
                  tmajor(cdim), tmajor(cdim), tmajor(D)],
        out_specs=[tmajor(D),
                   pl.BlockSpec((sb, GROUPS, gw, STATE), lambda i: (i, 0, 0, 0))],
        out_shape=[jax.ShapeDtypeStruct((DEC_SEQ, DEC_BATCH, D), F32),
                   jax.ShapeDtypeStruct((DEC_BATCH, GROUPS, gw, STATE), F32)],
        compiler_params=_params("arbitrary"),
        name="ssd_sample_state",
    )(e3, state, cc_t.reshape(DEC_SEQ, DEC_BATCH, cdim), bb_t.reshape(DEC_SEQ, DEC_BATCH, cdim),
      xw_t.reshape(DEC_SEQ, DEC_BATCH, D))


def _ssds_post_kernel(ypre_ref, efull_ref, yo_ref, z_ref, nw_ref, o_ref):
    get_y = lambda cols: ypre_ref[:, cols] + efull_ref[:, cols] * yo_ref[:, cols]
    _gated_group_norm(get_y, z_ref, nw_ref, o_ref)


def _ssds_post_call(ypre, efull, yo_t, proj, nw):
    step = pl.BlockSpec((DEC_BATCH, D), lambda t: (t, 0))
    return pl.pallas_call(
        _ssds_post_kernel,
        grid=(DEC_SEQ,),
        in_specs=[step, step, step,
                  pl.BlockSpec((DEC_BATCH, D), lambda t: (NP_ROWS // DEC_BATCH + t, PZ)),
                  pl.BlockSpec((1, D), lambda t: (0, 0))],
        out_specs=step,
        out_shape=jax.ShapeDtypeStruct((NS_ROWS, D), BF16),
        compiler_params=_params("arbitrary"),
        name="ssd_sample_post",
    )(ypre, efull, yo_t, proj, nw)


def _mlp_kernel(u_ref, v_ref, lnw_ref, lnb_ref, ws_ref, bsx_ref, wsx_ref, bsx4_ref,
                y_ref, cv_ref, wm_ref, cvs_ref):
    i = pl.program_id(0)
    T = CHUNK

    @pl.when(i == 0)
    def _():
        row = lax.broadcasted_iota(jnp.int32, (T, T), 0)
        lane = lax.broadcasted_iota(jnp.int32, (T, T), 1)
        for g in range(MLP_GROUPS):
            wm_ref[g] = jnp.where(row >= lane, ws_ref[g], 0.0).astype(BF16)

    def vnorm(rows):
        vg = v_ref[rows, :].astype(F32)
        xc = vg - jnp.mean(vg, axis=-1, keepdims=True)
        y = xc * lax.rsqrt(jnp.mean(xc * xc, axis=-1, keepdims=True) + EPS)
        return y * lnw_ref[...] + lnb_ref[...]

    @pl.when(i < NP_TILES)
    def _():
        for cc in range(TM // T):
            rows = slice(cc * T, (cc + 1) * T)
            vnb = vnorm(rows).astype(BF16)
            for g in range(MLP_GROUPS):
                cols = slice(g * MLP_GROUP_DIM, (g + 1) * MLP_GROUP_DIM)
                sv = _dot(wm_ref[g], vnb[:, cols]) + bsx_ref[:, cols]
                y_ref[rows, cols] = (u_ref[rows, cols].astype(F32) * sv).astype(BF16)

    @pl.when(i >= NP_TILES)
    def _():
        B = DEC_BATCH
        for t in range(DEC_SEQ):
            rows = slice(t * B, (t + 1) * B)
            vn = vnorm(rows)
            cvs_ref[rows, :] = vn
            cv_ref[:, t, :] = vn
        for t in range(DEC_SEQ):
            rows = slice(t * B, (t + 1) * B)
            acc = bsx4_ref[t:t + 1, :]
            for s in range(t + 1):
                acc = acc + wsx_ref[4 * t + s:4 * t + s + 1, :] * cvs_ref[s * B:(s + 1) * B, :]
            y_ref[rows, :] = (u_ref[rows, :].astype(F32) * acc).astype(BF16)


def _mlp_call(proj, lnw, lnb, ws, bsx, wsx, bsx4):
    full = lambda shape: pl.BlockSpec(shape, lambda i: (0,) * len(shape))
    return pl.pallas_call(
        _mlp_kernel,
        grid=(N_TILES,),
        in_specs=[pl.BlockSpec((TM, D), lambda i: (i, PU)),
                  pl.BlockSpec((TM, D), lambda i: (i, PV)),
                  full((1, D)), full((1, D)),
                  full((MLP_GROUPS, CHUNK, CHUNK)),
                  full((CHUNK, D)), full((16, D)), full((8, D))],
        out_specs=[pl.BlockSpec((TM, D), lambda i: (i, 0)),
                   full((DEC_BATCH, DEC_SEQ, D))],
        out_shape=[jax.ShapeDtypeStruct((M_ROWS, D), BF16),
                   jax.ShapeDtypeStruct((DEC_BATCH, DEC_SEQ, D), F32)],
        scratch_shapes=[pltpu.VMEM((MLP_GROUPS, CHUNK, CHUNK), BF16),
                        pltpu.VMEM((NS_ROWS, D), F32)],
        compiler_params=_params("arbitrary"),
        name="gmlp",
    )(proj, proj, lnw, lnb, ws, bsx, wsx, bsx4)


def _cast_rows(src_ref, dst_ref, chunk=256):
    def body(r, carry):
        rows = pl.ds(pl.multiple_of(r * chunk, chunk), chunk)
        dst_ref[rows, :] = src_ref[rows, :].astype(dst_ref.dtype)
        return carry
    lax.fori_loop(0, src_ref.shape[0] // chunk, body, 0)


def _per_tile(i, prompt_fn, sample_fn):
    pl.when(i < NP_MT)(prompt_fn)
    pl.when(i == NP_MT)(sample_fn)


def _merge_kernel(ysp_ref, yss_ref, ym_ref, ga_ref, gb_ref, w1_ref, w2_ref, o_ref, w1b, w2b):
    i = pl.program_id(1)

    @pl.when(i == 0)
    def _():
        _cast_rows(w1_ref, w1b)
        _cast_rows(w2_ref, w2b)

    def emit(ys, rows):
        a1 = _dot(ys, w1b[...])
        a2 = _dot(ym_ref[rows, :], w2b[...])
        o_ref[rows, :] = (jax.nn.sigmoid(ga_ref[rows, :].astype(F32)) * a1
                          + jax.nn.sigmoid(gb_ref[rows, :].astype(F32)) * a2).astype(BF16)

    _per_tile(i,
              lambda: emit(ysp_ref[...], slice(None)),
              lambda: emit(yss_ref[...], slice(0, NS_ROWS)))


def _merge_call(ysp, yss, ym, proj, w1, w2):
    tn = 512
    nb = D // tn
    return pl.pallas_call(
        _merge_kernel,
        grid=(nb, N_MT),
        in_specs=[pl.BlockSpec((TMM, D), lambda j, i: (jnp.minimum(i, NP_MT - 1), 0)),
                  pl.BlockSpec((NS_ROWS, D), lambda j, i: (0, 0)),
                  pl.BlockSpec((TMM, D), lambda j, i: (i, 0)),
                  pl.BlockSpec((TMM, tn), lambda j, i: (i, PGA * nb + j)),
                  pl.BlockSpec((TMM, tn), lambda j, i: (i, PGB * nb + j)),
                  pl.BlockSpec((D, tn), lambda j, i: (0, j)),
                  pl.BlockSpec((D, tn), lambda j, i: (0, j))],
        out_specs=pl.BlockSpec((TMM, tn), lambda j, i: (i, j)),
        out_shape=jax.ShapeDtypeStruct((M_ROWS, D), BF16),
        scratch_shapes=[pltpu.VMEM((D, tn), BF16), pltpu.VMEM((D, tn), BF16)],
        compiler_params=_params("arbitrary", "arbitrary"),
        name="branch_merge",
    )(ysp, yss, ym, proj, proj, w1, w2)


def _resid_kernel(a_ref, w_ref, rp_ref, rs_ref, gp_ref, gs_ref, o_ref, wb):
    i = pl.program_id(1)

    @pl.when(i == 0)
    def _():
        _cast_rows(w_ref, wb)

    def prompt():
        o_ref[...] = rp_ref[...] + gp_ref[0] * _dot(a_ref[...], wb[...])

    def sample():
        acc = _dot(a_ref[0:NS_ROWS, :], wb[...])
        tn = acc.shape[-1]
        acc3 = acc.reshape(DEC_SEQ, DEC_BATCH, tn) * gs_ref[...][None]
        o_ref[0:NS_ROWS, :] = rs_ref[...] + acc3.reshape(NS_ROWS, tn)

    _per_tile(i, prompt, sample)


def _resid_call(a, w, rp, rs, rs_block, mod_p, mod_s, k_gate, tn, name, single_buffer_w=False):
    kdim = a.shape[1]
    nb = D // tn
    w_mode = dict(pipeline_mode=pl.Buffered(1)) if single_buffer_w else {}
    return pl.pallas_call(
        _resid_kernel,
        grid=(nb, N_MT),
        in_specs=[pl.BlockSpec((TMM, kdim), lambda j, i: (i, 0)),
                  pl.BlockSpec((kdim, tn), lambda j, i: (0, j), **w_mode),
                  pl.BlockSpec((TMM, tn), lambda j, i: (jnp.minimum(i, NP_MT - 1), j)),
                  pl.BlockSpec((NS_ROWS, tn), lambda j, i: (rs_block, j)),
                  pl.BlockSpec((1, 1, tn), lambda j, i: (_seq_of_mtile(i), 0, k_gate * nb + j)),
                  pl.BlockSpec((DEC_BATCH, tn), lambda j, i: (0, k_gate * nb + j))],
        out_specs=pl.BlockSpec((TMM, tn), lambda j, i: (i, j)),
        out_shape=jax.ShapeDtypeStruct((M_ROWS, D), F32),
        scratch_shapes=[pltpu.VMEM((kdim, tn), BF16)],
        compiler_params=_params("arbitrary", "arbitrary"),
        name=name,
    )(a, w, rp, rs, mod_p, mod_s)


TN_FF = 512
N_FF_BLOCKS = D_FF // TN_FF


def _up_kernel(a_ref, wa_ref, wv_ref, cw_ref, cb_ref, fst_ref,
               h_ref, fcp_ref, fcs_ref, wab, wvb, acc_a, acc_v):
    i = pl.program_id(1)
    rc = 64

    @pl.when(i == 0)
    def _():
        _cast_rows(wa_ref, wab)
        _cast_rows(wv_ref, wvb)

    def prompt():
        starts_sequence = i % MT_PER_SEQ == 0
        acc_a[0:8, :] = jnp.where(starts_sequence, 0.0, acc_a[TMM:TMM + 8, :])
        x = a_ref[...]
        acc_a[8:8 + TMM, :] = _dot(x, wab[...])
        acc_v[...] = _dot(x, wvb[...])
        cw, cb = cw_ref[...], cb_ref[...]
        for r in range(0, TMM, rc):
            conv = (cb + cw[2:3] * acc_a[8 + r:8 + r + rc, :]
                    + cw[1:2] * acc_a[7 + r:7 + r + rc, :]
                    + cw[0:1] * acc_a[6 + r:6 + r + rc, :])
            h_ref[r:r + rc, :] = (jax.nn.gelu(conv) * acc_v[r:r + rc, :]).astype(BF16)
        fcp_ref[0] = acc_a[TMM + 6:TMM + 8, :]

    def sample():
        B = DEC_BATCH
        x = a_ref[0:NS_ROWS, :]
        acc_a[8:8 + NS_ROWS, :] = _dot(x, wab[...])
        acc_v[0:NS_ROWS, :] = _dot(x, wvb[...])
        cw, cb = cw_ref[...], cb_ref[...]

        def pre(t, r):
            if t < 0:
                return fst_ref[r:r + rc, t + 2, :]
            return acc_a[8 + t * B + r:8 + t * B + r + rc, :]

        for t in range(DEC_SEQ):
            for r in range(0, B, rc):
                conv = cb + cw[2:3] * pre(t, r) + cw[1:2] * pre(t - 1, r) + cw[0:1] * pre(t - 2, r)
                h_ref[t * B + r:t * B + r + rc, :] = (
                    jax.nn.gelu(conv) * acc_v[t * B + r:t * B + r + rc, :]).astype(BF16)
        fcs_ref[:, 0, :] = acc_a[8 + 2 * B:8 + 3 * B, :]
        fcs_ref[:, 1, :] = acc_a[8 + 3 * B:8 + 4 * B, :]

    _per_tile(i, prompt, sample)


def _up_call(n2, w_up, cw, cb, fst):
    return pl.pallas_call(
        _up_kernel,
        grid=(N_FF_BLOCKS, N_MT),
        in_specs=[pl.BlockSpec((TMM, D), lambda j, i: (i, 0)),
                  pl.BlockSpec((D, TN_FF), lambda j, i: (0, j)),
                  pl.BlockSpec((D, TN_FF), lambda j, i: (0, N_FF_BLOCKS + j)),
                  pl.BlockSpec((3, TN_FF), lambda j, i: (0, j)),
                  pl.BlockSpec((1, TN_FF), lambda j, i: (0, j)),
                  pl.BlockSpec((DEC_BATCH, 2, TN_FF), lambda j, i: (0, 0, j))],
        out_specs=[pl.BlockSpec((TMM, TN_FF), lambda j, i: (i, j)),
                   pl.BlockSpec((1, 2, TN_FF), lambda j, i: (_seq_of_mtile(i), 0, j)),
                   pl.BlockSpec((DEC_BATCH, 2, TN_FF), lambda j, i: (0, 0, j))],
        out_shape=[jax.ShapeDtypeStruct((M_ROWS, D_FF), BF16),
                   jax.ShapeDtypeStruct((BATCH, 2, D_FF), F32),
                   jax.ShapeDtypeStruct((DEC_BATCH, 2, D_FF), F32)],
        scratch_shapes=[pltpu.VMEM((D, TN_FF), BF16), pltpu.VMEM((D, TN_FF), BF16),
                        pltpu.VMEM((TMM + 8, TN_FF), F32), pltpu.VMEM((TMM, TN_FF), F32)],
        compiler_params=_params("arbitrary", "arbitrary"),
        name="ffn_up",
    )(n2, w_up, w_up, cw, cb, fst)


def _final_kernel(x_ref, w_ref, yp_ref, ys_ref):
    i = pl.program_id(0)
    y = _rms(x_ref[...]) * w_ref[...]

    @pl.when(i < NP_TILES)
    def _():
        yp_ref[...] = y

    @pl.when(i >= NP_TILES)
    def _():
        for t in range(DEC_SEQ):
            ys_ref[:, t, :] = y[t * DEC_BATCH:(t + 1) * DEC_BATCH]


def _final_call(x3, w):
    return pl.pallas_call(
        _final_kernel,
        grid=(N_TILES,),
        in_specs=[pl.BlockSpec((TM, D), lambda i: (i, 0)),
                  pl.BlockSpec((1, D), lambda i: (0, 0))],
        out_specs=[pl.BlockSpec((TM, D), lambda i: (jnp.minimum(i, NP_TILES - 1), 0)),
                   pl.BlockSpec((DEC_BATCH, DEC_SEQ, D), lambda i: (0, 0, 0))],
        out_shape=[jax.ShapeDtypeStruct((NP_ROWS, D), F32),
                   jax.ShapeDtypeStruct((DEC_BATCH, DEC_SEQ, D), F32)],
        compiler_params=_params("arbitrary"),
        name="final_norm",
    )(x3, w)


def _to_time_major(a):
    return jnp.transpose(a, (1, 0, 2))


def kernel(x_prompt, x_sample, state_ssm, state_ssd_conv, state_ffn_conv, c_prompt, c_sample,
           norm1_w, w_ada, b_ada, w_in, ssd_conv_w, ssd_conv_b, dt_bias, a_log, d_skip,
           ssd_norm_w, mlp_ln_w, mlp_ln_b, w_spatial, b_spatial, w_ssd_o, w_mlp_o, w_out,
           norm2_w, w_up, ffn_conv_w, ffn_conv_b, w_down, final_norm_w):
    assert w_in.shape[0] == 1, "single-layer trunk"
    row = lambda v: v.reshape(1, -1)

    xp = x_prompt.reshape(NP_ROWS, D)
    xs = _to_time_major(x_sample).reshape(NS_ROWS, D)

    mod_p, mod_s = _ada_call(c_prompt, c_sample, w_ada[0], row(b_ada[0]))
    mod_p = mod_p.reshape(BATCH, 1, 6 * D)
    K_SHIFT1, K_SCALE1, K_GATE1, K_SHIFT2, K_SCALE2, K_GATE2 = range(6)

    xp_spec = pl.BlockSpec((TM, D), lambda i: (jnp.minimum(i, NP_TILES - 1), 0))
    xs_spec = pl.BlockSpec((TM, D), lambda i: (0, 0))
    w_in_t = w_in[0].T
    w_dt_t = w_in_t[DT_COL:DT_COL + LANE]
    n1, dt_raw = _norm_call(xp, xp_spec, xs, xs_spec, row(norm1_w[0]), mod_p, mod_s,
                            K_SCALE1, K_SHIFT1, w_dt_t)
    cst_t = _to_time_major(state_ssd_conv[0])
    proj, cst_p, ncs_t = _inproj_call(n1, w_in_t, ssd_conv_w[0], row(ssd_conv_b[0]), cst_t)

    hp = jnp.zeros((8, LANE), F32)
    hp = hp.at[0, :HEADS].set(dt_bias[0]).at[1, :HEADS].set(a_log[0]).at[2, :HEADS].set(d_skip[0])
    hpc = hp.T
    triu = jnp.triu(jnp.ones((CHUNK, CHUNK), F32)).astype(BF16)
    dskx = row(jnp.repeat(d_skip[0], HEADDIM))
    nw = row(ssd_norm_w[0])

    y_ssd_p, ssm_p = _ssd_prompt_call(proj, dt_raw, hp, hpc, dskx, nw, triu)

    head_of_col = jnp.arange(D) // HEADDIM
    ex = (jnp.arange(LANE)[:, None] == head_of_col[None, :]).astype(BF16)
    grp_of_head = jnp.arange(LANE) // (HEADS // GROUPS)
    seg = ((jnp.arange(GROUPS * STATE)[:, None] // STATE == grp_of_head[None, :])
           & (jnp.arange(LANE)[None, :] < HEADS)).astype(BF16)
    ypre, efull, xw_t, bb_t, cc_t, e3 = _ssds_prep_call(proj, dt_raw, hp, dskx, ex, seg)
    gw = (HEADS // GROUPS) * HEADDIM
    yo_t, ssm_s = _ssds_state_call(
        e3[:, :HEADS], state_ssm[0].reshape(DEC_BATCH, GROUPS, gw, STATE), cc_t, bb_t, xw_t)
    y_ssd_s = _ssds_post_call(ypre, efull, yo_t.reshape(NS_ROWS, D), proj, nw)

    per_col = lambda a: jnp.repeat(a, MLP_GROUP_DIM, axis=1)
    bsx = per_col(b_spatial[0][:, :CHUNK].T)
    ws4 = w_spatial[0][:, :DEC_SEQ, :DEC_SEQ]
    wsx = per_col(jnp.transpose(ws4, (1, 2, 0)).reshape(DEC_SEQ * DEC_SEQ, MLP_GROUPS))
    bsx4 = jnp.pad(bsx[:DEC_SEQ], ((0, 8 - DEC_SEQ), (0, 0)))
    y_mlp, cv_s = _mlp_call(proj, row(mlp_ln_w[0]), row(mlp_ln_b[0]), w_spatial[0], bsx, wsx, bsx4)

    mixed = _merge_call(y_ssd_p, y_ssd_s, y_mlp, proj, w_ssd_o[0], w_mlp_o[0])
    x2 = _resid_call(mixed, w_out[0], xp, xs, 0, mod_p, mod_s, K_GATE1, 1024, "out_proj")

    x2p_spec = pl.BlockSpec((TM, D), lambda i: (jnp.minimum(i, NP_TILES - 1), 0))
    x2s_spec = pl.BlockSpec((TM, D), lambda i: (NP_TILES, 0))
    (n2,) = _norm_call(x2, x2p_spec, x2, x2s_spec, row(norm2_w[0]), mod_p, mod_s,
                       K_SCALE2, K_SHIFT2, None)
    h, ffn_p, ffn_s = _up_call(n2, w_up[0], ffn_conv_w[0], row(ffn_conv_b[0]), state_ffn_conv[0])
    x3 = _resid_call(h, w_down[0], x2, x2, NP_TILES, mod_p, mod_s, K_GATE2, 512, "ffn_down",
                     single_buffer_w=True)

    y_p, y_s = _final_call(x3, row(final_norm_w))

    from_t = lambda a, t, c: jnp.transpose(a.reshape(t, DEC_BATCH, c), (1, 0, 2))
    return (y_p.reshape(BATCH, SEQ, D),
            y_s,
            ssm_p.reshape(1, BATCH, HEADS, HEADDIM, STATE),
            ssm_s.reshape(1, DEC_BATCH, HEADS, HEADDIM, STATE),
            cst_p[None],
            from_t(ncs_t, 3, CONV_DIM)[None],
            ffn_p[None],
            ffn_s[None],
            cv_s[None])
```

```python
import functools

import jax
import jax.numpy as jnp
from jax import lax
from jax.experimental import pallas as pl
from jax.experimental.pallas import tpu as pltpu

F32 = jnp.float32
BF16 = jnp.bfloat16

D = 2048
BATCH, SEQ = 4, 2048
DEC_BATCH, DEC_SEQ = 128, 4
NP_ROWS = BATCH * SEQ
NS_ROWS = DEC_BATCH * DEC_SEQ
M_ROWS = NP_ROWS + NS_ROWS
TM = 512
NP_TILES = NP_ROWS // TM
TMM = 1024
NP_MT = NP_ROWS // TMM
N_MT = NP_MT + 1
MT_PER_SEQ = SEQ // TMM
HEADS, HEADDIM, GROUPS, STATE = 32, 64, 4, 128
CHUNK = 128
N_CHUNKS = SEQ // CHUNK
CONV_DIM = D + 2 * GROUPS * STATE
MLP_GROUPS = 8
MLP_GROUP_DIM = D // MLP_GROUPS
D_FF = 5632
EPS = 1e-6
DT_COL = D + CONV_DIM
TN_IN = 1024
LANE = 128
VMEM_LIMIT = 56 * 1024 * 1024


def _params(*sem, flags=None):
    return pltpu.CompilerParams(dimension_semantics=sem, vmem_limit_bytes=VMEM_LIMIT, flags=flags)


def _dot(a, b):
    return jnp.dot(a, b, preferred_element_type=F32)


def _split_bf16(v, terms):
    out = []
    r = v
    for _ in range(terms):
        p = r.astype(BF16)
        out.append(p)
        r = r - p.astype(F32)
    return out


def _dot_exact_rhs(v, e, terms=3):
    acc = None
    for p in _split_bf16(v, terms):
        d = _dot(p, e)
        acc = d if acc is None else acc + d
    return acc


def _silu(x):
    return x * jax.nn.sigmoid(x)


def _softplus(x):
    return jnp.maximum(x, 0.0) + jnp.log1p(jnp.exp(-jnp.abs(x)))


def _rms(x):
    return x * lax.rsqrt(jnp.mean(x * x, axis=-1, keepdims=True) + EPS)


def _ada_kernel(cp_ref, cs_ref, w_ref, b_ref, op_ref, os_ref):
    w = w_ref[...].astype(BF16)
    cp8 = jnp.concatenate([cp_ref[...], jnp.zeros((8 - BATCH, D), F32)], axis=0)
    op_ref[...] = (_dot(_silu(cp8).astype(BF16), w) + b_ref[...])[0:BATCH]
    os_ref[...] = _dot(_silu(cs_ref[...]).astype(BF16), w) + b_ref[...]


def _ada_call(c_prompt, c_sample, w, b):
    tn = 1024
    return pl.pallas_call(
        _ada_kernel,
        grid=(6 * D // tn,),
        in_specs=[pl.BlockSpec((BATCH, D), lambda j: (0, 0)),
                  pl.BlockSpec((DEC_BATCH, D), lambda j: (0, 0)),
                  pl.BlockSpec((D, tn), lambda j: (0, j)),
                  pl.BlockSpec((1, tn), lambda j: (0, j))],
        out_specs=[pl.BlockSpec((BATCH, tn), lambda j: (0, j)),
                   pl.BlockSpec((DEC_BATCH, tn), lambda j: (0, j))],
        out_shape=[jax.ShapeDtypeStruct((BATCH, 6 * D), F32),
                   jax.ShapeDtypeStruct((DEC_BATCH, 6 * D), F32)],
        compiler_params=_params("arbitrary"),
        name="ada_mod",
    )(c_prompt, c_sample, w, b)


def _norm_kernel(with_dt, xp_ref, xs_ref, nw_ref, scp_ref, shp_ref, scs_ref, shs_ref, *rest):
    if with_dt:
        wdt_ref, n_ref, dt_ref = rest
    else:
        (n_ref,) = rest
    i = pl.program_id(0)

    def emit(n, rows):
        nb = n.astype(BF16)
        n_ref[rows, :] = nb
        if with_dt:
            dt_ref[rows, :] = lax.dot_general(nb, wdt_ref[...].astype(BF16),
                                              (((1,), (1,)), ((), ())),
                                              preferred_element_type=F32)

    @pl.when(i < NP_MT)
    def _():
        for r in range(0, TMM, TM):
            rows = slice(r, r + TM)
            y = _rms(xp_ref[rows, :]) * nw_ref[...]
            emit(y * (1.0 + scp_ref[0]) + shp_ref[0], rows)

    @pl.when(i == NP_MT)
    def _():
        y = _rms(xs_ref[...]) * nw_ref[...]
        y3 = y.reshape(DEC_SEQ, DEC_BATCH, D)
        emit((y3 * (1.0 + scs_ref[...])[None] + shs_ref[...][None]).reshape(NS_ROWS, D),
             slice(0, NS_ROWS))


def _seq_of_mtile(i):
    return jnp.minimum(i // MT_PER_SEQ, BATCH - 1)


def _norm_call(xp, xp_spec, xs, xs_spec, nw, mod_p, mod_s, k_scale, k_shift, w_in_t):
    with_dt = w_in_t is not None
    in_specs = [
        xp_spec, xs_spec,
        pl.BlockSpec((1, D), lambda i: (0, 0)),
        pl.BlockSpec((1, 1, D), lambda i: (_seq_of_mtile(i), 0, k_scale)),
        pl.BlockSpec((1, 1, D), lambda i: (_seq_of_mtile(i), 0, k_shift)),
        pl.BlockSpec((DEC_BATCH, D), lambda i: (0, k_scale)),
        pl.BlockSpec((DEC_BATCH, D), lambda i: (0, k_shift)),
    ]
    args = [xp, xs, nw, mod_p, mod_p, mod_s, mod_s]
    out_specs = [pl.BlockSpec((TMM, D), lambda i: (i, 0))]
    out_shape = [jax.ShapeDtypeStruct((M_ROWS, D), BF16)]
    if with_dt:
        in_specs.append(pl.BlockSpec((pl.Element(LANE), pl.Element(D)), lambda i: (DT_COL, 0)))
        args.append(w_in_t)
        out_specs.append(pl.BlockSpec((TMM, LANE), lambda i: (i, 0)))
        out_shape.append(jax.ShapeDtypeStruct((M_ROWS, LANE), F32))
    return pl.pallas_call(
        functools.partial(_norm_kernel, with_dt),
        grid=(N_MT,),
        in_specs=in_specs,
        out_specs=out_specs,
        out_shape=out_shape,
        compiler_params=_params("arbitrary"),
        name="norm_mod_dt" if with_dt else "norm_mod",
    )(*args)


N_IN_BLOCKS = 13
UVG_ROW = DT_COL + HEADS
FIRST_CONV_BLOCK = 10


def _in_src_row(j):
    row = jnp.where(j < 2, j * TN_IN,
                    jnp.where(j < FIRST_CONV_BLOCK, UVG_ROW + (j - 2) * TN_IN,
                              D + (j - FIRST_CONV_BLOCK) * TN_IN))
    return pl.multiple_of(row, HEADS)


def _in_conv_block(j):
    return jnp.maximum(j - FIRST_CONV_BLOCK, 0)


def _inproj_kernel(a_ref, wt_ref, cw_ref, cb_ref, cst_ref, o_ref, csp_ref, css_ref, wbf_ref, acc_ref):
    j = pl.program_id(0)
    i = pl.program_id(1)
    rc = 64
    B = DEC_BATCH

    @pl.when(i == 0)
    def _():
        for r in range(TN_IN // LANE):
            rows = slice(r * LANE, (r + 1) * LANE)
            wbf_ref[:, rows] = wt_ref[rows, :].T.astype(BF16)

    def elementwise(fn, rows):
        def body():
            acc_ref[8:8 + rows, :] = _dot(a_ref[0:rows, :], wbf_ref[...])
            for r in range(0, rows, rc):
                val = acc_ref[8 + r:8 + r + rc, :]
                o_ref[r:r + rc, :] = (val if fn is None else fn(val)).astype(o_ref.dtype)
        return body

    def conv_prompt():
        starts_sequence = i % MT_PER_SEQ == 0
        acc_ref[0:8, :] = jnp.where(starts_sequence, 0.0, acc_ref[TMM:TMM + 8, :])
        acc_ref[8:8 + TMM, :] = _dot(a_ref[...], wbf_ref[...])
        cw, cb = cw_ref[...], cb_ref[...]
        for r in range(0, TMM, rc):
            conv = cb + cw[3:4] * acc_ref[8 + r:8 + r + rc, :]
            for k in range(3):
                conv = conv + cw[k:k + 1] * acc_ref[5 + k + r:5 + k + r + rc, :]
            o_ref[r:r + rc, :] = _silu(conv).astype(o_ref.dtype)
        csp_ref[0] = acc_ref[TMM + 5:TMM + 8, :]

    def conv_sample():
        acc_ref[8:8 + NS_ROWS, :] = _dot(a_ref[0:NS_ROWS, :], wbf_ref[...])
        cw, cb = cw_ref[...], cb_ref[...]

        def pre(t, r):
            if t < 0:
                return cst_ref[t + 3, r:r + rc, :]
            return acc_ref[8 + t * B + r:8 + t * B + r + rc, :]

        for t in range(DEC_SEQ):
            for r in range(0, B, rc):
                conv = cb + cw[3:4] * pre(t, r)
                for k in range(3):
                    conv = conv + cw[k:k + 1] * pre(t - 3 + k, r)
                o_ref[t * B + r:t * B + r + rc, :] = _silu(conv).astype(o_ref.dtype)
        for t in range(1, DEC_SEQ):
            css_ref[t - 1] = acc_ref[8 + t * B:8 + (t + 1) * B, :]

    is_conv = j >= FIRST_CONV_BLOCK
    kinds = ((j < 2, _silu),
             (jnp.logical_and(j >= 2, j < 6), jax.nn.gelu),
             (jnp.logical_and(j >= 6, j < FIRST_CONV_BLOCK), None))
    for cond, fn in kinds:
        pl.when(jnp.logical_and(cond, i < NP_MT))(elementwise(fn, TMM))
        pl.when(jnp.logical_and(cond, i == NP_MT))(elementwise(fn, NS_ROWS))
    pl.when(jnp.logical_and(is_conv, i < NP_MT))(conv_prompt)
    pl.when(jnp.logical_and(is_conv, i == NP_MT))(conv_sample)


def _inproj_call(n1, w_in_t, cw, cb, cst_t):
    cblk = _in_conv_block
    seq = lambda j, i: jnp.where(j < FIRST_CONV_BLOCK, 0, _seq_of_mtile(i))
    return pl.pallas_call(
        _inproj_kernel,
        grid=(N_IN_BLOCKS, N_MT),
        in_specs=[pl.BlockSpec((TMM, D), lambda j, i: (i, 0)),
                  pl.BlockSpec((pl.Element(TN_IN), pl.Element(D)),
                               lambda j, i: (_in_src_row(j), 0)),
                  pl.BlockSpec((4, TN_IN), lambda j, i: (0, cblk(j))),
                  pl.BlockSpec((1, TN_IN), lambda j, i: (0, cblk(j))),
                  pl.BlockSpec((3, DEC_BATCH, TN_IN), lambda j, i: (0, 0, cblk(j)))],
        out_specs=[pl.BlockSpec((TMM, TN_IN), lambda j, i: (i, j)),
                   pl.BlockSpec((1, 3, TN_IN), lambda j, i: (seq(j, i), 0, cblk(j))),
                   pl.BlockSpec((3, DEC_BATCH, TN_IN), lambda j, i: (0, 0, cblk(j)))],
        out_shape=[jax.ShapeDtypeStruct((M_ROWS, N_IN_BLOCKS * TN_IN), BF16),
                   jax.ShapeDtypeStruct((BATCH, 3, CONV_DIM), F32),
                   jax.ShapeDtypeStruct((3, DEC_BATCH, CONV_DIM), F32)],
        scratch_shapes=[pltpu.VMEM((D, TN_IN), BF16), pltpu.VMEM((TMM + 8, TN_IN), F32)],
        compiler_params=_params("arbitrary", "arbitrary"),
        name="in_proj",
    )(n1, w_in_t, cw, cb, cst_t)


PZ, PU, PV, PGA, PGB, PX = 0, 1, 2, 3, 4, 5
PBC_1024 = 12


def _gated_group_norm(get_y, zact_ref, nw_ref, o_ref):
    gw = D // GROUPS
    for g in range(GROUPS):
        cols = slice(g * gw, (g + 1) * gw)
        gg = _rms(get_y(cols) * zact_ref[:, cols].astype(F32))
        o_ref[:, cols] = (gg * nw_ref[:, cols]).astype(o_ref.dtype)


CHUNKS_PER_STEP = 4


def _ssd_prompt_kernel(z_ref, x_ref, bc_ref, dt_ref, hp_ref, hpc_ref, dskx_ref, nw_ref, triu_ref,
                       y_ref, ssm_ref, st_ref, yscr_ref):
    c = pl.program_id(1)
    T = CHUNK
    cdim = GROUPS * STATE

    @pl.when(c == 0)
    def _():
        st_ref[...] = jnp.zeros(st_ref.shape, F32)

    hp = hp_ref[...]
    a_col = -jnp.exp(hpc_ref[0:HEADS, 1:2])
    triu = triu_ref[...]
    row = lax.broadcasted_iota(jnp.int32, (T, T), 0)
    lane = lax.broadcasted_iota(jnp.int32, (T, T), 1)
    causal = row >= lane
    left = lane < HEADDIM
    mask_l = jnp.where(left, 1.0, 0.0).astype(BF16)
    nt = (((1,), (1,)), ((), ()))
    pairs_per_group = HEADS // GROUPS // 2

    for sub in range(CHUNKS_PER_STEP):
        rows = slice(sub * T, (sub + 1) * T)
        dt_t = _softplus((dt_ref[rows, :] + hp[0:1, :]).T[0:HEADS, :])
        cs_t = _dot_exact_rhs(dt_t * a_col, triu)
        rsub_t = cs_t - jnp.log(dt_t)
        cs = jnp.concatenate([cs_t, jnp.zeros((LANE - HEADS, T), F32)], axis=0).T

        for g in range(GROUPS):
            c_b = bc_ref[rows, cdim + g * STATE:cdim + (g + 1) * STATE]
            b_b = bc_ref[rows, g * STATE:(g + 1) * STATE]
            cb = lax.dot_general(c_b, b_b, nt, preferred_element_type=F32)
            b_t = b_b.astype(F32).T
            for k4 in range(pairs_per_group):
                k = g * pairs_per_group + k4
                cols = slice(k * LANE, (k + 1) * LANE)
                xpb = x_ref[rows, cols]
                x_lo = xpb * mask_l
                xbd = jnp.concatenate([x_lo, xpb - x_lo], axis=0)
                st = st_ref[k]
                yraw = _dot(c_b, st.astype(BF16))
                scores, bws, colbs, alasts = [], [], [], []
                for h in (2 * k, 2 * k + 1):
                    colb = jnp.broadcast_to(cs[:, h:h + 1], (T, T))
                    alast = cs_t[h:h + 1, T - 1:T]
                    decay_dt = jnp.exp(jnp.where(causal, colb - rsub_t[h:h + 1, :], -jnp.inf))
                    scores.append((cb * decay_dt).astype(BF16))
                    wrow = jnp.exp(alast - cs_t[h:h + 1, :]) * dt_t[h:h + 1, :]
                    bws.append((b_t * wrow).astype(BF16))
                    colbs.append(colb)
                    alasts.append(alast)
                ecol = jnp.exp(jnp.where(left, colbs[0], colbs[1]))
                elast = jnp.exp(jnp.where(left[0:1], alasts[0], alasts[1]))
                yscr_ref[rows, cols] = (_dot(jnp.concatenate(scores, axis=1), xbd) + ecol * yraw
                                        + dskx_ref[:, cols] * xpb.astype(F32))
                st_ref[k] = elast * st + _dot(jnp.concatenate(bws, axis=1), xbd)

    _gated_group_norm(lambda cols: yscr_ref[:, cols], z_ref, nw_ref, y_ref)

    @pl.when(c == N_CHUNKS // CHUNKS_PER_STEP - 1)
    def _():
        for k in range(HEADS // 2):
            ssm_ref[0, k] = st_ref[k].T


def _ssd_prompt_call(proj, dt_raw, hp, hpc, dskx, nw, triu):
    steps = N_CHUNKS // CHUNKS_PER_STEP
    rows = CHUNK * CHUNKS_PER_STEP
    row = lambda b, c: b * steps + c
    const = lambda b, c: (0, 0)
    return pl.pallas_call(
        _ssd_prompt_kernel,
        grid=(BATCH, steps),
        in_specs=[pl.BlockSpec((rows, D), lambda b, c: (row(b, c), PZ)),
                  pl.BlockSpec((rows, D), lambda b, c: (row(b, c), PX)),
                  pl.BlockSpec((rows, 2 * GROUPS * STATE),
                               lambda b, c: (row(b, c), PBC_1024)),
                  pl.BlockSpec((rows, LANE), lambda b, c: (row(b, c), 0)),
                  pl.BlockSpec((8, LANE), const),
                  pl.BlockSpec((LANE, 8), const),
                  pl.BlockSpec((1, D), const),
                  pl.BlockSpec((1, D), const),
                  pl.BlockSpec((CHUNK, CHUNK), const)],
        out_specs=[pl.BlockSpec((rows, D), lambda b, c: (row(b, c), 0)),
                   pl.BlockSpec((1, HEADS // 2, LANE, STATE), lambda b, c: (b, 0, 0, 0))],
        out_shape=[jax.ShapeDtypeStruct((NP_ROWS, D), BF16),
                   jax.ShapeDtypeStruct((BATCH, HEADS // 2, LANE, STATE), F32)],
        scratch_shapes=[pltpu.VMEM((HEADS // 2, STATE, LANE), F32),
                        pltpu.VMEM((rows, D), F32)],
        compiler_params=_params("arbitrary", "arbitrary"),
        name="ssd_prompt",
    )(proj, proj, proj, dt_raw, hp, hpc, dskx, nw, triu)


def _ssds_prep_step(tt, x_ref, bc_ref, dt_ref, hp_ref, dskx_ref, ex_ref, seg_ref,
                    ypre_ref, efull_ref, xw_ref, bb_ref, cc_ref, e3_ref):
    B = DEC_BATCH
    cdim = GROUPS * STATE
    blk = lambda t: slice(t * B, (t + 1) * B)
    x_of = lambda t: x_ref[blk(t), :].astype(F32)
    b_of = lambda t: bc_ref[blk(t), 0:cdim].astype(F32)
    c_of = lambda t: bc_ref[blk(t), cdim:2 * cdim].astype(F32)

    bb_ref[...] = b_of(tt)
    cc_ref[...] = c_of(tt)

    hp = hp_ref[...]
    a_neg = -jnp.exp(hp[1:2, :])
    dts, css = [], []
    run = None
    for t in range(DEC_SEQ):
        dt = _softplus(dt_ref[blk(t), :] + hp[0:1, :])
        run = dt * a_neg if run is None else run + dt * a_neg
        dts.append(dt)
        css.append(run)

    ex = ex_ref[...]
    e3_ref[...] = jnp.exp(css[-1])
    efull_ref[...] = _dot_exact_rhs(jnp.exp(css[tt]), ex)
    w_t = jnp.exp(css[-1] - css[tt]) * dts[tt]
    xw_ref[...] = x_of(tt) * _dot_exact_rhs(w_t, ex)

    seg = seg_ref[...]
    acc = dskx_ref[...] * x_of(tt)
    c_t = c_of(tt)
    for s in range(tt + 1):
        cbh = _dot_exact_rhs(c_t * b_of(s), seg)
        g_ts = cbh * jnp.exp(css[tt] - css[s]) * dts[s]
        acc = acc + _dot_exact_rhs(g_ts, ex) * x_of(s)
    ypre_ref[...] = acc


def _ssds_prep_kernel(*refs):
    t = pl.program_id(0)
    for tt in range(DEC_SEQ):
        pl.when(t == tt)(functools.partial(_ssds_prep_step, tt, *refs))


def _ssds_prep_call(proj, dt_raw, hp, dskx, ex, seg):
    full = lambda shape: pl.BlockSpec(shape, lambda t: (0,) * len(shape))
    step = lambda width: pl.BlockSpec((DEC_BATCH, width), lambda t: (t, 0))
    cdim = GROUPS * STATE
    return pl.pallas_call(
        _ssds_prep_kernel,
        grid=(DEC_SEQ,),
        in_specs=[pl.BlockSpec((NS_ROWS, D), lambda t: (NP_TILES, PX)),
                  pl.BlockSpec((NS_ROWS, 2 * cdim), lambda t: (NP_TILES, PBC_1024)),
                  pl.BlockSpec((NS_ROWS, LANE), lambda t: (NP_TILES, 0)),
                  full((8, LANE)), full((1, D)), full((LANE, D)), full((cdim, LANE))],
        out_specs=[step(D), step(D), step(D), step(cdim), step(cdim),
                   full((DEC_BATCH, LANE))],
        out_shape=[jax.ShapeDtypeStruct((NS_ROWS, D), F32),
                   jax.ShapeDtypeStruct((NS_ROWS, D), F32),
                   jax.ShapeDtypeStruct((NS_ROWS, D), F32),
                   jax.ShapeDtypeStruct((NS_ROWS, cdim), F32),
                   jax.ShapeDtypeStruct((NS_ROWS, cdim), F32),
                   jax.ShapeDtypeStruct((DEC_BATCH, LANE), F32)],
        compiler_params=_params("arbitrary"),
        name="ssd_sample_prep",
    )(proj, proj, dt_raw, hp, dskx, ex, seg)


SEQ_PER_STEP = 8


def _ssds_state_kernel(e3_ref, st_ref, cc_ref, bb_ref, xw_ref, yo_ref, so_ref):
    blk = pl.program_id(0)
    nt = (((1,), (1,)), ((), ()))
    tn = (((0,), (0,)), ((), ()))
    hpg = HEADS // GROUPS
    gw = hpg * HEADDIM

    def rows_of(ref, s, cols):
        v = ref[:, s, cols]
        return jnp.concatenate([v, jnp.zeros((8 - DEC_SEQ, v.shape[-1]), F32)], axis=0).astype(BF16)

    for s in range(SEQ_PER_STEP):
        b = blk * SEQ_PER_STEP + s
        for g in range(GROUPS):
            h0 = st_ref[s, g]
            c_g = rows_of(cc_ref, s, slice(g * STATE, (g + 1) * STATE))
            yraw = lax.dot_general(c_g, h0.astype(BF16), nt, preferred_element_type=F32)
            yo_ref[:, s, g * gw:(g + 1) * gw] = yraw[0:DEC_SEQ]
            x_g = rows_of(xw_ref, s, slice(g * gw, (g + 1) * gw))
            b_g = rows_of(bb_ref, s, slice(g * STATE, (g + 1) * STATE))
            dh = lax.dot_general(x_g, b_g, tn, preferred_element_type=F32)
            for hh in range(hpg):
                rows = slice(hh * HEADDIM, (hh + 1) * HEADDIM)
                so_ref[s, g, rows, :] = e3_ref[b, g * hpg + hh] * h0[rows] + dh[rows]


def _ssds_state_call(e3, state, cc_t, bb_t, xw_t):
    sb = SEQ_PER_STEP
    gw = (HEADS // GROUPS) * HEADDIM
    cdim = GROUPS * STATE
    tmajor = lambda width: pl.BlockSpec((DEC_SEQ, sb, width), lambda i: (0, i, 0))
    return pl.pallas_call(
        _ssds_state_kernel,
        grid=(DEC_BATCH // sb,),
        in_specs=[pl.BlockSpec(memory_space=pltpu.SMEM),
                  pl.BlockSpec((sb, GROUPS, gw, STATE), lambda i: (i, 0, 0, 0)),
                  tmajor(cdim), tmajor(cdim), tmajor(D)],
        out_specs=[tmajor(D),
                   pl.BlockSpec((sb, GROUPS, gw, STATE), lambda i: (i, 0, 0, 0))],
        out_shape=[jax.ShapeDtypeStruct((DEC_SEQ, DEC_BATCH, D), F32),
                   jax.ShapeDtypeStruct((DEC_BATCH, GROUPS, gw, STATE), F32)],
        compiler_params=_params("arbitrary"),
        name="ssd_sample_state",
    )(e3, state, cc_t.reshape(DEC_SEQ, DEC_BATCH, cdim), bb_t.reshape(DEC_SEQ, DEC_BATCH, cdim),
      xw_t.reshape(DEC_SEQ, DEC_BATCH, D))


def _ssds_post_kernel(ypre_ref, efull_ref, yo_ref, z_ref, nw_ref, o_ref):
    get_y = lambda cols: ypre_ref[:, cols] + efull_ref[:, cols] * yo_ref[:, cols]
    _gated_group_norm(get_y, z_ref, nw_ref, o_ref)


def _ssds_post_call(ypre, efull, yo_t, proj, nw):
    step = pl.BlockSpec((DEC_BATCH, D), lambda t: (t, 0))
    return pl.pallas_call(
        _ssds_post_kernel,
        grid=(DEC_SEQ,),
        in_specs=[step, step, step,
                  pl.BlockSpec((DEC_BATCH, D), lambda t: (NP_ROWS // DEC_BATCH + t, PZ)),
                  pl.BlockSpec((1, D), lambda t: (0, 0))],
        out_specs=step,
        out_shape=jax.ShapeDtypeStruct((NS_ROWS, D), BF16),
        compiler_params=_params("arbitrary"),
        name="ssd_sample_post",
    )(ypre, efull, yo_t, proj, nw)


def _mlp_kernel(u_ref, v_ref, lnw_ref, lnb_ref, ws_ref, bsx_ref, wsx_ref, bsx4_ref,
                y_ref, cv_ref, wm_ref, cvs_ref):
    i = pl.program_id(0)
    T = CHUNK

    @pl.when(i == 0)
    def _():
        row = lax.broadcasted_iota(jnp.int32, (T, T), 0)
        lane = lax.broadcasted_iota(jnp.int32, (T, T), 1)
        for g in range(MLP_GROUPS):
            wm_ref[g] = jnp.where(row >= lane, ws_ref[g], 0.0).astype(BF16)

    def vnorm(rows):
        vg = v_ref[rows, :].astype(F32)
        xc = vg - jnp.mean(vg, axis=-1, keepdims=True)
        y = xc * lax.rsqrt(jnp.mean(xc * xc, axis=-1, keepdims=True) + EPS)
        return y * lnw_ref[...] + lnb_ref[...]

    @pl.when(i < NP_MT)
    def _():
        for cc in range(TMM // T):
            rows = slice(cc * T, (cc + 1) * T)
            vnb = vnorm(rows).astype(BF16)
            for g in range(MLP_GROUPS):
                cols = slice(g * MLP_GROUP_DIM, (g + 1) * MLP_GROUP_DIM)
                sv = _dot(wm_ref[g], vnb[:, cols]) + bsx_ref[:, cols]
                y_ref[rows, cols] = (u_ref[rows, cols].astype(F32) * sv).astype(BF16)

    @pl.when(i == NP_MT)
    def _():
        B = DEC_BATCH
        for t in range(DEC_SEQ):
            rows = slice(t * B, (t + 1) * B)
            vn = vnorm(rows)
            cvs_ref[rows, :] = vn
            cv_ref[:, t, :] = vn
        for t in range(DEC_SEQ):
            rows = slice(t * B, (t + 1) * B)
            acc = bsx4_ref[t:t + 1, :]
            for s in range(t + 1):
                acc = acc + wsx_ref[4 * t + s:4 * t + s + 1, :] * cvs_ref[s * B:(s + 1) * B, :]
            y_ref[rows, :] = (u_ref[rows, :].astype(F32) * acc).astype(BF16)


def _mlp_call(proj, lnw, lnb, ws, bsx, wsx, bsx4):
    full = lambda shape: pl.BlockSpec(shape, lambda i: (0,) * len(shape))
    return pl.pallas_call(
        _mlp_kernel,
        grid=(N_MT,),
        in_specs=[pl.BlockSpec((TMM, D), lambda i: (i, PU)),
                  pl.BlockSpec((TMM, D), lambda i: (i, PV)),
                  full((1, D)), full((1, D)),
                  full((MLP_GROUPS, CHUNK, CHUNK)),
                  full((CHUNK, D)), full((16, D)), full((8, D))],
        out_specs=[pl.BlockSpec((TMM, D), lambda i: (i, 0)),
                   full((DEC_BATCH, DEC_SEQ, D))],
        out_shape=[jax.ShapeDtypeStruct((M_ROWS, D), BF16),
                   jax.ShapeDtypeStruct((DEC_BATCH, DEC_SEQ, D), F32)],
        scratch_shapes=[pltpu.VMEM((MLP_GROUPS, CHUNK, CHUNK), BF16),
                        pltpu.VMEM((NS_ROWS, D), F32)],
        compiler_params=_params("arbitrary"),
        name="gmlp",
    )(proj, proj, lnw, lnb, ws, bsx, wsx, bsx4)


def _cast_rows(src_ref, dst_ref, chunk=256):
    def body(r, carry):
        rows = pl.ds(pl.multiple_of(r * chunk, chunk), chunk)
        dst_ref[rows, :] = src_ref[rows, :].astype(dst_ref.dtype)
        return carry
    lax.fori_loop(0, src_ref.shape[0] // chunk, body, 0)


def _per_tile(i, prompt_fn, sample_fn):
    pl.when(i < NP_MT)(prompt_fn)
    pl.when(i == NP_MT)(sample_fn)


def _merge_kernel(ysp_ref, yss_ref, ym_ref, ga_ref, gb_ref, w1_ref, w2_ref, o_ref, w1b, w2b):
    i = pl.program_id(1)

    @pl.when(i == 0)
    def _():
        _cast_rows(w1_ref, w1b)
        _cast_rows(w2_ref, w2b)

    def emit(ys, rows):
        a1 = _dot(ys, w1b[...])
        a2 = _dot(ym_ref[rows, :], w2b[...])
        o_ref[rows, :] = (jax.nn.sigmoid(ga_ref[rows, :].astype(F32)) * a1
                          + jax.nn.sigmoid(gb_ref[rows, :].astype(F32)) * a2).astype(BF16)

    _per_tile(i,
              lambda: emit(ysp_ref[...], slice(None)),
              lambda: emit(yss_ref[...], slice(0, NS_ROWS)))


def _merge_call(ysp, yss, ym, proj, w1, w2):
    tn = 512
    nb = D // tn
    return pl.pallas_call(
        _merge_kernel,
        grid=(nb, N_MT),
        in_specs=[pl.BlockSpec((TMM, D), lambda j, i: (jnp.minimum(i, NP_MT - 1), 0)),
                  pl.BlockSpec((NS_ROWS, D), lambda j, i: (0, 0)),
                  pl.BlockSpec((TMM, D), lambda j, i: (i, 0)),
                  pl.BlockSpec((TMM, tn), lambda j, i: (i, PGA * nb + j)),
                  pl.BlockSpec((TMM, tn), lambda j, i: (i, PGB * nb + j)),
                  pl.BlockSpec((D, tn), lambda j, i: (0, j)),
                  pl.BlockSpec((D, tn), lambda j, i: (0, j))],
        out_specs=pl.BlockSpec((TMM, tn), lambda j, i: (i, j)),
        out_shape=jax.ShapeDtypeStruct((M_ROWS, D), BF16),
        scratch_shapes=[pltpu.VMEM((D, tn), BF16), pltpu.VMEM((D, tn), BF16)],
        compiler_params=_params("arbitrary", "arbitrary"),
        name="branch_merge",
    )(ysp, yss, ym, proj, proj, w1, w2)


def _resid_kernel(a_ref, w_ref, rp_ref, rs_ref, gp_ref, gs_ref, o_ref, wb):
    i = pl.program_id(1)

    @pl.when(i == 0)
    def _():
        _cast_rows(w_ref, wb)

    def prompt():
        o_ref[...] = rp_ref[...] + gp_ref[0] * _dot(a_ref[...], wb[...])

    def sample():
        acc = _dot(a_ref[0:NS_ROWS, :], wb[...])
        tn = acc.shape[-1]
        acc3 = acc.reshape(DEC_SEQ, DEC_BATCH, tn) * gs_ref[...][None]
        o_ref[0:NS_ROWS, :] = rs_ref[...] + acc3.reshape(NS_ROWS, tn)

    _per_tile(i, prompt, sample)


def _resid_call(a, w, rp, rs, rs_block, mod_p, mod_s, k_gate, tn, name, single_buffer_w=False):
    kdim = a.shape[1]
    nb = D // tn
    w_mode = dict(pipeline_mode=pl.Buffered(1)) if single_buffer_w else {}
    return pl.pallas_call(
        _resid_kernel,
        grid=(nb, N_MT),
        in_specs=[pl.BlockSpec((TMM, kdim), lambda j, i: (i, 0)),
                  pl.BlockSpec((kdim, tn), lambda j, i: (0, j), **w_mode),
                  pl.BlockSpec((TMM, tn), lambda j, i: (jnp.minimum(i, NP_MT - 1), j)),
                  pl.BlockSpec((NS_ROWS, tn), lambda j, i: (rs_block, j)),
                  pl.BlockSpec((1, 1, tn), lambda j, i: (_seq_of_mtile(i), 0, k_gate * nb + j)),
                  pl.BlockSpec((DEC_BATCH, tn), lambda j, i: (0, k_gate * nb + j))],
        out_specs=pl.BlockSpec((TMM, tn), lambda j, i: (i, j)),
        out_shape=jax.ShapeDtypeStruct((M_ROWS, D), F32),
        scratch_shapes=[pltpu.VMEM((kdim, tn), BF16)],
        compiler_params=_params("arbitrary", "arbitrary"),
        name=name,
    )(a, w, rp, rs, mod_p, mod_s)


TN_FF = 512
N_FF_BLOCKS = D_FF // TN_FF


def _up_kernel(a_ref, wa_ref, wv_ref, cw_ref, cb_ref, fst_ref,
               h_ref, fcp_ref, fcs_ref, wab, wvb, acc_a, acc_v):
    i = pl.program_id(1)
    rc = 64

    @pl.when(i == 0)
    def _():
        _cast_rows(wa_ref, wab)
        _cast_rows(wv_ref, wvb)

    def prompt():
        starts_sequence = i % MT_PER_SEQ == 0
        acc_a[0:8, :] = jnp.where(starts_sequence, 0.0, acc_a[TMM:TMM + 8, :])
        x = a_ref[...]
        acc_a[8:8 + TMM, :] = _dot(x, wab[...])
        acc_v[...] = _dot(x, wvb[...])
        cw, cb = cw_ref[...], cb_ref[...]
        for r in range(0, TMM, rc):
            conv = (cb + cw[2:3] * acc_a[8 + r:8 + r + rc, :]
                    + cw[1:2] * acc_a[7 + r:7 + r + rc, :]
                    + cw[0:1] * acc_a[6 + r:6 + r + rc, :])
            h_ref[r:r + rc, :] = (jax.nn.gelu(conv) * acc_v[r:r + rc, :]).astype(BF16)
        fcp_ref[0] = acc_a[TMM + 6:TMM + 8, :]

    def sample():
        B = DEC_BATCH
        x = a_ref[0:NS_ROWS, :]
        acc_a[8:8 + NS_ROWS, :] = _dot(x, wab[...])
        acc_v[0:NS_ROWS, :] = _dot(x, wvb[...])
        cw, cb = cw_ref[...], cb_ref[...]

        def pre(t, r):
            if t < 0:
                return fst_ref[r:r + rc, t + 2, :]
            return acc_a[8 + t * B + r:8 + t * B + r + rc, :]

        for t in range(DEC_SEQ):
            for r in range(0, B, rc):
                conv = cb + cw[2:3] * pre(t, r) + cw[1:2] * pre(t - 1, r) + cw[0:1] * pre(t - 2, r)
                h_ref[t * B + r:t * B + r + rc, :] = (
                    jax.nn.gelu(conv) * acc_v[t * B + r:t * B + r + rc, :]).astype(BF16)
        fcs_ref[:, 0, :] = acc_a[8 + 2 * B:8 + 3 * B, :]
        fcs_ref[:, 1, :] = acc_a[8 + 3 * B:8 + 4 * B, :]

    _per_tile(i, prompt, sample)


def _up_call(n2, w_up, cw, cb, fst):
    return pl.pallas_call(
        _up_kernel,
        grid=(N_FF_BLOCKS, N_MT),
        in_specs=[pl.BlockSpec((TMM, D), lambda j, i: (i, 0)),
                  pl.BlockSpec((D, TN_FF), lambda j, i: (0, j)),
                  pl.BlockSpec((D, TN_FF), lambda j, i: (0, N_FF_BLOCKS + j)),
                  pl.BlockSpec((3, TN_FF), lambda j, i: (0, j)),
                  pl.BlockSpec((1, TN_FF), lambda j, i: (0, j)),
                  pl.BlockSpec((DEC_BATCH, 2, TN_FF), lambda j, i: (0, 0, j))],
        out_specs=[pl.BlockSpec((TMM, TN_FF), lambda j, i: (i, j)),
                   pl.BlockSpec((1, 2, TN_FF), lambda j, i: (_seq_of_mtile(i), 0, j)),
                   pl.BlockSpec((DEC_BATCH, 2, TN_FF), lambda j, i: (0, 0, j))],
        out_shape=[jax.ShapeDtypeStruct((M_ROWS, D_FF), BF16),
                   jax.ShapeDtypeStruct((BATCH, 2, D_FF), F32),
                   jax.ShapeDtypeStruct((DEC_BATCH, 2, D_FF), F32)],
        scratch_shapes=[pltpu.VMEM((D, TN_FF), BF16), pltpu.VMEM((D, TN_FF), BF16),
                        pltpu.VMEM((TMM + 8, TN_FF), F32), pltpu.VMEM((TMM, TN_FF), F32)],
        compiler_params=_params("arbitrary", "arbitrary"),
        name="ffn_up",
    )(n2, w_up, w_up, cw, cb, fst)


def _final_kernel(x_ref, w_ref, yp_ref, ys_ref):
    i = pl.program_id(0)

    @pl.when(i < NP_MT)
    def _():
        for r in range(0, TMM, TM):
            yp_ref[r:r + TM, :] = _rms(x_ref[r:r + TM, :]) * w_ref[...]

    @pl.when(i == NP_MT)
    def _():
        y = _rms(x_ref[0:NS_ROWS, :]) * w_ref[...]
        for t in range(DEC_SEQ):
            ys_ref[:, t, :] = y[t * DEC_BATCH:(t + 1) * DEC_BATCH]


def _final_call(x3, w):
    return pl.pallas_call(
        _final_kernel,
        grid=(N_MT,),
        in_specs=[pl.BlockSpec((TMM, D), lambda i: (i, 0)),
                  pl.BlockSpec((1, D), lambda i: (0, 0))],
        out_specs=[pl.BlockSpec((TMM, D), lambda i: (jnp.minimum(i, NP_MT - 1), 0)),
                   pl.BlockSpec((DEC_BATCH, DEC_SEQ, D), lambda i: (0, 0, 0))],
        out_shape=[jax.ShapeDtypeStruct((NP_ROWS, D), F32),
                   jax.ShapeDtypeStruct((DEC_BATCH, DEC_SEQ, D), F32)],
        compiler_params=_params("arbitrary"),
        name="final_norm",
    )(x3, w)


def _to_time_major(a):
    return jnp.transpose(a, (1, 0, 2))


def kernel(x_prompt, x_sample, state_ssm, state_ssd_conv, state_ffn_conv, c_prompt, c_sample,
           norm1_w, w_ada, b_ada, w_in, ssd_conv_w, ssd_conv_b, dt_bias, a_log, d_skip,
           ssd_norm_w, mlp_ln_w, mlp_ln_b, w_spatial, b_spatial, w_ssd_o, w_mlp_o, w_out,
           norm2_w, w_up, ffn_conv_w, ffn_conv_b, w_down, final_norm_w):
    assert w_in.shape[0] == 1, "single-layer trunk"
    row = lambda v: v.reshape(1, -1)

    xp = x_prompt.reshape(NP_ROWS, D)
    xs = _to_time_major(x_sample).reshape(NS_ROWS, D)

    mod_p, mod_s = _ada_call(c_prompt, c_sample, w_ada[0], row(b_ada[0]))
    mod_p = mod_p.reshape(BATCH, 1, 6 * D)
    K_SHIFT1, K_SCALE1, K_GATE1, K_SHIFT2, K_SCALE2, K_GATE2 = range(6)

    xp_spec = pl.BlockSpec((TMM, D), lambda i: (jnp.minimum(i, NP_MT - 1), 0))
    xs_spec = pl.BlockSpec((NS_ROWS, D), lambda i: (0, 0))
    w_in_t = w_in[0].T
    n1, dt_raw = _norm_call(xp, xp_spec, xs, xs_spec, row(norm1_w[0]), mod_p, mod_s,
                            K_SCALE1, K_SHIFT1, w_in_t)
    cst_t = _to_time_major(state_ssd_conv[0])
    proj, cst_p, ncs_t = _inproj_call(n1, w_in_t, ssd_conv_w[0], row(ssd_conv_b[0]), cst_t)

    hp = jnp.zeros((8, LANE), F32)
    hp = hp.at[0, :HEADS].set(dt_bias[0]).at[1, :HEADS].set(a_log[0]).at[2, :HEADS].set(d_skip[0])
    hpc = hp.T
    triu = jnp.triu(jnp.ones((CHUNK, CHUNK), F32)).astype(BF16)
    dskx = row(jnp.repeat(d_skip[0], HEADDIM))
    nw = row(ssd_norm_w[0])

    y_ssd_p, ssm_p = _ssd_prompt_call(proj, dt_raw, hp, hpc, dskx, nw, triu)

    head_of_col = jnp.arange(D) // HEADDIM
    ex = (jnp.arange(LANE)[:, None] == head_of_col[None, :]).astype(BF16)
    grp_of_head = jnp.arange(LANE) // (HEADS // GROUPS)
    seg = ((jnp.arange(GROUPS * STATE)[:, None] // STATE == grp_of_head[None, :])
           & (jnp.arange(LANE)[None, :] < HEADS)).astype(BF16)
    ypre, efull, xw_t, bb_t, cc_t, e3 = _ssds_prep_call(proj, dt_raw, hp, dskx, ex, seg)
    gw = (HEADS // GROUPS) * HEADDIM
    yo_t, ssm_s = _ssds_state_call(
        e3[:, :HEADS], state_ssm[0].reshape(DEC_BATCH, GROUPS, gw, STATE), cc_t, bb_t, xw_t)
    y_ssd_s = _ssds_post_call(ypre, efull, yo_t.reshape(NS_ROWS, D), proj, nw)

    per_col = lambda a: jnp.repeat(a, MLP_GROUP_DIM, axis=1)
    bsx = per_col(b_spatial[0][:, :CHUNK].T)
    ws4 = w_spatial[0][:, :DEC_SEQ, :DEC_SEQ]
    wsx = per_col(jnp.transpose(ws4, (1, 2, 0)).reshape(DEC_SEQ * DEC_SEQ, MLP_GROUPS))
    bsx4 = jnp.pad(bsx[:DEC_SEQ], ((0, 8 - DEC_SEQ), (0, 0)))
    y_mlp, cv_s = _mlp_call(proj, row(mlp_ln_w[0]), row(mlp_ln_b[0]), w_spatial[0], bsx, wsx, bsx4)

    mixed = _merge_call(y_ssd_p, y_ssd_s, y_mlp, proj, w_ssd_o[0], w_mlp_o[0])
    x2 = _resid_call(mixed, w_out[0], xp, xs, 0, mod_p, mod_s, K_GATE1, 1024, "out_proj")

    x2p_spec = pl.BlockSpec((TMM, D), lambda i: (jnp.minimum(i, NP_MT - 1), 0))
    x2s_spec = pl.BlockSpec((NS_ROWS, D), lambda i: (NP_TILES, 0))
    (n2,) = _norm_call(x2, x2p_spec, x2, x2s_spec, row(norm2_w[0]), mod_p, mod_s,
                       K_SCALE2, K_SHIFT2, None)
    h, ffn_p, ffn_s = _up_call(n2, w_up[0], ffn_conv_w[0], row(ffn_conv_b[0]), state_ffn_conv[0])
    x3 = _resid_call(h, w_down[0], x2, x2, NP_TILES, mod_p, mod_s, K_GATE2, 512, "ffn_down",
                     single_buffer_w=True)

    y_p, y_s = _final_call(x3, row(final_norm_w))

    from_t = lambda a, t, c: jnp.transpose(a.reshape(t, DEC_BATCH, c), (1, 0, 2))
    return (y_p.reshape(BATCH, SEQ, D),
            y_s,
            ssm_p.reshape(1, BATCH, HEADS, HEADDIM, STATE),
            ssm_s.reshape(1, DEC_BATCH, HEADS, HEADDIM, STATE),
            cst_p[None],
            from_t(ncs_t, 3, CONV_DIM)[None],
            ffn_p[None],
            ffn_s[None],
            cv_s[None])
```

```python
import functools

import jax
import jax.numpy as jnp
from jax import lax
from jax.experimental import pallas as pl
from jax.experimental.pallas import tpu as pltpu

F32 = jnp.float32
BF16 = jnp.bfloat16

D = 2048
BATCH, SEQ = 4, 2048
DEC_BATCH, DEC_SEQ = 128, 4
NP_ROWS = BATCH * SEQ
NS_ROWS = DEC_BATCH * DEC_SEQ
M_ROWS = NP_ROWS + NS_ROWS
TM = 512
NP_TILES = NP_ROWS // TM
TMM = 1024
NP_MT = NP_ROWS // TMM
N_MT = NP_MT + 1
MT_PER_SEQ = SEQ // TMM
HEADS, HEADDIM, GROUPS, STATE = 32, 64, 4, 128
CHUNK = 128
N_CHUNKS = SEQ // CHUNK
CONV_DIM = D + 2 * GROUPS * STATE
MLP_GROUPS = 8
MLP_GROUP_DIM = D // MLP_GROUPS
D_FF = 5632
EPS = 1e-6
DT_COL = D + CONV_DIM
TN_IN = 1024
LANE = 128
VMEM_LIMIT = 56 * 1024 * 1024


def _params(*sem, flags=None):
    return pltpu.CompilerParams(dimension_semantics=sem, vmem_limit_bytes=VMEM_LIMIT, flags=flags)


def _dot(a, b):
    return jnp.dot(a, b, preferred_element_type=F32)


def _split_bf16(v, terms):
    out = []
    r = v
    for _ in range(terms):
        p = r.astype(BF16)
        out.append(p)
        r = r - p.astype(F32)
    return out


def _dot_exact_rhs(v, e, terms=3):
    acc = None
    for p in _split_bf16(v, terms):
        d = _dot(p, e)
        acc = d if acc is None else acc + d
    return acc


def _silu(x):
    return x * jax.nn.sigmoid(x)


def _softplus(x):
    return jnp.maximum(x, 0.0) + jnp.log1p(jnp.exp(-jnp.abs(x)))


def _rms(x):
    return x * lax.rsqrt(jnp.mean(x * x, axis=-1, keepdims=True) + EPS)


def _ada_kernel(cp_ref, cs_ref, w_ref, b_ref, op_ref, os_ref):
    w = w_ref[...].astype(BF16)
    cp8 = jnp.concatenate([cp_ref[...], jnp.zeros((8 - BATCH, D), F32)], axis=0)
    op_ref[...] = (_dot(_silu(cp8).astype(BF16), w) + b_ref[...])[0:BATCH]
    os_ref[...] = _dot(_silu(cs_ref[...]).astype(BF16), w) + b_ref[...]


def _ada_call(c_prompt, c_sample, w, b):
    tn = 1024
    return pl.pallas_call(
        _ada_kernel,
        grid=(6 * D // tn,),
        in_specs=[pl.BlockSpec((BATCH, D), lambda j: (0, 0)),
                  pl.BlockSpec((DEC_BATCH, D), lambda j: (0, 0)),
                  pl.BlockSpec((D, tn), lambda j: (0, j)),
                  pl.BlockSpec((1, tn), lambda j: (0, j))],
        out_specs=[pl.BlockSpec((BATCH, tn), lambda j: (0, j)),
                   pl.BlockSpec((DEC_BATCH, tn), lambda j: (0, j))],
        out_shape=[jax.ShapeDtypeStruct((BATCH, 6 * D), F32),
                   jax.ShapeDtypeStruct((DEC_BATCH, 6 * D), F32)],
        compiler_params=_params("arbitrary"),
        name="ada_mod",
    )(c_prompt, c_sample, w, b)


def _norm_kernel(with_dt, xp_ref, xs_ref, nw_ref, scp_ref, shp_ref, scs_ref, shs_ref, *rest):
    if with_dt:
        wdt_ref, n_ref, dt_ref = rest
    else:
        (n_ref,) = rest
    i = pl.program_id(0)

    def emit(n, rows):
        nb = n.astype(BF16)
        n_ref[rows, :] = nb
        if with_dt:
            dt_ref[rows, :] = lax.dot_general(nb, wdt_ref[...].astype(BF16),
                                              (((1,), (1,)), ((), ())),
                                              preferred_element_type=F32)

    @pl.when(i < NP_MT)
    def _():
        for r in range(0, TMM, TM):
            rows = slice(r, r + TM)
            y = _rms(xp_ref[rows, :]) * nw_ref[...]
            emit(y * (1.0 + scp_ref[0]) + shp_ref[0], rows)

    @pl.when(i == NP_MT)
    def _():
        y = _rms(xs_ref[...]) * nw_ref[...]
        y3 = y.reshape(DEC_SEQ, DEC_BATCH, D)
        emit((y3 * (1.0 + scs_ref[...])[None] + shs_ref[...][None]).reshape(NS_ROWS, D),
             slice(0, NS_ROWS))


def _seq_of_mtile(i):
    return jnp.minimum(i // MT_PER_SEQ, BATCH - 1)


def _norm_call(xp, xp_spec, xs, xs_spec, nw, mod_p, mod_s, k_scale, k_shift, w_in_t):
    with_dt = w_in_t is not None
    in_specs = [
        xp_spec, xs_spec,
        pl.BlockSpec((1, D), lambda i: (0, 0)),
        pl.BlockSpec((1, 1, D), lambda i: (_seq_of_mtile(i), 0, k_scale)),
        pl.BlockSpec((1, 1, D), lambda i: (_seq_of_mtile(i), 0, k_shift)),
        pl.BlockSpec((DEC_BATCH, D), lambda i: (0, k_scale)),
        pl.BlockSpec((DEC_BATCH, D), lambda i: (0, k_shift)),
    ]
    args = [xp, xs, nw, mod_p, mod_p, mod_s, mod_s]
    out_specs = [pl.BlockSpec((TMM, D), lambda i: (i, 0))]
    out_shape = [jax.ShapeDtypeStruct((M_ROWS, D), BF16)]
    if with_dt:
        in_specs.append(pl.BlockSpec((pl.Element(LANE), pl.Element(D)), lambda i: (DT_COL, 0)))
        args.append(w_in_t)
        out_specs.append(pl.BlockSpec((TMM, LANE), lambda i: (i, 0)))
        out_shape.append(jax.ShapeDtypeStruct((M_ROWS, LANE), F32))
    return pl.pallas_call(
        functools.partial(_norm_kernel, with_dt),
        grid=(N_MT,),
        in_specs=in_specs,
        out_specs=out_specs,
        out_shape=out_shape,
        compiler_params=_params("arbitrary"),
        name="norm_mod_dt" if with_dt else "norm_mod",
    )(*args)


N_IN_BLOCKS = 13
UVG_ROW = DT_COL + HEADS
FIRST_CONV_BLOCK = 10


def _in_src_row(j):
    row = jnp.where(j < 2, j * TN_IN,
                    jnp.where(j < FIRST_CONV_BLOCK, UVG_ROW + (j - 2) * TN_IN,
                              D + (j - FIRST_CONV_BLOCK) * TN_IN))
    return pl.multiple_of(row, HEADS)


def _in_conv_block(j):
    return jnp.maximum(j - FIRST_CONV_BLOCK, 0)


def _inproj_kernel(a_ref, wt_ref, cw_ref, cb_ref, cst_ref, o_ref, csp_ref, css_ref, wbf_ref, acc_ref):
    j = pl.program_id(0)
    i = pl.program_id(1)
    rc = 64
    B = DEC_BATCH

    @pl.when(i == 0)
    def _():
        for r in range(TN_IN // LANE):
            rows = slice(r * LANE, (r + 1) * LANE)
            wbf_ref[:, rows] = wt_ref[rows, :].T.astype(BF16)

    def elementwise(fn, rows):
        def body():
            acc_ref[8:8 + rows, :] = _dot(a_ref[0:rows, :], wbf_ref[...])
            for r in range(0, rows, rc):
                val = acc_ref[8 + r:8 + r + rc, :]
                o_ref[r:r + rc, :] = (val if fn is None else fn(val)).astype(o_ref.dtype)
        return body

    def conv_prompt():
        starts_sequence = i % MT_PER_SEQ == 0
        acc_ref[0:8, :] = jnp.where(starts_sequence, 0.0, acc_ref[TMM:TMM + 8, :])
        acc_ref[8:8 + TMM, :] = _dot(a_ref[...], wbf_ref[...])
        cw, cb = cw_ref[...], cb_ref[...]
        for r in range(0, TMM, rc):
            conv = cb + cw[3:4] * acc_ref[8 + r:8 + r + rc, :]
            for k in range(3):
                conv = conv + cw[k:k + 1] * acc_ref[5 + k + r:5 + k + r + rc, :]
            o_ref[r:r + rc, :] = _silu(conv).astype(o_ref.dtype)
        csp_ref[0] = acc_ref[TMM + 5:TMM + 8, :]

    def conv_sample():
        acc_ref[8:8 + NS_ROWS, :] = _dot(a_ref[0:NS_ROWS, :], wbf_ref[...])
        cw, cb = cw_ref[...], cb_ref[...]

        def pre(t, r):
            if t < 0:
                return cst_ref[t + 3, r:r + rc, :]
            return acc_ref[8 + t * B + r:8 + t * B + r + rc, :]

        for t in range(DEC_SEQ):
            for r in range(0, B, rc):
                conv = cb + cw[3:4] * pre(t, r)
                for k in range(3):
                    conv = conv + cw[k:k + 1] * pre(t - 3 + k, r)
                o_ref[t * B + r:t * B + r + rc, :] = _silu(conv).astype(o_ref.dtype)
        for t in range(1, DEC_SEQ):
            css_ref[t - 1] = acc_ref[8 + t * B:8 + (t + 1) * B, :]

    is_conv = j >= FIRST_CONV_BLOCK
    kinds = ((j < 2, _silu),
             (jnp.logical_and(j >= 2, j < 6), jax.nn.gelu),
             (jnp.logical_and(j >= 6, j < FIRST_CONV_BLOCK), None))
    for cond, fn in kinds:
        pl.when(jnp.logical_and(cond, i < NP_MT))(elementwise(fn, TMM))
        pl.when(jnp.logical_and(cond, i == NP_MT))(elementwise(fn, NS_ROWS))
    pl.when(jnp.logical_and(is_conv, i < NP_MT))(conv_prompt)
    pl.when(jnp.logical_and(is_conv, i == NP_MT))(conv_sample)


def _inproj_call(n1, w_in_t, cw, cb, cst_t):
    cblk = _in_conv_block
    seq = lambda j, i: jnp.where(j < FIRST_CONV_BLOCK, 0, _seq_of_mtile(i))
    return pl.pallas_call(
        _inproj_kernel,
        grid=(N_IN_BLOCKS, N_MT),
        in_specs=[pl.BlockSpec((TMM, D), lambda j, i: (i, 0)),
                  pl.BlockSpec((pl.Element(TN_IN), pl.Element(D)),
                               lambda j, i: (_in_src_row(j), 0)),
                  pl.BlockSpec((4, TN_IN), lambda j, i: (0, cblk(j))),
                  pl.BlockSpec((1, TN_IN), lambda j, i: (0, cblk(j))),
                  pl.BlockSpec((3, DEC_BATCH, TN_IN), lambda j, i: (0, 0, cblk(j)))],
        out_specs=[pl.BlockSpec((TMM, TN_IN), lambda j, i: (i, j)),
                   pl.BlockSpec((1, 3, TN_IN), lambda j, i: (seq(j, i), 0, cblk(j))),
                   pl.BlockSpec((3, DEC_BATCH, TN_IN), lambda j, i: (0, 0, cblk(j)))],
        out_shape=[jax.ShapeDtypeStruct((M_ROWS, N_IN_BLOCKS * TN_IN), BF16),
                   jax.ShapeDtypeStruct((BATCH, 3, CONV_DIM), F32),
                   jax.ShapeDtypeStruct((3, DEC_BATCH, CONV_DIM), F32)],
        scratch_shapes=[pltpu.VMEM((D, TN_IN), BF16), pltpu.VMEM((TMM + 8, TN_IN), F32)],
        compiler_params=_params("arbitrary", "arbitrary"),
        name="in_proj",
    )(n1, w_in_t, cw, cb, cst_t)


PZ, PU, PV, PGA, PGB, PX = 0, 1, 2, 3, 4, 5
PBC_1024 = 12


def _gated_group_norm(get_y, zact_ref, nw_ref, o_ref, rows=slice(None)):
    gw = D // GROUPS
    for g in range(GROUPS):
        cols = slice(g * gw, (g + 1) * gw)
        gg = _rms(get_y(cols) * zact_ref[rows, cols].astype(F32))
        o_ref[rows, cols] = (gg * nw_ref[:, cols]).astype(o_ref.dtype)


CHUNKS_PER_STEP = 4


def _ssd_prompt_kernel(z_ref, x_ref, bc_ref, dt_ref, hp_ref, hpc_ref, dskx_ref, nw_ref, triu_ref,
                       y_ref, ssm_ref, st_ref, yscr_ref):
    c = pl.program_id(1)
    T = CHUNK
    cdim = GROUPS * STATE

    @pl.when(c == 0)
    def _():
        st_ref[...] = jnp.zeros(st_ref.shape, F32)

    hp = hp_ref[...]
    a_col = -jnp.exp(hpc_ref[0:HEADS, 1:2])
    triu = triu_ref[...]
    row = lax.broadcasted_iota(jnp.int32, (T, T), 0)
    lane = lax.broadcasted_iota(jnp.int32, (T, T), 1)
    causal = row >= lane
    left = lane < HEADDIM
    mask_l = jnp.where(left, 1.0, 0.0).astype(BF16)
    nt = (((1,), (1,)), ((), ()))
    pairs_per_group = HEADS // GROUPS // 2

    for sub in range(CHUNKS_PER_STEP):
        rows = slice(sub * T, (sub + 1) * T)
        y_buf = yscr_ref.at[sub % 2]
        dt_t = _softplus((dt_ref[rows, :] + hp[0:1, :]).T[0:HEADS, :])
        cs_t = _dot_exact_rhs(dt_t * a_col, triu)
        rsub_t = cs_t - jnp.log(dt_t)
        cs = jnp.concatenate([cs_t, jnp.zeros((LANE - HEADS, T), F32)], axis=0).T

        for g in range(GROUPS):
            c_b = bc_ref[rows, cdim + g * STATE:cdim + (g + 1) * STATE]
            b_b = bc_ref[rows, g * STATE:(g + 1) * STATE]
            cb = lax.dot_general(c_b, b_b, nt, preferred_element_type=F32)
            b_t = b_b.astype(F32).T
            for k4 in range(pairs_per_group):
                k = g * pairs_per_group + k4
                cols = slice(k * LANE, (k + 1) * LANE)
                xpb = x_ref[rows, cols]
                x_lo = xpb * mask_l
                xbd = jnp.concatenate([x_lo, xpb - x_lo], axis=0)
                st = st_ref[k]
                yraw = _dot(c_b, st.astype(BF16))
                scores, bws, colbs, alasts = [], [], [], []
                for h in (2 * k, 2 * k + 1):
                    colb = jnp.broadcast_to(cs[:, h:h + 1], (T, T))
                    alast = cs_t[h:h + 1, T - 1:T]
                    decay_dt = jnp.exp(jnp.where(causal, colb - rsub_t[h:h + 1, :], -jnp.inf))
                    scores.append((cb * decay_dt).astype(BF16))
                    wrow = jnp.exp(alast - cs_t[h:h + 1, :]) * dt_t[h:h + 1, :]
                    bws.append((b_t * wrow).astype(BF16))
                    colbs.append(colb)
                    alasts.append(alast)
                ecol = jnp.exp(jnp.where(left, colbs[0], colbs[1]))
                elast = jnp.exp(jnp.where(left[0:1], alasts[0], alasts[1]))
                y_buf[:, cols] = (_dot(jnp.concatenate(scores, axis=1), xbd) + ecol * yraw
                                  + dskx_ref[:, cols] * xpb.astype(F32))
                st_ref[k] = elast * st + _dot(jnp.concatenate(bws, axis=1), xbd)
        _gated_group_norm(lambda cols: y_buf[:, cols], z_ref, nw_ref, y_ref, rows)

    @pl.when(c == N_CHUNKS // CHUNKS_PER_STEP - 1)
    def _():
        for k in range(HEADS // 2):
            ssm_ref[0, k] = st_ref[k].T


def _ssds_prep_step(tt, x_ref, bc_ref, dt_ref, hp_ref, dskx_ref, ex_ref, seg_ref,
                    ypre_ref, efull_ref, xw_ref, bb_ref, cc_ref, e3_ref):
    B = DEC_BATCH
    cdim = GROUPS * STATE
    blk = lambda t: slice(t * B, (t + 1) * B)
    x_of = lambda t: x_ref[blk(t), :].astype(F32)
    b_of = lambda t: bc_ref[blk(t), 0:cdim].astype(F32)
    c_of = lambda t: bc_ref[blk(t), cdim:2 * cdim].astype(F32)

    bb_ref[...] = b_of(tt)
    cc_ref[...] = c_of(tt)

    hp = hp_ref[...]
    a_neg = -jnp.exp(hp[1:2, :])
    dts, css = [], []
    run = None
    for t in range(DEC_SEQ):
        dt = _softplus(dt_ref[blk(t), :] + hp[0:1, :])
        run = dt * a_neg if run is None else run + dt * a_neg
        dts.append(dt)
        css.append(run)

    ex = ex_ref[...]
    e3_ref[...] = jnp.exp(css[-1])
    efull_ref[...] = _dot_exact_rhs(jnp.exp(css[tt]), ex)
    w_t = jnp.exp(css[-1] - css[tt]) * dts[tt]
    xw_ref[...] = x_of(tt) * _dot_exact_rhs(w_t, ex)

    seg = seg_ref[...]
    acc = dskx_ref[...] * x_of(tt)
    c_t = c_of(tt)
    for s in range(tt + 1):
        cbh = _dot_exact_rhs(c_t * b_of(s), seg)
        g_ts = cbh * jnp.exp(css[tt] - css[s]) * dts[s]
        acc = acc + _dot_exact_rhs(g_ts, ex) * x_of(s)
    ypre_ref[...] = acc


def _ssds_prep_kernel(*refs):
    t = pl.program_id(0)
    for tt in range(DEC_SEQ):
        pl.when(t == tt)(functools.partial(_ssds_prep_step, tt, *refs))


def _ssds_prep_call(proj, dt_raw, hp, dskx, ex, seg):
    full = lambda shape: pl.BlockSpec(shape, lambda t: (0,) * len(shape))
    step = lambda width: pl.BlockSpec((DEC_BATCH, width), lambda t: (t, 0))
    cdim = GROUPS * STATE
    return pl.pallas_call(
        _ssds_prep_kernel,
        grid=(DEC_SEQ,),
        in_specs=[pl.BlockSpec((NS_ROWS, D), lambda t: (NP_TILES, PX)),
                  pl.BlockSpec((NS_ROWS, 2 * cdim), lambda t: (NP_TILES, PBC_1024)),
                  pl.BlockSpec((NS_ROWS, LANE), lambda t: (NP_TILES, 0)),
                  full((8, LANE)), full((1, D)), full((LANE, D)), full((cdim, LANE))],
        out_specs=[step(D), step(D), step(D), step(cdim), step(cdim),
                   full((DEC_BATCH, LANE))],
        out_shape=[jax.ShapeDtypeStruct((NS_ROWS, D), F32),
                   jax.ShapeDtypeStruct((NS_ROWS, D), F32),
                   jax.ShapeDtypeStruct((NS_ROWS, D), F32),
                   jax.ShapeDtypeStruct((NS_ROWS, cdim), F32),
                   jax.ShapeDtypeStruct((NS_ROWS, cdim), F32),
                   jax.ShapeDtypeStruct((DEC_BATCH, LANE), F32)],
        compiler_params=_params("arbitrary"),
        name="ssd_sample_prep",
    )(proj, proj, dt_raw, hp, dskx, ex, seg)


SEQ_PER_STEP = 8


def _ssds_state_body(blk, e3_ref, st_ref, cc_ref, bb_ref, xw_ref, yo_ref, so_ref):
    nt = (((1,), (1,)), ((), ()))
    tn = (((0,), (0,)), ((), ()))
    hpg = HEADS // GROUPS
    gw = hpg * HEADDIM

    def rows_of(ref, s, cols):
        v = ref[:, s, cols]
        return jnp.concatenate([v, jnp.zeros((8 - DEC_SEQ, v.shape[-1]), F32)], axis=0).astype(BF16)

    for s in range(SEQ_PER_STEP):
        b = blk * SEQ_PER_STEP + s
        for g in range(GROUPS):
            h0 = st_ref[s, g]
            c_g = rows_of(cc_ref, s, slice(g * STATE, (g + 1) * STATE))
            yraw = lax.dot_general(c_g, h0.astype(BF16), nt, preferred_element_type=F32)
            yo_ref[:, s, g * gw:(g + 1) * gw] = yraw[0:DEC_SEQ]
            x_g = rows_of(xw_ref, s, slice(g * gw, (g + 1) * gw))
            b_g = rows_of(bb_ref, s, slice(g * STATE, (g + 1) * STATE))
            dh = lax.dot_general(x_g, b_g, tn, preferred_element_type=F32)
            for hh in range(hpg):
                rows = slice(hh * HEADDIM, (hh + 1) * HEADDIM)
                so_ref[s, g, rows, :] = e3_ref[b, g * hpg + hh] * h0[rows] + dh[rows]


PROMPT_STEPS = N_CHUNKS // CHUNKS_PER_STEP
assert BATCH * PROMPT_STEPS * SEQ_PER_STEP == DEC_BATCH


def _ssd_fused_kernel(z_ref, x_ref, bc_ref, dt_ref, hp_ref, hpc_ref, dskx_ref, nw_ref, triu_ref,
                      e3_ref, sti_ref, cc_ref, bb_ref, xw_ref,
                      y_ref, ssm_ref, yo_ref, so_ref, st_ref, yscr_ref):
    _ssd_prompt_kernel(z_ref, x_ref, bc_ref, dt_ref, hp_ref, hpc_ref, dskx_ref, nw_ref, triu_ref,
                       y_ref, ssm_ref, st_ref, yscr_ref)
    blk = pl.program_id(0) * PROMPT_STEPS + pl.program_id(1)
    _ssds_state_body(blk, e3_ref, sti_ref, cc_ref, bb_ref, xw_ref, yo_ref, so_ref)


def _ssd_fused_call(proj, dt_raw, hp, hpc, dskx, nw, triu, e3, state, cc_t, bb_t, xw_t):
    rows = CHUNK * CHUNKS_PER_STEP
    sb = SEQ_PER_STEP
    gw = (HEADS // GROUPS) * HEADDIM
    cdim = GROUPS * STATE
    step = lambda b, c: b * PROMPT_STEPS + c
    const = lambda b, c: (0, 0)
    tmajor = lambda width: pl.BlockSpec((DEC_SEQ, sb, width), lambda b, c: (0, step(b, c), 0))
    st_spec = pl.BlockSpec((sb, GROUPS, gw, STATE), lambda b, c: (step(b, c), 0, 0, 0))
    return pl.pallas_call(
        _ssd_fused_kernel,
        grid=(BATCH, PROMPT_STEPS),
        in_specs=[pl.BlockSpec((rows, D), lambda b, c: (step(b, c), PZ)),
                  pl.BlockSpec((rows, D), lambda b, c: (step(b, c), PX)),
                  pl.BlockSpec((rows, 2 * cdim), lambda b, c: (step(b, c), PBC_1024)),
                  pl.BlockSpec((rows, LANE), lambda b, c: (step(b, c), 0)),
                  pl.BlockSpec((8, LANE), const),
                  pl.BlockSpec((LANE, 8), const),
                  pl.BlockSpec((1, D), const),
                  pl.BlockSpec((1, D), const),
                  pl.BlockSpec((CHUNK, CHUNK), const),
                  pl.BlockSpec(memory_space=pltpu.SMEM),
                  st_spec, tmajor(cdim), tmajor(cdim), tmajor(D)],
        out_specs=[pl.BlockSpec((rows, D), lambda b, c: (step(b, c), 0)),
                   pl.BlockSpec((1, HEADS // 2, LANE, STATE), lambda b, c: (b, 0, 0, 0)),
                   tmajor(D), st_spec],
        out_shape=[jax.ShapeDtypeStruct((NP_ROWS, D), BF16),
                   jax.ShapeDtypeStruct((BATCH, HEADS // 2, LANE, STATE), F32),
                   jax.ShapeDtypeStruct((DEC_SEQ, DEC_BATCH, D), F32),
                   jax.ShapeDtypeStruct((DEC_BATCH, GROUPS, gw, STATE), F32)],
        scratch_shapes=[pltpu.VMEM((HEADS // 2, STATE, LANE), F32),
                        pltpu.VMEM((2, CHUNK, D), F32)],
        compiler_params=_params("arbitrary", "arbitrary"),
        name="ssd_prompt_and_sample_state",
    )(proj, proj, proj, dt_raw, hp, hpc, dskx, nw, triu,
      e3, state, cc_t.reshape(DEC_SEQ, DEC_BATCH, cdim), bb_t.reshape(DEC_SEQ, DEC_BATCH, cdim),
      xw_t.reshape(DEC_SEQ, DEC_BATCH, D))


def _ssds_post_kernel(ypre_ref, efull_ref, yo_ref, z_ref, nw_ref, o_ref):
    get_y = lambda cols: ypre_ref[:, cols] + efull_ref[:, cols] * yo_ref[:, cols]
    _gated_group_norm(get_y, z_ref, nw_ref, o_ref)


def _ssds_post_call(ypre, efull, yo_t, proj, nw):
    step = pl.BlockSpec((DEC_BATCH, D), lambda t: (t, 0))
    return pl.pallas_call(
        _ssds_post_kernel,
        grid=(DEC_SEQ,),
        in_specs=[step, step, step,
                  pl.BlockSpec((DEC_BATCH, D), lambda t: (NP_ROWS // DEC_BATCH + t, PZ)),
                  pl.BlockSpec((1, D), lambda t: (0, 0))],
        out_specs=step,
        out_shape=jax.ShapeDtypeStruct((NS_ROWS, D), BF16),
        compiler_params=_params("arbitrary"),
        name="ssd_sample_post",
    )(ypre, efull, yo_t, proj, nw)


def _mlp_kernel(u_ref, v_ref, lnw_ref, lnb_ref, ws_ref, bsx_ref, wsx_ref, bsx4_ref,
                y_ref, cv_ref, wm_ref, cvs_ref):
    i = pl.program_id(0)
    T = CHUNK

    @pl.when(i == 0)
    def _():
        row = lax.broadcasted_iota(jnp.int32, (T, T), 0)
        lane = lax.broadcasted_iota(jnp.int32, (T, T), 1)
        for g in range(MLP_GROUPS):
            wm_ref[g] = jnp.where(row >= lane, ws_ref[g], 0.0).astype(BF16)

    def vnorm(rows):
        vg = v_ref[rows, :].astype(F32)
        xc = vg - jnp.mean(vg, axis=-1, keepdims=True)
        y = xc * lax.rsqrt(jnp.mean(xc * xc, axis=-1, keepdims=True) + EPS)
        return y * lnw_ref[...] + lnb_ref[...]

    @pl.when(i < NP_MT)
    def _():
        for cc in range(TMM // T):
            rows = slice(cc * T, (cc + 1) * T)
            vnb = vnorm(rows).astype(BF16)
            for g in range(MLP_GROUPS):
                cols = slice(g * MLP_GROUP_DIM, (g + 1) * MLP_GROUP_DIM)
                sv = _dot(wm_ref[g], vnb[:, cols]) + bsx_ref[:, cols]
                y_ref[rows, cols] = (u_ref[rows, cols].astype(F32) * sv).astype(BF16)

    @pl.when(i == NP_MT)
    def _():
        B = DEC_BATCH
        for t in range(DEC_SEQ):
            rows = slice(t * B, (t + 1) * B)
            vn = vnorm(rows)
            cvs_ref[rows, :] = vn
            cv_ref[:, t, :] = vn
        for t in range(DEC_SEQ):
            rows = slice(t * B, (t + 1) * B)
            acc = bsx4_ref[t:t + 1, :]
            for s in range(t + 1):
                acc = acc + wsx_ref[4 * t + s:4 * t + s + 1, :] * cvs_ref[s * B:(s + 1) * B, :]
            y_ref[rows, :] = (u_ref[rows, :].astype(F32) * acc).astype(BF16)


def _mlp_call(proj, lnw, lnb, ws, bsx, wsx, bsx4):
    full = lambda shape: pl.BlockSpec(shape, lambda i: (0,) * len(shape))
    return pl.pallas_call(
        _mlp_kernel,
        grid=(N_MT,),
        in_specs=[pl.BlockSpec((TMM, D), lambda i: (i, PU)),
                  pl.BlockSpec((TMM, D), lambda i: (i, PV)),
                  full((1, D)), full((1, D)),
                  full((MLP_GROUPS, CHUNK, CHUNK)),
                  full((CHUNK, D)), full((16, D)), full((8, D))],
        out_specs=[pl.BlockSpec((TMM, D), lambda i: (i, 0)),
                   full((DEC_BATCH, DEC_SEQ, D))],
        out_shape=[jax.ShapeDtypeStruct((M_ROWS, D), BF16),
                   jax.ShapeDtypeStruct((DEC_BATCH, DEC_SEQ, D), F32)],
        scratch_shapes=[pltpu.VMEM((MLP_GROUPS, CHUNK, CHUNK), BF16),
                        pltpu.VMEM((NS_ROWS, D), F32)],
        compiler_params=_params("arbitrary"),
        name="gmlp",
    )(proj, proj, lnw, lnb, ws, bsx, wsx, bsx4)


def _cast_rows(src_ref, dst_ref, chunk=256):
    def body(r, carry):
        rows = pl.ds(pl.multiple_of(r * chunk, chunk), chunk)
        dst_ref[rows, :] = src_ref[rows, :].astype(dst_ref.dtype)
        return carry
    lax.fori_loop(0, src_ref.shape[0] // chunk, body, 0)


def _per_tile(i, prompt_fn, sample_fn):
    pl.when(i < NP_MT)(prompt_fn)
    pl.when(i == NP_MT)(sample_fn)


def _merge_kernel(ysp_ref, yss_ref, ym_ref, ga_ref, gb_ref, w1_ref, w2_ref, o_ref, w1b, w2b):
    i = pl.program_id(1)

    @pl.when(i == 0)
    def _():
        _cast_rows(w1_ref, w1b)
        _cast_rows(w2_ref, w2b)

    def emit(ys, rows):
        a1 = _dot(ys, w1b[...])
        a2 = _dot(ym_ref[rows, :], w2b[...])
        o_ref[rows, :] = (jax.nn.sigmoid(ga_ref[rows, :].astype(F32)) * a1
                          + jax.nn.sigmoid(gb_ref[rows, :].astype(F32)) * a2).astype(BF16)

    _per_tile(i,
              lambda: emit(ysp_ref[...], slice(None)),
              lambda: emit(yss_ref[...], slice(0, NS_ROWS)))


def _merge_call(ysp, yss, ym, proj, w1, w2):
    tn = 512
    nb = D // tn
    return pl.pallas_call(
        _merge_kernel,
        grid=(nb, N_MT),
        in_specs=[pl.BlockSpec((TMM, D), lambda j, i: (jnp.minimum(i, NP_MT - 1), 0)),
                  pl.BlockSpec((NS_ROWS, D), lambda j, i: (0, 0)),
                  pl.BlockSpec((TMM, D), lambda j, i: (i, 0)),
                  pl.BlockSpec((TMM, tn), lambda j, i: (i, PGA * nb + j)),
                  pl.BlockSpec((TMM, tn), lambda j, i: (i, PGB * nb + j)),
                  pl.BlockSpec((D, tn), lambda j, i: (0, j)),
                  pl.BlockSpec((D, tn), lambda j, i: (0, j))],
        out_specs=pl.BlockSpec((TMM, tn), lambda j, i: (i, j)),
        out_shape=jax.ShapeDtypeStruct((M_ROWS, D), BF16),
        scratch_shapes=[pltpu.VMEM((D, tn), BF16), pltpu.VMEM((D, tn), BF16)],
        compiler_params=_params("arbitrary", "arbitrary"),
        name="branch_merge",
    )(ysp, yss, ym, proj, proj, w1, w2)


def _resid_kernel(a_ref, w_ref, rp_ref, rs_ref, gp_ref, gs_ref, o_ref, wb):
    i = pl.program_id(1)

    @pl.when(i == 0)
    def _():
        _cast_rows(w_ref, wb)

    def prompt():
        o_ref[...] = rp_ref[...] + gp_ref[0] * _dot(a_ref[...], wb[...])

    def sample():
        acc = _dot(a_ref[0:NS_ROWS, :], wb[...])
        tn = acc.shape[-1]
        acc3 = acc.reshape(DEC_SEQ, DEC_BATCH, tn) * gs_ref[...][None]
        o_ref[0:NS_ROWS, :] = rs_ref[...] + acc3.reshape(NS_ROWS, tn)

    _per_tile(i, prompt, sample)


def _resid_call(a, w, rp, rs, rs_block, mod_p, mod_s, k_gate, tn, name, single_buffer_w=False):
    kdim = a.shape[1]
    nb = D // tn
    w_mode = dict(pipeline_mode=pl.Buffered(1)) if single_buffer_w else {}
    return pl.pallas_call(
        _resid_kernel,
        grid=(nb, N_MT),
        in_specs=[pl.BlockSpec((TMM, kdim), lambda j, i: (i, 0)),
                  pl.BlockSpec((kdim, tn), lambda j, i: (0, j), **w_mode),
                  pl.BlockSpec((TMM, tn), lambda j, i: (jnp.minimum(i, NP_MT - 1), j)),
                  pl.BlockSpec((NS_ROWS, tn), lambda j, i: (rs_block, j)),
                  pl.BlockSpec((1, 1, tn), lambda j, i: (_seq_of_mtile(i), 0, k_gate * nb + j)),
                  pl.BlockSpec((DEC_BATCH, tn), lambda j, i: (0, k_gate * nb + j))],
        out_specs=pl.BlockSpec((TMM, tn), lambda j, i: (i, j)),
        out_shape=jax.ShapeDtypeStruct((M_ROWS, D), F32),
        scratch_shapes=[pltpu.VMEM((kdim, tn), BF16)],
        compiler_params=_params("arbitrary", "arbitrary"),
        name=name,
    )(a, w, rp, rs, mod_p, mod_s)


TN_FF = 512
N_FF_BLOCKS = D_FF // TN_FF


def _up_kernel(a_ref, wa_ref, wv_ref, cw_ref, cb_ref, fst_ref,
               h_ref, fcp_ref, fcs_ref, wab, wvb, acc_a, acc_v):
    i = pl.program_id(1)
    rc = 64

    @pl.when(i == 0)
    def _():
        _cast_rows(wa_ref, wab)
        _cast_rows(wv_ref, wvb)

    def prompt():
        starts_sequence = i % MT_PER_SEQ == 0
        acc_a[0:8, :] = jnp.where(starts_sequence, 0.0, acc_a[TMM:TMM + 8, :])
        x = a_ref[...]
        acc_a[8:8 + TMM, :] = _dot(x, wab[...])
        acc_v[...] = _dot(x, wvb[...])
        cw, cb = cw_ref[...], cb_ref[...]
        for r in range(0, TMM, rc):
            conv = (cb + cw[2:3] * acc_a[8 + r:8 + r + rc, :]
                    + cw[1:2] * acc_a[7 + r:7 + r + rc, :]
                    + cw[0:1] * acc_a[6 + r:6 + r + rc, :])
            h_ref[r:r + rc, :] = (jax.nn.gelu(conv) * acc_v[r:r + rc, :]).astype(BF16)
        fcp_ref[0] = acc_a[TMM + 6:TMM + 8, :]

    def sample():
        B = DEC_BATCH
        x = a_ref[0:NS_ROWS, :]
        acc_a[8:8 + NS_ROWS, :] = _dot(x, wab[...])
        acc_v[0:NS_ROWS, :] = _dot(x, wvb[...])
        cw, cb = cw_ref[...], cb_ref[...]

        def pre(t, r):
            if t < 0:
                return fst_ref[r:r + rc, t + 2, :]
            return acc_a[8 + t * B + r:8 + t * B + r + rc, :]

        for t in range(DEC_SEQ):
            for r in range(0, B, rc):
                conv = cb + cw[2:3] * pre(t, r) + cw[1:2] * pre(t - 1, r) + cw[0:1] * pre(t - 2, r)
                h_ref[t * B + r:t * B + r + rc, :] = (
                    jax.nn.gelu(conv) * acc_v[t * B + r:t * B + r + rc, :]).astype(BF16)
        fcs_ref[:, 0, :] = acc_a[8 + 2 * B:8 + 3 * B, :]
        fcs_ref[:, 1, :] = acc_a[8 + 3 * B:8 + 4 * B, :]

    _per_tile(i, prompt, sample)


def _up_call(n2, w_up, cw, cb, fst):
    return pl.pallas_call(
        _up_kernel,
        grid=(N_FF_BLOCKS, N_MT),
        in_specs=[pl.BlockSpec((TMM, D), lambda j, i: (i, 0)),
                  pl.BlockSpec((D, TN_FF), lambda j, i: (0, j)),
                  pl.BlockSpec((D, TN_FF), lambda j, i: (0, N_FF_BLOCKS + j)),
                  pl.BlockSpec((3, TN_FF), lambda j, i: (0, j)),
                  pl.BlockSpec((1, TN_FF), lambda j, i: (0, j)),
                  pl.BlockSpec((DEC_BATCH, 2, TN_FF), lambda j, i: (0, 0, j))],
        out_specs=[pl.BlockSpec((TMM, TN_FF), lambda j, i: (i, j)),
                   pl.BlockSpec((1, 2, TN_FF), lambda j, i: (_seq_of_mtile(i), 0, j)),
                   pl.BlockSpec((DEC_BATCH, 2, TN_FF), lambda j, i: (0, 0, j))],
        out_shape=[jax.ShapeDtypeStruct((M_ROWS, D_FF), BF16),
                   jax.ShapeDtypeStruct((BATCH, 2, D_FF), F32),
                   jax.ShapeDtypeStruct((DEC_BATCH, 2, D_FF), F32)],
        scratch_shapes=[pltpu.VMEM((D, TN_FF), BF16), pltpu.VMEM((D, TN_FF), BF16),
                        pltpu.VMEM((TMM + 8, TN_FF), F32), pltpu.VMEM((TMM, TN_FF), F32)],
        compiler_params=_params("arbitrary", "arbitrary"),
        name="ffn_up",
    )(n2, w_up, w_up, cw, cb, fst)


def _final_kernel(x_ref, w_ref, yp_ref, ys_ref):
    i = pl.program_id(0)

    @pl.when(i < NP_MT)
    def _():
        for r in range(0, TMM, TM):
            yp_ref[r:r + TM, :] = _rms(x_ref[r:r + TM, :]) * w_ref[...]

    @pl.when(i == NP_MT)
    def _():
        y = _rms(x_ref[0:NS_ROWS, :]) * w_ref[...]
        for t in range(DEC_SEQ):
            ys_ref[:, t, :] = y[t * DEC_BATCH:(t + 1) * DEC_BATCH]


def _final_call(x3, w):
    return pl.pallas_call(
        _final_kernel,
        grid=(N_MT,),
        in_specs=[pl.BlockSpec((TMM, D), lambda i: (i, 0)),
                  pl.BlockSpec((1, D), lambda i: (0, 0))],
        out_specs=[pl.BlockSpec((TMM, D), lambda i: (jnp.minimum(i, NP_MT - 1), 0)),
                   pl.BlockSpec((DEC_BATCH, DEC_SEQ, D), lambda i: (0, 0, 0))],
        out_shape=[jax.ShapeDtypeStruct((NP_ROWS, D), F32),
                   jax.ShapeDtypeStruct((DEC_BATCH, DEC_SEQ, D), F32)],
        compiler_params=_params("arbitrary"),
        name="final_norm",
    )(x3, w)


def _to_time_major(a):
    return jnp.transpose(a, (1, 0, 2))


def kernel(x_prompt, x_sample, state_ssm, state_ssd_conv, state_ffn_conv, c_prompt, c_sample,
           norm1_w, w_ada, b_ada, w_in, ssd_conv_w, ssd_conv_b, dt_bias, a_log, d_skip,
           ssd_norm_w, mlp_ln_w, mlp_ln_b, w_spatial, b_spatial, w_ssd_o, w_mlp_o, w_out,
           norm2_w, w_up, ffn_conv_w, ffn_conv_b, w_down, final_norm_w):
    assert w_in.shape[0] == 1, "single-layer trunk"
    row = lambda v: v.reshape(1, -1)

    xp = x_prompt.reshape(NP_ROWS, D)
    xs = _to_time_major(x_sample).reshape(NS_ROWS, D)

    mod_p, mod_s = _ada_call(c_prompt, c_sample, w_ada[0], row(b_ada[0]))
    mod_p = mod_p.reshape(BATCH, 1, 6 * D)
    K_SHIFT1, K_SCALE1, K_GATE1, K_SHIFT2, K_SCALE2, K_GATE2 = range(6)

    xp_spec = pl.BlockSpec((TMM, D), lambda i: (jnp.minimum(i, NP_MT - 1), 0))
    xs_spec = pl.BlockSpec((NS_ROWS, D), lambda i: (0, 0))
    w_in_t = w_in[0].T
    n1, dt_raw = _norm_call(xp, xp_spec, xs, xs_spec, row(norm1_w[0]), mod_p, mod_s,
                            K_SCALE1, K_SHIFT1, w_in_t)
    cst_t = _to_time_major(state_ssd_conv[0])
    proj, cst_p, ncs_t = _inproj_call(n1, w_in_t, ssd_conv_w[0], row(ssd_conv_b[0]), cst_t)

    hp = jnp.zeros((8, LANE), F32)
    hp = hp.at[0, :HEADS].set(dt_bias[0]).at[1, :HEADS].set(a_log[0]).at[2, :HEADS].set(d_skip[0])
    hpc = hp.T
    triu = jnp.triu(jnp.ones((CHUNK, CHUNK), F32)).astype(BF16)
    dskx = row(jnp.repeat(d_skip[0], HEADDIM))
    nw = row(ssd_norm_w[0])

    head_of_col = jnp.arange(D) // HEADDIM
    ex = (jnp.arange(LANE)[:, None] == head_of_col[None, :]).astype(BF16)
    grp_of_head = jnp.arange(LANE) // (HEADS // GROUPS)
    seg = ((jnp.arange(GROUPS * STATE)[:, None] // STATE == grp_of_head[None, :])
           & (jnp.arange(LANE)[None, :] < HEADS)).astype(BF16)
    ypre, efull, xw_t, bb_t, cc_t, e3 = _ssds_prep_call(proj, dt_raw, hp, dskx, ex, seg)
    gw = (HEADS // GROUPS) * HEADDIM
    y_ssd_p, ssm_p, yo_t, ssm_s = _ssd_fused_call(
        proj, dt_raw, hp, hpc, dskx, nw, triu,
        e3[:, :HEADS], state_ssm[0].reshape(DEC_BATCH, GROUPS, gw, STATE), cc_t, bb_t, xw_t)
    y_ssd_s = _ssds_post_call(ypre, efull, yo_t.reshape(NS_ROWS, D), proj, nw)

    per_col = lambda a: jnp.repeat(a, MLP_GROUP_DIM, axis=1)
    bsx = per_col(b_spatial[0][:, :CHUNK].T)
    ws4 = w_spatial[0][:, :DEC_SEQ, :DEC_SEQ]
    wsx = per_col(jnp.transpose(ws4, (1, 2, 0)).reshape(DEC_SEQ * DEC_SEQ, MLP_GROUPS))
    bsx4 = jnp.pad(bsx[:DEC_SEQ], ((0, 8 - DEC_SEQ), (0, 0)))
    y_mlp, cv_s = _mlp_call(proj, row(mlp_ln_w[0]), row(mlp_ln_b[0]), w_spatial[0], bsx, wsx, bsx4)

    mixed = _merge_call(y_ssd_p, y_ssd_s, y_mlp, proj, w_ssd_o[0], w_mlp_o[0])
    x2 = _resid_call(mixed, w_out[0], xp, xs, 0, mod_p, mod_s, K_GATE1, 1024, "out_proj")

    x2p_spec = pl.BlockSpec((TMM, D), lambda i: (jnp.minimum(i, NP_MT - 1), 0))
    x2s_spec = pl.BlockSpec((NS_ROWS, D), lambda i: (NP_TILES, 0))
    (n2,) = _norm_call(x2, x2p_spec, x2, x2s_spec, row(norm2_w[0]), mod_p, mod_s,
                       K_SCALE2, K_SHIFT2, None)
    h, ffn_p, ffn_s = _up_call(n2, w_up[0], ffn_conv_w[0], row(ffn_conv_b[0]), state_ffn_conv[0])
    x3 = _resid_call(h, w_down[0], x2, x2, NP_TILES, mod_p, mod_s, K_GATE2, 512, "ffn_down",
                     single_buffer_w=True)

    y_p, y_s = _final_call(x3, row(final_norm_w))

    from_t = lambda a, t, c: jnp.transpose(a.reshape(t, DEC_BATCH, c), (1, 0, 2))
    return (y_p.reshape(BATCH, SEQ, D),
            y_s,
            ssm_p.reshape(1, BATCH, HEADS, HEADDIM, STATE),
            ssm_s.reshape(1, DEC_BATCH, HEADS, HEADDIM, STATE),
            cst_p[None],
            from_t(ncs_t, 3, CONV_DIM)[None],
            ffn_p[None],
            ffn_s[None],
            cv_s[None])
```

```python
import functools

import jax
import jax.numpy as jnp
from jax import lax
from jax.experimental import pallas as pl
from jax.experimental.pallas import tpu as pltpu

F32 = jnp.float32
BF16 = jnp.bfloat16

D = 2048
BATCH, SEQ = 4, 2048
DEC_BATCH, DEC_SEQ = 128, 4
NP_ROWS = BATCH * SEQ
NS_ROWS = DEC_BATCH * DEC_SEQ
M_ROWS = NP_ROWS + NS_ROWS
TM = 512
NP_TILES = NP_ROWS // TM
TMM = 1024
NP_MT = NP_ROWS // TMM
N_MT = NP_MT + 1
MT_PER_SEQ = SEQ // TMM
HEADS, HEADDIM, GROUPS, STATE = 32, 64, 4, 128
CHUNK = 128
N_CHUNKS = SEQ // CHUNK
CONV_DIM = D + 2 * GROUPS * STATE
MLP_GROUPS = 8
MLP_GROUP_DIM = D // MLP_GROUPS
D_FF = 5632
EPS = 1e-6
DT_COL = D + CONV_DIM
TN_IN = 1024
LANE = 128
VMEM_LIMIT = 56 * 1024 * 1024


def _params(*sem, flags=None):
    return pltpu.CompilerParams(dimension_semantics=sem, vmem_limit_bytes=VMEM_LIMIT, flags=flags)


def _dot(a, b):
    return jnp.dot(a, b, preferred_element_type=F32)


def _dot_nt(a, b_t):
    return lax.dot_general(a, b_t, (((1,), (1,)), ((), ())), preferred_element_type=F32)


def _split_bf16(v, terms):
    out = []
    r = v
    for _ in range(terms):
        p = r.astype(BF16)
        out.append(p)
        r = r - p.astype(F32)
    return out


def _dot_exact_rhs(v, e, terms=3):
    acc = None
    for p in _split_bf16(v, terms):
        d = _dot(p, e)
        acc = d if acc is None else acc + d
    return acc


def _silu(x):
    return x * jax.nn.sigmoid(x)


def _softplus(x):
    return jnp.maximum(x, 0.0) + jnp.log1p(jnp.exp(-jnp.abs(x)))


def _rms(x):
    return x * lax.rsqrt(jnp.mean(x * x, axis=-1, keepdims=True) + EPS)


def _ada_kernel(cp_ref, cs_ref, w_ref, b_ref, op_ref, os_ref):
    w = w_ref[...].astype(BF16)
    cp8 = jnp.concatenate([cp_ref[...], jnp.zeros((8 - BATCH, D), F32)], axis=0)
    op_ref[...] = (_dot(_silu(cp8).astype(BF16), w) + b_ref[...])[0:BATCH]
    os_ref[...] = _dot(_silu(cs_ref[...]).astype(BF16), w) + b_ref[...]


def _ada_call(c_prompt, c_sample, w, b):
    tn = 1024
    return pl.pallas_call(
        _ada_kernel,
        grid=(6 * D // tn,),
        in_specs=[pl.BlockSpec((BATCH, D), lambda j: (0, 0)),
                  pl.BlockSpec((DEC_BATCH, D), lambda j: (0, 0)),
                  pl.BlockSpec((D, tn), lambda j: (0, j)),
                  pl.BlockSpec((1, tn), lambda j: (0, j))],
        out_specs=[pl.BlockSpec((BATCH, tn), lambda j: (0, j)),
                   pl.BlockSpec((DEC_BATCH, tn), lambda j: (0, j))],
        out_shape=[jax.ShapeDtypeStruct((BATCH, 6 * D), F32),
                   jax.ShapeDtypeStruct((DEC_BATCH, 6 * D), F32)],
        compiler_params=_params("arbitrary"),
        name="ada_mod",
    )(c_prompt, c_sample, w, b)


def _norm_kernel(with_dt, xp_ref, xs_ref, nw_ref, scp_ref, shp_ref, scs_ref, shs_ref, *rest):
    if with_dt:
        wdt_ref, n_ref, dt_ref = rest
    else:
        (n_ref,) = rest
    i = pl.program_id(0)

    def emit(n, rows):
        nb = n.astype(BF16)
        n_ref[rows, :] = nb
        if with_dt:
            dt_ref[rows, :] = lax.dot_general(nb, wdt_ref[...].astype(BF16),
                                              (((1,), (1,)), ((), ())),
                                              preferred_element_type=F32)

    @pl.when(i < NP_MT)
    def _():
        for r in range(0, TMM, TM):
            rows = slice(r, r + TM)
            y = _rms(xp_ref[rows, :]) * nw_ref[...]
            emit(y * (1.0 + scp_ref[0]) + shp_ref[0], rows)

    @pl.when(i == NP_MT)
    def _():
        y = _rms(xs_ref[...]) * nw_ref[...]
        y3 = y.reshape(DEC_SEQ, DEC_BATCH, D)
        emit((y3 * (1.0 + scs_ref[...])[None] + shs_ref[...][None]).reshape(NS_ROWS, D),
             slice(0, NS_ROWS))


def _seq_of_mtile(i):
    return jnp.minimum(i // MT_PER_SEQ, BATCH - 1)


def _norm_call(xp, xp_spec, xs, xs_spec, nw, mod_p, mod_s, k_scale, k_shift, w_in_t):
    with_dt = w_in_t is not None
    in_specs = [
        xp_spec, xs_spec,
        pl.BlockSpec((1, D), lambda i: (0, 0)),
        pl.BlockSpec((1, 1, D), lambda i: (_seq_of_mtile(i), 0, k_scale)),
        pl.BlockSpec((1, 1, D), lambda i: (_seq_of_mtile(i), 0, k_shift)),
        pl.BlockSpec((DEC_BATCH, D), lambda i: (0, k_scale)),
        pl.BlockSpec((DEC_BATCH, D), lambda i: (0, k_shift)),
    ]
    args = [xp, xs, nw, mod_p, mod_p, mod_s, mod_s]
    out_specs = [pl.BlockSpec((TMM, D), lambda i: (i, 0))]
    out_shape = [jax.ShapeDtypeStruct((M_ROWS, D), BF16)]
    if with_dt:
        in_specs.append(pl.BlockSpec((pl.Element(LANE), pl.Element(D)), lambda i: (DT_COL, 0)))
        args.append(w_in_t)
        out_specs.append(pl.BlockSpec((TMM, LANE), lambda i: (i, 0)))
        out_shape.append(jax.ShapeDtypeStruct((M_ROWS, LANE), F32))
    return pl.pallas_call(
        functools.partial(_norm_kernel, with_dt),
        grid=(N_MT,),
        in_specs=in_specs,
        out_specs=out_specs,
        out_shape=out_shape,
        compiler_params=_params("arbitrary"),
        name="norm_mod_dt" if with_dt else "norm_mod",
    )(*args)


N_IN_BLOCKS = 13
UVG_ROW = DT_COL + HEADS
FIRST_CONV_BLOCK = 10


def _in_src_row(j):
    row = jnp.where(j < 2, j * TN_IN,
                    jnp.where(j < FIRST_CONV_BLOCK, UVG_ROW + (j - 2) * TN_IN,
                              D + (j - FIRST_CONV_BLOCK) * TN_IN))
    return pl.multiple_of(row, HEADS)


def _in_conv_block(j):
    return jnp.maximum(j - FIRST_CONV_BLOCK, 0)


def _inproj_kernel(a_ref, wt_ref, cw_ref, cb_ref, cst_ref, o_ref, csp_ref, css_ref, wbf_ref, acc_ref):
    j = pl.program_id(0)
    i = pl.program_id(1)
    rc = 64
    B = DEC_BATCH

    @pl.when(i == 0)
    def _():
        for r in range(TN_IN // LANE):
            rows = slice(r * LANE, (r + 1) * LANE)
            wbf_ref[rows, :] = wt_ref[rows, :].astype(BF16)

    def elementwise(fn, rows):
        def body():
            acc_ref[8:8 + rows, :] = _dot_nt(a_ref[0:rows, :], wbf_ref[...])
            for r in range(0, rows, rc):
                val = acc_ref[8 + r:8 + r + rc, :]
                o_ref[r:r + rc, :] = (val if fn is None else fn(val)).astype(o_ref.dtype)
        return body

    def conv_prompt():
        starts_sequence = i % MT_PER_SEQ == 0
        acc_ref[0:8, :] = jnp.where(starts_sequence, 0.0, acc_ref[TMM:TMM + 8, :])
        acc_ref[8:8 + TMM, :] = _dot_nt(a_ref[...], wbf_ref[...])
        cw, cb = cw_ref[...], cb_ref[...]
        for r in range(0, TMM, rc):
            conv = cb + cw[3:4] * acc_ref[8 + r:8 + r + rc, :]
            for k in range(3):
                conv = conv + cw[k:k + 1] * acc_ref[5 + k + r:5 + k + r + rc, :]
            o_ref[r:r + rc, :] = _silu(conv).astype(o_ref.dtype)
        csp_ref[0] = acc_ref[TMM + 5:TMM + 8, :]

    def conv_sample():
        acc_ref[8:8 + NS_ROWS, :] = _dot_nt(a_ref[0:NS_ROWS, :], wbf_ref[...])
        cw, cb = cw_ref[...], cb_ref[...]

        def pre(t, r):
            if t < 0:
                return cst_ref[t + 3, r:r + rc, :]
            return acc_ref[8 + t * B + r:8 + t * B + r + rc, :]

        for t in range(DEC_SEQ):
            for r in range(0, B, rc):
                conv = cb + cw[3:4] * pre(t, r)
                for k in range(3):
                    conv = conv + cw[k:k + 1] * pre(t - 3 + k, r)
                o_ref[t * B + r:t * B + r + rc, :] = _silu(conv).astype(o_ref.dtype)
        for t in range(1, DEC_SEQ):
            css_ref[t - 1] = acc_ref[8 + t * B:8 + (t + 1) * B, :]

    is_conv = j >= FIRST_CONV_BLOCK
    kinds = ((j < 2, _silu),
             (jnp.logical_and(j >= 2, j < 6), jax.nn.gelu),
             (jnp.logical_and(j >= 6, j < FIRST_CONV_BLOCK), None))
    for cond, fn in kinds:
        pl.when(jnp.logical_and(cond, i < NP_MT))(elementwise(fn, TMM))
        pl.when(jnp.logical_and(cond, i == NP_MT))(elementwise(fn, NS_ROWS))
    pl.when(jnp.logical_and(is_conv, i < NP_MT))(conv_prompt)
    pl.when(jnp.logical_and(is_conv, i == NP_MT))(conv_sample)


def _inproj_call(n1, w_in_t, cw, cb, cst_t):
    cblk = _in_conv_block
    seq = lambda j, i: jnp.where(j < FIRST_CONV_BLOCK, 0, _seq_of_mtile(i))
    return pl.pallas_call(
        _inproj_kernel,
        grid=(N_IN_BLOCKS, N_MT),
        in_specs=[pl.BlockSpec((TMM, D), lambda j, i: (i, 0)),
                  pl.BlockSpec((pl.Element(TN_IN), pl.Element(D)),
                               lambda j, i: (_in_src_row(j), 0)),
                  pl.BlockSpec((4, TN_IN), lambda j, i: (0, cblk(j))),
                  pl.BlockSpec((1, TN_IN), lambda j, i: (0, cblk(j))),
                  pl.BlockSpec((3, DEC_BATCH, TN_IN), lambda j, i: (0, 0, cblk(j)))],
        out_specs=[pl.BlockSpec((TMM, TN_IN), lambda j, i: (i, j)),
                   pl.BlockSpec((1, 3, TN_IN), lambda j, i: (seq(j, i), 0, cblk(j))),
                   pl.BlockSpec((3, DEC_BATCH, TN_IN), lambda j, i: (0, 0, cblk(j)))],
        out_shape=[jax.ShapeDtypeStruct((M_ROWS, N_IN_BLOCKS * TN_IN), BF16),
                   jax.ShapeDtypeStruct((BATCH, 3, CONV_DIM), F32),
                   jax.ShapeDtypeStruct((3, DEC_BATCH, CONV_DIM), F32)],
        scratch_shapes=[pltpu.VMEM((TN_IN, D), BF16), pltpu.VMEM((TMM + 8, TN_IN), F32)],
        compiler_params=_params("arbitrary", "arbitrary"),
        name="in_proj",
    )(n1, w_in_t, cw, cb, cst_t)


PZ, PU, PV, PGA, PGB, PX = 0, 1, 2, 3, 4, 5
PBC_1024 = 12


def _gated_group_norm(get_y, zact_ref, nw_ref, o_ref, rows=slice(None)):
    gw = D // GROUPS
    for g in range(GROUPS):
        cols = slice(g * gw, (g + 1) * gw)
        gg = _rms(get_y(cols) * zact_ref[rows, cols].astype(F32))
        o_ref[rows, cols] = (gg * nw_ref[:, cols]).astype(o_ref.dtype)


CHUNKS_PER_STEP = 4


def _ssd_prompt_kernel(z_ref, x_ref, bc_ref, dt_ref, hp_ref, hpc_ref, dskx_ref, nw_ref, triu_ref,
                       y_ref, ssm_ref, st_ref, yscr_ref):
    c = pl.program_id(1)
    T = CHUNK
    cdim = GROUPS * STATE

    @pl.when(c == 0)
    def _():
        st_ref[...] = jnp.zeros(st_ref.shape, F32)

    hp = hp_ref[...]
    a_col = -jnp.exp(hpc_ref[0:HEADS, 1:2])
    triu = triu_ref[...]
    row = lax.broadcasted_iota(jnp.int32, (T, T), 0)
    lane = lax.broadcasted_iota(jnp.int32, (T, T), 1)
    causal = row >= lane
    left = lane < HEADDIM
    mask_l = jnp.where(left, 1.0, 0.0).astype(BF16)
    nt = (((1,), (1,)), ((), ()))
    pairs_per_group = HEADS // GROUPS // 2

    for sub in range(CHUNKS_PER_STEP):
        rows = slice(sub * T, (sub + 1) * T)
        y_buf = yscr_ref.at[sub % 2]
        dt_t = _softplus((dt_ref[rows, :] + hp[0:1, :]).T[0:HEADS, :])
        cs_t = _dot_exact_rhs(dt_t * a_col, triu)
        rsub_t = cs_t - jnp.log(dt_t)
        cs = jnp.concatenate([cs_t, jnp.zeros((LANE - HEADS, T), F32)], axis=0).T

        for g in range(GROUPS):
            c_b = bc_ref[rows, cdim + g * STATE:cdim + (g + 1) * STATE]
            b_b = bc_ref[rows, g * STATE:(g + 1) * STATE]
            cb = lax.dot_general(c_b, b_b, nt, preferred_element_type=F32)
            b_t = b_b.astype(F32).T
            for k4 in range(pairs_per_group):
                k = g * pairs_per_group + k4
                cols = slice(k * LANE, (k + 1) * LANE)
                xpb = x_ref[rows, cols]
                x_lo = xpb * mask_l
                xbd = jnp.concatenate([x_lo, xpb - x_lo], axis=0)
                st = st_ref[k]
                yraw = _dot(c_b, st.astype(BF16))
                scores, bws, colbs, alasts = [], [], [], []
                for h in (2 * k, 2 * k + 1):
                    colb = jnp.broadcast_to(cs[:, h:h + 1], (T, T))
                    alast = cs_t[h:h + 1, T - 1:T]
                    decay_dt = jnp.exp(jnp.where(causal, colb - rsub_t[h:h + 1, :], -jnp.inf))
                    scores.append((cb * decay_dt).astype(BF16))
                    wrow = jnp.exp(alast - cs_t[h:h + 1, :]) * dt_t[h:h + 1, :]
                    bws.append((b_t * wrow).astype(BF16))
                    colbs.append(colb)
                    alasts.append(alast)
                ecol = jnp.exp(jnp.where(left, colbs[0], colbs[1]))
                elast = jnp.exp(jnp.where(left[0:1], alasts[0], alasts[1]))
                y_buf[:, cols] = (_dot(jnp.concatenate(scores, axis=1), xbd) + ecol * yraw
                                  + dskx_ref[:, cols] * xpb.astype(F32))
                st_ref[k] = elast * st + _dot(jnp.concatenate(bws, axis=1), xbd)
        _gated_group_norm(lambda cols: y_buf[:, cols], z_ref, nw_ref, y_ref, rows)

    @pl.when(c == N_CHUNKS // CHUNKS_PER_STEP - 1)
    def _():
        for k in range(HEADS // 2):
            ssm_ref[0, k] = st_ref[k].T


def _ssds_prep_step(tt, x_ref, bc_ref, dt_ref, hp_ref, dskx_ref, ex_ref, seg_ref,
                    ypre_ref, efull_ref, xw_ref, bb_ref, cc_ref, e3_ref):
    B = DEC_BATCH
    cdim = GROUPS * STATE
    blk = lambda t: slice(t * B, (t + 1) * B)
    x_of = lambda t: x_ref[blk(t), :].astype(F32)
    b_of = lambda t: bc_ref[blk(t), 0:cdim].astype(F32)
    c_of = lambda t: bc_ref[blk(t), cdim:2 * cdim].astype(F32)

    bb_ref[...] = b_of(tt)
    cc_ref[...] = c_of(tt)

    hp = hp_ref[...]
    a_neg = -jnp.exp(hp[1:2, :])
    dts, css = [], []
    run = None
    for t in range(DEC_SEQ):
        dt = _softplus(dt_ref[blk(t), :] + hp[0:1, :])
        run = dt * a_neg if run is None else run + dt * a_neg
        dts.append(dt)
        css.append(run)

    ex = ex_ref[...]
    e3_ref[...] = jnp.exp(css[-1])
    efull_ref[...] = _dot_exact_rhs(jnp.exp(css[tt]), ex)
    w_t = jnp.exp(css[-1] - css[tt]) * dts[tt]
    xw_ref[...] = x_of(tt) * _dot_exact_rhs(w_t, ex)

    seg = seg_ref[...]
    acc = dskx_ref[...] * x_of(tt)
    c_t = c_of(tt)
    for s in range(tt + 1):
        cbh = _dot_exact_rhs(c_t * b_of(s), seg)
        g_ts = cbh * jnp.exp(css[tt] - css[s]) * dts[s]
        acc = acc + _dot_exact_rhs(g_ts, ex) * x_of(s)
    ypre_ref[...] = acc


def _ssds_prep_kernel(*refs):
    t = pl.program_id(0)
    for tt in range(DEC_SEQ):
        pl.when(t == tt)(functools.partial(_ssds_prep_step, tt, *refs))


def _ssds_prep_call(proj, dt_raw, hp, dskx, ex, seg):
    full = lambda shape: pl.BlockSpec(shape, lambda t: (0,) * len(shape))
    step = lambda width: pl.BlockSpec((DEC_BATCH, width), lambda t: (t, 0))
    cdim = GROUPS * STATE
    return pl.pallas_call(
        _ssds_prep_kernel,
        grid=(DEC_SEQ,),
        in_specs=[pl.BlockSpec((NS_ROWS, D), lambda t: (NP_TILES, PX)),
                  pl.BlockSpec((NS_ROWS, 2 * cdim), lambda t: (NP_TILES, PBC_1024)),
                  pl.BlockSpec((NS_ROWS, LANE), lambda t: (NP_TILES, 0)),
                  full((8, LANE)), full((1, D)), full((LANE, D)), full((cdim, LANE))],
        out_specs=[step(D), step(D), step(D), step(cdim), step(cdim),
                   full((DEC_BATCH, LANE))],
        out_shape=[jax.ShapeDtypeStruct((NS_ROWS, D), F32),
                   jax.ShapeDtypeStruct((NS_ROWS, D), F32),
                   jax.ShapeDtypeStruct((NS_ROWS, D), F32),
                   jax.ShapeDtypeStruct((NS_ROWS, cdim), F32),
                   jax.ShapeDtypeStruct((NS_ROWS, cdim), F32),
                   jax.ShapeDtypeStruct((DEC_BATCH, LANE), F32)],
        compiler_params=_params("arbitrary"),
        name="ssd_sample_prep",
    )(proj, proj, dt_raw, hp, dskx, ex, seg)


SEQ_PER_STEP = 8


def _ssds_state_body(blk, e3_ref, st_ref, cc_ref, bb_ref, xw_ref, yo_ref, so_ref):
    nt = (((1,), (1,)), ((), ()))
    tn = (((0,), (0,)), ((), ()))
    hpg = HEADS // GROUPS
    gw = hpg * HEADDIM

    def rows_of(ref, s, cols):
        v = ref[:, s, cols]
        return jnp.concatenate([v, jnp.zeros((8 - DEC_SEQ, v.shape[-1]), F32)], axis=0).astype(BF16)

    for s in range(SEQ_PER_STEP):
        b = blk * SEQ_PER_STEP + s
        for g in range(GROUPS):
            h0 = st_ref[s, g]
            c_g = rows_of(cc_ref, s, slice(g * STATE, (g + 1) * STATE))
            yraw = lax.dot_general(c_g, h0.astype(BF16), nt, preferred_element_type=F32)
            yo_ref[:, s, g * gw:(g + 1) * gw] = yraw[0:DEC_SEQ]
            x_g = rows_of(xw_ref, s, slice(g * gw, (g + 1) * gw))
            b_g = rows_of(bb_ref, s, slice(g * STATE, (g + 1) * STATE))
            dh = lax.dot_general(x_g, b_g, tn, preferred_element_type=F32)
            for hh in range(hpg):
                rows = slice(hh * HEADDIM, (hh + 1) * HEADDIM)
                so_ref[s, g, rows, :] = e3_ref[b, g * hpg + hh] * h0[rows] + dh[rows]


PROMPT_STEPS = N_CHUNKS // CHUNKS_PER_STEP
assert BATCH * PROMPT_STEPS * SEQ_PER_STEP == DEC_BATCH


def _ssd_fused_kernel(z_ref, x_ref, bc_ref, dt_ref, hp_ref, hpc_ref, dskx_ref, nw_ref, triu_ref,
                      e3_ref, sti_ref, cc_ref, bb_ref, xw_ref,
                      y_ref, ssm_ref, yo_ref, so_ref, st_ref, yscr_ref):
    _ssd_prompt_kernel(z_ref, x_ref, bc_ref, dt_ref, hp_ref, hpc_ref, dskx_ref, nw_ref, triu_ref,
                       y_ref, ssm_ref, st_ref, yscr_ref)
    blk = pl.program_id(0) * PROMPT_STEPS + pl.program_id(1)
    _ssds_state_body(blk, e3_ref, sti_ref, cc_ref, bb_ref, xw_ref, yo_ref, so_ref)


def _ssd_fused_call(proj, dt_raw, hp, hpc, dskx, nw, triu, e3, state, cc_t, bb_t, xw_t):
    rows = CHUNK * CHUNKS_PER_STEP
    sb = SEQ_PER_STEP
    gw = (HEADS // GROUPS) * HEADDIM
    cdim = GROUPS * STATE
    step = lambda b, c: b * PROMPT_STEPS + c
    const = lambda b, c: (0, 0)
    tmajor = lambda width: pl.BlockSpec((DEC_SEQ, sb, width), lambda b, c: (0, step(b, c), 0))
    st_spec = pl.BlockSpec((sb, GROUPS, gw, STATE), lambda b, c: (step(b, c), 0, 0, 0))
    return pl.pallas_call(
        _ssd_fused_kernel,
        grid=(BATCH, PROMPT_STEPS),
        in_specs=[pl.BlockSpec((rows, D), lambda b, c: (step(b, c), PZ)),
                  pl.BlockSpec((rows, D), lambda b, c: (step(b, c), PX)),
                  pl.BlockSpec((rows, 2 * cdim), lambda b, c: (step(b, c), PBC_1024)),
                  pl.BlockSpec((rows, LANE), lambda b, c: (step(b, c), 0)),
                  pl.BlockSpec((8, LANE), const),
                  pl.BlockSpec((LANE, 8), const),
                  pl.BlockSpec((1, D), const),
                  pl.BlockSpec((1, D), const),
                  pl.BlockSpec((CHUNK, CHUNK), const),
                  pl.BlockSpec(memory_space=pltpu.SMEM),
                  st_spec, tmajor(cdim), tmajor(cdim), tmajor(D)],
        out_specs=[pl.BlockSpec((rows, D), lambda b, c: (step(b, c), 0)),
                   pl.BlockSpec((1, HEADS // 2, LANE, STATE), lambda b, c: (b, 0, 0, 0)),
                   tmajor(D), st_spec],
        out_shape=[jax.ShapeDtypeStruct((NP_ROWS, D), BF16),
                   jax.ShapeDtypeStruct((BATCH, HEADS // 2, LANE, STATE), F32),
                   jax.ShapeDtypeStruct((DEC_SEQ, DEC_BATCH, D), F32),
                   jax.ShapeDtypeStruct((DEC_BATCH, GROUPS, gw, STATE), F32)],
        scratch_shapes=[pltpu.VMEM((HEADS // 2, STATE, LANE), F32),
                        pltpu.VMEM((2, CHUNK, D), F32)],
        compiler_params=_params("arbitrary", "arbitrary"),
        name="ssd_prompt_and_sample_state",
    )(proj, proj, proj, dt_raw, hp, hpc, dskx, nw, triu,
      e3, state, cc_t.reshape(DEC_SEQ, DEC_BATCH, cdim), bb_t.reshape(DEC_SEQ, DEC_BATCH, cdim),
      xw_t.reshape(DEC_SEQ, DEC_BATCH, D))


def _ssds_post_kernel(ypre_ref, efull_ref, yo_ref, z_ref, nw_ref, o_ref):
    get_y = lambda cols: ypre_ref[:, cols] + efull_ref[:, cols] * yo_ref[:, cols]
    _gated_group_norm(get_y, z_ref, nw_ref, o_ref)


def _ssds_post_call(ypre, efull, yo_t, proj, nw):
    step = pl.BlockSpec((DEC_BATCH, D), lambda t: (t, 0))
    return pl.pallas_call(
        _ssds_post_kernel,
        grid=(DEC_SEQ,),
        in_specs=[step, step, step,
                  pl.BlockSpec((DEC_BATCH, D), lambda t: (NP_ROWS // DEC_BATCH + t, PZ)),
                  pl.BlockSpec((1, D), lambda t: (0, 0))],
        out_specs=step,
        out_shape=jax.ShapeDtypeStruct((NS_ROWS, D), BF16),
        compiler_params=_params("arbitrary"),
        name="ssd_sample_post",
    )(ypre, efull, yo_t, proj, nw)


def _mlp_kernel(u_ref, v_ref, lnw_ref, lnb_ref, ws_ref, bsx_ref, wsx_ref, bsx4_ref,
                y_ref, cv_ref, wm_ref, cvs_ref):
    i = pl.program_id(0)
    T = CHUNK

    @pl.when(i == 0)
    def _():
        row = lax.broadcasted_iota(jnp.int32, (T, T), 0)
        lane = lax.broadcasted_iota(jnp.int32, (T, T), 1)
        for g in range(MLP_GROUPS):
            wm_ref[g] = jnp.where(row >= lane, ws_ref[g], 0.0).astype(BF16)

    def vnorm(rows):
        vg = v_ref[rows, :].astype(F32)
        xc = vg - jnp.mean(vg, axis=-1, keepdims=True)
        y = xc * lax.rsqrt(jnp.mean(xc * xc, axis=-1, keepdims=True) + EPS)
        return y * lnw_ref[...] + lnb_ref[...]

    @pl.when(i < NP_MT)
    def _():
        for cc in range(TMM // T):
            rows = slice(cc * T, (cc + 1) * T)
            vnb = vnorm(rows).astype(BF16)
            for g in range(MLP_GROUPS):
                cols = slice(g * MLP_GROUP_DIM, (g + 1) * MLP_GROUP_DIM)
                sv = _dot(wm_ref[g], vnb[:, cols]) + bsx_ref[:, cols]
                y_ref[rows, cols] = (u_ref[rows, cols].astype(F32) * sv).astype(BF16)

    @pl.when(i == NP_MT)
    def _():
        B = DEC_BATCH
        for t in range(DEC_SEQ):
            rows = slice(t * B, (t + 1) * B)
            vn = vnorm(rows)
            cvs_ref[rows, :] = vn
            cv_ref[:, t, :] = vn
        for t in range(DEC_SEQ):
            rows = slice(t * B, (t + 1) * B)
            acc = bsx4_ref[t:t + 1, :]
            for s in range(t + 1):
                acc = acc + wsx_ref[4 * t + s:4 * t + s + 1, :] * cvs_ref[s * B:(s + 1) * B, :]
            y_ref[rows, :] = (u_ref[rows, :].astype(F32) * acc).astype(BF16)


def _mlp_call(proj, lnw, lnb, ws, bsx, wsx, bsx4):
    full = lambda shape: pl.BlockSpec(shape, lambda i: (0,) * len(shape))
    return pl.pallas_call(
        _mlp_kernel,
        grid=(N_MT,),
        in_specs=[pl.BlockSpec((TMM, D), lambda i: (i, PU)),
                  pl.BlockSpec((TMM, D), lambda i: (i, PV)),
                  full((1, D)), full((1, D)),
                  full((MLP_GROUPS, CHUNK, CHUNK)),
                  full((CHUNK, D)), full((16, D)), full((8, D))],
        out_specs=[pl.BlockSpec((TMM, D), lambda i: (i, 0)),
                   full((DEC_BATCH, DEC_SEQ, D))],
        out_shape=[jax.ShapeDtypeStruct((M_ROWS, D), BF16),
                   jax.ShapeDtypeStruct((DEC_BATCH, DEC_SEQ, D), F32)],
        scratch_shapes=[pltpu.VMEM((MLP_GROUPS, CHUNK, CHUNK), BF16),
                        pltpu.VMEM((NS_ROWS, D), F32)],
        compiler_params=_params("arbitrary"),
        name="gmlp",
    )(proj, proj, lnw, lnb, ws, bsx, wsx, bsx4)


def _cast_rows(src_ref, dst_ref, chunk=256):
    def body(r, carry):
        rows = pl.ds(pl.multiple_of(r * chunk, chunk), chunk)
        dst_ref[rows, :] = src_ref[rows, :].astype(dst_ref.dtype)
        return carry
    lax.fori_loop(0, src_ref.shape[0] // chunk, body, 0)


def _per_tile(i, prompt_fn, sample_fn):
    pl.when(i < NP_MT)(prompt_fn)
    pl.when(i == NP_MT)(sample_fn)


def _merge_kernel(ysp_ref, yss_ref, ym_ref, ga_ref, gb_ref, w1_ref, w2_ref, o_ref, w1b, w2b):
    i = pl.program_id(1)

    @pl.when(i == 0)
    def _():
        _cast_rows(w1_ref, w1b)
        _cast_rows(w2_ref, w2b)

    def emit(ys, rows):
        a1 = _dot(ys, w1b[...])
        a2 = _dot(ym_ref[rows, :], w2b[...])
        o_ref[rows, :] = (jax.nn.sigmoid(ga_ref[rows, :].astype(F32)) * a1
                          + jax.nn.sigmoid(gb_ref[rows, :].astype(F32)) * a2).astype(BF16)

    _per_tile(i,
              lambda: emit(ysp_ref[...], slice(None)),
              lambda: emit(yss_ref[...], slice(0, NS_ROWS)))


def _merge_call(ysp, yss, ym, proj, w1, w2):
    tn = 512
    nb = D // tn
    return pl.pallas_call(
        _merge_kernel,
        grid=(nb, N_MT),
        in_specs=[pl.BlockSpec((TMM, D), lambda j, i: (jnp.minimum(i, NP_MT - 1), 0)),
                  pl.BlockSpec((NS_ROWS, D), lambda j, i: (0, 0)),
                  pl.BlockSpec((TMM, D), lambda j, i: (i, 0)),
                  pl.BlockSpec((TMM, tn), lambda j, i: (i, PGA * nb + j)),
                  pl.BlockSpec((TMM, tn), lambda j, i: (i, PGB * nb + j)),
                  pl.BlockSpec((D, tn), lambda j, i: (0, j)),
                  pl.BlockSpec((D, tn), lambda j, i: (0, j))],
        out_specs=pl.BlockSpec((TMM, tn), lambda j, i: (i, j)),
        out_shape=jax.ShapeDtypeStruct((M_ROWS, D), BF16),
        scratch_shapes=[pltpu.VMEM((D, tn), BF16), pltpu.VMEM((D, tn), BF16)],
        compiler_params=_params("arbitrary", "arbitrary"),
        name="branch_merge",
    )(ysp, yss, ym, proj, proj, w1, w2)


def _resid_kernel(a_ref, w_ref, rp_ref, rs_ref, gp_ref, gs_ref, o_ref, wb):
    i = pl.program_id(1)

    @pl.when(i == 0)
    def _():
        _cast_rows(w_ref, wb)

    def prompt():
        o_ref[...] = rp_ref[...] + gp_ref[0] * _dot(a_ref[...], wb[...])

    def sample():
        acc = _dot(a_ref[0:NS_ROWS, :], wb[...])
        tn = acc.shape[-1]
        acc3 = acc.reshape(DEC_SEQ, DEC_BATCH, tn) * gs_ref[...][None]
        o_ref[0:NS_ROWS, :] = rs_ref[...] + acc3.reshape(NS_ROWS, tn)

    _per_tile(i, prompt, sample)


def _resid_call(a, w, rp, rs, rs_block, mod_p, mod_s, k_gate, tn, name, single_buffer_w=False):
    kdim = a.shape[1]
    nb = D // tn
    w_mode = dict(pipeline_mode=pl.Buffered(1)) if single_buffer_w else {}
    return pl.pallas_call(
        _resid_kernel,
        grid=(nb, N_MT),
        in_specs=[pl.BlockSpec((TMM, kdim), lambda j, i: (i, 0)),
                  pl.BlockSpec((kdim, tn), lambda j, i: (0, j), **w_mode),
                  pl.BlockSpec((TMM, tn), lambda j, i: (jnp.minimum(i, NP_MT - 1), j)),
                  pl.BlockSpec((NS_ROWS, tn), lambda j, i: (rs_block, j)),
                  pl.BlockSpec((1, 1, tn), lambda j, i: (_seq_of_mtile(i), 0, k_gate * nb + j)),
                  pl.BlockSpec((DEC_BATCH, tn), lambda j, i: (0, k_gate * nb + j))],
        out_specs=pl.BlockSpec((TMM, tn), lambda j, i: (i, j)),
        out_shape=jax.ShapeDtypeStruct((M_ROWS, D), F32),
        scratch_shapes=[pltpu.VMEM((kdim, tn), BF16)],
        compiler_params=_params("arbitrary", "arbitrary"),
        name=name,
    )(a, w, rp, rs, mod_p, mod_s)


TN_FF = 512
N_FF_BLOCKS = D_FF // TN_FF


def _up_kernel(a_ref, wa_ref, wv_ref, cw_ref, cb_ref, fst_ref,
               h_ref, fcp_ref, fcs_ref, wab, wvb, acc_a, acc_v):
    i = pl.program_id(1)
    rc = 64

    @pl.when(i == 0)
    def _():
        _cast_rows(wa_ref, wab)
        _cast_rows(wv_ref, wvb)

    def prompt():
        starts_sequence = i % MT_PER_SEQ == 0
        acc_a[0:8, :] = jnp.where(starts_sequence, 0.0, acc_a[TMM:TMM + 8, :])
        x = a_ref[...]
        acc_a[8:8 + TMM, :] = _dot(x, wab[...])
        acc_v[...] = _dot(x, wvb[...])
        cw, cb = cw_ref[...], cb_ref[...]
        for r in range(0, TMM, rc):
            conv = (cb + cw[2:3] * acc_a[8 + r:8 + r + rc, :]
                    + cw[1:2] * acc_a[7 + r:7 + r + rc, :]
                    + cw[0:1] * acc_a[6 + r:6 + r + rc, :])
            h_ref[r:r + rc, :] = (jax.nn.gelu(conv) * acc_v[r:r + rc, :]).astype(BF16)
        fcp_ref[0] = acc_a[TMM + 6:TMM + 8, :]

    def sample():
        B = DEC_BATCH
        x = a_ref[0:NS_ROWS, :]
        acc_a[8:8 + NS_ROWS, :] = _dot(x, wab[...])
        acc_v[0:NS_ROWS, :] = _dot(x, wvb[...])
        cw, cb = cw_ref[...], cb_ref[...]

        def pre(t, r):
            if t < 0:
                return fst_ref[r:r + rc, t + 2, :]
            return acc_a[8 + t * B + r:8 + t * B + r + rc, :]

        for t in range(DEC_SEQ):
            for r in range(0, B, rc):
                conv = cb + cw[2:3] * pre(t, r) + cw[1:2] * pre(t - 1, r) + cw[0:1] * pre(t - 2, r)
                h_ref[t * B + r:t * B + r + rc, :] = (
                    jax.nn.gelu(conv) * acc_v[t * B + r:t * B + r + rc, :]).astype(BF16)
        fcs_ref[:, 0, :] = acc_a[8 + 2 * B:8 + 3 * B, :]
        fcs_ref[:, 1, :] = acc_a[8 + 3 * B:8 + 4 * B, :]

    _per_tile(i, prompt, sample)


def _up_call(n2, w_up, cw, cb, fst):
    return pl.pallas_call(
        _up_kernel,
        grid=(N_FF_BLOCKS, N_MT),
        in_specs=[pl.BlockSpec((TMM, D), lambda j, i: (i, 0)),
                  pl.BlockSpec((D, TN_FF), lambda j, i: (0, j)),
                  pl.BlockSpec((D, TN_FF), lambda j, i: (0, N_FF_BLOCKS + j)),
                  pl.BlockSpec((3, TN_FF), lambda j, i: (0, j)),
                  pl.BlockSpec((1, TN_FF), lambda j, i: (0, j)),
                  pl.BlockSpec((DEC_BATCH, 2, TN_FF), lambda j, i: (0, 0, j))],
        out_specs=[pl.BlockSpec((TMM, TN_FF), lambda j, i: (i, j)),
                   pl.BlockSpec((1, 2, TN_FF), lambda j, i: (_seq_of_mtile(i), 0, j)),
                   pl.BlockSpec((DEC_BATCH, 2, TN_FF), lambda j, i: (0, 0, j))],
        out_shape=[jax.ShapeDtypeStruct((M_ROWS, D_FF), BF16),
                   jax.ShapeDtypeStruct((BATCH, 2, D_FF), F32),
                   jax.ShapeDtypeStruct((DEC_BATCH, 2, D_FF), F32)],
        scratch_shapes=[pltpu.VMEM((D, TN_FF), BF16), pltpu.VMEM((D, TN_FF), BF16),
                        pltpu.VMEM((TMM + 8, TN_FF), F32), pltpu.VMEM((TMM, TN_FF), F32)],
        compiler_params=_params("arbitrary", "arbitrary"),
        name="ffn_up",
    )(n2, w_up, w_up, cw, cb, fst)


def _final_kernel(x_ref, w_ref, yp_ref, ys_ref):
    i = pl.program_id(0)

    @pl.when(i < NP_MT)
    def _():
        for r in range(0, TMM, TM):
            yp_ref[r:r + TM, :] = _rms(x_ref[r:r + TM, :]) * w_ref[...]

    @pl.when(i == NP_MT)
    def _():
        y = _rms(x_ref[0:NS_ROWS, :]) * w_ref[...]
        for t in range(DEC_SEQ):
            ys_ref[:, t, :] = y[t * DEC_BATCH:(t + 1) * DEC_BATCH]


def _final_call(x3, w):
    return pl.pallas_call(
        _final_kernel,
        grid=(N_MT,),
        in_specs=[pl.BlockSpec((TMM, D), lambda i: (i, 0)),
                  pl.BlockSpec((1, D), lambda i: (0, 0))],
        out_specs=[pl.BlockSpec((TMM, D), lambda i: (jnp.minimum(i, NP_MT - 1), 0)),
                   pl.BlockSpec((DEC_BATCH, DEC_SEQ, D), lambda i: (0, 0, 0))],
        out_shape=[jax.ShapeDtypeStruct((NP_ROWS, D), F32),
                   jax.ShapeDtypeStruct((DEC_BATCH, DEC_SEQ, D), F32)],
        compiler_params=_params("arbitrary"),
        name="final_norm",
    )(x3, w)


def _to_time_major(a):
    return jnp.transpose(a, (1, 0, 2))


def kernel(x_prompt, x_sample, state_ssm, state_ssd_conv, state_ffn_conv, c_prompt, c_sample,
           norm1_w, w_ada, b_ada, w_in, ssd_conv_w, ssd_conv_b, dt_bias, a_log, d_skip,
           ssd_norm_w, mlp_ln_w, mlp_ln_b, w_spatial, b_spatial, w_ssd_o, w_mlp_o, w_out,
           norm2_w, w_up, ffn_conv_w, ffn_conv_b, w_down, final_norm_w):
    assert w_in.shape[0] == 1, "single-layer trunk"
    row = lambda v: v.reshape(1, -1)

    xp = x_prompt.reshape(NP_ROWS, D)
    xs = _to_time_major(x_sample).reshape(NS_ROWS, D)

    mod_p, mod_s = _ada_call(c_prompt, c_sample, w_ada[0], row(b_ada[0]))
    mod_p = mod_p.reshape(BATCH, 1, 6 * D)
    K_SHIFT1, K_SCALE1, K_GATE1, K_SHIFT2, K_SCALE2, K_GATE2 = range(6)

    xp_spec = pl.BlockSpec((TMM, D), lambda i: (jnp.minimum(i, NP_MT - 1), 0))
    xs_spec = pl.BlockSpec((NS_ROWS, D), lambda i: (0, 0))
    w_in_t = w_in[0].T
    n1, dt_raw = _norm_call(xp, xp_spec, xs, xs_spec, row(norm1_w[0]), mod_p, mod_s,
                            K_SCALE1, K_SHIFT1, w_in_t)
    cst_t = _to_time_major(state_ssd_conv[0])
    proj, cst_p, ncs_t = _inproj_call(n1, w_in_t, ssd_conv_w[0], row(ssd_conv_b[0]), cst_t)

    hp = jnp.zeros((8, LANE), F32)
    hp = hp.at[0, :HEADS].set(dt_bias[0]).at[1, :HEADS].set(a_log[0]).at[2, :HEADS].set(d_skip[0])
    hpc = hp.T
    triu = jnp.triu(jnp.ones((CHUNK, CHUNK), F32)).astype(BF16)
    dskx = row(jnp.repeat(d_skip[0], HEADDIM))
    nw = row(ssd_norm_w[0])

    head_of_col = jnp.arange(D) // HEADDIM
    ex = (jnp.arange(LANE)[:, None] == head_of_col[None, :]).astype(BF16)
    grp_of_head = jnp.arange(LANE) // (HEADS // GROUPS)
    seg = ((jnp.arange(GROUPS * STATE)[:, None] // STATE == grp_of_head[None, :])
           & (jnp.arange(LANE)[None, :] < HEADS)).astype(BF16)
    ypre, efull, xw_t, bb_t, cc_t, e3 = _ssds_prep_call(proj, dt_raw, hp, dskx, ex, seg)
    gw = (HEADS // GROUPS) * HEADDIM
    y_ssd_p, ssm_p, yo_t, ssm_s = _ssd_fused_call(
        proj, dt_raw, hp, hpc, dskx, nw, triu,
        e3[:, :HEADS], state_ssm[0].reshape(DEC_BATCH, GROUPS, gw, STATE), cc_t, bb_t, xw_t)
    y_ssd_s = _ssds_post_call(ypre, efull, yo_t.reshape(NS_ROWS, D), proj, nw)

    per_col = lambda a: jnp.repeat(a, MLP_GROUP_DIM, axis=1)
    bsx = per_col(b_spatial[0][:, :CHUNK].T)
    ws4 = w_spatial[0][:, :DEC_SEQ, :DEC_SEQ]
    wsx = per_col(jnp.transpose(ws4, (1, 2, 0)).reshape(DEC_SEQ * DEC_SEQ, MLP_GROUPS))
    bsx4 = jnp.pad(bsx[:DEC_SEQ], ((0, 8 - DEC_SEQ), (0, 0)))
    y_mlp, cv_s = _mlp_call(proj, row(mlp_ln_w[0]), row(mlp_ln_b[0]), w_spatial[0], bsx, wsx, bsx4)

    mixed = _merge_call(y_ssd_p, y_ssd_s, y_mlp, proj, w_ssd_o[0], w_mlp_o[0])
    x2 = _resid_call(mixed, w_out[0], xp, xs, 0, mod_p, mod_s, K_GATE1, 1024, "out_proj")

    x2p_spec = pl.BlockSpec((TMM, D), lambda i: (jnp.minimum(i, NP_MT - 1), 0))
    x2s_spec = pl.BlockSpec((NS_ROWS, D), lambda i: (NP_TILES, 0))
    (n2,) = _norm_call(x2, x2p_spec, x2, x2s_spec, row(norm2_w[0]), mod_p, mod_s,
                       K_SCALE2, K_SHIFT2, None)
    h, ffn_p, ffn_s = _up_call(n2, w_up[0], ffn_conv_w[0], row(ffn_conv_b[0]), state_ffn_conv[0])
    x3 = _resid_call(h, w_down[0], x2, x2, NP_TILES, mod_p, mod_s, K_GATE2, 512, "ffn_down",
                     single_buffer_w=True)

    y_p, y_s = _final_call(x3, row(final_norm_w))

    from_t = lambda a, t, c: jnp.transpose(a.reshape(t, DEC_BATCH, c), (1, 0, 2))
    return (y_p.reshape(BATCH, SEQ, D),
            y_s,
            ssm_p.reshape(1, BATCH, HEADS, HEADDIM, STATE),
            ssm_s.reshape(1, DEC_BATCH, HEADS, HEADDIM, STATE),
            cst_p[None],
            from_t(ncs_t, 3, CONV_DIM)[None],
            ffn_p[None],
            ffn_s[None],
            cv_s[None])
```

```python
import functools

import jax
import jax.numpy as jnp
from jax import lax
from jax.experimental import pallas as pl
from jax.experimental.pallas import tpu as pltpu

F32 = jnp.float32
BF16 = jnp.bfloat16

D = 2048
BATCH, SEQ = 4, 2048
DEC_BATCH, DEC_SEQ = 128, 4
NP_ROWS = BATCH * SEQ
NS_ROWS = DEC_BATCH * DEC_SEQ
M_ROWS = NP_ROWS + NS_ROWS
TM = 512
NP_TILES = NP_ROWS // TM
TMM = 1024
NP_MT = NP_ROWS // TMM
N_MT = NP_MT + 1
MT_PER_SEQ = SEQ // TMM
HEADS, HEADDIM, GROUPS, STATE = 32, 64, 4, 128
CHUNK = 128
N_CHUNKS = SEQ // CHUNK
CONV_DIM = D + 2 * GROUPS * STATE
MLP_GROUPS = 8
MLP_GROUP_DIM = D // MLP_GROUPS
D_FF = 5632
EPS = 1e-6
DT_COL = D + CONV_DIM
TN_IN = 1024
LANE = 128
VMEM_LIMIT = 56 * 1024 * 1024


def _params(*sem, flags=None):
    return pltpu.CompilerParams(dimension_semantics=sem, vmem_limit_bytes=VMEM_LIMIT, flags=flags)


def _dot(a, b):
    return jnp.dot(a, b, preferred_element_type=F32)


def _dot_nt(a, b_t):
    return lax.dot_general(a, b_t, (((1,), (1,)), ((), ())), preferred_element_type=F32)


def _split_bf16(v, terms):
    out = []
    r = v
    for _ in range(terms):
        p = r.astype(BF16)
        out.append(p)
        r = r - p.astype(F32)
    return out


def _dot_exact_rhs(v, e, terms=3):
    acc = None
    for p in _split_bf16(v, terms):
        d = _dot(p, e)
        acc = d if acc is None else acc + d
    return acc


def _silu(x):
    return x * jax.nn.sigmoid(x)


def _softplus(x):
    return jnp.maximum(x, 0.0) + jnp.log1p(jnp.exp(-jnp.abs(x)))


def _rms(x):
    return x * lax.rsqrt(jnp.mean(x * x, axis=-1, keepdims=True) + EPS)


def _ada_kernel(cp_ref, cs_ref, w_ref, b_ref, op_ref, os_ref):
    w = w_ref[...].astype(BF16)
    cp8 = jnp.concatenate([cp_ref[...], jnp.zeros((8 - BATCH, D), F32)], axis=0)
    op_ref[:, 0, :] = (_dot(_silu(cp8).astype(BF16), w) + b_ref[...])[0:BATCH]
    os_ref[...] = _dot(_silu(cs_ref[...]).astype(BF16), w) + b_ref[...]


def _ada_call(c_prompt, c_sample, w, b):
    tn = 1024
    return pl.pallas_call(
        _ada_kernel,
        grid=(6 * D // tn,),
        in_specs=[pl.BlockSpec((BATCH, D), lambda j: (0, 0)),
                  pl.BlockSpec((DEC_BATCH, D), lambda j: (0, 0)),
                  pl.BlockSpec((D, tn), lambda j: (0, j)),
                  pl.BlockSpec((1, tn), lambda j: (0, j))],
        out_specs=[pl.BlockSpec((BATCH, 1, tn), lambda j: (0, 0, j)),
                   pl.BlockSpec((DEC_BATCH, tn), lambda j: (0, j))],
        out_shape=[jax.ShapeDtypeStruct((BATCH, 1, 6 * D), F32),
                   jax.ShapeDtypeStruct((DEC_BATCH, 6 * D), F32)],
        compiler_params=_params("arbitrary"),
        name="ada_mod",
    )(c_prompt, c_sample, w, b)


def _sample_rows(ref):
    if len(ref.shape) == 2:
        return ref[...]
    return jnp.concatenate([ref[:, t, :] for t in range(DEC_SEQ)], axis=0)


def _norm_kernel(with_dt, xp_ref, xs_ref, nw_ref, scp_ref, shp_ref, scs_ref, shs_ref, *rest):
    if with_dt:
        wdt_ref, n_ref, dt_ref = rest
    else:
        (n_ref,) = rest
    i = pl.program_id(0)

    def emit(n, rows):
        nb = n.astype(BF16)
        n_ref[rows, :] = nb
        if with_dt:
            dt_ref[rows, :] = lax.dot_general(nb, wdt_ref[...].astype(BF16),
                                              (((1,), (1,)), ((), ())),
                                              preferred_element_type=F32)

    @pl.when(i < NP_MT)
    def _():
        for r in range(0, TMM, TM):
            rows = slice(r, r + TM)
            y = _rms(xp_ref[rows, :]) * nw_ref[...]
            emit(y * (1.0 + scp_ref[0]) + shp_ref[0], rows)

    @pl.when(i == NP_MT)
    def _():
        y = _rms(_sample_rows(xs_ref)) * nw_ref[...]
        y3 = y.reshape(DEC_SEQ, DEC_BATCH, D)
        emit((y3 * (1.0 + scs_ref[...])[None] + shs_ref[...][None]).reshape(NS_ROWS, D),
             slice(0, NS_ROWS))


def _seq_of_mtile(i):
    return jnp.minimum(i // MT_PER_SEQ, BATCH - 1)


def _norm_call(xp, xp_spec, xs, xs_spec, nw, mod_p, mod_s, k_scale, k_shift, w_in_t):
    with_dt = w_in_t is not None
    in_specs = [
        xp_spec, xs_spec,
        pl.BlockSpec((1, D), lambda i: (0, 0)),
        pl.BlockSpec((1, 1, D), lambda i: (_seq_of_mtile(i), 0, k_scale)),
        pl.BlockSpec((1, 1, D), lambda i: (_seq_of_mtile(i), 0, k_shift)),
        pl.BlockSpec((DEC_BATCH, D), lambda i: (0, k_scale)),
        pl.BlockSpec((DEC_BATCH, D), lambda i: (0, k_shift)),
    ]
    args = [xp, xs, nw, mod_p, mod_p, mod_s, mod_s]
    out_specs = [pl.BlockSpec((TMM, D), lambda i: (i, 0))]
    out_shape = [jax.ShapeDtypeStruct((M_ROWS, D), BF16)]
    if with_dt:
        in_specs.append(pl.BlockSpec((pl.Element(LANE), pl.Element(D)), lambda i: (DT_COL, 0)))
        args.append(w_in_t)
        out_specs.append(pl.BlockSpec((TMM, LANE), lambda i: (i, 0)))
        out_shape.append(jax.ShapeDtypeStruct((M_ROWS, LANE), F32))
    return pl.pallas_call(
        functools.partial(_norm_kernel, with_dt),
        grid=(N_MT,),
        in_specs=in_specs,
        out_specs=out_specs,
        out_shape=out_shape,
        compiler_params=_params("arbitrary"),
        name="norm_mod_dt" if with_dt else "norm_mod",
    )(*args)


N_IN_BLOCKS = 13
UVG_ROW = DT_COL + HEADS
FIRST_CONV_BLOCK = 10


def _in_src_row(j):
    row = jnp.where(j < 2, j * TN_IN,
                    jnp.where(j < FIRST_CONV_BLOCK, UVG_ROW + (j - 2) * TN_IN,
                              D + (j - FIRST_CONV_BLOCK) * TN_IN))
    return pl.multiple_of(row, HEADS)


def _in_conv_block(j):
    return jnp.maximum(j - FIRST_CONV_BLOCK, 0)


def _inproj_kernel(a_ref, wt_ref, cw_ref, cb_ref, cst_ref, o_ref, csp_ref, css_ref, wbf_ref, acc_ref):
    j = pl.program_id(0)
    i = pl.program_id(1)
    rc = 64
    B = DEC_BATCH

    @pl.when(i == 0)
    def _():
        for r in range(TN_IN // LANE):
            rows = slice(r * LANE, (r + 1) * LANE)
            wbf_ref[rows, :] = wt_ref[rows, :].astype(BF16)

    def elementwise(fn, rows):
        def body():
            acc_ref[8:8 + rows, :] = _dot_nt(a_ref[0:rows, :], wbf_ref[...])
            for r in range(0, rows, rc):
                val = acc_ref[8 + r:8 + r + rc, :]
                o_ref[r:r + rc, :] = (val if fn is None else fn(val)).astype(o_ref.dtype)
        return body

    def conv_prompt():
        starts_sequence = i % MT_PER_SEQ == 0
        acc_ref[0:8, :] = jnp.where(starts_sequence, 0.0, acc_ref[TMM:TMM + 8, :])
        acc_ref[8:8 + TMM, :] = _dot_nt(a_ref[...], wbf_ref[...])
        cw, cb = cw_ref[...], cb_ref[...]
        for r in range(0, TMM, rc):
            conv = cb + cw[3:4] * acc_ref[8 + r:8 + r + rc, :]
            for k in range(3):
                conv = conv + cw[k:k + 1] * acc_ref[5 + k + r:5 + k + r + rc, :]
            o_ref[r:r + rc, :] = _silu(conv).astype(o_ref.dtype)
        csp_ref[0] = acc_ref[TMM + 5:TMM + 8, :]

    def conv_sample():
        acc_ref[8:8 + NS_ROWS, :] = _dot_nt(a_ref[0:NS_ROWS, :], wbf_ref[...])
        cw, cb = cw_ref[...], cb_ref[...]

        def pre(t, r):
            if t < 0:
                return cst_ref[t + 3, r:r + rc, :]
            return acc_ref[8 + t * B + r:8 + t * B + r + rc, :]

        for t in range(DEC_SEQ):
            for r in range(0, B, rc):
                conv = cb + cw[3:4] * pre(t, r)
                for k in range(3):
                    conv = conv + cw[k:k + 1] * pre(t - 3 + k, r)
                o_ref[t * B + r:t * B + r + rc, :] = _silu(conv).astype(o_ref.dtype)
        for t in range(1, DEC_SEQ):
            css_ref[t - 1] = acc_ref[8 + t * B:8 + (t + 1) * B, :]

    is_conv = j >= FIRST_CONV_BLOCK
    kinds = ((j < 2, _silu),
             (jnp.logical_and(j >= 2, j < 6), jax.nn.gelu),
             (jnp.logical_and(j >= 6, j < FIRST_CONV_BLOCK), None))
    for cond, fn in kinds:
        pl.when(jnp.logical_and(cond, i < NP_MT))(elementwise(fn, TMM))
        pl.when(jnp.logical_and(cond, i == NP_MT))(elementwise(fn, NS_ROWS))
    pl.when(jnp.logical_and(is_conv, i < NP_MT))(conv_prompt)
    pl.when(jnp.logical_and(is_conv, i == NP_MT))(conv_sample)


def _inproj_call(n1, w_in_t, cw, cb, cst_t):
    cblk = _in_conv_block
    seq = lambda j, i: jnp.where(j < FIRST_CONV_BLOCK, 0, _seq_of_mtile(i))
    return pl.pallas_call(
        _inproj_kernel,
        grid=(N_IN_BLOCKS, N_MT),
        in_specs=[pl.BlockSpec((TMM, D), lambda j, i: (i, 0)),
                  pl.BlockSpec((pl.Element(TN_IN), pl.Element(D)),
                               lambda j, i: (_in_src_row(j), 0)),
                  pl.BlockSpec((4, TN_IN), lambda j, i: (0, cblk(j))),
                  pl.BlockSpec((1, TN_IN), lambda j, i: (0, cblk(j))),
                  pl.BlockSpec((3, DEC_BATCH, TN_IN), lambda j, i: (0, 0, cblk(j)))],
        out_specs=[pl.BlockSpec((TMM, TN_IN), lambda j, i: (i, j)),
                   pl.BlockSpec((1, 3, TN_IN), lambda j, i: (seq(j, i), 0, cblk(j))),
                   pl.BlockSpec((3, DEC_BATCH, TN_IN), lambda j, i: (0, 0, cblk(j)))],
        out_shape=[jax.ShapeDtypeStruct((M_ROWS, N_IN_BLOCKS * TN_IN), BF16),
                   jax.ShapeDtypeStruct((BATCH, 3, CONV_DIM), F32),
                   jax.ShapeDtypeStruct((3, DEC_BATCH, CONV_DIM), F32)],
        scratch_shapes=[pltpu.VMEM((TN_IN, D), BF16), pltpu.VMEM((TMM + 8, TN_IN), F32)],
        compiler_params=_params("arbitrary", "arbitrary"),
        name="in_proj",
    )(n1, w_in_t, cw, cb, cst_t)


PZ, PU, PV, PGA, PGB, PX = 0, 1, 2, 3, 4, 5
PBC_1024 = 12


def _gated_group_norm(get_y, zact_ref, nw_ref, o_ref, rows=slice(None)):
    gw = D // GROUPS
    for g in range(GROUPS):
        cols = slice(g * gw, (g + 1) * gw)
        gg = _rms(get_y(cols) * zact_ref[rows, cols].astype(F32))
        o_ref[rows, cols] = (gg * nw_ref[:, cols]).astype(o_ref.dtype)


CHUNKS_PER_STEP = 4


def _ssd_prompt_kernel(z_ref, x_ref, bc_ref, dt_ref, hp_ref, hpc_ref, dskx_ref, nw_ref, triu_ref,
                       y_ref, ssm_ref, st_ref, yscr_ref):
    c = pl.program_id(1)
    T = CHUNK
    cdim = GROUPS * STATE

    @pl.when(c == 0)
    def _():
        st_ref[...] = jnp.zeros(st_ref.shape, F32)

    hp = hp_ref[...]
    a_col = -jnp.exp(hpc_ref[0:HEADS, 1:2])
    triu = triu_ref[...]
    row = lax.broadcasted_iota(jnp.int32, (T, T), 0)
    lane = lax.broadcasted_iota(jnp.int32, (T, T), 1)
    causal = row >= lane
    left = lane < HEADDIM
    mask_l = jnp.where(left, 1.0, 0.0).astype(BF16)
    nt = (((1,), (1,)), ((), ()))
    pairs_per_group = HEADS // GROUPS // 2

    for sub in range(CHUNKS_PER_STEP):
        rows = slice(sub * T, (sub + 1) * T)
        y_buf = yscr_ref.at[sub % 2]
        dt_t = _softplus((dt_ref[rows, :] + hp[0:1, :]).T[0:HEADS, :])
        cs_t = _dot_exact_rhs(dt_t * a_col, triu)
        rsub_t = cs_t - jnp.log(dt_t)
        cs = jnp.concatenate([cs_t, jnp.zeros((LANE - HEADS, T), F32)], axis=0).T

        for g in range(GROUPS):
            c_b = bc_ref[rows, cdim + g * STATE:cdim + (g + 1) * STATE]
            b_b = bc_ref[rows, g * STATE:(g + 1) * STATE]
            cb = lax.dot_general(c_b, b_b, nt, preferred_element_type=F32)
            b_t = b_b.astype(F32).T
            for k4 in range(pairs_per_group):
                k = g * pairs_per_group + k4
                cols = slice(k * LANE, (k + 1) * LANE)
                xpb = x_ref[rows, cols]
                x_lo = xpb * mask_l
                xbd = jnp.concatenate([x_lo, xpb - x_lo], axis=0)
                st = st_ref[k]
                yraw = _dot(c_b, st.astype(BF16))
                scores, bws, colbs, alasts = [], [], [], []
                for h in (2 * k, 2 * k + 1):
                    colb = jnp.broadcast_to(cs[:, h:h + 1], (T, T))
                    alast = cs_t[h:h + 1, T - 1:T]
                    decay_dt = jnp.exp(jnp.where(causal, colb - rsub_t[h:h + 1, :], -jnp.inf))
                    scores.append((cb * decay_dt).astype(BF16))
                    wrow = jnp.exp(alast - cs_t[h:h + 1, :]) * dt_t[h:h + 1, :]
                    bws.append((b_t * wrow).astype(BF16))
                    colbs.append(colb)
                    alasts.append(alast)
                ecol = jnp.exp(jnp.where(left, colbs[0], colbs[1]))
                elast = jnp.exp(jnp.where(left[0:1], alasts[0], alasts[1]))
                y_buf[:, cols] = (_dot(jnp.concatenate(scores, axis=1), xbd) + ecol * yraw
                                  + dskx_ref[:, cols] * xpb.astype(F32))
                st_ref[k] = elast * st + _dot(jnp.concatenate(bws, axis=1), xbd)
        _gated_group_norm(lambda cols: y_buf[:, cols], z_ref, nw_ref, y_ref, rows)

    @pl.when(c == N_CHUNKS // CHUNKS_PER_STEP - 1)
    def _():
        for k in range(HEADS // 2):
            ssm_ref[0, k] = st_ref[k].T


def _ssds_prep_step(tt, x_ref, bc_ref, dt_ref, hp_ref, dskx_ref, ex_ref, seg_ref,
                    ypre_ref, efull_ref, xw_ref, bb_ref, cc_ref, e3_ref):
    B = DEC_BATCH
    cdim = GROUPS * STATE
    blk = lambda t: slice(t * B, (t + 1) * B)
    x_of = lambda t: x_ref[blk(t), :].astype(F32)
    b_of = lambda t: bc_ref[blk(t), 0:cdim].astype(F32)
    c_of = lambda t: bc_ref[blk(t), cdim:2 * cdim].astype(F32)

    bb_ref[...] = b_of(tt)
    cc_ref[...] = c_of(tt)

    hp = hp_ref[...]
    a_neg = -jnp.exp(hp[1:2, :])
    dts, css = [], []
    run = None
    for t in range(DEC_SEQ):
        dt = _softplus(dt_ref[blk(t), :] + hp[0:1, :])
        run = dt * a_neg if run is None else run + dt * a_neg
        dts.append(dt)
        css.append(run)

    ex = ex_ref[...]
    e3_ref[...] = jnp.exp(css[-1])
    efull_ref[...] = _dot_exact_rhs(jnp.exp(css[tt]), ex)
    w_t = jnp.exp(css[-1] - css[tt]) * dts[tt]
    xw_ref[...] = x_of(tt) * _dot_exact_rhs(w_t, ex)

    seg = seg_ref[...]
    acc = dskx_ref[...] * x_of(tt)
    c_t = c_of(tt)
    for s in range(tt + 1):
        cbh = _dot_exact_rhs(c_t * b_of(s), seg)
        g_ts = cbh * jnp.exp(css[tt] - css[s]) * dts[s]
        acc = acc + _dot_exact_rhs(g_ts, ex) * x_of(s)
    ypre_ref[...] = acc


def _ssds_prep_kernel(*refs):
    t = pl.program_id(0)
    for tt in range(DEC_SEQ):
        pl.when(t == tt)(functools.partial(_ssds_prep_step, tt, *refs))


def _ssds_prep_call(proj, dt_raw, hp, dskx, ex, seg):
    full = lambda shape: pl.BlockSpec(shape, lambda t: (0,) * len(shape))
    step = lambda width: pl.BlockSpec((DEC_BATCH, width), lambda t: (t, 0))
    cdim = GROUPS * STATE
    return pl.pallas_call(
        _ssds_prep_kernel,
        grid=(DEC_SEQ,),
        in_specs=[pl.BlockSpec((NS_ROWS, D), lambda t: (NP_TILES, PX)),
                  pl.BlockSpec((NS_ROWS, 2 * cdim), lambda t: (NP_TILES, PBC_1024)),
                  pl.BlockSpec((NS_ROWS, LANE), lambda t: (NP_TILES, 0)),
                  full((8, LANE)), full((1, D)), full((LANE, D)), full((cdim, LANE))],
        out_specs=[step(D), step(D), step(D), step(cdim), step(cdim),
                   full((DEC_BATCH, LANE))],
        out_shape=[jax.ShapeDtypeStruct((NS_ROWS, D), F32),
                   jax.ShapeDtypeStruct((NS_ROWS, D), F32),
                   jax.ShapeDtypeStruct((NS_ROWS, D), F32),
                   jax.ShapeDtypeStruct((NS_ROWS, cdim), F32),
                   jax.ShapeDtypeStruct((NS_ROWS, cdim), F32),
                   jax.ShapeDtypeStruct((DEC_BATCH, LANE), F32)],
        compiler_params=_params("arbitrary"),
        name="ssd_sample_prep",
    )(proj, proj, dt_raw, hp, dskx, ex, seg)


SEQ_PER_STEP = 8


def _ssds_state_body(blk, e3_ref, st_ref, cc_ref, bb_ref, xw_ref, yo_ref, so_ref):
    nt = (((1,), (1,)), ((), ()))
    tn = (((0,), (0,)), ((), ()))
    hpg = HEADS // GROUPS
    gw = hpg * HEADDIM

    def rows_of(ref, s, cols):
        v = ref[:, s, cols]
        return jnp.concatenate([v, jnp.zeros((8 - DEC_SEQ, v.shape[-1]), F32)], axis=0).astype(BF16)

    for s in range(SEQ_PER_STEP):
        b = blk * SEQ_PER_STEP + s
        for g in range(GROUPS):
            h0 = st_ref[s, g]
            c_g = rows_of(cc_ref, s, slice(g * STATE, (g + 1) * STATE))
            yraw = lax.dot_general(c_g, h0.astype(BF16), nt, preferred_element_type=F32)
            yo_ref[:, s, g * gw:(g + 1) * gw] = yraw[0:DEC_SEQ]
            x_g = rows_of(xw_ref, s, slice(g * gw, (g + 1) * gw))
            b_g = rows_of(bb_ref, s, slice(g * STATE, (g + 1) * STATE))
            dh = lax.dot_general(x_g, b_g, tn, preferred_element_type=F32)
            for hh in range(hpg):
                rows = slice(hh * HEADDIM, (hh + 1) * HEADDIM)
                so_ref[s, g, rows, :] = e3_ref[b, g * hpg + hh] * h0[rows] + dh[rows]


PROMPT_STEPS = N_CHUNKS // CHUNKS_PER_STEP
assert BATCH * PROMPT_STEPS * SEQ_PER_STEP == DEC_BATCH


def _ssd_fused_kernel(z_ref, x_ref, bc_ref, dt_ref, hp_ref, hpc_ref, dskx_ref, nw_ref, triu_ref,
                      e3_ref, sti_ref, cc_ref, bb_ref, xw_ref,
                      y_ref, ssm_ref, yo_ref, so_ref, st_ref, yscr_ref):
    _ssd_prompt_kernel(z_ref, x_ref, bc_ref, dt_ref, hp_ref, hpc_ref, dskx_ref, nw_ref, triu_ref,
                       y_ref, ssm_ref, st_ref, yscr_ref)
    blk = pl.program_id(0) * PROMPT_STEPS + pl.program_id(1)
    _ssds_state_body(blk, e3_ref, sti_ref, cc_ref, bb_ref, xw_ref, yo_ref, so_ref)


def _ssd_fused_call(proj, dt_raw, hp, hpc, dskx, nw, triu, e3, state, cc_t, bb_t, xw_t):
    rows = CHUNK * CHUNKS_PER_STEP
    sb = SEQ_PER_STEP
    gw = (HEADS // GROUPS) * HEADDIM
    cdim = GROUPS * STATE
    step = lambda b, c: b * PROMPT_STEPS + c
    const = lambda b, c: (0, 0)
    tmajor = lambda width: pl.BlockSpec((DEC_SEQ, sb, width), lambda b, c: (0, step(b, c), 0))
    st_spec = pl.BlockSpec((sb, GROUPS, gw, STATE), lambda b, c: (step(b, c), 0, 0, 0))
    return pl.pallas_call(
        _ssd_fused_kernel,
        grid=(BATCH, PROMPT_STEPS),
        in_specs=[pl.BlockSpec((rows, D), lambda b, c: (step(b, c), PZ)),
                  pl.BlockSpec((rows, D), lambda b, c: (step(b, c), PX)),
                  pl.BlockSpec((rows, 2 * cdim), lambda b, c: (step(b, c), PBC_1024)),
                  pl.BlockSpec((rows, LANE), lambda b, c: (step(b, c), 0)),
                  pl.BlockSpec((8, LANE), const),
                  pl.BlockSpec((LANE, 8), const),
                  pl.BlockSpec((1, D), const),
                  pl.BlockSpec((1, D), const),
                  pl.BlockSpec((CHUNK, CHUNK), const),
                  pl.BlockSpec(memory_space=pltpu.SMEM),
                  st_spec, tmajor(cdim), tmajor(cdim), tmajor(D)],
        out_specs=[pl.BlockSpec((rows, D), lambda b, c: (step(b, c), 0)),
                   pl.BlockSpec((1, HEADS // 2, LANE, STATE), lambda b, c: (b, 0, 0, 0)),
                   tmajor(D), st_spec],
        out_shape=[jax.ShapeDtypeStruct((NP_ROWS, D), BF16),
                   jax.ShapeDtypeStruct((BATCH, HEADS // 2, LANE, STATE), F32),
                   jax.ShapeDtypeStruct((DEC_SEQ, DEC_BATCH, D), F32),
                   jax.ShapeDtypeStruct((DEC_BATCH, GROUPS, gw, STATE), F32)],
        scratch_shapes=[pltpu.VMEM((HEADS // 2, STATE, LANE), F32),
                        pltpu.VMEM((2, CHUNK, D), F32)],
        compiler_params=_params("arbitrary", "arbitrary"),
        name="ssd_prompt_and_sample_state",
    )(proj, proj, proj, dt_raw, hp, hpc, dskx, nw, triu,
      e3, state, cc_t.reshape(DEC_SEQ, DEC_BATCH, cdim), bb_t.reshape(DEC_SEQ, DEC_BATCH, cdim),
      xw_t.reshape(DEC_SEQ, DEC_BATCH, D))


def _ssds_post_kernel(ypre_ref, efull_ref, yo_ref, z_ref, nw_ref, o_ref):
    get_y = lambda cols: ypre_ref[:, cols] + efull_ref[:, cols] * yo_ref[:, cols]
    _gated_group_norm(get_y, z_ref, nw_ref, o_ref)


def _ssds_post_call(ypre, efull, yo_t, proj, nw):
    step = pl.BlockSpec((DEC_BATCH, D), lambda t: (t, 0))
    return pl.pallas_call(
        _ssds_post_kernel,
        grid=(DEC_SEQ,),
        in_specs=[step, step, step,
                  pl.BlockSpec((DEC_BATCH, D), lambda t: (NP_ROWS // DEC_BATCH + t, PZ)),
                  pl.BlockSpec((1, D), lambda t: (0, 0))],
        out_specs=step,
        out_shape=jax.ShapeDtypeStruct((NS_ROWS, D), BF16),
        compiler_params=_params("arbitrary"),
        name="ssd_sample_post",
    )(ypre, efull, yo_t, proj, nw)


def _mlp_kernel(u_ref, v_ref, lnw_ref, lnb_ref, ws_ref, bsx_ref, wsx_ref, bsx4_ref,
                y_ref, cv_ref, wm_ref, cvs_ref):
    i = pl.program_id(0)
    T = CHUNK

    @pl.when(i == 0)
    def _():
        row = lax.broadcasted_iota(jnp.int32, (T, T), 0)
        lane = lax.broadcasted_iota(jnp.int32, (T, T), 1)
        for g in range(MLP_GROUPS):
            wm_ref[g] = jnp.where(row >= lane, ws_ref[g], 0.0).astype(BF16)

    def vnorm(rows):
        vg = v_ref[rows, :].astype(F32)
        xc = vg - jnp.mean(vg, axis=-1, keepdims=True)
        y = xc * lax.rsqrt(jnp.mean(xc * xc, axis=-1, keepdims=True) + EPS)
        return y * lnw_ref[...] + lnb_ref[...]

    @pl.when(i < NP_MT)
    def _():
        for cc in range(TMM // T):
            rows = slice(cc * T, (cc + 1) * T)
            vnb = vnorm(rows).astype(BF16)
            for g in range(MLP_GROUPS):
                cols = slice(g * MLP_GROUP_DIM, (g + 1) * MLP_GROUP_DIM)
                sv = _dot(wm_ref[g], vnb[:, cols]) + bsx_ref[:, cols]
                y_ref[rows, cols] = (u_ref[rows, cols].astype(F32) * sv).astype(BF16)

    @pl.when(i == NP_MT)
    def _():
        B = DEC_BATCH
        for t in range(DEC_SEQ):
            rows = slice(t * B, (t + 1) * B)
            vn = vnorm(rows)
            cvs_ref[rows, :] = vn
            cv_ref[:, t, :] = vn
        for t in range(DEC_SEQ):
            rows = slice(t * B, (t + 1) * B)
            acc = bsx4_ref[t:t + 1, :]
            for s in range(t + 1):
                acc = acc + wsx_ref[4 * t + s:4 * t + s + 1, :] * cvs_ref[s * B:(s + 1) * B, :]
            y_ref[rows, :] = (u_ref[rows, :].astype(F32) * acc).astype(BF16)


def _mlp_call(proj, lnw, lnb, ws, bsx, wsx, bsx4):
    full = lambda shape: pl.BlockSpec(shape, lambda i: (0,) * len(shape))
    return pl.pallas_call(
        _mlp_kernel,
        grid=(N_MT,),
        in_specs=[pl.BlockSpec((TMM, D), lambda i: (i, PU)),
                  pl.BlockSpec((TMM, D), lambda i: (i, PV)),
                  full((1, D)), full((1, D)),
                  full((MLP_GROUPS, CHUNK, CHUNK)),
                  full((CHUNK, D)), full((16, D)), full((8, D))],
        out_specs=[pl.BlockSpec((TMM, D), lambda i: (i, 0)),
                   full((DEC_BATCH, DEC_SEQ, D))],
        out_shape=[jax.ShapeDtypeStruct((M_ROWS, D), BF16),
                   jax.ShapeDtypeStruct((DEC_BATCH, DEC_SEQ, D), F32)],
        scratch_shapes=[pltpu.VMEM((MLP_GROUPS, CHUNK, CHUNK), BF16),
                        pltpu.VMEM((NS_ROWS, D), F32)],
        compiler_params=_params("arbitrary"),
        name="gmlp",
    )(proj, proj, lnw, lnb, ws, bsx, wsx, bsx4)


def _cast_rows(src_ref, dst_ref, chunk=256):
    def body(r, carry):
        rows = pl.ds(pl.multiple_of(r * chunk, chunk), chunk)
        dst_ref[rows, :] = src_ref[rows, :].astype(dst_ref.dtype)
        return carry
    lax.fori_loop(0, src_ref.shape[0] // chunk, body, 0)


def _per_tile(i, prompt_fn, sample_fn):
    pl.when(i < NP_MT)(prompt_fn)
    pl.when(i == NP_MT)(sample_fn)


def _merge_kernel(ysp_ref, yss_ref, ym_ref, ga_ref, gb_ref, w1_ref, w2_ref, o_ref, w1b, w2b):
    i = pl.program_id(1)

    @pl.when(i == 0)
    def _():
        _cast_rows(w1_ref, w1b)
        _cast_rows(w2_ref, w2b)

    def emit(ys, rows):
        a1 = _dot(ys, w1b[...])
        a2 = _dot(ym_ref[rows, :], w2b[...])
        o_ref[rows, :] = (jax.nn.sigmoid(ga_ref[rows, :].astype(F32)) * a1
                          + jax.nn.sigmoid(gb_ref[rows, :].astype(F32)) * a2).astype(BF16)

    _per_tile(i,
              lambda: emit(ysp_ref[...], slice(None)),
              lambda: emit(yss_ref[...], slice(0, NS_ROWS)))


def _merge_call(ysp, yss, ym, proj, w1, w2):
    tn = 512
    nb = D // tn
    return pl.pallas_call(
        _merge_kernel,
        grid=(nb, N_MT),
        in_specs=[pl.BlockSpec((TMM, D), lambda j, i: (jnp.minimum(i, NP_MT - 1), 0)),
                  pl.BlockSpec((NS_ROWS, D), lambda j, i: (0, 0)),
                  pl.BlockSpec((TMM, D), lambda j, i: (i, 0)),
                  pl.BlockSpec((TMM, tn), lambda j, i: (i, PGA * nb + j)),
                  pl.BlockSpec((TMM, tn), lambda j, i: (i, PGB * nb + j)),
                  pl.BlockSpec((D, tn), lambda j, i: (0, j)),
                  pl.BlockSpec((D, tn), lambda j, i: (0, j))],
        out_specs=pl.BlockSpec((TMM, tn), lambda j, i: (i, j)),
        out_shape=jax.ShapeDtypeStruct((M_ROWS, D), BF16),
        scratch_shapes=[pltpu.VMEM((D, tn), BF16), pltpu.VMEM((D, tn), BF16)],
        compiler_params=_params("arbitrary", "arbitrary"),
        name="branch_merge",
    )(ysp, yss, ym, proj, proj, w1, w2)


def _resid_kernel(a_ref, w_ref, rp_ref, rs_ref, gp_ref, gs_ref, o_ref, wb):
    i = pl.program_id(1)

    @pl.when(i == 0)
    def _():
        _cast_rows(w_ref, wb)

    def prompt():
        o_ref[...] = rp_ref[...] + gp_ref[0] * _dot(a_ref[...], wb[...])

    def sample():
        acc = _dot(a_ref[0:NS_ROWS, :], wb[...])
        tn = acc.shape[-1]
        acc3 = acc.reshape(DEC_SEQ, DEC_BATCH, tn) * gs_ref[...][None]
        o_ref[0:NS_ROWS, :] = _sample_rows(rs_ref) + acc3.reshape(NS_ROWS, tn)

    _per_tile(i, prompt, sample)


def _resid_call(a, w, rp, rs, rs_block, mod_p, mod_s, k_gate, tn, name, single_buffer_w=False):
    kdim = a.shape[1]
    nb = D // tn
    w_mode = dict(pipeline_mode=pl.Buffered(1)) if single_buffer_w else {}
    if rs.ndim == 3:
        rs_spec = pl.BlockSpec((DEC_BATCH, DEC_SEQ, tn), lambda j, i: (0, 0, j))
    else:
        rs_spec = pl.BlockSpec((NS_ROWS, tn), lambda j, i: (rs_block, j))
    return pl.pallas_call(
        _resid_kernel,
        grid=(nb, N_MT),
        in_specs=[pl.BlockSpec((TMM, kdim), lambda j, i: (i, 0)),
                  pl.BlockSpec((kdim, tn), lambda j, i: (0, j), **w_mode),
                  pl.BlockSpec((TMM, tn), lambda j, i: (jnp.minimum(i, NP_MT - 1), j)),
                  rs_spec,
                  pl.BlockSpec((1, 1, tn), lambda j, i: (_seq_of_mtile(i), 0, k_gate * nb + j)),
                  pl.BlockSpec((DEC_BATCH, tn), lambda j, i: (0, k_gate * nb + j))],
        out_specs=pl.BlockSpec((TMM, tn), lambda j, i: (i, j)),
        out_shape=jax.ShapeDtypeStruct((M_ROWS, D), F32),
        scratch_shapes=[pltpu.VMEM((kdim, tn), BF16)],
        compiler_params=_params("arbitrary", "arbitrary"),
        name=name,
    )(a, w, rp, rs, mod_p, mod_s)


TN_FF = 512
N_FF_BLOCKS = D_FF // TN_FF


def _up_kernel(a_ref, wa_ref, wv_ref, cw_ref, cb_ref, fst_ref,
               h_ref, fcp_ref, fcs_ref, wab, wvb, acc_a, acc_v):
    i = pl.program_id(1)
    rc = 64

    @pl.when(i == 0)
    def _():
        _cast_rows(wa_ref, wab)
        _cast_rows(wv_ref, wvb)

    def prompt():
        starts_sequence = i % MT_PER_SEQ == 0
        acc_a[0:8, :] = jnp.where(starts_sequence, 0.0, acc_a[TMM:TMM + 8, :])
        x = a_ref[...]
        acc_a[8:8 + TMM, :] = _dot(x, wab[...])
        acc_v[...] = _dot(x, wvb[...])
        cw, cb = cw_ref[...], cb_ref[...]
        for r in range(0, TMM, rc):
            conv = (cb + cw[2:3] * acc_a[8 + r:8 + r + rc, :]
                    + cw[1:2] * acc_a[7 + r:7 + r + rc, :]
                    + cw[0:1] * acc_a[6 + r:6 + r + rc, :])
            h_ref[r:r + rc, :] = (jax.nn.gelu(conv) * acc_v[r:r + rc, :]).astype(BF16)
        fcp_ref[0] = acc_a[TMM + 6:TMM + 8, :]

    def sample():
        B = DEC_BATCH
        x = a_ref[0:NS_ROWS, :]
        acc_a[8:8 + NS_ROWS, :] = _dot(x, wab[...])
        acc_v[0:NS_ROWS, :] = _dot(x, wvb[...])
        cw, cb = cw_ref[...], cb_ref[...]

        def pre(t, r):
            if t < 0:
                return fst_ref[r:r + rc, t + 2, :]
            return acc_a[8 + t * B + r:8 + t * B + r + rc, :]

        for t in range(DEC_SEQ):
            for r in range(0, B, rc):
                conv = cb + cw[2:3] * pre(t, r) + cw[1:2] * pre(t - 1, r) + cw[0:1] * pre(t - 2, r)
                h_ref[t * B + r:t * B + r + rc, :] = (
                    jax.nn.gelu(conv) * acc_v[t * B + r:t * B + r + rc, :]).astype(BF16)
        fcs_ref[:, 0, :] = acc_a[8 + 2 * B:8 + 3 * B, :]
        fcs_ref[:, 1, :] = acc_a[8 + 3 * B:8 + 4 * B, :]

    _per_tile(i, prompt, sample)


def _up_call(n2, w_up, cw, cb, fst):
    return pl.pallas_call(
        _up_kernel,
        grid=(N_FF_BLOCKS, N_MT),
        in_specs=[pl.BlockSpec((TMM, D), lambda j, i: (i, 0)),
                  pl.BlockSpec((D, TN_FF), lambda j, i: (0, j)),
                  pl.BlockSpec((D, TN_FF), lambda j, i: (0, N_FF_BLOCKS + j)),
                  pl.BlockSpec((3, TN_FF), lambda j, i: (0, j)),
                  pl.BlockSpec((1, TN_FF), lambda j, i: (0, j)),
                  pl.BlockSpec((DEC_BATCH, 2, TN_FF), lambda j, i: (0, 0, j))],
        out_specs=[pl.BlockSpec((TMM, TN_FF), lambda j, i: (i, j)),
                   pl.BlockSpec((1, 2, TN_FF), lambda j, i: (_seq_of_mtile(i), 0, j)),
                   pl.BlockSpec((DEC_BATCH, 2, TN_FF), lambda j, i: (0, 0, j))],
        out_shape=[jax.ShapeDtypeStruct((M_ROWS, D_FF), BF16),
                   jax.ShapeDtypeStruct((BATCH, 2, D_FF), F32),
                   jax.ShapeDtypeStruct((DEC_BATCH, 2, D_FF), F32)],
        scratch_shapes=[pltpu.VMEM((D, TN_FF), BF16), pltpu.VMEM((D, TN_FF), BF16),
                        pltpu.VMEM((TMM + 8, TN_FF), F32), pltpu.VMEM((TMM, TN_FF), F32)],
        compiler_params=_params("arbitrary", "arbitrary"),
        name="ffn_up",
    )(n2, w_up, w_up, cw, cb, fst)


def _final_kernel(x_ref, w_ref, yp_ref, ys_ref):
    i = pl.program_id(0)

    @pl.when(i < NP_MT)
    def _():
        for r in range(0, TMM, TM):
            yp_ref[r:r + TM, :] = _rms(x_ref[r:r + TM, :]) * w_ref[...]

    @pl.when(i == NP_MT)
    def _():
        y = _rms(x_ref[0:NS_ROWS, :]) * w_ref[...]
        for t in range(DEC_SEQ):
            ys_ref[:, t, :] = y[t * DEC_BATCH:(t + 1) * DEC_BATCH]


def _final_call(x3, w):
    return pl.pallas_call(
        _final_kernel,
        grid=(N_MT,),
        in_specs=[pl.BlockSpec((TMM, D), lambda i: (i, 0)),
                  pl.BlockSpec((1, D), lambda i: (0, 0))],
        out_specs=[pl.BlockSpec((TMM, D), lambda i: (jnp.minimum(i, NP_MT - 1), 0)),
                   pl.BlockSpec((DEC_BATCH, DEC_SEQ, D), lambda i: (0, 0, 0))],
        out_shape=[jax.ShapeDtypeStruct((NP_ROWS, D), F32),
                   jax.ShapeDtypeStruct((DEC_BATCH, DEC_SEQ, D), F32)],
        compiler_params=_params("arbitrary"),
        name="final_norm",
    )(x3, w)


def _to_time_major(a):
    return jnp.transpose(a, (1, 0, 2))


def kernel(x_prompt, x_sample, state_ssm, state_ssd_conv, state_ffn_conv, c_prompt, c_sample,
           norm1_w, w_ada, b_ada, w_in, ssd_conv_w, ssd_conv_b, dt_bias, a_log, d_skip,
           ssd_norm_w, mlp_ln_w, mlp_ln_b, w_spatial, b_spatial, w_ssd_o, w_mlp_o, w_out,
           norm2_w, w_up, ffn_conv_w, ffn_conv_b, w_down, final_norm_w):
    assert w_in.shape[0] == 1, "single-layer trunk"
    row = lambda v: v.reshape(1, -1)

    xp = x_prompt.reshape(NP_ROWS, D)
    xs = x_sample

    mod_p, mod_s = _ada_call(c_prompt, c_sample, w_ada[0], row(b_ada[0]))
    K_SHIFT1, K_SCALE1, K_GATE1, K_SHIFT2, K_SCALE2, K_GATE2 = range(6)

    xp_spec = pl.BlockSpec((TMM, D), lambda i: (jnp.minimum(i, NP_MT - 1), 0))
    xs_spec = pl.BlockSpec((DEC_BATCH, DEC_SEQ, D), lambda i: (0, 0, 0))
    w_in_t = w_in[0].T
    n1, dt_raw = _norm_call(xp, xp_spec, xs, xs_spec, row(norm1_w[0]), mod_p, mod_s,
                            K_SCALE1, K_SHIFT1, w_in_t)
    cst_t = _to_time_major(state_ssd_conv[0])
    proj, cst_p, ncs_t = _inproj_call(n1, w_in_t, ssd_conv_w[0], row(ssd_conv_b[0]), cst_t)

    hp = jnp.zeros((8, LANE), F32)
    hp = hp.at[0, :HEADS].set(dt_bias[0]).at[1, :HEADS].set(a_log[0]).at[2, :HEADS].set(d_skip[0])
    hpc = hp.T
    triu = jnp.triu(jnp.ones((CHUNK, CHUNK), F32)).astype(BF16)
    dskx = row(jnp.repeat(d_skip[0], HEADDIM))
    nw = row(ssd_norm_w[0])

    head_of_col = jnp.arange(D) // HEADDIM
    ex = (jnp.arange(LANE)[:, None] == head_of_col[None, :]).astype(BF16)
    grp_of_head = jnp.arange(LANE) // (HEADS // GROUPS)
    seg = ((jnp.arange(GROUPS * STATE)[:, None] // STATE == grp_of_head[None, :])
           & (jnp.arange(LANE)[None, :] < HEADS)).astype(BF16)
    ypre, efull, xw_t, bb_t, cc_t, e3 = _ssds_prep_call(proj, dt_raw, hp, dskx, ex, seg)
    gw = (HEADS // GROUPS) * HEADDIM
    y_ssd_p, ssm_p, yo_t, ssm_s = _ssd_fused_call(
        proj, dt_raw, hp, hpc, dskx, nw, triu,
        e3[:, :HEADS], state_ssm[0].reshape(DEC_BATCH, GROUPS, gw, STATE), cc_t, bb_t, xw_t)
    y_ssd_s = _ssds_post_call(ypre, efull, yo_t.reshape(NS_ROWS, D), proj, nw)

    per_col = lambda a: jnp.repeat(a, MLP_GROUP_DIM, axis=1)
    bsx = per_col(b_spatial[0][:, :CHUNK].T)
    ws4 = w_spatial[0][:, :DEC_SEQ, :DEC_SEQ]
    wsx = per_col(jnp.transpose(ws4, (1, 2, 0)).reshape(DEC_SEQ * DEC_SEQ, MLP_GROUPS))
    bsx4 = jnp.pad(bsx[:DEC_SEQ], ((0, 8 - DEC_SEQ), (0, 0)))
    y_mlp, cv_s = _mlp_call(proj, row(mlp_ln_w[0]), row(mlp_ln_b[0]), w_spatial[0], bsx, wsx, bsx4)

    mixed = _merge_call(y_ssd_p, y_ssd_s, y_mlp, proj, w_ssd_o[0], w_mlp_o[0])
    x2 = _resid_call(mixed, w_out[0], xp, xs, 0, mod_p, mod_s, K_GATE1, 1024, "out_proj")

    x2p_spec = pl.BlockSpec((TMM, D), lambda i: (jnp.minimum(i, NP_MT - 1), 0))
    x2s_spec = pl.BlockSpec((NS_ROWS, D), lambda i: (NP_TILES, 0))
    (n2,) = _norm_call(x2, x2p_spec, x2, x2s_spec, row(norm2_w[0]), mod_p, mod_s,
                       K_SCALE2, K_SHIFT2, None)
    h, ffn_p, ffn_s = _up_call(n2, w_up[0], ffn_conv_w[0], row(ffn_conv_b[0]), state_ffn_conv[0])
    x3 = _resid_call(h, w_down[0], x2, x2, NP_TILES, mod_p, mod_s, K_GATE2, 512, "ffn_down",
                     single_buffer_w=True)

    y_p, y_s = _final_call(x3, row(final_norm_w))

    from_t = lambda a, t, c: jnp.transpose(a.reshape(t, DEC_BATCH, c), (1, 0, 2))
    return (y_p.reshape(BATCH, SEQ, D),
            y_s,
            ssm_p.reshape(1, BATCH, HEADS, HEADDIM, STATE),
            ssm_s.reshape(1, DEC_BATCH, HEADS, HEADDIM, STATE),
            cst_p[None],
            from_t(ncs_t, 3, CONV_DIM)[None],
            ffn_p[None],
            ffn_s[None],
            cv_s[None])
```

```python
import functools

import jax
import jax.numpy as jnp
import numpy as np
from jax import lax
from jax.experimental import pallas as pl
from jax.experimental.pallas import tpu as pltpu

F32 = jnp.float32
BF16 = jnp.bfloat16

D = 2048
BATCH, SEQ = 4, 2048
DEC_BATCH, DEC_SEQ = 128, 4
NP_ROWS = BATCH * SEQ
NS_ROWS = DEC_BATCH * DEC_SEQ
M_ROWS = NP_ROWS + NS_ROWS
TM = 512
NP_TILES = NP_ROWS // TM
TMM = 1024
NP_MT = NP_ROWS // TMM
N_MT = NP_MT + 1
MT_PER_SEQ = SEQ // TMM
HEADS, HEADDIM, GROUPS, STATE = 32, 64, 4, 128
CHUNK = 128
N_CHUNKS = SEQ // CHUNK
CONV_DIM = D + 2 * GROUPS * STATE
MLP_GROUPS = 8
MLP_GROUP_DIM = D // MLP_GROUPS
D_FF = 5632
EPS = 1e-6
DT_COL = D + CONV_DIM
TN_IN = 1024
LANE = 128
VMEM_LIMIT = 56 * 1024 * 1024


def _params(*sem, flags=None):
    return pltpu.CompilerParams(dimension_semantics=sem, vmem_limit_bytes=VMEM_LIMIT, flags=flags)


def _dot(a, b):
    return jnp.dot(a, b, preferred_element_type=F32)


def _dot_nt(a, b_t):
    return lax.dot_general(a, b_t, (((1,), (1,)), ((), ())), preferred_element_type=F32)


def _split_bf16(v, terms):
    out = []
    r = v
    for _ in range(terms):
        p = r.astype(BF16)
        out.append(p)
        r = r - p.astype(F32)
    return out


def _dot_exact_rhs(v, e, terms=3):
    acc = None
    for p in _split_bf16(v, terms):
        d = _dot(p, e)
        acc = d if acc is None else acc + d
    return acc


def _silu(x):
    return x * jax.nn.sigmoid(x)


def _softplus(x):
    return jnp.maximum(x, 0.0) + jnp.log1p(jnp.exp(-jnp.abs(x)))


def _rms(x):
    return x * lax.rsqrt(jnp.mean(x * x, axis=-1, keepdims=True) + EPS)


def _ada_kernel(cp_ref, cs_ref, w_ref, b_ref, op_ref, os_ref):
    w = w_ref[...].astype(BF16)
    cp8 = jnp.concatenate([cp_ref[...], jnp.zeros((8 - BATCH, D), F32)], axis=0)
    op_ref[:, 0, :] = (_dot(_silu(cp8).astype(BF16), w) + b_ref[...])[0:BATCH]
    os_ref[...] = _dot(_silu(cs_ref[...]).astype(BF16), w) + b_ref[...]


def _ada_call(c_prompt, c_sample, w, b):
    tn = 1024
    return pl.pallas_call(
        _ada_kernel,
        grid=(6 * D // tn,),
        in_specs=[pl.BlockSpec((BATCH, D), lambda j: (0, 0)),
                  pl.BlockSpec((DEC_BATCH, D), lambda j: (0, 0)),
                  pl.BlockSpec((D, tn), lambda j: (0, j)),
                  pl.BlockSpec((1, tn), lambda j: (0, j))],
        out_specs=[pl.BlockSpec((BATCH, 1, tn), lambda j: (0, 0, j)),
                   pl.BlockSpec((DEC_BATCH, tn), lambda j: (0, j))],
        out_shape=[jax.ShapeDtypeStruct((BATCH, 1, 6 * D), F32),
                   jax.ShapeDtypeStruct((DEC_BATCH, 6 * D), F32)],
        compiler_params=_params("arbitrary"),
        name="ada_mod",
    )(c_prompt, c_sample, w, b)


def _sample_rows(ref):
    if len(ref.shape) == 2:
        return ref[...]
    return jnp.concatenate([ref[:, t, :] for t in range(DEC_SEQ)], axis=0)


def _norm_kernel(with_dt, xp_ref, xs_ref, nw_ref, scp_ref, shp_ref, scs_ref, shs_ref, *rest):
    if with_dt:
        wdt_ref, n_ref, dt_ref = rest
    else:
        (n_ref,) = rest
    i = pl.program_id(0)

    def emit(n, rows):
        nb = n.astype(BF16)
        n_ref[rows, :] = nb
        if with_dt:
            dt_ref[rows, :] = lax.dot_general(nb, wdt_ref[...].astype(BF16),
                                              (((1,), (1,)), ((), ())),
                                              preferred_element_type=F32)

    @pl.when(i < NP_MT)
    def _():
        for r in range(0, TMM, TM):
            rows = slice(r, r + TM)
            y = _rms(xp_ref[rows, :]) * nw_ref[...]
            emit(y * (1.0 + scp_ref[0]) + shp_ref[0], rows)

    @pl.when(i == NP_MT)
    def _():
        y = _rms(_sample_rows(xs_ref)) * nw_ref[...]
        y3 = y.reshape(DEC_SEQ, DEC_BATCH, D)
        emit((y3 * (1.0 + scs_ref[...])[None] + shs_ref[...][None]).reshape(NS_ROWS, D),
             slice(0, NS_ROWS))


def _seq_of_mtile(i):
    return jnp.minimum(i // MT_PER_SEQ, BATCH - 1)


def _norm_call(xp, xp_spec, xs, xs_spec, nw, mod_p, mod_s, k_scale, k_shift, w_in_t):
    with_dt = w_in_t is not None
    in_specs = [
        xp_spec, xs_spec,
        pl.BlockSpec((1, D), lambda i: (0, 0)),
        pl.BlockSpec((1, 1, D), lambda i: (_seq_of_mtile(i), 0, k_scale)),
        pl.BlockSpec((1, 1, D), lambda i: (_seq_of_mtile(i), 0, k_shift)),
        pl.BlockSpec((DEC_BATCH, D), lambda i: (0, k_scale)),
        pl.BlockSpec((DEC_BATCH, D), lambda i: (0, k_shift)),
    ]
    args = [xp, xs, nw, mod_p, mod_p, mod_s, mod_s]
    out_specs = [pl.BlockSpec((TMM, D), lambda i: (i, 0))]
    out_shape = [jax.ShapeDtypeStruct((M_ROWS, D), BF16)]
    if with_dt:
        in_specs.append(pl.BlockSpec((pl.Element(LANE), pl.Element(D)), lambda i: (DT_COL, 0)))
        args.append(w_in_t)
        out_specs.append(pl.BlockSpec((TMM, LANE), lambda i: (i, 0)))
        out_shape.append(jax.ShapeDtypeStruct((M_ROWS, LANE), F32))
    return pl.pallas_call(
        functools.partial(_norm_kernel, with_dt),
        grid=(N_MT,),
        in_specs=in_specs,
        out_specs=out_specs,
        out_shape=out_shape,
        compiler_params=_params("arbitrary"),
        name="norm_mod_dt" if with_dt else "norm_mod",
    )(*args)


N_IN_BLOCKS = 13
UVG_ROW = DT_COL + HEADS
FIRST_CONV_BLOCK = 10


def _in_src_row(j):
    row = jnp.where(j < 2, j * TN_IN,
                    jnp.where(j < FIRST_CONV_BLOCK, UVG_ROW + (j - 2) * TN_IN,
                              D + (j - FIRST_CONV_BLOCK) * TN_IN))
    return pl.multiple_of(row, HEADS)


def _in_conv_block(j):
    return jnp.maximum(j - FIRST_CONV_BLOCK, 0)


def _inproj_kernel(a_ref, wt_ref, cw_ref, cb_ref, cst_ref, o_ref, csp_ref, css_ref, wbf_ref, acc_ref):
    j = pl.program_id(0)
    i = pl.program_id(1)
    rc = 64
    B = DEC_BATCH

    @pl.when(i == 0)
    def _():
        for r in range(TN_IN // LANE):
            rows = slice(r * LANE, (r + 1) * LANE)
            wbf_ref[rows, :] = wt_ref[rows, :].astype(BF16)

    def elementwise(fn, rows):
        def body():
            acc_ref[8:8 + rows, :] = _dot_nt(a_ref[0:rows, :], wbf_ref[...])
            for r in range(0, rows, rc):
                val = acc_ref[8 + r:8 + r + rc, :]
                o_ref[r:r + rc, :] = (val if fn is None else fn(val)).astype(o_ref.dtype)
        return body

    def conv_prompt():
        starts_sequence = i % MT_PER_SEQ == 0
        acc_ref[0:8, :] = jnp.where(starts_sequence, 0.0, acc_ref[TMM:TMM + 8, :])
        acc_ref[8:8 + TMM, :] = _dot_nt(a_ref[...], wbf_ref[...])
        cw, cb = cw_ref[...], cb_ref[...]
        for r in range(0, TMM, rc):
            conv = cb + cw[3:4] * acc_ref[8 + r:8 + r + rc, :]
            for k in range(3):
                conv = conv + cw[k:k + 1] * acc_ref[5 + k + r:5 + k + r + rc, :]
            o_ref[r:r + rc, :] = _silu(conv).astype(o_ref.dtype)
        csp_ref[0] = acc_ref[TMM + 5:TMM + 8, :]

    def conv_sample():
        acc_ref[8:8 + NS_ROWS, :] = _dot_nt(a_ref[0:NS_ROWS, :], wbf_ref[...])
        cw, cb = cw_ref[...], cb_ref[...]

        def pre(t, r):
            if t < 0:
                return cst_ref[t + 3, r:r + rc, :]
            return acc_ref[8 + t * B + r:8 + t * B + r + rc, :]

        for t in range(DEC_SEQ):
            for r in range(0, B, rc):
                conv = cb + cw[3:4] * pre(t, r)
                for k in range(3):
                    conv = conv + cw[k:k + 1] * pre(t - 3 + k, r)
                o_ref[t * B + r:t * B + r + rc, :] = _silu(conv).astype(o_ref.dtype)
        for t in range(1, DEC_SEQ):
            css_ref[t - 1] = acc_ref[8 + t * B:8 + (t + 1) * B, :]

    is_conv = j >= FIRST_CONV_BLOCK
    kinds = ((j < 2, _silu),
             (jnp.logical_and(j >= 2, j < 6), jax.nn.gelu),
             (jnp.logical_and(j >= 6, j < FIRST_CONV_BLOCK), None))
    for cond, fn in kinds:
        pl.when(jnp.logical_and(cond, i < NP_MT))(elementwise(fn, TMM))
        pl.when(jnp.logical_and(cond, i == NP_MT))(elementwise(fn, NS_ROWS))
    pl.when(jnp.logical_and(is_conv, i < NP_MT))(conv_prompt)
    pl.when(jnp.logical_and(is_conv, i == NP_MT))(conv_sample)


def _inproj_call(n1, w_in_t, cw, cb, cst_t):
    cblk = _in_conv_block
    seq = lambda j, i: jnp.where(j < FIRST_CONV_BLOCK, 0, _seq_of_mtile(i))
    return pl.pallas_call(
        _inproj_kernel,
        grid=(N_IN_BLOCKS, N_MT),
        in_specs=[pl.BlockSpec((TMM, D), lambda j, i: (i, 0)),
                  pl.BlockSpec((pl.Element(TN_IN), pl.Element(D)),
                               lambda j, i: (_in_src_row(j), 0)),
                  pl.BlockSpec((4, TN_IN), lambda j, i: (0, cblk(j))),
                  pl.BlockSpec((1, TN_IN), lambda j, i: (0, cblk(j))),
                  pl.BlockSpec((3, DEC_BATCH, TN_IN), lambda j, i: (0, 0, cblk(j)))],
        out_specs=[pl.BlockSpec((TMM, TN_IN), lambda j, i: (i, j)),
                   pl.BlockSpec((1, 3, TN_IN), lambda j, i: (seq(j, i), 0, cblk(j))),
                   pl.BlockSpec((3, DEC_BATCH, TN_IN), lambda j, i: (0, 0, cblk(j)))],
        out_shape=[jax.ShapeDtypeStruct((M_ROWS, N_IN_BLOCKS * TN_IN), BF16),
                   jax.ShapeDtypeStruct((BATCH, 3, CONV_DIM), F32),
                   jax.ShapeDtypeStruct((3, DEC_BATCH, CONV_DIM), F32)],
        scratch_shapes=[pltpu.VMEM((TN_IN, D), BF16), pltpu.VMEM((TMM + 8, TN_IN), F32)],
        compiler_params=_params("arbitrary", "arbitrary"),
        name="in_proj",
    )(n1, w_in_t, cw, cb, cst_t)


PZ, PU, PV, PGA, PGB, PX = 0, 1, 2, 3, 4, 5
PBC_1024 = 12


def _gated_group_norm(get_y, zact_ref, nw_ref, o_ref, rows=slice(None)):
    gw = D // GROUPS
    for g in range(GROUPS):
        cols = slice(g * gw, (g + 1) * gw)
        gg = _rms(get_y(cols) * zact_ref[rows, cols].astype(F32))
        o_ref[rows, cols] = (gg * nw_ref[:, cols]).astype(o_ref.dtype)


CHUNKS_PER_STEP = 4


def _ssd_prompt_kernel(z_ref, x_ref, bc_ref, dt_ref, hp_ref, hpc_ref, dskx_ref, nw_ref, triu_ref,
                       y_ref, ssm_ref, st_ref, yscr_ref):
    c = pl.program_id(1)
    T = CHUNK
    cdim = GROUPS * STATE

    @pl.when(c == 0)
    def _():
        st_ref[...] = jnp.zeros(st_ref.shape, F32)

    hp = hp_ref[...]
    a_col = -jnp.exp(hpc_ref[0:HEADS, 1:2])
    triu = triu_ref[...]
    row = lax.broadcasted_iota(jnp.int32, (T, T), 0)
    lane = lax.broadcasted_iota(jnp.int32, (T, T), 1)
    causal = row >= lane
    left = lane < HEADDIM
    mask_l = jnp.where(left, 1.0, 0.0).astype(BF16)
    nt = (((1,), (1,)), ((), ()))
    pairs_per_group = HEADS // GROUPS // 2

    for sub in range(CHUNKS_PER_STEP):
        rows = slice(sub * T, (sub + 1) * T)
        y_buf = yscr_ref.at[sub % 2]
        dt_t = _softplus((dt_ref[rows, :] + hp[0:1, :]).T[0:HEADS, :])
        cs_t = _dot_exact_rhs(dt_t * a_col, triu)
        rsub_t = cs_t - jnp.log(dt_t)
        cs = jnp.concatenate([cs_t, jnp.zeros((LANE - HEADS, T), F32)], axis=0).T

        for g in range(GROUPS):
            c_b = bc_ref[rows, cdim + g * STATE:cdim + (g + 1) * STATE]
            b_b = bc_ref[rows, g * STATE:(g + 1) * STATE]
            cb = lax.dot_general(c_b, b_b, nt, preferred_element_type=F32)
            b_t = b_b.astype(F32).T
            for k4 in range(pairs_per_group):
                k = g * pairs_per_group + k4
                cols = slice(k * LANE, (k + 1) * LANE)
                xpb = x_ref[rows, cols]
                x_lo = xpb * mask_l
                xbd = jnp.concatenate([x_lo, xpb - x_lo], axis=0)
                st = st_ref[k]
                yraw = _dot(c_b, st.astype(BF16))
                scores, bws, colbs, alasts = [], [], [], []
                for h in (2 * k, 2 * k + 1):
                    colb = jnp.broadcast_to(cs[:, h:h + 1], (T, T))
                    alast = cs_t[h:h + 1, T - 1:T]
                    decay_dt = jnp.exp(jnp.where(causal, colb - rsub_t[h:h + 1, :], -jnp.inf))
                    scores.append((cb * decay_dt).astype(BF16))
                    wrow = jnp.exp(alast - cs_t[h:h + 1, :]) * dt_t[h:h + 1, :]
                    bws.append((b_t * wrow).astype(BF16))
                    colbs.append(colb)
                    alasts.append(alast)
                ecol = jnp.exp(jnp.where(left, colbs[0], colbs[1]))
                elast = jnp.exp(jnp.where(left[0:1], alasts[0], alasts[1]))
                y_buf[:, cols] = (_dot(jnp.concatenate(scores, axis=1), xbd) + ecol * yraw
                                  + dskx_ref[:, cols] * xpb.astype(F32))
                st_ref[k] = elast * st + _dot(jnp.concatenate(bws, axis=1), xbd)
        _gated_group_norm(lambda cols: y_buf[:, cols], z_ref, nw_ref, y_ref, rows)

    @pl.when(c == N_CHUNKS // CHUNKS_PER_STEP - 1)
    def _():
        for k in range(HEADS // 2):
            ssm_ref[0, k] = st_ref[k].T


def _ssds_prep_step(tt, x_ref, bc_ref, dt_ref, hp_ref, dskx_ref, ex_ref, seg_ref,
                    ypre_ref, efull_ref, xw_ref, bb_ref, cc_ref, e3_ref):
    B = DEC_BATCH
    cdim = GROUPS * STATE
    blk = lambda t: slice(t * B, (t + 1) * B)
    x_of = lambda t: x_ref[blk(t), :].astype(F32)
    b_of = lambda t: bc_ref[blk(t), 0:cdim].astype(F32)
    c_of = lambda t: bc_ref[blk(t), cdim:2 * cdim].astype(F32)

    bb_ref[...] = b_of(tt)
    cc_ref[...] = c_of(tt)

    hp = hp_ref[...]
    a_neg = -jnp.exp(hp[1:2, :])
    dts, css = [], []
    run = None
    for t in range(DEC_SEQ):
        dt = _softplus(dt_ref[blk(t), :] + hp[0:1, :])
        run = dt * a_neg if run is None else run + dt * a_neg
        dts.append(dt)
        css.append(run)

    ex = ex_ref[...]
    e3_ref[...] = jnp.exp(css[-1])
    efull_ref[...] = _dot_exact_rhs(jnp.exp(css[tt]), ex)
    w_t = jnp.exp(css[-1] - css[tt]) * dts[tt]
    xw_ref[...] = x_of(tt) * _dot_exact_rhs(w_t, ex)

    seg = seg_ref[...]
    acc = dskx_ref[...] * x_of(tt)
    c_t = c_of(tt)
    for s in range(tt + 1):
        cbh = _dot_exact_rhs(c_t * b_of(s), seg)
        g_ts = cbh * jnp.exp(css[tt] - css[s]) * dts[s]
        acc = acc + _dot_exact_rhs(g_ts, ex) * x_of(s)
    ypre_ref[...] = acc


def _ssds_prep_kernel(*refs):
    t = pl.program_id(0)
    for tt in range(DEC_SEQ):
        pl.when(t == tt)(functools.partial(_ssds_prep_step, tt, *refs))


def _ssds_prep_call(proj, dt_raw, hp, dskx, ex, seg):
    full = lambda shape: pl.BlockSpec(shape, lambda t: (0,) * len(shape))
    step = lambda width: pl.BlockSpec((DEC_BATCH, width), lambda t: (t, 0))
    cdim = GROUPS * STATE
    return pl.pallas_call(
        _ssds_prep_kernel,
        grid=(DEC_SEQ,),
        in_specs=[pl.BlockSpec((NS_ROWS, D), lambda t: (NP_TILES, PX)),
                  pl.BlockSpec((NS_ROWS, 2 * cdim), lambda t: (NP_TILES, PBC_1024)),
                  pl.BlockSpec((NS_ROWS, LANE), lambda t: (NP_TILES, 0)),
                  full((8, LANE)), full((1, D)), full((LANE, D)), full((cdim, LANE))],
        out_specs=[step(D), step(D), step(D), step(cdim), step(cdim),
                   full((DEC_BATCH, LANE))],
        out_shape=[jax.ShapeDtypeStruct((NS_ROWS, D), F32),
                   jax.ShapeDtypeStruct((NS_ROWS, D), F32),
                   jax.ShapeDtypeStruct((NS_ROWS, D), F32),
                   jax.ShapeDtypeStruct((NS_ROWS, cdim), F32),
                   jax.ShapeDtypeStruct((NS_ROWS, cdim), F32),
                   jax.ShapeDtypeStruct((DEC_BATCH, LANE), F32)],
        compiler_params=_params("arbitrary"),
        name="ssd_sample_prep",
    )(proj, proj, dt_raw, hp, dskx, ex, seg)


SEQ_PER_STEP = 8


def _ssds_state_body(blk, e3_ref, st_ref, cc_ref, bb_ref, xw_ref, yo_ref, so_ref):
    nt = (((1,), (1,)), ((), ()))
    tn = (((0,), (0,)), ((), ()))
    hpg = HEADS // GROUPS
    gw = hpg * HEADDIM

    def rows_of(ref, s, cols):
        v = ref[:, s, cols]
        return jnp.concatenate([v, jnp.zeros((8 - DEC_SEQ, v.shape[-1]), F32)], axis=0).astype(BF16)

    for s in range(SEQ_PER_STEP):
        b = blk * SEQ_PER_STEP + s
        for g in range(GROUPS):
            h0 = st_ref[s, g]
            c_g = rows_of(cc_ref, s, slice(g * STATE, (g + 1) * STATE))
            yraw = lax.dot_general(c_g, h0.astype(BF16), nt, preferred_element_type=F32)
            yo_ref[:, s, g * gw:(g + 1) * gw] = yraw[0:DEC_SEQ]
            x_g = rows_of(xw_ref, s, slice(g * gw, (g + 1) * gw))
            b_g = rows_of(bb_ref, s, slice(g * STATE, (g + 1) * STATE))
            dh = lax.dot_general(x_g, b_g, tn, preferred_element_type=F32)
            for hh in range(hpg):
                rows = slice(hh * HEADDIM, (hh + 1) * HEADDIM)
                so_ref[s, g, rows, :] = e3_ref[b, g * hpg + hh] * h0[rows] + dh[rows]


PROMPT_STEPS = N_CHUNKS // CHUNKS_PER_STEP
assert BATCH * PROMPT_STEPS * SEQ_PER_STEP == DEC_BATCH


def _ssd_fused_kernel(z_ref, x_ref, bc_ref, dt_ref, hp_ref, hpc_ref, dskx_ref, nw_ref, triu_ref,
                      e3_ref, sti_ref, cc_ref, bb_ref, xw_ref,
                      y_ref, ssm_ref, yo_ref, so_ref, st_ref, yscr_ref):
    _ssd_prompt_kernel(z_ref, x_ref, bc_ref, dt_ref, hp_ref, hpc_ref, dskx_ref, nw_ref, triu_ref,
                       y_ref, ssm_ref, st_ref, yscr_ref)
    blk = pl.program_id(0) * PROMPT_STEPS + pl.program_id(1)
    _ssds_state_body(blk, e3_ref, sti_ref, cc_ref, bb_ref, xw_ref, yo_ref, so_ref)


def _ssd_fused_call(proj, dt_raw, hp, hpc, dskx, nw, triu, e3, state, cc_t, bb_t, xw_t):
    rows = CHUNK * CHUNKS_PER_STEP
    sb = SEQ_PER_STEP
    gw = (HEADS // GROUPS) * HEADDIM
    cdim = GROUPS * STATE
    step = lambda b, c: b * PROMPT_STEPS + c
    const = lambda b, c: (0, 0)
    tmajor = lambda width: pl.BlockSpec((DEC_SEQ, sb, width), lambda b, c: (0, step(b, c), 0))
    st_spec = pl.BlockSpec((sb, GROUPS, gw, STATE), lambda b, c: (step(b, c), 0, 0, 0))
    return pl.pallas_call(
        _ssd_fused_kernel,
        grid=(BATCH, PROMPT_STEPS),
        in_specs=[pl.BlockSpec((rows, D), lambda b, c: (step(b, c), PZ)),
                  pl.BlockSpec((rows, D), lambda b, c: (step(b, c), PX)),
                  pl.BlockSpec((rows, 2 * cdim), lambda b, c: (step(b, c), PBC_1024)),
                  pl.BlockSpec((rows, LANE), lambda b, c: (step(b, c), 0)),
                  pl.BlockSpec((8, LANE), const),
                  pl.BlockSpec((LANE, 8), const),
                  pl.BlockSpec((1, D), const),
                  pl.BlockSpec((1, D), const),
                  pl.BlockSpec((CHUNK, CHUNK), const),
                  pl.BlockSpec(memory_space=pltpu.SMEM),
                  st_spec, tmajor(cdim), tmajor(cdim), tmajor(D)],
        out_specs=[pl.BlockSpec((rows, D), lambda b, c: (step(b, c), 0)),
                   pl.BlockSpec((1, HEADS // 2, LANE, STATE), lambda b, c: (b, 0, 0, 0)),
                   tmajor(D), st_spec],
        out_shape=[jax.ShapeDtypeStruct((NP_ROWS, D), BF16),
                   jax.ShapeDtypeStruct((BATCH, HEADS // 2, LANE, STATE), F32),
                   jax.ShapeDtypeStruct((DEC_SEQ, DEC_BATCH, D), F32),
                   jax.ShapeDtypeStruct((DEC_BATCH, GROUPS, gw, STATE), F32)],
        scratch_shapes=[pltpu.VMEM((HEADS // 2, STATE, LANE), F32),
                        pltpu.VMEM((2, CHUNK, D), F32)],
        compiler_params=_params("arbitrary", "arbitrary"),
        name="ssd_prompt_and_sample_state",
    )(proj, proj, proj, dt_raw, hp, hpc, dskx, nw, triu,
      e3, state, cc_t.reshape(DEC_SEQ, DEC_BATCH, cdim), bb_t.reshape(DEC_SEQ, DEC_BATCH, cdim),
      xw_t.reshape(DEC_SEQ, DEC_BATCH, D))


def _ssds_post_kernel(ypre_ref, efull_ref, yo_ref, z_ref, nw_ref, o_ref):
    get_y = lambda cols: ypre_ref[:, cols] + efull_ref[:, cols] * yo_ref[:, cols]
    _gated_group_norm(get_y, z_ref, nw_ref, o_ref)


def _ssds_post_call(ypre, efull, yo_t, proj, nw):
    step = pl.BlockSpec((DEC_BATCH, D), lambda t: (t, 0))
    return pl.pallas_call(
        _ssds_post_kernel,
        grid=(DEC_SEQ,),
        in_specs=[step, step, step,
                  pl.BlockSpec((DEC_BATCH, D), lambda t: (NP_ROWS // DEC_BATCH + t, PZ)),
                  pl.BlockSpec((1, D), lambda t: (0, 0))],
        out_specs=step,
        out_shape=jax.ShapeDtypeStruct((NS_ROWS, D), BF16),
        compiler_params=_params("arbitrary"),
        name="ssd_sample_post",
    )(ypre, efull, yo_t, proj, nw)


def _mlp_kernel(u_ref, v_ref, lnw_ref, lnb_ref, ws_ref, bsx_ref, wsx_ref, bsx4_ref,
                y_ref, cv_ref, wm_ref, cvs_ref):
    i = pl.program_id(0)
    T = CHUNK

    @pl.when(i == 0)
    def _():
        row = lax.broadcasted_iota(jnp.int32, (T, T), 0)
        lane = lax.broadcasted_iota(jnp.int32, (T, T), 1)
        for g in range(MLP_GROUPS):
            wm_ref[g] = jnp.where(row >= lane, ws_ref[g], 0.0).astype(BF16)

    def vnorm(rows):
        vg = v_ref[rows, :].astype(F32)
        xc = vg - jnp.mean(vg, axis=-1, keepdims=True)
        y = xc * lax.rsqrt(jnp.mean(xc * xc, axis=-1, keepdims=True) + EPS)
        return y * lnw_ref[...] + lnb_ref[...]

    @pl.when(i < NP_MT)
    def _():
        for cc in range(TMM // T):
            rows = slice(cc * T, (cc + 1) * T)
            vnb = vnorm(rows).astype(BF16)
            for g in range(MLP_GROUPS):
                cols = slice(g * MLP_GROUP_DIM, (g + 1) * MLP_GROUP_DIM)
                sv = _dot(wm_ref[g], vnb[:, cols]) + bsx_ref[:, cols]
                y_ref[rows, cols] = (u_ref[rows, cols].astype(F32) * sv).astype(BF16)

    @pl.when(i == NP_MT)
    def _():
        B = DEC_BATCH
        for t in range(DEC_SEQ):
            rows = slice(t * B, (t + 1) * B)
            vn = vnorm(rows)
            cvs_ref[rows, :] = vn
            cv_ref[:, t, :] = vn
        for t in range(DEC_SEQ):
            rows = slice(t * B, (t + 1) * B)
            acc = bsx4_ref[t:t + 1, :]
            for s in range(t + 1):
                acc = acc + wsx_ref[4 * t + s:4 * t + s + 1, :] * cvs_ref[s * B:(s + 1) * B, :]
            y_ref[rows, :] = (u_ref[rows, :].astype(F32) * acc).astype(BF16)


def _mlp_call(proj, lnw, lnb, ws, bsx, wsx, bsx4):
    full = lambda shape: pl.BlockSpec(shape, lambda i: (0,) * len(shape))
    return pl.pallas_call(
        _mlp_kernel,
        grid=(N_MT,),
        in_specs=[pl.BlockSpec((TMM, D), lambda i: (i, PU)),
                  pl.BlockSpec((TMM, D), lambda i: (i, PV)),
                  full((1, D)), full((1, D)),
                  full((MLP_GROUPS, CHUNK, CHUNK)),
                  full((CHUNK, D)), full((16, D)), full((8, D))],
        out_specs=[pl.BlockSpec((TMM, D), lambda i: (i, 0)),
                   full((DEC_BATCH, DEC_SEQ, D))],
        out_shape=[jax.ShapeDtypeStruct((M_ROWS, D), BF16),
                   jax.ShapeDtypeStruct((DEC_BATCH, DEC_SEQ, D), F32)],
        scratch_shapes=[pltpu.VMEM((MLP_GROUPS, CHUNK, CHUNK), BF16),
                        pltpu.VMEM((NS_ROWS, D), F32)],
        compiler_params=_params("arbitrary"),
        name="gmlp",
    )(proj, proj, lnw, lnb, ws, bsx, wsx, bsx4)


def _cast_rows(src_ref, dst_ref, chunk=256):
    def body(r, carry):
        rows = pl.ds(pl.multiple_of(r * chunk, chunk), chunk)
        dst_ref[rows, :] = src_ref[rows, :].astype(dst_ref.dtype)
        return carry
    lax.fori_loop(0, src_ref.shape[0] // chunk, body, 0)


def _per_tile(i, prompt_fn, sample_fn):
    pl.when(i < NP_MT)(prompt_fn)
    pl.when(i == NP_MT)(sample_fn)


def _merge_kernel(ysp_ref, yss_ref, ym_ref, ga_ref, gb_ref, w1_ref, w2_ref, o_ref, w1b, w2b):
    i = pl.program_id(1)

    @pl.when(i == 0)
    def _():
        _cast_rows(w1_ref, w1b)
        _cast_rows(w2_ref, w2b)

    def emit(ys, rows):
        a1 = _dot(ys, w1b[...])
        a2 = _dot(ym_ref[rows, :], w2b[...])
        o_ref[rows, :] = (jax.nn.sigmoid(ga_ref[rows, :].astype(F32)) * a1
                          + jax.nn.sigmoid(gb_ref[rows, :].astype(F32)) * a2).astype(BF16)

    _per_tile(i,
              lambda: emit(ysp_ref[...], slice(None)),
              lambda: emit(yss_ref[...], slice(0, NS_ROWS)))


def _merge_call(ysp, yss, ym, proj, w1, w2):
    tn = 512
    nb = D // tn
    return pl.pallas_call(
        _merge_kernel,
        grid=(nb, N_MT),
        in_specs=[pl.BlockSpec((TMM, D), lambda j, i: (jnp.minimum(i, NP_MT - 1), 0)),
                  pl.BlockSpec((NS_ROWS, D), lambda j, i: (0, 0)),
                  pl.BlockSpec((TMM, D), lambda j, i: (i, 0)),
                  pl.BlockSpec((TMM, tn), lambda j, i: (i, PGA * nb + j)),
                  pl.BlockSpec((TMM, tn), lambda j, i: (i, PGB * nb + j)),
                  pl.BlockSpec((D, tn), lambda j, i: (0, j)),
                  pl.BlockSpec((D, tn), lambda j, i: (0, j))],
        out_specs=pl.BlockSpec((TMM, tn), lambda j, i: (i, j)),
        out_shape=jax.ShapeDtypeStruct((M_ROWS, D), BF16),
        scratch_shapes=[pltpu.VMEM((D, tn), BF16), pltpu.VMEM((D, tn), BF16)],
        compiler_params=_params("arbitrary", "arbitrary"),
        name="branch_merge",
    )(ysp, yss, ym, proj, proj, w1, w2)


def _resid_kernel(a_ref, w_ref, rp_ref, rs_ref, gp_ref, gs_ref, o_ref, wb):
    i = pl.program_id(1)

    @pl.when(i == 0)
    def _():
        _cast_rows(w_ref, wb)

    def prompt():
        o_ref[...] = rp_ref[...] + gp_ref[0] * _dot(a_ref[...], wb[...])

    def sample():
        acc = _dot(a_ref[0:NS_ROWS, :], wb[...])
        tn = acc.shape[-1]
        acc3 = acc.reshape(DEC_SEQ, DEC_BATCH, tn) * gs_ref[...][None]
        o_ref[0:NS_ROWS, :] = _sample_rows(rs_ref) + acc3.reshape(NS_ROWS, tn)

    _per_tile(i, prompt, sample)


def _resid_call(a, w, rp, rs, rs_block, mod_p, mod_s, k_gate, tn, name, single_buffer_w=False):
    kdim = a.shape[1]
    nb = D // tn
    w_mode = dict(pipeline_mode=pl.Buffered(1)) if single_buffer_w else {}
    if rs.ndim == 3:
        rs_spec = pl.BlockSpec((DEC_BATCH, DEC_SEQ, tn), lambda j, i: (0, 0, j))
    else:
        rs_spec = pl.BlockSpec((NS_ROWS, tn), lambda j, i: (rs_block, j))
    return pl.pallas_call(
        _resid_kernel,
        grid=(nb, N_MT),
        in_specs=[pl.BlockSpec((TMM, kdim), lambda j, i: (i, 0)),
                  pl.BlockSpec((kdim, tn), lambda j, i: (0, j), **w_mode),
                  pl.BlockSpec((TMM, tn), lambda j, i: (jnp.minimum(i, NP_MT - 1), j)),
                  rs_spec,
                  pl.BlockSpec((1, 1, tn), lambda j, i: (_seq_of_mtile(i), 0, k_gate * nb + j)),
                  pl.BlockSpec((DEC_BATCH, tn), lambda j, i: (0, k_gate * nb + j))],
        out_specs=pl.BlockSpec((TMM, tn), lambda j, i: (i, j)),
        out_shape=jax.ShapeDtypeStruct((M_ROWS, D), F32),
        scratch_shapes=[pltpu.VMEM((kdim, tn), BF16)],
        compiler_params=_params("arbitrary", "arbitrary"),
        name=name,
    )(a, w, rp, rs, mod_p, mod_s)


TN_FF = 512
N_FF_BLOCKS = D_FF // TN_FF


def _up_kernel(a_ref, wa_ref, wv_ref, cw_ref, cb_ref, fst_ref,
               h_ref, fcp_ref, fcs_ref, wab, wvb, acc_a, acc_v):
    i = pl.program_id(1)
    rc = 64

    @pl.when(i == 0)
    def _():
        _cast_rows(wa_ref, wab)
        _cast_rows(wv_ref, wvb)

    def prompt():
        starts_sequence = i % MT_PER_SEQ == 0
        acc_a[0:8, :] = jnp.where(starts_sequence, 0.0, acc_a[TMM:TMM + 8, :])
        x = a_ref[...]
        acc_a[8:8 + TMM, :] = _dot(x, wab[...])
        acc_v[...] = _dot(x, wvb[...])
        cw, cb = cw_ref[...], cb_ref[...]
        for r in range(0, TMM, rc):
            conv = (cb + cw[2:3] * acc_a[8 + r:8 + r + rc, :]
                    + cw[1:2] * acc_a[7 + r:7 + r + rc, :]
                    + cw[0:1] * acc_a[6 + r:6 + r + rc, :])
            h_ref[r:r + rc, :] = (jax.nn.gelu(conv) * acc_v[r:r + rc, :]).astype(BF16)
        fcp_ref[0] = acc_a[TMM + 6:TMM + 8, :]

    def sample():
        B = DEC_BATCH
        x = a_ref[0:NS_ROWS, :]
        acc_a[8:8 + NS_ROWS, :] = _dot(x, wab[...])
        acc_v[0:NS_ROWS, :] = _dot(x, wvb[...])
        cw, cb = cw_ref[...], cb_ref[...]

        def pre(t, r):
            if t < 0:
                return fst_ref[r:r + rc, t + 2, :]
            return acc_a[8 + t * B + r:8 + t * B + r + rc, :]

        for t in range(DEC_SEQ):
            for r in range(0, B, rc):
                conv = cb + cw[2:3] * pre(t, r) + cw[1:2] * pre(t - 1, r) + cw[0:1] * pre(t - 2, r)
                h_ref[t * B + r:t * B + r + rc, :] = (
                    jax.nn.gelu(conv) * acc_v[t * B + r:t * B + r + rc, :]).astype(BF16)
        fcs_ref[:, 0, :] = acc_a[8 + 2 * B:8 + 3 * B, :]
        fcs_ref[:, 1, :] = acc_a[8 + 3 * B:8 + 4 * B, :]

    _per_tile(i, prompt, sample)


def _up_call(n2, w_up, cw, cb, fst):
    return pl.pallas_call(
        _up_kernel,
        grid=(N_FF_BLOCKS, N_MT),
        in_specs=[pl.BlockSpec((TMM, D), lambda j, i: (i, 0)),
                  pl.BlockSpec((D, TN_FF), lambda j, i: (0, j)),
                  pl.BlockSpec((D, TN_FF), lambda j, i: (0, N_FF_BLOCKS + j)),
                  pl.BlockSpec((3, TN_FF), lambda j, i: (0, j)),
                  pl.BlockSpec((1, TN_FF), lambda j, i: (0, j)),
                  pl.BlockSpec((DEC_BATCH, 2, TN_FF), lambda j, i: (0, 0, j))],
        out_specs=[pl.BlockSpec((TMM, TN_FF), lambda j, i: (i, j)),
                   pl.BlockSpec((1, 2, TN_FF), lambda j, i: (_seq_of_mtile(i), 0, j)),
                   pl.BlockSpec((DEC_BATCH, 2, TN_FF), lambda j, i: (0, 0, j))],
        out_shape=[jax.ShapeDtypeStruct((M_ROWS, D_FF), BF16),
                   jax.ShapeDtypeStruct((BATCH, 2, D_FF), F32),
                   jax.ShapeDtypeStruct((DEC_BATCH, 2, D_FF), F32)],
        scratch_shapes=[pltpu.VMEM((D, TN_FF), BF16), pltpu.VMEM((D, TN_FF), BF16),
                        pltpu.VMEM((TMM + 8, TN_FF), F32), pltpu.VMEM((TMM, TN_FF), F32)],
        compiler_params=_params("arbitrary", "arbitrary"),
        name="ffn_up",
    )(n2, w_up, w_up, cw, cb, fst)


def _final_kernel(x_ref, w_ref, yp_ref, ys_ref):
    i = pl.program_id(0)

    @pl.when(i < NP_MT)
    def _():
        for r in range(0, TMM, TM):
            yp_ref[r:r + TM, :] = _rms(x_ref[r:r + TM, :]) * w_ref[...]

    @pl.when(i == NP_MT)
    def _():
        y = _rms(x_ref[0:NS_ROWS, :]) * w_ref[...]
        for t in range(DEC_SEQ):
            ys_ref[:, t, :] = y[t * DEC_BATCH:(t + 1) * DEC_BATCH]


def _final_call(x3, w):
    return pl.pallas_call(
        _final_kernel,
        grid=(N_MT,),
        in_specs=[pl.BlockSpec((TMM, D), lambda i: (i, 0)),
                  pl.BlockSpec((1, D), lambda i: (0, 0))],
        out_specs=[pl.BlockSpec((TMM, D), lambda i: (jnp.minimum(i, NP_MT - 1), 0)),
                   pl.BlockSpec((DEC_BATCH, DEC_SEQ, D), lambda i: (0, 0, 0))],
        out_shape=[jax.ShapeDtypeStruct((NP_ROWS, D), F32),
                   jax.ShapeDtypeStruct((DEC_BATCH, DEC_SEQ, D), F32)],
        compiler_params=_params("arbitrary"),
        name="final_norm",
    )(x3, w)


def _to_time_major(a):
    return jnp.transpose(a, (1, 0, 2))


def kernel(x_prompt, x_sample, state_ssm, state_ssd_conv, state_ffn_conv, c_prompt, c_sample,
           norm1_w, w_ada, b_ada, w_in, ssd_conv_w, ssd_conv_b, dt_bias, a_log, d_skip,
           ssd_norm_w, mlp_ln_w, mlp_ln_b, w_spatial, b_spatial, w_ssd_o, w_mlp_o, w_out,
           norm2_w, w_up, ffn_conv_w, ffn_conv_b, w_down, final_norm_w):
    assert w_in.shape[0] == 1, "single-layer trunk"
    row = lambda v: v.reshape(1, -1)

    xp = x_prompt.reshape(NP_ROWS, D)
    xs = x_sample

    mod_p, mod_s = _ada_call(c_prompt, c_sample, w_ada[0], row(b_ada[0]))
    K_SHIFT1, K_SCALE1, K_GATE1, K_SHIFT2, K_SCALE2, K_GATE2 = range(6)

    xp_spec = pl.BlockSpec((TMM, D), lambda i: (jnp.minimum(i, NP_MT - 1), 0))
    xs_spec = pl.BlockSpec((DEC_BATCH, DEC_SEQ, D), lambda i: (0, 0, 0))
    w_in_t = w_in[0].T
    n1, dt_raw = _norm_call(xp, xp_spec, xs, xs_spec, row(norm1_w[0]), mod_p, mod_s,
                            K_SCALE1, K_SHIFT1, w_in_t)
    cst_t = _to_time_major(state_ssd_conv[0])
    proj, cst_p, ncs_t = _inproj_call(n1, w_in_t, ssd_conv_w[0], row(ssd_conv_b[0]), cst_t)

    hp = jnp.pad(jnp.stack([dt_bias[0], a_log[0], d_skip[0]]), ((0, 5), (0, LANE - HEADS)))
    hpc = hp.T
    triu = jnp.asarray(np.triu(np.ones((CHUNK, CHUNK), np.float32)), dtype=BF16)
    dskx = row(jnp.repeat(d_skip[0], HEADDIM))
    nw = row(ssd_norm_w[0])

    head_of_col = np.arange(D) // HEADDIM
    ex = jnp.asarray(np.arange(LANE)[:, None] == head_of_col[None, :], dtype=BF16)
    grp_of_head = np.arange(LANE) // (HEADS // GROUPS)
    seg = jnp.asarray((np.arange(GROUPS * STATE)[:, None] // STATE == grp_of_head[None, :])
                      & (np.arange(LANE)[None, :] < HEADS), dtype=BF16)
    ypre, efull, xw_t, bb_t, cc_t, e3 = _ssds_prep_call(proj, dt_raw, hp, dskx, ex, seg)
    gw = (HEADS // GROUPS) * HEADDIM
    y_ssd_p, ssm_p, yo_t, ssm_s = _ssd_fused_call(
        proj, dt_raw, hp, hpc, dskx, nw, triu,
        e3[:, :HEADS], state_ssm[0].reshape(DEC_BATCH, GROUPS, gw, STATE), cc_t, bb_t, xw_t)
    y_ssd_s = _ssds_post_call(ypre, efull, yo_t.reshape(NS_ROWS, D), proj, nw)

    per_col = lambda a: jnp.repeat(a, MLP_GROUP_DIM, axis=1)
    bsx = per_col(b_spatial[0][:, :CHUNK].T)
    ws4 = w_spatial[0][:, :DEC_SEQ, :DEC_SEQ]
    wsx = per_col(jnp.transpose(ws4, (1, 2, 0)).reshape(DEC_SEQ * DEC_SEQ, MLP_GROUPS))
    bsx4 = jnp.pad(bsx[:DEC_SEQ], ((0, 8 - DEC_SEQ), (0, 0)))
    y_mlp, cv_s = _mlp_call(proj, row(mlp_ln_w[0]), row(mlp_ln_b[0]), w_spatial[0], bsx, wsx, bsx4)

    mixed = _merge_call(y_ssd_p, y_ssd_s, y_mlp, proj, w_ssd_o[0], w_mlp_o[0])
    x2 = _resid_call(mixed, w_out[0], xp, xs, 0, mod_p, mod_s, K_GATE1, 1024, "out_proj")

    x2p_spec = pl.BlockSpec((TMM, D), lambda i: (jnp.minimum(i, NP_MT - 1), 0))
    x2s_spec = pl.BlockSpec((NS_ROWS, D), lambda i: (NP_TILES, 0))
    (n2,) = _norm_call(x2, x2p_spec, x2, x2s_spec, row(norm2_w[0]), mod_p, mod_s,
                       K_SCALE2, K_SHIFT2, None)
    h, ffn_p, ffn_s = _up_call(n2, w_up[0], ffn_conv_w[0], row(ffn_conv_b[0]), state_ffn_conv[0])
    x3 = _resid_call(h, w_down[0], x2, x2, NP_TILES, mod_p, mod_s, K_GATE2, 512, "ffn_down",
                     single_buffer_w=True)

    y_p, y_s = _final_call(x3, row(final_norm_w))

    from_t = lambda a, t, c: jnp.transpose(a.reshape(t, DEC_BATCH, c), (1, 0, 2))
    return (y_p.reshape(BATCH, SEQ, D),
            y_s,
            ssm_p.reshape(1, BATCH, HEADS, HEADDIM, STATE),
            ssm_s.reshape(1, DEC_BATCH, HEADS, HEADDIM, STATE),
            cst_p[None],
            from_t(ncs_t, 3, CONV_DIM)[None],
            ffn_p[None],
            ffn_s[None],
            cv_s[None])
```

```python
import functools

import jax
import jax.numpy as jnp
import numpy as np
from jax import lax
from jax.experimental import pallas as pl
from jax.experimental.pallas import tpu as pltpu

F32 = jnp.float32
BF16 = jnp.bfloat16

D = 2048
BATCH, SEQ = 4, 2048
DEC_BATCH, DEC_SEQ = 128, 4
NP_ROWS = BATCH * SEQ
NS_ROWS = DEC_BATCH * DEC_SEQ
M_ROWS = NP_ROWS + NS_ROWS
TM = 512
NP_TILES = NP_ROWS // TM
TMM = 1024
NP_MT = NP_ROWS // TMM
N_MT = NP_MT + 1
MT_PER_SEQ = SEQ // TMM
HEADS, HEADDIM, GROUPS, STATE = 32, 64, 4, 128
CHUNK = 128
N_CHUNKS = SEQ // CHUNK
CONV_DIM = D + 2 * GROUPS * STATE
MLP_GROUPS = 8
MLP_GROUP_DIM = D // MLP_GROUPS
D_FF = 5632
EPS = 1e-6
DT_COL = D + CONV_DIM
TN_IN = 1024
LANE = 128
VMEM_LIMIT = 56 * 1024 * 1024


def _params(*sem, flags=None):
    return pltpu.CompilerParams(dimension_semantics=sem, vmem_limit_bytes=VMEM_LIMIT, flags=flags)


def _dot(a, b):
    return jnp.dot(a, b, preferred_element_type=F32)


def _dot_nt(a, b_t):
    return lax.dot_general(a, b_t, (((1,), (1,)), ((), ())), preferred_element_type=F32)


def _split_bf16(v, terms):
    out = []
    r = v
    for _ in range(terms):
        p = r.astype(BF16)
        out.append(p)
        r = r - p.astype(F32)
    return out


def _dot_exact_rhs(v, e, terms=3):
    acc = None
    for p in _split_bf16(v, terms):
        d = _dot(p, e)
        acc = d if acc is None else acc + d
    return acc


def _silu(x):
    return x * jax.nn.sigmoid(x)


def _softplus(x):
    return jnp.maximum(x, 0.0) + jnp.log1p(jnp.exp(-jnp.abs(x)))


def _rms(x):
    return x * lax.rsqrt(jnp.mean(x * x, axis=-1, keepdims=True) + EPS)


def _ada_kernel(cp_ref, cs_ref, w_ref, b_ref, op_ref, os_ref):
    w = w_ref[...].astype(BF16)
    cp8 = jnp.concatenate([cp_ref[...], jnp.zeros((8 - BATCH, D), F32)], axis=0)
    op_ref[:, 0, :] = (_dot(_silu(cp8).astype(BF16), w) + b_ref[...])[0:BATCH]
    os_ref[...] = _dot(_silu(cs_ref[...]).astype(BF16), w) + b_ref[...]


def _ada_call(c_prompt, c_sample, w, b):
    tn = 1024
    return pl.pallas_call(
        _ada_kernel,
        grid=(6 * D // tn,),
        in_specs=[pl.BlockSpec((BATCH, D), lambda j: (0, 0)),
                  pl.BlockSpec((DEC_BATCH, D), lambda j: (0, 0)),
                  pl.BlockSpec((D, tn), lambda j: (0, j)),
                  pl.BlockSpec((1, tn), lambda j: (0, j))],
        out_specs=[pl.BlockSpec((BATCH, 1, tn), lambda j: (0, 0, j)),
                   pl.BlockSpec((DEC_BATCH, tn), lambda j: (0, j))],
        out_shape=[jax.ShapeDtypeStruct((BATCH, 1, 6 * D), F32),
                   jax.ShapeDtypeStruct((DEC_BATCH, 6 * D), F32)],
        compiler_params=_params("arbitrary"),
        name="ada_mod",
    )(c_prompt, c_sample, w, b)


def _sample_rows(ref):
    if len(ref.shape) == 2:
        return ref[...]
    return jnp.concatenate([ref[:, t, :] for t in range(DEC_SEQ)], axis=0)


def _norm_kernel(with_dt, xp_ref, xs_ref, nw_ref, scp_ref, shp_ref, scs_ref, shs_ref, *rest):
    if with_dt:
        wdt_ref, n_ref, dt_ref = rest
    else:
        (n_ref,) = rest
    i = pl.program_id(0)

    def emit(n, rows):
        nb = n.astype(BF16)
        n_ref[rows, :] = nb
        if with_dt:
            dt_ref[rows, :] = lax.dot_general(nb, wdt_ref[...].astype(BF16),
                                              (((1,), (1,)), ((), ())),
                                              preferred_element_type=F32)

    @pl.when(i < NP_MT)
    def _():
        for r in range(0, TMM, TM):
            rows = slice(r, r + TM)
            y = _rms(xp_ref[rows, :]) * nw_ref[...]
            emit(y * (1.0 + scp_ref[0]) + shp_ref[0], rows)

    @pl.when(i == NP_MT)
    def _():
        y = _rms(_sample_rows(xs_ref)) * nw_ref[...]
        y3 = y.reshape(DEC_SEQ, DEC_BATCH, D)
        emit((y3 * (1.0 + scs_ref[...])[None] + shs_ref[...][None]).reshape(NS_ROWS, D),
             slice(0, NS_ROWS))


def _seq_of_mtile(i):
    return jnp.minimum(i // MT_PER_SEQ, BATCH - 1)


def _norm_call(xp, xp_spec, xs, xs_spec, nw, mod_p, mod_s, k_scale, k_shift, w_in_t):
    with_dt = w_in_t is not None
    in_specs = [
        xp_spec, xs_spec,
        pl.BlockSpec((1, D), lambda i: (0, 0)),
        pl.BlockSpec((1, 1, D), lambda i: (_seq_of_mtile(i), 0, k_scale)),
        pl.BlockSpec((1, 1, D), lambda i: (_seq_of_mtile(i), 0, k_shift)),
        pl.BlockSpec((DEC_BATCH, D), lambda i: (0, k_scale)),
        pl.BlockSpec((DEC_BATCH, D), lambda i: (0, k_shift)),
    ]
    args = [xp, xs, nw, mod_p, mod_p, mod_s, mod_s]
    out_specs = [pl.BlockSpec((TMM, D), lambda i: (i, 0))]
    out_shape = [jax.ShapeDtypeStruct((M_ROWS, D), BF16)]
    if with_dt:
        in_specs.append(pl.BlockSpec((pl.Element(LANE), pl.Element(D)), lambda i: (DT_COL, 0)))
        args.append(w_in_t)
        out_specs.append(pl.BlockSpec((TMM, LANE), lambda i: (i, 0)))
        out_shape.append(jax.ShapeDtypeStruct((M_ROWS, LANE), F32))
    return pl.pallas_call(
        functools.partial(_norm_kernel, with_dt),
        grid=(N_MT,),
        in_specs=in_specs,
        out_specs=out_specs,
        out_shape=out_shape,
        compiler_params=_params("arbitrary"),
        name="norm_mod_dt" if with_dt else "norm_mod",
    )(*args)


N_IN_BLOCKS = 13
UVG_ROW = DT_COL + HEADS
FIRST_CONV_BLOCK = 10


def _in_src_row(j):
    row = jnp.where(j < 2, j * TN_IN,
                    jnp.where(j < FIRST_CONV_BLOCK, UVG_ROW + (j - 2) * TN_IN,
                              D + (j - FIRST_CONV_BLOCK) * TN_IN))
    return pl.multiple_of(row, HEADS)


def _in_conv_block(j):
    return jnp.maximum(j - FIRST_CONV_BLOCK, 0)


def _inproj_kernel(a_ref, wt_ref, cw_ref, cb_ref, cst_ref, o_ref, csp_ref, css_ref, wbf_ref, acc_ref):
    j = pl.program_id(0)
    i = pl.program_id(1)
    rc = 64
    B = DEC_BATCH

    @pl.when(i == 0)
    def _():
        for r in range(TN_IN // LANE):
            rows = slice(r * LANE, (r + 1) * LANE)
            wbf_ref[rows, :] = wt_ref[rows, :].astype(BF16)

    def elementwise(fn, rows):
        def body():
            acc_ref[8:8 + rows, :] = _dot_nt(a_ref[0:rows, :], wbf_ref[...])
            for r in range(0, rows, rc):
                val = acc_ref[8 + r:8 + r + rc, :]
                o_ref[r:r + rc, :] = (val if fn is None else fn(val)).astype(o_ref.dtype)
        return body

    def conv_prompt():
        starts_sequence = i % MT_PER_SEQ == 0
        acc_ref[0:8, :] = jnp.where(starts_sequence, 0.0, acc_ref[TMM:TMM + 8, :])
        acc_ref[8:8 + TMM, :] = _dot_nt(a_ref[...], wbf_ref[...])
        cw, cb = cw_ref[...], cb_ref[...]
        for r in range(0, TMM, rc):
            conv = cb + cw[3:4] * acc_ref[8 + r:8 + r + rc, :]
            for k in range(3):
                conv = conv + cw[k:k + 1] * acc_ref[5 + k + r:5 + k + r + rc, :]
            o_ref[r:r + rc, :] = _silu(conv).astype(o_ref.dtype)
        csp_ref[0] = acc_ref[TMM + 5:TMM + 8, :]

    def conv_sample():
        acc_ref[8:8 + NS_ROWS, :] = _dot_nt(a_ref[0:NS_ROWS, :], wbf_ref[...])
        cw, cb = cw_ref[...], cb_ref[...]

        def pre(t, r):
            if t < 0:
                return cst_ref[r:r + rc, t + 3, :]
            return acc_ref[8 + t * B + r:8 + t * B + r + rc, :]

        for t in range(DEC_SEQ):
            for r in range(0, B, rc):
                conv = cb + cw[3:4] * pre(t, r)
                for k in range(3):
                    conv = conv + cw[k:k + 1] * pre(t - 3 + k, r)
                o_ref[t * B + r:t * B + r + rc, :] = _silu(conv).astype(o_ref.dtype)
        for t in range(1, DEC_SEQ):
            css_ref[:, t - 1, :] = acc_ref[8 + t * B:8 + (t + 1) * B, :]

    is_conv = j >= FIRST_CONV_BLOCK
    kinds = ((j < 2, _silu),
             (jnp.logical_and(j >= 2, j < 6), jax.nn.gelu),
             (jnp.logical_and(j >= 6, j < FIRST_CONV_BLOCK), None))
    for cond, fn in kinds:
        pl.when(jnp.logical_and(cond, i < NP_MT))(elementwise(fn, TMM))
        pl.when(jnp.logical_and(cond, i == NP_MT))(elementwise(fn, NS_ROWS))
    pl.when(jnp.logical_and(is_conv, i < NP_MT))(conv_prompt)
    pl.when(jnp.logical_and(is_conv, i == NP_MT))(conv_sample)


def _inproj_call(n1, w_in_t, cw, cb, cst):
    cblk = _in_conv_block
    seq = lambda j, i: jnp.where(j < FIRST_CONV_BLOCK, 0, _seq_of_mtile(i))
    return pl.pallas_call(
        _inproj_kernel,
        grid=(N_IN_BLOCKS, N_MT),
        in_specs=[pl.BlockSpec((TMM, D), lambda j, i: (i, 0)),
                  pl.BlockSpec((pl.Element(TN_IN), pl.Element(D)),
                               lambda j, i: (_in_src_row(j), 0)),
                  pl.BlockSpec((4, TN_IN), lambda j, i: (0, cblk(j))),
                  pl.BlockSpec((1, TN_IN), lambda j, i: (0, cblk(j))),
                  pl.BlockSpec((DEC_BATCH, 3, TN_IN), lambda j, i: (0, 0, cblk(j)))],
        out_specs=[pl.BlockSpec((TMM, TN_IN), lambda j, i: (i, j)),
                   pl.BlockSpec((1, 3, TN_IN), lambda j, i: (seq(j, i), 0, cblk(j))),
                   pl.BlockSpec((DEC_BATCH, 3, TN_IN), lambda j, i: (0, 0, cblk(j)))],
        out_shape=[jax.ShapeDtypeStruct((M_ROWS, N_IN_BLOCKS * TN_IN), BF16),
                   jax.ShapeDtypeStruct((BATCH, 3, CONV_DIM), F32),
                   jax.ShapeDtypeStruct((DEC_BATCH, 3, CONV_DIM), F32)],
        scratch_shapes=[pltpu.VMEM((TN_IN, D), BF16), pltpu.VMEM((TMM + 8, TN_IN), F32)],
        compiler_params=_params("arbitrary", "arbitrary"),
        name="in_proj",
    )(n1, w_in_t, cw, cb, cst)


PZ, PU, PV, PGA, PGB, PX = 0, 1, 2, 3, 4, 5
PBC_1024 = 12


def _gated_group_norm(get_y, zact_ref, nw_ref, o_ref, rows=slice(None)):
    gw = D // GROUPS
    for g in range(GROUPS):
        cols = slice(g * gw, (g + 1) * gw)
        gg = _rms(get_y(cols) * zact_ref[rows, cols].astype(F32))
        o_ref[rows, cols] = (gg * nw_ref[:, cols]).astype(o_ref.dtype)


CHUNKS_PER_STEP = 4


def _ssd_prompt_kernel(z_ref, x_ref, bc_ref, dt_ref, hp_ref, hpc_ref, dskx_ref, nw_ref, triu_ref,
                       y_ref, ssm_ref, st_ref, yscr_ref):
    c = pl.program_id(1)
    T = CHUNK
    cdim = GROUPS * STATE

    @pl.when(c == 0)
    def _():
        st_ref[...] = jnp.zeros(st_ref.shape, F32)

    hp = hp_ref[...]
    a_col = -jnp.exp(hpc_ref[0:HEADS, 1:2])
    triu = triu_ref[...]
    row = lax.broadcasted_iota(jnp.int32, (T, T), 0)
    lane = lax.broadcasted_iota(jnp.int32, (T, T), 1)
    causal = row >= lane
    left = lane < HEADDIM
    mask_l = jnp.where(left, 1.0, 0.0).astype(BF16)
    nt = (((1,), (1,)), ((), ()))
    pairs_per_group = HEADS // GROUPS // 2

    for sub in range(CHUNKS_PER_STEP):
        rows = slice(sub * T, (sub + 1) * T)
        y_buf = yscr_ref.at[sub % 2]
        dt_t = _softplus((dt_ref[rows, :] + hp[0:1, :]).T[0:HEADS, :])
        cs_t = _dot_exact_rhs(dt_t * a_col, triu)
        rsub_t = cs_t - jnp.log(dt_t)
        cs = jnp.concatenate([cs_t, jnp.zeros((LANE - HEADS, T), F32)], axis=0).T

        for g in range(GROUPS):
            c_b = bc_ref[rows, cdim + g * STATE:cdim + (g + 1) * STATE]
            b_b = bc_ref[rows, g * STATE:(g + 1) * STATE]
            cb = lax.dot_general(c_b, b_b, nt, preferred_element_type=F32)
            b_t = b_b.astype(F32).T
            for k4 in range(pairs_per_group):
                k = g * pairs_per_group + k4
                cols = slice(k * LANE, (k + 1) * LANE)
                xpb = x_ref[rows, cols]
                x_lo = xpb * mask_l
                xbd = jnp.concatenate([x_lo, xpb - x_lo], axis=0)
                st = st_ref[k]
                yraw = _dot(c_b, st.astype(BF16))
                scores, bws, colbs, alasts = [], [], [], []
                for h in (2 * k, 2 * k + 1):
                    colb = jnp.broadcast_to(cs[:, h:h + 1], (T, T))
                    alast = cs_t[h:h + 1, T - 1:T]
                    decay_dt = jnp.exp(jnp.where(causal, colb - rsub_t[h:h + 1, :], -jnp.inf))
                    scores.append((cb * decay_dt).astype(BF16))
                    wrow = jnp.exp(alast - cs_t[h:h + 1, :]) * dt_t[h:h + 1, :]
                    bws.append((b_t * wrow).astype(BF16))
                    colbs.append(colb)
                    alasts.append(alast)
                ecol = jnp.exp(jnp.where(left, colbs[0], colbs[1]))
                elast = jnp.exp(jnp.where(left[0:1], alasts[0], alasts[1]))
                y_buf[:, cols] = (_dot(jnp.concatenate(scores, axis=1), xbd) + ecol * yraw
                                  + dskx_ref[:, cols] * xpb.astype(F32))
                st_ref[k] = elast * st + _dot(jnp.concatenate(bws, axis=1), xbd)
        _gated_group_norm(lambda cols: y_buf[:, cols], z_ref, nw_ref, y_ref, rows)

    @pl.when(c == N_CHUNKS // CHUNKS_PER_STEP - 1)
    def _():
        for k in range(HEADS // 2):
            ssm_ref[0, k] = st_ref[k].T


def _ssds_prep_step(tt, x_ref, bc_ref, dt_ref, hp_ref, dskx_ref, ex_ref, seg_ref,
                    ypre_ref, efull_ref, xw_ref, bb_ref, cc_ref, e3_ref):
    B = DEC_BATCH
    cdim = GROUPS * STATE
    blk = lambda t: slice(t * B, (t + 1) * B)
    x_of = lambda t: x_ref[blk(t), :].astype(F32)
    b_of = lambda t: bc_ref[blk(t), 0:cdim].astype(F32)
    c_of = lambda t: bc_ref[blk(t), cdim:2 * cdim].astype(F32)

    bb_ref[...] = b_of(tt)
    cc_ref[...] = c_of(tt)

    hp = hp_ref[...]
    a_neg = -jnp.exp(hp[1:2, :])
    dts, css = [], []
    run = None
    for t in range(DEC_SEQ):
        dt = _softplus(dt_ref[blk(t), :] + hp[0:1, :])
        run = dt * a_neg if run is None else run + dt * a_neg
        dts.append(dt)
        css.append(run)

    ex = ex_ref[...]
    e3_ref[...] = jnp.exp(css[-1])
    efull_ref[...] = _dot_exact_rhs(jnp.exp(css[tt]), ex)
    w_t = jnp.exp(css[-1] - css[tt]) * dts[tt]
    xw_ref[...] = x_of(tt) * _dot_exact_rhs(w_t, ex)

    seg = seg_ref[...]
    acc = dskx_ref[...] * x_of(tt)
    c_t = c_of(tt)
    for s in range(tt + 1):
        cbh = _dot_exact_rhs(c_t * b_of(s), seg)
        g_ts = cbh * jnp.exp(css[tt] - css[s]) * dts[s]
        acc = acc + _dot_exact_rhs(g_ts, ex) * x_of(s)
    ypre_ref[...] = acc


def _ssds_prep_kernel(*refs):
    t = pl.program_id(0)
    for tt in range(DEC_SEQ):
        pl.when(t == tt)(functools.partial(_ssds_prep_step, tt, *refs))


def _ssds_prep_call(proj, dt_raw, hp, dskx, ex, seg):
    full = lambda shape: pl.BlockSpec(shape, lambda t: (0,) * len(shape))
    step = lambda width: pl.BlockSpec((DEC_BATCH, width), lambda t: (t, 0))
    cdim = GROUPS * STATE
    return pl.pallas_call(
        _ssds_prep_kernel,
        grid=(DEC_SEQ,),
        in_specs=[pl.BlockSpec((NS_ROWS, D), lambda t: (NP_TILES, PX)),
                  pl.BlockSpec((NS_ROWS, 2 * cdim), lambda t: (NP_TILES, PBC_1024)),
                  pl.BlockSpec((NS_ROWS, LANE), lambda t: (NP_TILES, 0)),
                  full((8, LANE)), full((1, D)), full((LANE, D)), full((cdim, LANE))],
        out_specs=[step(D), step(D), step(D), step(cdim), step(cdim),
                   full((DEC_BATCH, LANE))],
        out_shape=[jax.ShapeDtypeStruct((NS_ROWS, D), F32),
                   jax.ShapeDtypeStruct((NS_ROWS, D), F32),
                   jax.ShapeDtypeStruct((NS_ROWS, D), F32),
                   jax.ShapeDtypeStruct((NS_ROWS, cdim), F32),
                   jax.ShapeDtypeStruct((NS_ROWS, cdim), F32),
                   jax.ShapeDtypeStruct((DEC_BATCH, LANE), F32)],
        compiler_params=_params("arbitrary"),
        name="ssd_sample_prep",
    )(proj, proj, dt_raw, hp, dskx, ex, seg)


SEQ_PER_STEP = 8


def _ssds_state_body(blk, e3_ref, st_ref, cc_ref, bb_ref, xw_ref, yo_ref, so_ref):
    nt = (((1,), (1,)), ((), ()))
    tn = (((0,), (0,)), ((), ()))
    hpg = HEADS // GROUPS
    gw = hpg * HEADDIM

    def rows_of(ref, s, cols):
        v = ref[:, s, cols]
        return jnp.concatenate([v, jnp.zeros((8 - DEC_SEQ, v.shape[-1]), F32)], axis=0).astype(BF16)

    for s in range(SEQ_PER_STEP):
        b = blk * SEQ_PER_STEP + s
        for g in range(GROUPS):
            h0 = st_ref[s, g]
            c_g = rows_of(cc_ref, s, slice(g * STATE, (g + 1) * STATE))
            yraw = lax.dot_general(c_g, h0.astype(BF16), nt, preferred_element_type=F32)
            yo_ref[:, s, g * gw:(g + 1) * gw] = yraw[0:DEC_SEQ]
            x_g = rows_of(xw_ref, s, slice(g * gw, (g + 1) * gw))
            b_g = rows_of(bb_ref, s, slice(g * STATE, (g + 1) * STATE))
            dh = lax.dot_general(x_g, b_g, tn, preferred_element_type=F32)
            for hh in range(hpg):
                rows = slice(hh * HEADDIM, (hh + 1) * HEADDIM)
                so_ref[s, g, rows, :] = e3_ref[b, g * hpg + hh] * h0[rows] + dh[rows]


PROMPT_STEPS = N_CHUNKS // CHUNKS_PER_STEP
assert BATCH * PROMPT_STEPS * SEQ_PER_STEP == DEC_BATCH


def _ssd_fused_kernel(z_ref, x_ref, bc_ref, dt_ref, hp_ref, hpc_ref, dskx_ref, nw_ref, triu_ref,
                      e3_ref, sti_ref, cc_ref, bb_ref, xw_ref,
                      y_ref, ssm_ref, yo_ref, so_ref, st_ref, yscr_ref):
    _ssd_prompt_kernel(z_ref, x_ref, bc_ref, dt_ref, hp_ref, hpc_ref, dskx_ref, nw_ref, triu_ref,
                       y_ref, ssm_ref, st_ref, yscr_ref)
    blk = pl.program_id(0) * PROMPT_STEPS + pl.program_id(1)
    _ssds_state_body(blk, e3_ref, sti_ref, cc_ref, bb_ref, xw_ref, yo_ref, so_ref)


def _ssd_fused_call(proj, dt_raw, hp, hpc, dskx, nw, triu, e3, state, cc_t, bb_t, xw_t):
    rows = CHUNK * CHUNKS_PER_STEP
    sb = SEQ_PER_STEP
    gw = (HEADS // GROUPS) * HEADDIM
    cdim = GROUPS * STATE
    step = lambda b, c: b * PROMPT_STEPS + c
    const = lambda b, c: (0, 0)
    tmajor = lambda width: pl.BlockSpec((DEC_SEQ, sb, width), lambda b, c: (0, step(b, c), 0))
    st_spec = pl.BlockSpec((sb, GROUPS, gw, STATE), lambda b, c: (step(b, c), 0, 0, 0))
    return pl.pallas_call(
        _ssd_fused_kernel,
        grid=(BATCH, PROMPT_STEPS),
        in_specs=[pl.BlockSpec((rows, D), lambda b, c: (step(b, c), PZ)),
                  pl.BlockSpec((rows, D), lambda b, c: (step(b, c), PX)),
                  pl.BlockSpec((rows, 2 * cdim), lambda b, c: (step(b, c), PBC_1024)),
                  pl.BlockSpec((rows, LANE), lambda b, c: (step(b, c), 0)),
                  pl.BlockSpec((8, LANE), const),
                  pl.BlockSpec((LANE, 8), const),
                  pl.BlockSpec((1, D), const),
                  pl.BlockSpec((1, D), const),
                  pl.BlockSpec((CHUNK, CHUNK), const),
                  pl.BlockSpec(memory_space=pltpu.SMEM),
                  st_spec, tmajor(cdim), tmajor(cdim), tmajor(D)],
        out_specs=[pl.BlockSpec((rows, D), lambda b, c: (step(b, c), 0)),
                   pl.BlockSpec((1, HEADS // 2, LANE, STATE), lambda b, c: (b, 0, 0, 0)),
                   tmajor(D), st_spec],
        out_shape=[jax.ShapeDtypeStruct((NP_ROWS, D), BF16),
                   jax.ShapeDtypeStruct((BATCH, HEADS // 2, LANE, STATE), F32),
                   jax.ShapeDtypeStruct((DEC_SEQ, DEC_BATCH, D), F32),
                   jax.ShapeDtypeStruct((DEC_BATCH, GROUPS, gw, STATE), F32)],
        scratch_shapes=[pltpu.VMEM((HEADS // 2, STATE, LANE), F32),
                        pltpu.VMEM((2, CHUNK, D), F32)],
        compiler_params=_params("arbitrary", "arbitrary"),
        name="ssd_prompt_and_sample_state",
    )(proj, proj, proj, dt_raw, hp, hpc, dskx, nw, triu,
      e3, state, cc_t.reshape(DEC_SEQ, DEC_BATCH, cdim), bb_t.reshape(DEC_SEQ, DEC_BATCH, cdim),
      xw_t.reshape(DEC_SEQ, DEC_BATCH, D))


def _ssds_post_kernel(ypre_ref, efull_ref, yo_ref, z_ref, nw_ref, o_ref):
    get_y = lambda cols: ypre_ref[:, cols] + efull_ref[:, cols] * yo_ref[:, cols]
    _gated_group_norm(get_y, z_ref, nw_ref, o_ref)


def _ssds_post_call(ypre, efull, yo_t, proj, nw):
    step = pl.BlockSpec((DEC_BATCH, D), lambda t: (t, 0))
    return pl.pallas_call(
        _ssds_post_kernel,
        grid=(DEC_SEQ,),
        in_specs=[step, step, step,
                  pl.BlockSpec((DEC_BATCH, D), lambda t: (NP_ROWS // DEC_BATCH + t, PZ)),
                  pl.BlockSpec((1, D), lambda t: (0, 0))],
        out_specs=step,
        out_shape=jax.ShapeDtypeStruct((NS_ROWS, D), BF16),
        compiler_params=_params("arbitrary"),
        name="ssd_sample_post",
    )(ypre, efull, yo_t, proj, nw)


def _mlp_kernel(u_ref, v_ref, lnw_ref, lnb_ref, ws_ref, bsx_ref, wsx_ref,
                y_ref, cv_ref, wm_ref, cvs_ref):
    i = pl.program_id(0)
    T = CHUNK

    @pl.when(i == 0)
    def _():
        row = lax.broadcasted_iota(jnp.int32, (T, T), 0)
        lane = lax.broadcasted_iota(jnp.int32, (T, T), 1)
        for g in range(MLP_GROUPS):
            wm_ref[g] = jnp.where(row >= lane, ws_ref[g], 0.0).astype(BF16)

    def vnorm(rows):
        vg = v_ref[rows, :].astype(F32)
        xc = vg - jnp.mean(vg, axis=-1, keepdims=True)
        y = xc * lax.rsqrt(jnp.mean(xc * xc, axis=-1, keepdims=True) + EPS)
        return y * lnw_ref[...] + lnb_ref[...]

    @pl.when(i < NP_MT)
    def _():
        for cc in range(TMM // T):
            rows = slice(cc * T, (cc + 1) * T)
            vnb = vnorm(rows).astype(BF16)
            for g in range(MLP_GROUPS):
                cols = slice(g * MLP_GROUP_DIM, (g + 1) * MLP_GROUP_DIM)
                sv = _dot(wm_ref[g], vnb[:, cols]) + bsx_ref[:, cols]
                y_ref[rows, cols] = (u_ref[rows, cols].astype(F32) * sv).astype(BF16)

    @pl.when(i == NP_MT)
    def _():
        B = DEC_BATCH
        for t in range(DEC_SEQ):
            rows = slice(t * B, (t + 1) * B)
            vn = vnorm(rows)
            cvs_ref[rows, :] = vn
            cv_ref[:, t, :] = vn
        for t in range(DEC_SEQ):
            rows = slice(t * B, (t + 1) * B)
            acc = bsx_ref[t:t + 1, :]
            for s in range(t + 1):
                acc = acc + wsx_ref[4 * t + s:4 * t + s + 1, :] * cvs_ref[s * B:(s + 1) * B, :]
            y_ref[rows, :] = (u_ref[rows, :].astype(F32) * acc).astype(BF16)


def _mlp_call(proj, lnw, lnb, ws, bsx, wsx):
    full = lambda shape: pl.BlockSpec(shape, lambda i: (0,) * len(shape))
    return pl.pallas_call(
        _mlp_kernel,
        grid=(N_MT,),
        in_specs=[pl.BlockSpec((TMM, D), lambda i: (i, PU)),
                  pl.BlockSpec((TMM, D), lambda i: (i, PV)),
                  full((1, D)), full((1, D)),
                  full((MLP_GROUPS, CHUNK, CHUNK)),
                  full((CHUNK, D)), full((16, D))],
        out_specs=[pl.BlockSpec((TMM, D), lambda i: (i, 0)),
                   full((DEC_BATCH, DEC_SEQ, D))],
        out_shape=[jax.ShapeDtypeStruct((M_ROWS, D), BF16),
                   jax.ShapeDtypeStruct((DEC_BATCH, DEC_SEQ, D), F32)],
        scratch_shapes=[pltpu.VMEM((MLP_GROUPS, CHUNK, CHUNK), BF16),
                        pltpu.VMEM((NS_ROWS, D), F32)],
        compiler_params=_params("arbitrary"),
        name="gmlp",
    )(proj, proj, lnw, lnb, ws, bsx, wsx)


def _cast_rows(src_ref, dst_ref, chunk=256):
    def body(r, carry):
        rows = pl.ds(pl.multiple_of(r * chunk, chunk), chunk)
        dst_ref[rows, :] = src_ref[rows, :].astype(dst_ref.dtype)
        return carry
    lax.fori_loop(0, src_ref.shape[0] // chunk, body, 0)


def _per_tile(i, prompt_fn, sample_fn):
    pl.when(i < NP_MT)(prompt_fn)
    pl.when(i == NP_MT)(sample_fn)


def _merge_kernel(ysp_ref, yss_ref, ym_ref, ga_ref, gb_ref, w1_ref, w2_ref, o_ref, w1b, w2b):
    i = pl.program_id(1)

    @pl.when(i == 0)
    def _():
        _cast_rows(w1_ref, w1b)
        _cast_rows(w2_ref, w2b)

    def emit(ys, rows):
        a1 = _dot(ys, w1b[...])
        a2 = _dot(ym_ref[rows, :], w2b[...])
        o_ref[rows, :] = (jax.nn.sigmoid(ga_ref[rows, :].astype(F32)) * a1
                          + jax.nn.sigmoid(gb_ref[rows, :].astype(F32)) * a2).astype(BF16)

    _per_tile(i,
              lambda: emit(ysp_ref[...], slice(None)),
              lambda: emit(yss_ref[...], slice(0, NS_ROWS)))


def _merge_call(ysp, yss, ym, proj, w1, w2):
    tn = 512
    nb = D // tn
    return pl.pallas_call(
        _merge_kernel,
        grid=(nb, N_MT),
        in_specs=[pl.BlockSpec((TMM, D), lambda j, i: (jnp.minimum(i, NP_MT - 1), 0)),
                  pl.BlockSpec((NS_ROWS, D), lambda j, i: (0, 0)),
                  pl.BlockSpec((TMM, D), lambda j, i: (i, 0)),
                  pl.BlockSpec((TMM, tn), lambda j, i: (i, PGA * nb + j)),
                  pl.BlockSpec((TMM, tn), lambda j, i: (i, PGB * nb + j)),
                  pl.BlockSpec((D, tn), lambda j, i: (0, j)),
                  pl.BlockSpec((D, tn), lambda j, i: (0, j))],
        out_specs=pl.BlockSpec((TMM, tn), lambda j, i: (i, j)),
        out_shape=jax.ShapeDtypeStruct((M_ROWS, D), BF16),
        scratch_shapes=[pltpu.VMEM((D, tn), BF16), pltpu.VMEM((D, tn), BF16)],
        compiler_params=_params("arbitrary", "arbitrary"),
        name="branch_merge",
    )(ysp, yss, ym, proj, proj, w1, w2)


def _resid_kernel(a_ref, w_ref, rp_ref, rs_ref, gp_ref, gs_ref, o_ref, wb):
    i = pl.program_id(1)

    @pl.when(i == 0)
    def _():
        _cast_rows(w_ref, wb)

    def prompt():
        o_ref[...] = rp_ref[...] + gp_ref[0] * _dot(a_ref[...], wb[...])

    def sample():
        acc = _dot(a_ref[0:NS_ROWS, :], wb[...])
        tn = acc.shape[-1]
        acc3 = acc.reshape(DEC_SEQ, DEC_BATCH, tn) * gs_ref[...][None]
        o_ref[0:NS_ROWS, :] = _sample_rows(rs_ref) + acc3.reshape(NS_ROWS, tn)

    _per_tile(i, prompt, sample)


def _resid_call(a, w, rp, rs, rs_block, mod_p, mod_s, k_gate, tn, name, single_buffer_w=False):
    kdim = a.shape[1]
    nb = D // tn
    w_mode = dict(pipeline_mode=pl.Buffered(1)) if single_buffer_w else {}
    if rs.ndim == 3:
        rs_spec = pl.BlockSpec((DEC_BATCH, DEC_SEQ, tn), lambda j, i: (0, 0, j))
    else:
        rs_spec = pl.BlockSpec((NS_ROWS, tn), lambda j, i: (rs_block, j))
    return pl.pallas_call(
        _resid_kernel,
        grid=(nb, N_MT),
        in_specs=[pl.BlockSpec((TMM, kdim), lambda j, i: (i, 0)),
                  pl.BlockSpec((kdim, tn), lambda j, i: (0, j), **w_mode),
                  pl.BlockSpec((TMM, tn), lambda j, i: (jnp.minimum(i, NP_MT - 1), j)),
                  rs_spec,
                  pl.BlockSpec((1, 1, tn), lambda j, i: (_seq_of_mtile(i), 0, k_gate * nb + j)),
                  pl.BlockSpec((DEC_BATCH, tn), lambda j, i: (0, k_gate * nb + j))],
        out_specs=pl.BlockSpec((TMM, tn), lambda j, i: (i, j)),
        out_shape=jax.ShapeDtypeStruct((M_ROWS, D), F32),
        scratch_shapes=[pltpu.VMEM((kdim, tn), BF16)],
        compiler_params=_params("arbitrary", "arbitrary"),
        name=name,
    )(a, w, rp, rs, mod_p, mod_s)


TN_FF = 512
N_FF_BLOCKS = D_FF // TN_FF


def _up_kernel(a_ref, wa_ref, wv_ref, cw_ref, cb_ref, fst_ref,
               h_ref, fcp_ref, fcs_ref, wab, wvb, acc_a, acc_v):
    i = pl.program_id(1)
    rc = 64

    @pl.when(i == 0)
    def _():
        _cast_rows(wa_ref, wab)
        _cast_rows(wv_ref, wvb)

    def prompt():
        starts_sequence = i % MT_PER_SEQ == 0
        acc_a[0:8, :] = jnp.where(starts_sequence, 0.0, acc_a[TMM:TMM + 8, :])
        x = a_ref[...]
        acc_a[8:8 + TMM, :] = _dot(x, wab[...])
        acc_v[...] = _dot(x, wvb[...])
        cw, cb = cw_ref[...], cb_ref[...]
        for r in range(0, TMM, rc):
            conv = (cb + cw[2:3] * acc_a[8 + r:8 + r + rc, :]
                    + cw[1:2] * acc_a[7 + r:7 + r + rc, :]
                    + cw[0:1] * acc_a[6 + r:6 + r + rc, :])
            h_ref[r:r + rc, :] = (jax.nn.gelu(conv) * acc_v[r:r + rc, :]).astype(BF16)
        fcp_ref[0] = acc_a[TMM + 6:TMM + 8, :]

    def sample():
        B = DEC_BATCH
        x = a_ref[0:NS_ROWS, :]
        acc_a[8:8 + NS_ROWS, :] = _dot(x, wab[...])
        acc_v[0:NS_ROWS, :] = _dot(x, wvb[...])
        cw, cb = cw_ref[...], cb_ref[...]

        def pre(t, r):
            if t < 0:
                return fst_ref[r:r + rc, t + 2, :]
            return acc_a[8 + t * B + r:8 + t * B + r + rc, :]

        for t in range(DEC_SEQ):
            for r in range(0, B, rc):
                conv = cb + cw[2:3] * pre(t, r) + cw[1:2] * pre(t - 1, r) + cw[0:1] * pre(t - 2, r)
                h_ref[t * B + r:t * B + r + rc, :] = (
                    jax.nn.gelu(conv) * acc_v[t * B + r:t * B + r + rc, :]).astype(BF16)
        fcs_ref[:, 0, :] = acc_a[8 + 2 * B:8 + 3 * B, :]
        fcs_ref[:, 1, :] = acc_a[8 + 3 * B:8 + 4 * B, :]

    _per_tile(i, prompt, sample)


def _up_call(n2, w_up, cw, cb, fst):
    return pl.pallas_call(
        _up_kernel,
        grid=(N_FF_BLOCKS, N_MT),
        in_specs=[pl.BlockSpec((TMM, D), lambda j, i: (i, 0)),
                  pl.BlockSpec((D, TN_FF), lambda j, i: (0, j)),
                  pl.BlockSpec((D, TN_FF), lambda j, i: (0, N_FF_BLOCKS + j)),
                  pl.BlockSpec((3, TN_FF), lambda j, i: (0, j)),
                  pl.BlockSpec((1, TN_FF), lambda j, i: (0, j)),
                  pl.BlockSpec((DEC_BATCH, 2, TN_FF), lambda j, i: (0, 0, j))],
        out_specs=[pl.BlockSpec((TMM, TN_FF), lambda j, i: (i, j)),
                   pl.BlockSpec((1, 2, TN_FF), lambda j, i: (_seq_of_mtile(i), 0, j)),
                   pl.BlockSpec((DEC_BATCH, 2, TN_FF), lambda j, i: (0, 0, j))],
        out_shape=[jax.ShapeDtypeStruct((M_ROWS, D_FF), BF16),
                   jax.ShapeDtypeStruct((BATCH, 2, D_FF), F32),
                   jax.ShapeDtypeStruct((DEC_BATCH, 2, D_FF), F32)],
        scratch_shapes=[pltpu.VMEM((D, TN_FF), BF16), pltpu.VMEM((D, TN_FF), BF16),
                        pltpu.VMEM((TMM + 8, TN_FF), F32), pltpu.VMEM((TMM, TN_FF), F32)],
        compiler_params=_params("arbitrary", "arbitrary"),
        name="ffn_up",
    )(n2, w_up, w_up, cw, cb, fst)


def _final_kernel(x_ref, w_ref, yp_ref, ys_ref):
    i = pl.program_id(0)

    @pl.when(i < NP_MT)
    def _():
        for r in range(0, TMM, TM):
            yp_ref[r:r + TM, :] = _rms(x_ref[r:r + TM, :]) * w_ref[...]

    @pl.when(i == NP_MT)
    def _():
        y = _rms(x_ref[0:NS_ROWS, :]) * w_ref[...]
        for t in range(DEC_SEQ):
            ys_ref[:, t, :] = y[t * DEC_BATCH:(t + 1) * DEC_BATCH]


def _final_call(x3, w):
    return pl.pallas_call(
        _final_kernel,
        grid=(N_MT,),
        in_specs=[pl.BlockSpec((TMM, D), lambda i: (i, 0)),
                  pl.BlockSpec((1, D), lambda i: (0, 0))],
        out_specs=[pl.BlockSpec((TMM, D), lambda i: (jnp.minimum(i, NP_MT - 1), 0)),
                   pl.BlockSpec((DEC_BATCH, DEC_SEQ, D), lambda i: (0, 0, 0))],
        out_shape=[jax.ShapeDtypeStruct((NP_ROWS, D), F32),
                   jax.ShapeDtypeStruct((DEC_BATCH, DEC_SEQ, D), F32)],
        compiler_params=_params("arbitrary"),
        name="final_norm",
    )(x3, w)


def kernel(x_prompt, x_sample, state_ssm, state_ssd_conv, state_ffn_conv, c_prompt, c_sample,
           norm1_w, w_ada, b_ada, w_in, ssd_conv_w, ssd_conv_b, dt_bias, a_log, d_skip,
           ssd_norm_w, mlp_ln_w, mlp_ln_b, w_spatial, b_spatial, w_ssd_o, w_mlp_o, w_out,
           norm2_w, w_up, ffn_conv_w, ffn_conv_b, w_down, final_norm_w):
    assert w_in.shape[0] == 1, "single-layer trunk"
    row = lambda v: v.reshape(1, -1)

    xp = x_prompt.reshape(NP_ROWS, D)
    xs = x_sample

    mod_p, mod_s = _ada_call(c_prompt, c_sample, w_ada[0], row(b_ada[0]))
    K_SHIFT1, K_SCALE1, K_GATE1, K_SHIFT2, K_SCALE2, K_GATE2 = range(6)

    xp_spec = pl.BlockSpec((TMM, D), lambda i: (jnp.minimum(i, NP_MT - 1), 0))
    xs_spec = pl.BlockSpec((DEC_BATCH, DEC_SEQ, D), lambda i: (0, 0, 0))
    w_in_t = w_in[0].T
    n1, dt_raw = _norm_call(xp, xp_spec, xs, xs_spec, row(norm1_w[0]), mod_p, mod_s,
                            K_SCALE1, K_SHIFT1, w_in_t)
    proj, cst_p, cst_s = _inproj_call(n1, w_in_t, ssd_conv_w[0], row(ssd_conv_b[0]), state_ssd_conv[0])

    hp = jnp.pad(jnp.stack([dt_bias[0], a_log[0], d_skip[0]]), ((0, 5), (0, LANE - HEADS)))
    hpc = hp.T
    triu = jnp.asarray(np.triu(np.ones((CHUNK, CHUNK), np.float32)), dtype=BF16)
    dskx = row(jnp.repeat(d_skip[0], HEADDIM))
    nw = row(ssd_norm_w[0])

    head_of_col = np.arange(D) // HEADDIM
    ex = jnp.asarray(np.arange(LANE)[:, None] == head_of_col[None, :], dtype=BF16)
    grp_of_head = np.arange(LANE) // (HEADS // GROUPS)
    seg = jnp.asarray((np.arange(GROUPS * STATE)[:, None] // STATE == grp_of_head[None, :])
                      & (np.arange(LANE)[None, :] < HEADS), dtype=BF16)
    ypre, efull, xw_t, bb_t, cc_t, e3 = _ssds_prep_call(proj, dt_raw, hp, dskx, ex, seg)
    gw = (HEADS // GROUPS) * HEADDIM
    y_ssd_p, ssm_p, yo_t, ssm_s = _ssd_fused_call(
        proj, dt_raw, hp, hpc, dskx, nw, triu,
        e3, state_ssm[0].reshape(DEC_BATCH, GROUPS, gw, STATE), cc_t, bb_t, xw_t)
    y_ssd_s = _ssds_post_call(ypre, efull, yo_t.reshape(NS_ROWS, D), proj, nw)

    per_col = lambda a: jnp.repeat(a, MLP_GROUP_DIM, axis=1)
    bsx = per_col(b_spatial[0][:, :CHUNK].T)
    ws4 = w_spatial[0][:, :DEC_SEQ, :DEC_SEQ]
    wsx = per_col(jnp.transpose(ws4, (1, 2, 0)).reshape(DEC_SEQ * DEC_SEQ, MLP_GROUPS))
    y_mlp, cv_s = _mlp_call(proj, row(mlp_ln_w[0]), row(mlp_ln_b[0]), w_spatial[0], bsx, wsx)

    mixed = _merge_call(y_ssd_p, y_ssd_s, y_mlp, proj, w_ssd_o[0], w_mlp_o[0])
    x2 = _resid_call(mixed, w_out[0], xp, xs, 0, mod_p, mod_s, K_GATE1, 1024, "out_proj")

    x2p_spec = pl.BlockSpec((TMM, D), lambda i: (jnp.minimum(i, NP_MT - 1), 0))
    x2s_spec = pl.BlockSpec((NS_ROWS, D), lambda i: (NP_TILES, 0))
    (n2,) = _norm_call(x2, x2p_spec, x2, x2s_spec, row(norm2_w[0]), mod_p, mod_s,
                       K_SCALE2, K_SHIFT2, None)
    h, ffn_p, ffn_s = _up_call(n2, w_up[0], ffn_conv_w[0], row(ffn_conv_b[0]), state_ffn_conv[0])
    x3 = _resid_call(h, w_down[0], x2, x2, NP_TILES, mod_p, mod_s, K_GATE2, 512, "ffn_down",
                     single_buffer_w=True)

    y_p, y_s = _final_call(x3, row(final_norm_w))

    return (y_p.reshape(BATCH, SEQ, D),
            y_s,
            ssm_p.reshape(1, BATCH, HEADS, HEADDIM, STATE),
            ssm_s.reshape(1, DEC_BATCH, HEADS, HEADDIM, STATE),
            cst_p[None],
            cst_s[None],
            ffn_p[None],
            ffn_s[None],
            cv_s[None])
```

```python
import functools

import jax
import jax.numpy as jnp
import numpy as np
from jax import lax
from jax.experimental import pallas as pl
from jax.experimental.pallas import tpu as pltpu

F32 = jnp.float32
BF16 = jnp.bfloat16

D = 2048
BATCH, SEQ = 4, 2048
DEC_BATCH, DEC_SEQ = 128, 4
NP_ROWS = BATCH * SEQ
NS_ROWS = DEC_BATCH * DEC_SEQ
M_ROWS = NP_ROWS + NS_ROWS
TM = 512
NP_TILES = NP_ROWS // TM
TMM = 1024
NP_MT = NP_ROWS // TMM
N_MT = NP_MT + 1
MT_PER_SEQ = SEQ // TMM
HEADS, HEADDIM, GROUPS, STATE = 32, 64, 4, 128
CHUNK = 128
N_CHUNKS = SEQ // CHUNK
CONV_DIM = D + 2 * GROUPS * STATE
MLP_GROUPS = 8
MLP_GROUP_DIM = D // MLP_GROUPS
D_FF = 5632
EPS = 1e-6
DT_COL = D + CONV_DIM
TN_IN = 1024
LANE = 128
VMEM_LIMIT = 56 * 1024 * 1024


def _params(*sem, flags=None):
    return pltpu.CompilerParams(dimension_semantics=sem, vmem_limit_bytes=VMEM_LIMIT, flags=flags)


def _dot(a, b):
    return jnp.dot(a, b, preferred_element_type=F32)


def _dot_nt(a, b_t):
    return lax.dot_general(a, b_t, (((1,), (1,)), ((), ())), preferred_element_type=F32)


def _split_bf16(v, terms):
    out = []
    r = v
    for _ in range(terms):
        p = r.astype(BF16)
        out.append(p)
        r = r - p.astype(F32)
    return out


def _dot_exact_rhs(v, e, terms=3):
    acc = None
    for p in _split_bf16(v, terms):
        d = _dot(p, e)
        acc = d if acc is None else acc + d
    return acc


def _silu(x):
    return x * jax.nn.sigmoid(x)


def _softplus(x):
    return jnp.maximum(x, 0.0) + jnp.log1p(jnp.exp(-jnp.abs(x)))


def _rms(x):
    return x * lax.rsqrt(jnp.mean(x * x, axis=-1, keepdims=True) + EPS)


def _ada_kernel(cp_ref, cs_ref, w_ref, b_ref, op_ref, os_ref):
    w = w_ref[...].astype(BF16)
    cp8 = jnp.concatenate([cp_ref[...], jnp.zeros((8 - BATCH, D), F32)], axis=0)
    op_ref[:, 0, :] = (_dot(_silu(cp8).astype(BF16), w) + b_ref[...])[0:BATCH]
    os_ref[...] = _dot(_silu(cs_ref[...]).astype(BF16), w) + b_ref[...]


def _ada_call(c_prompt, c_sample, w, b):
    tn = 1024
    return pl.pallas_call(
        _ada_kernel,
        grid=(6 * D // tn,),
        in_specs=[pl.BlockSpec((BATCH, D), lambda j: (0, 0)),
                  pl.BlockSpec((DEC_BATCH, D), lambda j: (0, 0)),
                  pl.BlockSpec((D, tn), lambda j: (0, j)),
                  pl.BlockSpec((1, tn), lambda j: (0, j))],
        out_specs=[pl.BlockSpec((BATCH, 1, tn), lambda j: (0, 0, j)),
                   pl.BlockSpec((DEC_BATCH, tn), lambda j: (0, j))],
        out_shape=[jax.ShapeDtypeStruct((BATCH, 1, 6 * D), F32),
                   jax.ShapeDtypeStruct((DEC_BATCH, 6 * D), F32)],
        compiler_params=_params("arbitrary"),
        name="ada_mod",
    )(c_prompt, c_sample, w, b)


def _sample_rows(ref):
    if len(ref.shape) == 2:
        return ref[...]
    return jnp.concatenate([ref[:, t, :] for t in range(DEC_SEQ)], axis=0)


def _norm_kernel(with_dt, xp_ref, xs_ref, nw_ref, scp_ref, shp_ref, scs_ref, shs_ref, *rest):
    if with_dt:
        wdt_ref, n_ref, dt_ref = rest
    else:
        (n_ref,) = rest
    i = pl.program_id(0)

    def emit(n, rows):
        nb = n.astype(BF16)
        n_ref[rows, :] = nb
        if with_dt:
            dt_ref[rows, :] = lax.dot_general(nb, wdt_ref[...].astype(BF16),
                                              (((1,), (1,)), ((), ())),
                                              preferred_element_type=F32)

    @pl.when(i < NP_MT)
    def _():
        for r in range(0, TMM, TM):
            rows = slice(r, r + TM)
            y = _rms(xp_ref[rows, :]) * nw_ref[...]
            emit(y * (1.0 + scp_ref[0]) + shp_ref[0], rows)

    @pl.when(i == NP_MT)
    def _():
        y = _rms(_sample_rows(xs_ref)) * nw_ref[...]
        y3 = y.reshape(DEC_SEQ, DEC_BATCH, D)
        emit((y3 * (1.0 + scs_ref[...])[None] + shs_ref[...][None]).reshape(NS_ROWS, D),
             slice(0, NS_ROWS))


def _seq_of_mtile(i):
    return jnp.minimum(i // MT_PER_SEQ, BATCH - 1)


def _norm_call(xp, xp_spec, xs, xs_spec, nw, mod_p, mod_s, k_scale, k_shift, w_in_t):
    with_dt = w_in_t is not None
    in_specs = [
        xp_spec, xs_spec,
        pl.BlockSpec((1, D), lambda i: (0, 0)),
        pl.BlockSpec((1, 1, D), lambda i: (_seq_of_mtile(i), 0, k_scale)),
        pl.BlockSpec((1, 1, D), lambda i: (_seq_of_mtile(i), 0, k_shift)),
        pl.BlockSpec((DEC_BATCH, D), lambda i: (0, k_scale)),
        pl.BlockSpec((DEC_BATCH, D), lambda i: (0, k_shift)),
    ]
    args = [xp, xs, nw, mod_p, mod_p, mod_s, mod_s]
    out_specs = [pl.BlockSpec((TMM, D), lambda i: (i, 0))]
    out_shape = [jax.ShapeDtypeStruct((M_ROWS, D), BF16)]
    if with_dt:
        in_specs.append(pl.BlockSpec((pl.Element(LANE), pl.Element(D)), lambda i: (DT_COL, 0)))
        args.append(w_in_t)
        out_specs.append(pl.BlockSpec((TMM, LANE), lambda i: (i, 0)))
        out_shape.append(jax.ShapeDtypeStruct((M_ROWS, LANE), F32))
    return pl.pallas_call(
        functools.partial(_norm_kernel, with_dt),
        grid=(N_MT,),
        in_specs=in_specs,
        out_specs=out_specs,
        out_shape=out_shape,
        compiler_params=_params("arbitrary"),
        name="norm_mod_dt" if with_dt else "norm_mod",
    )(*args)


N_IN_BLOCKS = 13
UVG_ROW = DT_COL + HEADS
FIRST_CONV_BLOCK = 10


def _in_src_row(j):
    row = jnp.where(j < 2, j * TN_IN,
                    jnp.where(j < FIRST_CONV_BLOCK, UVG_ROW + (j - 2) * TN_IN,
                              D + (j - FIRST_CONV_BLOCK) * TN_IN))
    return pl.multiple_of(row, HEADS)


def _in_conv_block(j):
    return jnp.maximum(j - FIRST_CONV_BLOCK, 0)


def _inproj_kernel(a_ref, wt_ref, cw_ref, cb_ref, cst_ref, o_ref, csp_ref, css_ref,
                   wbf_ref, acc_ref, wst_ref, wsem):
    j = pl.program_id(0)
    i = pl.program_id(1)
    rc = 64
    B = DEC_BATCH

    def w_fetch(jj):
        return pltpu.make_async_copy(wt_ref.at[pl.ds(_in_src_row(jj), TN_IN), :], wst_ref, wsem)

    @pl.when(i == 0)
    def _():
        @pl.when(j == 0)
        def _():
            w_fetch(0).start()

        w_fetch(j).wait()
        for r in range(TN_IN // LANE):
            rows = slice(r * LANE, (r + 1) * LANE)
            wbf_ref[rows, :] = wst_ref[rows, :].astype(BF16)

        @pl.when(j + 1 < N_IN_BLOCKS)
        def _():
            w_fetch(j + 1).start()

    def elementwise(fn, rows):
        def body():
            acc_ref[8:8 + rows, :] = _dot_nt(a_ref[0:rows, :], wbf_ref[...])
            for r in range(0, rows, rc):
                val = acc_ref[8 + r:8 + r + rc, :]
                o_ref[r:r + rc, :] = (val if fn is None else fn(val)).astype(o_ref.dtype)
        return body

    def conv_prompt():
        starts_sequence = i % MT_PER_SEQ == 0
        acc_ref[0:8, :] = jnp.where(starts_sequence, 0.0, acc_ref[TMM:TMM + 8, :])
        acc_ref[8:8 + TMM, :] = _dot_nt(a_ref[...], wbf_ref[...])
        cw, cb = cw_ref[...], cb_ref[...]
        for r in range(0, TMM, rc):
            conv = cb + cw[3:4] * acc_ref[8 + r:8 + r + rc, :]
            for k in range(3):
                conv = conv + cw[k:k + 1] * acc_ref[5 + k + r:5 + k + r + rc, :]
            o_ref[r:r + rc, :] = _silu(conv).astype(o_ref.dtype)
        csp_ref[0] = acc_ref[TMM + 5:TMM + 8, :]

    def conv_sample():
        acc_ref[8:8 + NS_ROWS, :] = _dot_nt(a_ref[0:NS_ROWS, :], wbf_ref[...])
        cw, cb = cw_ref[...], cb_ref[...]

        def pre(t, r):
            if t < 0:
                return cst_ref[t + 3, r:r + rc, :]
            return acc_ref[8 + t * B + r:8 + t * B + r + rc, :]

        for t in range(DEC_SEQ):
            for r in range(0, B, rc):
                conv = cb + cw[3:4] * pre(t, r)
                for k in range(3):
                    conv = conv + cw[k:k + 1] * pre(t - 3 + k, r)
                o_ref[t * B + r:t * B + r + rc, :] = _silu(conv).astype(o_ref.dtype)
        for t in range(1, DEC_SEQ):
            css_ref[t - 1] = acc_ref[8 + t * B:8 + (t + 1) * B, :]

    is_conv = j >= FIRST_CONV_BLOCK
    kinds = ((j < 2, _silu),
             (jnp.logical_and(j >= 2, j < 6), jax.nn.gelu),
             (jnp.logical_and(j >= 6, j < FIRST_CONV_BLOCK), None))
    for cond, fn in kinds:
        pl.when(jnp.logical_and(cond, i < NP_MT))(elementwise(fn, TMM))
        pl.when(jnp.logical_and(cond, i == NP_MT))(elementwise(fn, NS_ROWS))
    pl.when(jnp.logical_and(is_conv, i < NP_MT))(conv_prompt)
    pl.when(jnp.logical_and(is_conv, i == NP_MT))(conv_sample)


def _inproj_call(n1, w_in_t, cw, cb, cst_t):
    cblk = _in_conv_block
    seq = lambda j, i: jnp.where(j < FIRST_CONV_BLOCK, 0, _seq_of_mtile(i))
    return pl.pallas_call(
        _inproj_kernel,
        grid=(N_IN_BLOCKS, N_MT),
        in_specs=[pl.BlockSpec((TMM, D), lambda j, i: (i, 0)),
                  pl.BlockSpec(memory_space=pl.ANY),
                  pl.BlockSpec((4, TN_IN), lambda j, i: (0, cblk(j))),
                  pl.BlockSpec((1, TN_IN), lambda j, i: (0, cblk(j))),
                  pl.BlockSpec((3, DEC_BATCH, TN_IN), lambda j, i: (0, 0, cblk(j)))],
        out_specs=[pl.BlockSpec((TMM, TN_IN), lambda j, i: (i, j)),
                   pl.BlockSpec((1, 3, TN_IN), lambda j, i: (seq(j, i), 0, cblk(j))),
                   pl.BlockSpec((3, DEC_BATCH, TN_IN), lambda j, i: (0, 0, cblk(j)))],
        out_shape=[jax.ShapeDtypeStruct((M_ROWS, N_IN_BLOCKS * TN_IN), BF16),
                   jax.ShapeDtypeStruct((BATCH, 3, CONV_DIM), F32),
                   jax.ShapeDtypeStruct((3, DEC_BATCH, CONV_DIM), F32)],
        scratch_shapes=[pltpu.VMEM((TN_IN, D), BF16), pltpu.VMEM((TMM + 8, TN_IN), F32),
                        pltpu.VMEM((TN_IN, D), F32), pltpu.SemaphoreType.DMA(())],
        compiler_params=_params("arbitrary", "arbitrary"),
        name="in_proj",
    )(n1, w_in_t, cw, cb, cst_t)


PZ, PU, PV, PGA, PGB, PX = 0, 1, 2, 3, 4, 5
PBC_1024 = 12


def _gated_group_norm(get_y, zact_ref, nw_ref, o_ref, rows=slice(None)):
    gw = D // GROUPS
    for g in range(GROUPS):
        cols = slice(g * gw, (g + 1) * gw)
        gg = _rms(get_y(cols) * zact_ref[rows, cols].astype(F32))
        o_ref[rows, cols] = (gg * nw_ref[:, cols]).astype(o_ref.dtype)


CHUNKS_PER_STEP = 4


def _ssd_prompt_kernel(z_ref, x_ref, bc_ref, dt_ref, hp_ref, hpc_ref, dskx_ref, nw_ref, triu_ref,
                       y_ref, ssm_ref, st_ref, yscr_ref):
    c = pl.program_id(1)
    T = CHUNK
    cdim = GROUPS * STATE

    @pl.when(c == 0)
    def _():
        st_ref[...] = jnp.zeros(st_ref.shape, F32)

    hp = hp_ref[...]
    a_col = -jnp.exp(hpc_ref[0:HEADS, 1:2])
    triu = triu_ref[...]
    row = lax.broadcasted_iota(jnp.int32, (T, T), 0)
    lane = lax.broadcasted_iota(jnp.int32, (T, T), 1)
    causal = row >= lane
    left = lane < HEADDIM
    mask_l = jnp.where(left, 1.0, 0.0).astype(BF16)
    nt = (((1,), (1,)), ((), ()))
    pairs_per_group = HEADS // GROUPS // 2

    for sub in range(CHUNKS_PER_STEP):
        rows = slice(sub * T, (sub + 1) * T)
        y_buf = yscr_ref.at[sub % 2]
        dt_t = _softplus((dt_ref[rows, :] + hp[0:1, :]).T[0:HEADS, :])
        cs_t = _dot_exact_rhs(dt_t * a_col, triu)
        rsub_t = cs_t - jnp.log(dt_t)
        cs = jnp.concatenate([cs_t, jnp.zeros((LANE - HEADS, T), F32)], axis=0).T

        for g in range(GROUPS):
            c_b = bc_ref[rows, cdim + g * STATE:cdim + (g + 1) * STATE]
            b_b = bc_ref[rows, g * STATE:(g + 1) * STATE]
            cb = lax.dot_general(c_b, b_b, nt, preferred_element_type=F32)
            b_t = b_b.astype(F32).T
            for k4 in range(pairs_per_group):
                k = g * pairs_per_group + k4
                cols = slice(k * LANE, (k + 1) * LANE)
                xpb = x_ref[rows, cols]
                x_lo = xpb * mask_l
                xbd = jnp.concatenate([x_lo, xpb - x_lo], axis=0)
                st = st_ref[k]
                yraw = _dot(c_b, st.astype(BF16))
                scores, bws, colbs, alasts = [], [], [], []
                for h in (2 * k, 2 * k + 1):
                    colb = jnp.broadcast_to(cs[:, h:h + 1], (T, T))
                    alast = cs_t[h:h + 1, T - 1:T]
                    decay_dt = jnp.exp(jnp.where(causal, colb - rsub_t[h:h + 1, :], -jnp.inf))
                    scores.append((cb * decay_dt).astype(BF16))
                    wrow = jnp.exp(alast - cs_t[h:h + 1, :]) * dt_t[h:h + 1, :]
                    bws.append((b_t * wrow).astype(BF16))
                    colbs.append(colb)
                    alasts.append(alast)
                ecol = jnp.exp(jnp.where(left, colbs[0], colbs[1]))
                elast = jnp.exp(jnp.where(left[0:1], alasts[0], alasts[1]))
                y_buf[:, cols] = (_dot(jnp.concatenate(scores, axis=1), xbd) + ecol * yraw
                                  + dskx_ref[:, cols] * xpb.astype(F32))
                st_ref[k] = elast * st + _dot(jnp.concatenate(bws, axis=1), xbd)
        _gated_group_norm(lambda cols: y_buf[:, cols], z_ref, nw_ref, y_ref, rows)

    @pl.when(c == N_CHUNKS // CHUNKS_PER_STEP - 1)
    def _():
        for k in range(HEADS // 2):
            ssm_ref[0, k] = st_ref[k].T


def _ssds_prep_step(tt, x_ref, bc_ref, dt_ref, hp_ref, dskx_ref, ex_ref, seg_ref,
                    ypre_ref, efull_ref, xw_ref, bb_ref, cc_ref, e3_ref):
    B = DEC_BATCH
    cdim = GROUPS * STATE
    blk = lambda t: slice(t * B, (t + 1) * B)
    x_of = lambda t: x_ref[blk(t), :].astype(F32)
    b_of = lambda t: bc_ref[blk(t), 0:cdim].astype(F32)
    c_of = lambda t: bc_ref[blk(t), cdim:2 * cdim].astype(F32)

    bb_ref[...] = b_of(tt)
    cc_ref[...] = c_of(tt)

    hp = hp_ref[...]
    a_neg = -jnp.exp(hp[1:2, :])
    dts, css = [], []
    run = None
    for t in range(DEC_SEQ):
        dt = _softplus(dt_ref[blk(t), :] + hp[0:1, :])
        run = dt * a_neg if run is None else run + dt * a_neg
        dts.append(dt)
        css.append(run)

    ex = ex_ref[...]
    e3_ref[...] = jnp.exp(css[-1])
    efull_ref[...] = _dot_exact_rhs(jnp.exp(css[tt]), ex)
    w_t = jnp.exp(css[-1] - css[tt]) * dts[tt]
    xw_ref[...] = x_of(tt) * _dot_exact_rhs(w_t, ex)

    seg = seg_ref[...]
    acc = dskx_ref[...] * x_of(tt)
    c_t = c_of(tt)
    for s in range(tt + 1):
        cbh = _dot_exact_rhs(c_t * b_of(s), seg)
        g_ts = cbh * jnp.exp(css[tt] - css[s]) * dts[s]
        acc = acc + _dot_exact_rhs(g_ts, ex) * x_of(s)
    ypre_ref[...] = acc


def _ssds_prep_kernel(*refs):
    t = pl.program_id(0)
    for tt in range(DEC_SEQ):
        pl.when(t == tt)(functools.partial(_ssds_prep_step, tt, *refs))


def _ssds_prep_call(proj, dt_raw, hp, dskx, ex, seg):
    full = lambda shape: pl.BlockSpec(shape, lambda t: (0,) * len(shape))
    step = lambda width: pl.BlockSpec((DEC_BATCH, width), lambda t: (t, 0))
    cdim = GROUPS * STATE
    return pl.pallas_call(
        _ssds_prep_kernel,
        grid=(DEC_SEQ,),
        in_specs=[pl.BlockSpec((NS_ROWS, D), lambda t: (NP_TILES, PX)),
                  pl.BlockSpec((NS_ROWS, 2 * cdim), lambda t: (NP_TILES, PBC_1024)),
                  pl.BlockSpec((NS_ROWS, LANE), lambda t: (NP_TILES, 0)),
                  full((8, LANE)), full((1, D)), full((LANE, D)), full((cdim, LANE))],
        out_specs=[step(D), step(D), step(D), step(cdim), step(cdim),
                   full((DEC_BATCH, LANE))],
        out_shape=[jax.ShapeDtypeStruct((NS_ROWS, D), F32),
                   jax.ShapeDtypeStruct((NS_ROWS, D), F32),
                   jax.ShapeDtypeStruct((NS_ROWS, D), F32),
                   jax.ShapeDtypeStruct((NS_ROWS, cdim), F32),
                   jax.ShapeDtypeStruct((NS_ROWS, cdim), F32),
                   jax.ShapeDtypeStruct((DEC_BATCH, LANE), F32)],
        compiler_params=_params("arbitrary"),
        name="ssd_sample_prep",
    )(proj, proj, dt_raw, hp, dskx, ex, seg)


SEQ_PER_STEP = 8


def _ssds_state_body(blk, e3_ref, st_ref, cc_ref, bb_ref, xw_ref, yo_ref, so_ref):
    nt = (((1,), (1,)), ((), ()))
    tn = (((0,), (0,)), ((), ()))
    hpg = HEADS // GROUPS
    gw = hpg * HEADDIM

    def rows_of(ref, s, cols):
        v = ref[:, s, cols]
        return jnp.concatenate([v, jnp.zeros((8 - DEC_SEQ, v.shape[-1]), F32)], axis=0).astype(BF16)

    for s in range(SEQ_PER_STEP):
        b = blk * SEQ_PER_STEP + s
        for g in range(GROUPS):
            h0 = st_ref[s, g]
            c_g = rows_of(cc_ref, s, slice(g * STATE, (g + 1) * STATE))
            yraw = lax.dot_general(c_g, h0.astype(BF16), nt, preferred_element_type=F32)
            yo_ref[:, s, g * gw:(g + 1) * gw] = yraw[0:DEC_SEQ]
            x_g = rows_of(xw_ref, s, slice(g * gw, (g + 1) * gw))
            b_g = rows_of(bb_ref, s, slice(g * STATE, (g + 1) * STATE))
            dh = lax.dot_general(x_g, b_g, tn, preferred_element_type=F32)
            for hh in range(hpg):
                rows = slice(hh * HEADDIM, (hh + 1) * HEADDIM)
                so_ref[s, g, rows, :] = e3_ref[b, g * hpg + hh] * h0[rows] + dh[rows]


PROMPT_STEPS = N_CHUNKS // CHUNKS_PER_STEP
assert BATCH * PROMPT_STEPS * SEQ_PER_STEP == DEC_BATCH


def _ssd_fused_kernel(z_ref, x_ref, bc_ref, dt_ref, hp_ref, hpc_ref, dskx_ref, nw_ref, triu_ref,
                      e3_ref, sti_ref, cc_ref, bb_ref, xw_ref,
                      y_ref, ssm_ref, yo_ref, so_ref, st_ref, yscr_ref):
    _ssd_prompt_kernel(z_ref, x_ref, bc_ref, dt_ref, hp_ref, hpc_ref, dskx_ref, nw_ref, triu_ref,
                       y_ref, ssm_ref, st_ref, yscr_ref)
    blk = pl.program_id(0) * PROMPT_STEPS + pl.program_id(1)
    _ssds_state_body(blk, e3_ref, sti_ref, cc_ref, bb_ref, xw_ref, yo_ref, so_ref)


def _ssd_fused_call(proj, dt_raw, hp, hpc, dskx, nw, triu, e3, state, cc_t, bb_t, xw_t):
    rows = CHUNK * CHUNKS_PER_STEP
    sb = SEQ_PER_STEP
    gw = (HEADS // GROUPS) * HEADDIM
    cdim = GROUPS * STATE
    step = lambda b, c: b * PROMPT_STEPS + c
    const = lambda b, c: (0, 0)
    tmajor = lambda width: pl.BlockSpec((DEC_SEQ, sb, width), lambda b, c: (0, step(b, c), 0))
    st_spec = pl.BlockSpec((sb, GROUPS, gw, STATE), lambda b, c: (step(b, c), 0, 0, 0))
    return pl.pallas_call(
        _ssd_fused_kernel,
        grid=(BATCH, PROMPT_STEPS),
        in_specs=[pl.BlockSpec((rows, D), lambda b, c: (step(b, c), PZ)),
                  pl.BlockSpec((rows, D), lambda b, c: (step(b, c), PX)),
                  pl.BlockSpec((rows, 2 * cdim), lambda b, c: (step(b, c), PBC_1024)),
                  pl.BlockSpec((rows, LANE), lambda b, c: (step(b, c), 0)),
                  pl.BlockSpec((8, LANE), const),
                  pl.BlockSpec((LANE, 8), const),
                  pl.BlockSpec((1, D), const),
                  pl.BlockSpec((1, D), const),
                  pl.BlockSpec((CHUNK, CHUNK), const),
                  pl.BlockSpec(memory_space=pltpu.SMEM),
                  st_spec, tmajor(cdim), tmajor(cdim), tmajor(D)],
        out_specs=[pl.BlockSpec((rows, D), lambda b, c: (step(b, c), 0)),
                   pl.BlockSpec((1, HEADS // 2, LANE, STATE), lambda b, c: (b, 0, 0, 0)),
                   tmajor(D), st_spec],
        out_shape=[jax.ShapeDtypeStruct((NP_ROWS, D), BF16),
                   jax.ShapeDtypeStruct((BATCH, HEADS // 2, LANE, STATE), F32),
                   jax.ShapeDtypeStruct((DEC_SEQ, DEC_BATCH, D), F32),
                   jax.ShapeDtypeStruct((DEC_BATCH, GROUPS, gw, STATE), F32)],
        scratch_shapes=[pltpu.VMEM((HEADS // 2, STATE, LANE), F32),
                        pltpu.VMEM((2, CHUNK, D), F32)],
        compiler_params=_params("arbitrary", "arbitrary"),
        name="ssd_prompt_and_sample_state",
    )(proj, proj, proj, dt_raw, hp, hpc, dskx, nw, triu,
      e3, state, cc_t.reshape(DEC_SEQ, DEC_BATCH, cdim), bb_t.reshape(DEC_SEQ, DEC_BATCH, cdim),
      xw_t.reshape(DEC_SEQ, DEC_BATCH, D))


def _ssds_post_kernel(ypre_ref, efull_ref, yo_ref, z_ref, nw_ref, o_ref):
    get_y = lambda cols: ypre_ref[:, cols] + efull_ref[:, cols] * yo_ref[:, cols]
    _gated_group_norm(get_y, z_ref, nw_ref, o_ref)


def _ssds_post_call(ypre, efull, yo_t, proj, nw):
    step = pl.BlockSpec((DEC_BATCH, D), lambda t: (t, 0))
    return pl.pallas_call(
        _ssds_post_kernel,
        grid=(DEC_SEQ,),
        in_specs=[step, step, step,
                  pl.BlockSpec((DEC_BATCH, D), lambda t: (NP_ROWS // DEC_BATCH + t, PZ)),
                  pl.BlockSpec((1, D), lambda t: (0, 0))],
        out_specs=step,
        out_shape=jax.ShapeDtypeStruct((NS_ROWS, D), BF16),
        compiler_params=_params("arbitrary"),
        name="ssd_sample_post",
    )(ypre, efull, yo_t, proj, nw)


def _mlp_kernel(u_ref, v_ref, lnw_ref, lnb_ref, ws_ref, bsx_ref, wsx_ref, bsx4_ref,
                y_ref, cv_ref, wm_ref, cvs_ref):
    i = pl.program_id(0)
    T = CHUNK

    @pl.when(i == 0)
    def _():
        row = lax.broadcasted_iota(jnp.int32, (T, T), 0)
        lane = lax.broadcasted_iota(jnp.int32, (T, T), 1)
        for g in range(MLP_GROUPS):
            wm_ref[g] = jnp.where(row >= lane, ws_ref[g], 0.0).astype(BF16)

    def vnorm(rows):
        vg = v_ref[rows, :].astype(F32)
        xc = vg - jnp.mean(vg, axis=-1, keepdims=True)
        y = xc * lax.rsqrt(jnp.mean(xc * xc, axis=-1, keepdims=True) + EPS)
        return y * lnw_ref[...] + lnb_ref[...]

    @pl.when(i < NP_MT)
    def _():
        for cc in range(TMM // T):
            rows = slice(cc * T, (cc + 1) * T)
            vnb = vnorm(rows).astype(BF16)
            for g in range(MLP_GROUPS):
                cols = slice(g * MLP_GROUP_DIM, (g + 1) * MLP_GROUP_DIM)
                sv = _dot(wm_ref[g], vnb[:, cols]) + bsx_ref[:, cols]
                y_ref[rows, cols] = (u_ref[rows, cols].astype(F32) * sv).astype(BF16)

    @pl.when(i == NP_MT)
    def _():
        B = DEC_BATCH
        for t in range(DEC_SEQ):
            rows = slice(t * B, (t + 1) * B)
            vn = vnorm(rows)
            cvs_ref[rows, :] = vn
            cv_ref[:, t, :] = vn
        for t in range(DEC_SEQ):
            rows = slice(t * B, (t + 1) * B)
            acc = bsx4_ref[t:t + 1, :]
            for s in range(t + 1):
                acc = acc + wsx_ref[4 * t + s:4 * t + s + 1, :] * cvs_ref[s * B:(s + 1) * B, :]
            y_ref[rows, :] = (u_ref[rows, :].astype(F32) * acc).astype(BF16)


def _mlp_call(proj, lnw, lnb, ws, bsx, wsx, bsx4):
    full = lambda shape: pl.BlockSpec(shape, lambda i: (0,) * len(shape))
    return pl.pallas_call(
        _mlp_kernel,
        grid=(N_MT,),
        in_specs=[pl.BlockSpec((TMM, D), lambda i: (i, PU)),
                  pl.BlockSpec((TMM, D), lambda i: (i, PV)),
                  full((1, D)), full((1, D)),
                  full((MLP_GROUPS, CHUNK, CHUNK)),
                  full((CHUNK, D)), full((16, D)), full((8, D))],
        out_specs=[pl.BlockSpec((TMM, D), lambda i: (i, 0)),
                   full((DEC_BATCH, DEC_SEQ, D))],
        out_shape=[jax.ShapeDtypeStruct((M_ROWS, D), BF16),
                   jax.ShapeDtypeStruct((DEC_BATCH, DEC_SEQ, D), F32)],
        scratch_shapes=[pltpu.VMEM((MLP_GROUPS, CHUNK, CHUNK), BF16),
                        pltpu.VMEM((NS_ROWS, D), F32)],
        compiler_params=_params("arbitrary"),
        name="gmlp",
    )(proj, proj, lnw, lnb, ws, bsx, wsx, bsx4)


def _cast_rows(src_ref, dst_ref, chunk=256):
    def body(r, carry):
        rows = pl.ds(pl.multiple_of(r * chunk, chunk), chunk)
        dst_ref[rows, :] = src_ref[rows, :].astype(dst_ref.dtype)
        return carry
    lax.fori_loop(0, src_ref.shape[0] // chunk, body, 0)


def _per_tile(i, prompt_fn, sample_fn):
    pl.when(i < NP_MT)(prompt_fn)
    pl.when(i == NP_MT)(sample_fn)


def _merge_kernel(ysp_ref, yss_ref, ym_ref, ga_ref, gb_ref, w1_ref, w2_ref, o_ref, w1b, w2b):
    i = pl.program_id(1)

    @pl.when(i == 0)
    def _():
        _cast_rows(w1_ref, w1b)
        _cast_rows(w2_ref, w2b)

    def emit(ys, rows):
        a1 = _dot(ys, w1b[...])
        a2 = _dot(ym_ref[rows, :], w2b[...])
        o_ref[rows, :] = (jax.nn.sigmoid(ga_ref[rows, :].astype(F32)) * a1
                          + jax.nn.sigmoid(gb_ref[rows, :].astype(F32)) * a2).astype(BF16)

    _per_tile(i,
              lambda: emit(ysp_ref[...], slice(None)),
              lambda: emit(yss_ref[...], slice(0, NS_ROWS)))


def _merge_call(ysp, yss, ym, proj, w1, w2):
    tn = 512
    nb = D // tn
    return pl.pallas_call(
        _merge_kernel,
        grid=(nb, N_MT),
        in_specs=[pl.BlockSpec((TMM, D), lambda j, i: (jnp.minimum(i, NP_MT - 1), 0)),
                  pl.BlockSpec((NS_ROWS, D), lambda j, i: (0, 0)),
                  pl.BlockSpec((TMM, D), lambda j, i: (i, 0)),
                  pl.BlockSpec((TMM, tn), lambda j, i: (i, PGA * nb + j)),
                  pl.BlockSpec((TMM, tn), lambda j, i: (i, PGB * nb + j)),
                  pl.BlockSpec((D, tn), lambda j, i: (0, j)),
                  pl.BlockSpec((D, tn), lambda j, i: (0, j))],
        out_specs=pl.BlockSpec((TMM, tn), lambda j, i: (i, j)),
        out_shape=jax.ShapeDtypeStruct((M_ROWS, D), BF16),
        scratch_shapes=[pltpu.VMEM((D, tn), BF16), pltpu.VMEM((D, tn), BF16)],
        compiler_params=_params("arbitrary", "arbitrary"),
        name="branch_merge",
    )(ysp, yss, ym, proj, proj, w1, w2)


def _resid_kernel(a_ref, w_ref, rp_ref, rs_ref, gp_ref, gs_ref, o_ref, wb):
    i = pl.program_id(1)

    @pl.when(i == 0)
    def _():
        _cast_rows(w_ref, wb)

    def prompt():
        o_ref[...] = rp_ref[...] + gp_ref[0] * _dot(a_ref[...], wb[...])

    def sample():
        acc = _dot(a_ref[0:NS_ROWS, :], wb[...])
        tn = acc.shape[-1]
        acc3 = acc.reshape(DEC_SEQ, DEC_BATCH, tn) * gs_ref[...][None]
        o_ref[0:NS_ROWS, :] = _sample_rows(rs_ref) + acc3.reshape(NS_ROWS, tn)

    _per_tile(i, prompt, sample)


def _resid_call(a, w, rp, rs, rs_block, mod_p, mod_s, k_gate, tn, name, single_buffer_w=False):
    kdim = a.shape[1]
    nb = D // tn
    w_mode = dict(pipeline_mode=pl.Buffered(1)) if single_buffer_w else {}
    if rs.ndim == 3:
        rs_spec = pl.BlockSpec((DEC_BATCH, DEC_SEQ, tn), lambda j, i: (0, 0, j))
    else:
        rs_spec = pl.BlockSpec((NS_ROWS, tn), lambda j, i: (rs_block, j))
    return pl.pallas_call(
        _resid_kernel,
        grid=(nb, N_MT),
        in_specs=[pl.BlockSpec((TMM, kdim), lambda j, i: (i, 0)),
                  pl.BlockSpec((kdim, tn), lambda j, i: (0, j), **w_mode),
                  pl.BlockSpec((TMM, tn), lambda j, i: (jnp.minimum(i, NP_MT - 1), j)),
                  rs_spec,
                  pl.BlockSpec((1, 1, tn), lambda j, i: (_seq_of_mtile(i), 0, k_gate * nb + j)),
                  pl.BlockSpec((DEC_BATCH, tn), lambda j, i: (0, k_gate * nb + j))],
        out_specs=pl.BlockSpec((TMM, tn), lambda j, i: (i, j)),
        out_shape=jax.ShapeDtypeStruct((M_ROWS, D), F32),
        scratch_shapes=[pltpu.VMEM((kdim, tn), BF16)],
        compiler_params=_params("arbitrary", "arbitrary"),
        name=name,
    )(a, w, rp, rs, mod_p, mod_s)


TN_FF = 512
N_FF_BLOCKS = D_FF // TN_FF


def _up_kernel(a_ref, w_ref, cw_ref, cb_ref, fst_ref,
               h_ref, fcp_ref, fcs_ref, wab, wvb, acc_a, acc_v, wsa, wsv, wsem):
    j = pl.program_id(0)
    i = pl.program_id(1)
    rc = 64

    def w_fetch(jj):
        col = lambda blk: pl.ds(pl.multiple_of(blk * TN_FF, TN_FF), TN_FF)
        return (pltpu.make_async_copy(w_ref.at[:, col(jj)], wsa, wsem.at[0]),
                pltpu.make_async_copy(w_ref.at[:, col(N_FF_BLOCKS + jj)], wsv, wsem.at[1]))

    @pl.when(i == 0)
    def _():
        @pl.when(j == 0)
        def _():
            for c in w_fetch(0):
                c.start()

        for c in w_fetch(j):
            c.wait()
        _cast_rows(wsa, wab)
        _cast_rows(wsv, wvb)

        @pl.when(j + 1 < N_FF_BLOCKS)
        def _():
            for c in w_fetch(j + 1):
                c.start()

    def prompt():
        starts_sequence = i % MT_PER_SEQ == 0
        acc_a[0:8, :] = jnp.where(starts_sequence, 0.0, acc_a[TMM:TMM + 8, :])
        x = a_ref[...]
        acc_a[8:8 + TMM, :] = _dot(x, wab[...])
        acc_v[...] = _dot(x, wvb[...])
        cw, cb = cw_ref[...], cb_ref[...]
        for r in range(0, TMM, rc):
            conv = (cb + cw[2:3] * acc_a[8 + r:8 + r + rc, :]
                    + cw[1:2] * acc_a[7 + r:7 + r + rc, :]
                    + cw[0:1] * acc_a[6 + r:6 + r + rc, :])
            h_ref[r:r + rc, :] = (jax.nn.gelu(conv) * acc_v[r:r + rc, :]).astype(BF16)
        fcp_ref[0] = acc_a[TMM + 6:TMM + 8, :]

    def sample():
        B = DEC_BATCH
        x = a_ref[0:NS_ROWS, :]
        acc_a[8:8 + NS_ROWS, :] = _dot(x, wab[...])
        acc_v[0:NS_ROWS, :] = _dot(x, wvb[...])
        cw, cb = cw_ref[...], cb_ref[...]

        def pre(t, r):
            if t < 0:
                return fst_ref[r:r + rc, t + 2, :]
            return acc_a[8 + t * B + r:8 + t * B + r + rc, :]

        for t in range(DEC_SEQ):
            for r in range(0, B, rc):
                conv = cb + cw[2:3] * pre(t, r) + cw[1:2] * pre(t - 1, r) + cw[0:1] * pre(t - 2, r)
                h_ref[t * B + r:t * B + r + rc, :] = (
                    jax.nn.gelu(conv) * acc_v[t * B + r:t * B + r + rc, :]).astype(BF16)
        fcs_ref[:, 0, :] = acc_a[8 + 2 * B:8 + 3 * B, :]
        fcs_ref[:, 1, :] = acc_a[8 + 3 * B:8 + 4 * B, :]

    _per_tile(i, prompt, sample)


def _up_call(n2, w_up, cw, cb, fst):
    return pl.pallas_call(
        _up_kernel,
        grid=(N_FF_BLOCKS, N_MT),
        in_specs=[pl.BlockSpec((TMM, D), lambda j, i: (i, 0)),
                  pl.BlockSpec(memory_space=pl.ANY),
                  pl.BlockSpec((3, TN_FF), lambda j, i: (0, j)),
                  pl.BlockSpec((1, TN_FF), lambda j, i: (0, j)),
                  pl.BlockSpec((DEC_BATCH, 2, TN_FF), lambda j, i: (0, 0, j))],
        out_specs=[pl.BlockSpec((TMM, TN_FF), lambda j, i: (i, j)),
                   pl.BlockSpec((1, 2, TN_FF), lambda j, i: (_seq_of_mtile(i), 0, j)),
                   pl.BlockSpec((DEC_BATCH, 2, TN_FF), lambda j, i: (0, 0, j))],
        out_shape=[jax.ShapeDtypeStruct((M_ROWS, D_FF), BF16),
                   jax.ShapeDtypeStruct((BATCH, 2, D_FF), F32),
                   jax.ShapeDtypeStruct((DEC_BATCH, 2, D_FF), F32)],
        scratch_shapes=[pltpu.VMEM((D, TN_FF), BF16), pltpu.VMEM((D, TN_FF), BF16),
                        pltpu.VMEM((TMM + 8, TN_FF), F32), pltpu.VMEM((TMM, TN_FF), F32),
                        pltpu.VMEM((D, TN_FF), F32), pltpu.VMEM((D, TN_FF), F32),
                        pltpu.SemaphoreType.DMA((2,))],
        compiler_params=_params("arbitrary", "arbitrary"),
        name="ffn_up",
    )(n2, w_up, cw, cb, fst)


def _final_kernel(x_ref, w_ref, yp_ref, ys_ref):
    i = pl.program_id(0)

    @pl.when(i < NP_MT)
    def _():
        for r in range(0, TMM, TM):
            yp_ref[r:r + TM, :] = _rms(x_ref[r:r + TM, :]) * w_ref[...]

    @pl.when(i == NP_MT)
    def _():
        y = _rms(x_ref[0:NS_ROWS, :]) * w_ref[...]
        for t in range(DEC_SEQ):
            ys_ref[:, t, :] = y[t * DEC_BATCH:(t + 1) * DEC_BATCH]


def _final_call(x3, w):
    return pl.pallas_call(
        _final_kernel,
        grid=(N_MT,),
        in_specs=[pl.BlockSpec((TMM, D), lambda i: (i, 0)),
                  pl.BlockSpec((1, D), lambda i: (0, 0))],
        out_specs=[pl.BlockSpec((TMM, D), lambda i: (jnp.minimum(i, NP_MT - 1), 0)),
                   pl.BlockSpec((DEC_BATCH, DEC_SEQ, D), lambda i: (0, 0, 0))],
        out_shape=[jax.ShapeDtypeStruct((NP_ROWS, D), F32),
                   jax.ShapeDtypeStruct((DEC_BATCH, DEC_SEQ, D), F32)],
        compiler_params=_params("arbitrary"),
        name="final_norm",
    )(x3, w)


def _to_time_major(a):
    return jnp.transpose(a, (1, 0, 2))


def kernel(x_prompt, x_sample, state_ssm, state_ssd_conv, state_ffn_conv, c_prompt, c_sample,
           norm1_w, w_ada, b_ada, w_in, ssd_conv_w, ssd_conv_b, dt_bias, a_log, d_skip,
           ssd_norm_w, mlp_ln_w, mlp_ln_b, w_spatial, b_spatial, w_ssd_o, w_mlp_o, w_out,
           norm2_w, w_up, ffn_conv_w, ffn_conv_b, w_down, final_norm_w):
    assert w_in.shape[0] == 1, "single-layer trunk"
    row = lambda v: v.reshape(1, -1)

    xp = x_prompt.reshape(NP_ROWS, D)
    xs = x_sample

    mod_p, mod_s = _ada_call(c_prompt, c_sample, w_ada[0], row(b_ada[0]))
    K_SHIFT1, K_SCALE1, K_GATE1, K_SHIFT2, K_SCALE2, K_GATE2 = range(6)

    xp_spec = pl.BlockSpec((TMM, D), lambda i: (jnp.minimum(i, NP_MT - 1), 0))
    xs_spec = pl.BlockSpec((DEC_BATCH, DEC_SEQ, D), lambda i: (0, 0, 0))
    w_in_t = w_in[0].T
    n1, dt_raw = _norm_call(xp, xp_spec, xs, xs_spec, row(norm1_w[0]), mod_p, mod_s,
                            K_SCALE1, K_SHIFT1, w_in_t)
    cst_t = _to_time_major(state_ssd_conv[0])
    proj, cst_p, ncs_t = _inproj_call(n1, w_in_t, ssd_conv_w[0], row(ssd_conv_b[0]), cst_t)

    hp = jnp.pad(jnp.stack([dt_bias[0], a_log[0], d_skip[0]]), ((0, 5), (0, LANE - HEADS)))
    hpc = hp.T
    triu = jnp.asarray(np.triu(np.ones((CHUNK, CHUNK), np.float32)), dtype=BF16)
    dskx = row(jnp.repeat(d_skip[0], HEADDIM))
    nw = row(ssd_norm_w[0])

    head_of_col = np.arange(D) // HEADDIM
    ex = jnp.asarray(np.arange(LANE)[:, None] == head_of_col[None, :], dtype=BF16)
    grp_of_head = np.arange(LANE) // (HEADS // GROUPS)
    seg = jnp.asarray((np.arange(GROUPS * STATE)[:, None] // STATE == grp_of_head[None, :])
                      & (np.arange(LANE)[None, :] < HEADS), dtype=BF16)
    ypre, efull, xw_t, bb_t, cc_t, e3 = _ssds_prep_call(proj, dt_raw, hp, dskx, ex, seg)
    gw = (HEADS // GROUPS) * HEADDIM
    y_ssd_p, ssm_p, yo_t, ssm_s = _ssd_fused_call(
        proj, dt_raw, hp, hpc, dskx, nw, triu,
        e3[:, :HEADS], state_ssm[0].reshape(DEC_BATCH, GROUPS, gw, STATE), cc_t, bb_t, xw_t)
    y_ssd_s = _ssds_post_call(ypre, efull, yo_t.reshape(NS_ROWS, D), proj, nw)

    per_col = lambda a: jnp.repeat(a, MLP_GROUP_DIM, axis=1)
    bsx = per_col(b_spatial[0][:, :CHUNK].T)
    ws4 = w_spatial[0][:, :DEC_SEQ, :DEC_SEQ]
    wsx = per_col(jnp.transpose(ws4, (1, 2, 0)).reshape(DEC_SEQ * DEC_SEQ, MLP_GROUPS))
    bsx4 = jnp.pad(bsx[:DEC_SEQ], ((0, 8 - DEC_SEQ), (0, 0)))
    y_mlp, cv_s = _mlp_call(proj, row(mlp_ln_w[0]), row(mlp_ln_b[0]), w_spatial[0], bsx, wsx, bsx4)

    mixed = _merge_call(y_ssd_p, y_ssd_s, y_mlp, proj, w_ssd_o[0], w_mlp_o[0])
    x2 = _resid_call(mixed, w_out[0], xp, xs, 0, mod_p, mod_s, K_GATE1, 1024, "out_proj")

    x2p_spec = pl.BlockSpec((TMM, D), lambda i: (jnp.minimum(i, NP_MT - 1), 0))
    x2s_spec = pl.BlockSpec((NS_ROWS, D), lambda i: (NP_TILES, 0))
    (n2,) = _norm_call(x2, x2p_spec, x2, x2s_spec, row(norm2_w[0]), mod_p, mod_s,
                       K_SCALE2, K_SHIFT2, None)
    h, ffn_p, ffn_s = _up_call(n2, w_up[0], ffn_conv_w[0], row(ffn_conv_b[0]), state_ffn_conv[0])
    x3 = _resid_call(h, w_down[0], x2, x2, NP_TILES, mod_p, mod_s, K_GATE2, 512, "ffn_down",
                     single_buffer_w=True)

    y_p, y_s = _final_call(x3, row(final_norm_w))

    from_t = lambda a, t, c: jnp.transpose(a.reshape(t, DEC_BATCH, c), (1, 0, 2))
    return (y_p.reshape(BATCH, SEQ, D),
            y_s,
            ssm_p.reshape(1, BATCH, HEADS, HEADDIM, STATE),
            ssm_s.reshape(1, DEC_BATCH, HEADS, HEADDIM, STATE),
            cst_p[None],
            from_t(ncs_t, 3, CONV_DIM)[None],
            ffn_p[None],
            ffn_s[None],
            cv_s[None])
```

```python
import functools

import jax
import jax.numpy as jnp
import numpy as np
from jax import lax
from jax.experimental import pallas as pl
from jax.experimental.pallas import tpu as pltpu

F32 = jnp.float32
BF16 = jnp.bfloat16

D = 2048
BATCH, SEQ = 4, 2048
DEC_BATCH, DEC_SEQ = 128, 4
NP_ROWS = BATCH * SEQ
NS_ROWS = DEC_BATCH * DEC_SEQ
M_ROWS = NP_ROWS + NS_ROWS
TM = 512
NP_TILES = NP_ROWS // TM
TMM = 1024
NP_MT = NP_ROWS // TMM
N_MT = NP_MT + 1
MT_PER_SEQ = SEQ // TMM
HEADS, HEADDIM, GROUPS, STATE = 32, 64, 4, 128
CHUNK = 128
N_CHUNKS = SEQ // CHUNK
CONV_DIM = D + 2 * GROUPS * STATE
MLP_GROUPS = 8
MLP_GROUP_DIM = D // MLP_GROUPS
D_FF = 5632
EPS = 1e-6
DT_COL = D + CONV_DIM
TN_IN = 1024
LANE = 128
VMEM_LIMIT = 56 * 1024 * 1024


def _params(*sem, flags=None):
    return pltpu.CompilerParams(dimension_semantics=sem, vmem_limit_bytes=VMEM_LIMIT, flags=flags)


def _dot(a, b):
    return jnp.dot(a, b, preferred_element_type=F32)


def _dot_nt(a, b_t):
    return lax.dot_general(a, b_t, (((1,), (1,)), ((), ())), preferred_element_type=F32)


def _split_bf16(v, terms):
    out = []
    r = v
    for _ in range(terms):
        p = r.astype(BF16)
        out.append(p)
        r = r - p.astype(F32)
    return out


def _dot_exact_rhs(v, e, terms=3):
    acc = None
    for p in _split_bf16(v, terms):
        d = _dot(p, e)
        acc = d if acc is None else acc + d
    return acc


def _silu(x):
    return x * jax.nn.sigmoid(x)


def _softplus(x):
    return jnp.maximum(x, 0.0) + jnp.log1p(jnp.exp(-jnp.abs(x)))


def _rms(x):
    return x * lax.rsqrt(jnp.mean(x * x, axis=-1, keepdims=True) + EPS)


def _ada_kernel(cp_ref, cs_ref, w_ref, b_ref, op_ref, os_ref):
    w = w_ref[...].astype(BF16)
    cp8 = jnp.concatenate([cp_ref[...], jnp.zeros((8 - BATCH, D), F32)], axis=0)
    op_ref[:, 0, :] = (_dot(_silu(cp8).astype(BF16), w) + b_ref[...])[0:BATCH]
    os_ref[...] = _dot(_silu(cs_ref[...]).astype(BF16), w) + b_ref[...]


def _ada_call(c_prompt, c_sample, w, b):
    tn = 1024
    return pl.pallas_call(
        _ada_kernel,
        grid=(6 * D // tn,),
        in_specs=[pl.BlockSpec((BATCH, D), lambda j: (0, 0)),
                  pl.BlockSpec((DEC_BATCH, D), lambda j: (0, 0)),
                  pl.BlockSpec((D, tn), lambda j: (0, j)),
                  pl.BlockSpec((1, tn), lambda j: (0, j))],
        out_specs=[pl.BlockSpec((BATCH, 1, tn), lambda j: (0, 0, j)),
                   pl.BlockSpec((DEC_BATCH, tn), lambda j: (0, j))],
        out_shape=[jax.ShapeDtypeStruct((BATCH, 1, 6 * D), F32),
                   jax.ShapeDtypeStruct((DEC_BATCH, 6 * D), F32)],
        compiler_params=_params("arbitrary"),
        name="ada_mod",
    )(c_prompt, c_sample, w, b)


def _sample_rows(ref):
    if len(ref.shape) == 2:
        return ref[...]
    return jnp.concatenate([ref[:, t, :] for t in range(DEC_SEQ)], axis=0)


def _norm_kernel(with_dt, xp_ref, xs_ref, nw_ref, scp_ref, shp_ref, scs_ref, shs_ref, *rest):
    if with_dt:
        wdt_ref, n_ref, dt_ref = rest
    else:
        (n_ref,) = rest
    i = pl.program_id(0)

    def emit(n, rows):
        nb = n.astype(BF16)
        n_ref[rows, :] = nb
        if with_dt:
            dt_ref[rows, :] = lax.dot_general(nb, wdt_ref[...].astype(BF16),
                                              (((1,), (1,)), ((), ())),
                                              preferred_element_type=F32)

    @pl.when(i < NP_MT)
    def _():
        for r in range(0, TMM, TM):
            rows = slice(r, r + TM)
            y = _rms(xp_ref[rows, :]) * nw_ref[...]
            emit(y * (1.0 + scp_ref[0]) + shp_ref[0], rows)

    @pl.when(i == NP_MT)
    def _():
        y = _rms(_sample_rows(xs_ref)) * nw_ref[...]
        y3 = y.reshape(DEC_SEQ, DEC_BATCH, D)
        emit((y3 * (1.0 + scs_ref[...])[None] + shs_ref[...][None]).reshape(NS_ROWS, D),
             slice(0, NS_ROWS))


def _seq_of_mtile(i):
    return jnp.minimum(i // MT_PER_SEQ, BATCH - 1)


def _norm_call(xp, xp_spec, xs, xs_spec, nw, mod_p, mod_s, k_scale, k_shift, w_in_t):
    with_dt = w_in_t is not None
    in_specs = [
        xp_spec, xs_spec,
        pl.BlockSpec((1, D), lambda i: (0, 0)),
        pl.BlockSpec((1, 1, D), lambda i: (_seq_of_mtile(i), 0, k_scale)),
        pl.BlockSpec((1, 1, D), lambda i: (_seq_of_mtile(i), 0, k_shift)),
        pl.BlockSpec((DEC_BATCH, D), lambda i: (0, k_scale)),
        pl.BlockSpec((DEC_BATCH, D), lambda i: (0, k_shift)),
    ]
    args = [xp, xs, nw, mod_p, mod_p, mod_s, mod_s]
    out_specs = [pl.BlockSpec((TMM, D), lambda i: (i, 0))]
    out_shape = [jax.ShapeDtypeStruct((M_ROWS, D), BF16)]
    if with_dt:
        in_specs.append(pl.BlockSpec((pl.Element(LANE), pl.Element(D)), lambda i: (DT_COL, 0)))
        args.append(w_in_t)
        out_specs.append(pl.BlockSpec((TMM, LANE), lambda i: (i, 0)))
        out_shape.append(jax.ShapeDtypeStruct((M_ROWS, LANE), F32))
    return pl.pallas_call(
        functools.partial(_norm_kernel, with_dt),
        grid=(N_MT,),
        in_specs=in_specs,
        out_specs=out_specs,
        out_shape=out_shape,
        compiler_params=_params("arbitrary"),
        name="norm_mod_dt" if with_dt else "norm_mod",
    )(*args)


N_IN_BLOCKS = 13
UVG_ROW = DT_COL + HEADS
FIRST_CONV_BLOCK = 10


def _in_src_row(j):
    row = jnp.where(j < 2, j * TN_IN,
                    jnp.where(j < FIRST_CONV_BLOCK, UVG_ROW + (j - 2) * TN_IN,
                              D + (j - FIRST_CONV_BLOCK) * TN_IN))
    return pl.multiple_of(row, HEADS)


def _in_conv_block(j):
    return jnp.maximum(j - FIRST_CONV_BLOCK, 0)


def _inproj_kernel(a_ref, wt_ref, cw_ref, cb_ref, cst_ref, o_ref, csp_ref, css_ref,
                   wbf_ref, acc_ref, wst_ref, wsem):
    j = pl.program_id(0)
    i = pl.program_id(1)
    rc = 64
    B = DEC_BATCH

    def w_fetch(jj):
        return pltpu.make_async_copy(wt_ref.at[pl.ds(_in_src_row(jj), TN_IN), :], wst_ref, wsem)

    @pl.when(i == 0)
    def _():
        @pl.when(j == 0)
        def _():
            w_fetch(0).start()

        w_fetch(j).wait()
        for r in range(TN_IN // LANE):
            rows = slice(r * LANE, (r + 1) * LANE)
            wbf_ref[rows, :] = wst_ref[rows, :].astype(BF16)

        @pl.when(j + 1 < N_IN_BLOCKS)
        def _():
            w_fetch(j + 1).start()

    def elementwise(fn, rows):
        def body():
            acc_ref[8:8 + rows, :] = _dot_nt(a_ref[0:rows, :], wbf_ref[...])
            for r in range(0, rows, rc):
                val = acc_ref[8 + r:8 + r + rc, :]
                o_ref[r:r + rc, :] = (val if fn is None else fn(val)).astype(o_ref.dtype)
        return body

    def conv_prompt():
        starts_sequence = i % MT_PER_SEQ == 0
        acc_ref[0:8, :] = jnp.where(starts_sequence, 0.0, acc_ref[TMM:TMM + 8, :])
        acc_ref[8:8 + TMM, :] = _dot_nt(a_ref[...], wbf_ref[...])
        cw, cb = cw_ref[...], cb_ref[...]
        for r in range(0, TMM, rc):
            conv = cb + cw[3:4] * acc_ref[8 + r:8 + r + rc, :]
            for k in range(3):
                conv = conv + cw[k:k + 1] * acc_ref[5 + k + r:5 + k + r + rc, :]
            o_ref[r:r + rc, :] = _silu(conv).astype(o_ref.dtype)
        csp_ref[0] = acc_ref[TMM + 5:TMM + 8, :]

    def conv_sample():
        acc_ref[8:8 + NS_ROWS, :] = _dot_nt(a_ref[0:NS_ROWS, :], wbf_ref[...])
        cw, cb = cw_ref[...], cb_ref[...]

        def pre(t, r):
            if t < 0:
                return cst_ref[t + 3, r:r + rc, :]
            return acc_ref[8 + t * B + r:8 + t * B + r + rc, :]

        for t in range(DEC_SEQ):
            for r in range(0, B, rc):
                conv = cb + cw[3:4] * pre(t, r)
                for k in range(3):
                    conv = conv + cw[k:k + 1] * pre(t - 3 + k, r)
                o_ref[t * B + r:t * B + r + rc, :] = _silu(conv).astype(o_ref.dtype)
        for t in range(1, DEC_SEQ):
            css_ref[t - 1] = acc_ref[8 + t * B:8 + (t + 1) * B, :]

    is_conv = j >= FIRST_CONV_BLOCK
    kinds = ((j < 2, _silu),
             (jnp.logical_and(j >= 2, j < 6), jax.nn.gelu),
             (jnp.logical_and(j >= 6, j < FIRST_CONV_BLOCK), None))
    for cond, fn in kinds:
        pl.when(jnp.logical_and(cond, i < NP_MT))(elementwise(fn, TMM))
        pl.when(jnp.logical_and(cond, i == NP_MT))(elementwise(fn, NS_ROWS))
    pl.when(jnp.logical_and(is_conv, i < NP_MT))(conv_prompt)
    pl.when(jnp.logical_and(is_conv, i == NP_MT))(conv_sample)


def _inproj_call(n1, w_in_t, cw, cb, cst_t):
    cblk = _in_conv_block
    seq = lambda j, i: jnp.where(j < FIRST_CONV_BLOCK, 0, _seq_of_mtile(i))
    return pl.pallas_call(
        _inproj_kernel,
        grid=(N_IN_BLOCKS, N_MT),
        in_specs=[pl.BlockSpec((TMM, D), lambda j, i: (i, 0)),
                  pl.BlockSpec(memory_space=pl.ANY),
                  pl.BlockSpec((4, TN_IN), lambda j, i: (0, cblk(j))),
                  pl.BlockSpec((1, TN_IN), lambda j, i: (0, cblk(j))),
                  pl.BlockSpec((3, DEC_BATCH, TN_IN), lambda j, i: (0, 0, cblk(j)))],
        out_specs=[pl.BlockSpec((TMM, TN_IN), lambda j, i: (i, j)),
                   pl.BlockSpec((1, 3, TN_IN), lambda j, i: (seq(j, i), 0, cblk(j))),
                   pl.BlockSpec((3, DEC_BATCH, TN_IN), lambda j, i: (0, 0, cblk(j)))],
        out_shape=[jax.ShapeDtypeStruct((M_ROWS, N_IN_BLOCKS * TN_IN), BF16),
                   jax.ShapeDtypeStruct((BATCH, 3, CONV_DIM), F32),
                   jax.ShapeDtypeStruct((3, DEC_BATCH, CONV_DIM), F32)],
        scratch_shapes=[pltpu.VMEM((TN_IN, D), BF16), pltpu.VMEM((TMM + 8, TN_IN), F32),
                        pltpu.VMEM((TN_IN, D), F32), pltpu.SemaphoreType.DMA(())],
        compiler_params=_params("arbitrary", "arbitrary"),
        name="in_proj",
    )(n1, w_in_t, cw, cb, cst_t)


PZ, PU, PV, PGA, PGB, PX = 0, 1, 2, 3, 4, 5
PBC_1024 = 12


def _gated_group_norm(get_y, zact_ref, nw_ref, o_ref, rows=slice(None)):
    gw = D // GROUPS
    for g in range(GROUPS):
        cols = slice(g * gw, (g + 1) * gw)
        gg = _rms(get_y(cols) * zact_ref[rows, cols].astype(F32))
        o_ref[rows, cols] = (gg * nw_ref[:, cols]).astype(o_ref.dtype)


CHUNKS_PER_STEP = 4


def _ssd_prompt_kernel(z_ref, x_ref, bc_ref, dt_ref, hp_ref, hpc_ref, dskx_ref, nw_ref, triu_ref,
                       y_ref, ssm_ref, st_ref, yscr_ref):
    c = pl.program_id(1)
    T = CHUNK
    cdim = GROUPS * STATE

    @pl.when(c == 0)
    def _():
        st_ref[...] = jnp.zeros(st_ref.shape, F32)

    hp = hp_ref[...]
    a_col = -jnp.exp(hpc_ref[0:HEADS, 1:2])
    triu = triu_ref[...]
    row = lax.broadcasted_iota(jnp.int32, (T, T), 0)
    lane = lax.broadcasted_iota(jnp.int32, (T, T), 1)
    causal = row >= lane
    left = lane < HEADDIM
    mask_l = jnp.where(left, 1.0, 0.0).astype(BF16)
    nt = (((1,), (1,)), ((), ()))
    pairs_per_group = HEADS // GROUPS // 2

    for sub in range(CHUNKS_PER_STEP):
        rows = slice(sub * T, (sub + 1) * T)
        y_buf = yscr_ref.at[sub % 2]
        dt_t = _softplus((dt_ref[rows, :] + hp[0:1, :]).T[0:HEADS, :])
        cs_t = _dot_exact_rhs(dt_t * a_col, triu)
        rsub_t = cs_t - jnp.log(dt_t)
        cs = jnp.concatenate([cs_t, jnp.zeros((LANE - HEADS, T), F32)], axis=0).T

        for g in range(GROUPS):
            c_b = bc_ref[rows, cdim + g * STATE:cdim + (g + 1) * STATE]
            b_b = bc_ref[rows, g * STATE:(g + 1) * STATE]
            cb = lax.dot_general(c_b, b_b, nt, preferred_element_type=F32)
            b_t = b_b.astype(F32).T
            for k4 in range(pairs_per_group):
                k = g * pairs_per_group + k4
                cols = slice(k * LANE, (k + 1) * LANE)
                xpb = x_ref[rows, cols]
                x_lo = xpb * mask_l
                xbd = jnp.concatenate([x_lo, xpb - x_lo], axis=0)
                st = st_ref[k]
                yraw = _dot(c_b, st.astype(BF16))
                scores, bws, colbs, alasts = [], [], [], []
                for h in (2 * k, 2 * k + 1):
                    colb = jnp.broadcast_to(cs[:, h:h + 1], (T, T))
                    alast = cs_t[h:h + 1, T - 1:T]
                    decay_dt = jnp.exp(jnp.where(causal, colb - rsub_t[h:h + 1, :], -jnp.inf))
                    scores.append((cb * decay_dt).astype(BF16))
                    wrow = jnp.exp(alast - cs_t[h:h + 1, :]) * dt_t[h:h + 1, :]
                    bws.append((b_t * wrow).astype(BF16))
                    colbs.append(colb)
                    alasts.append(alast)
                ecol = jnp.exp(jnp.where(left, colbs[0], colbs[1]))
                elast = jnp.exp(jnp.where(left[0:1], alasts[0], alasts[1]))
                y_buf[:, cols] = (_dot(jnp.concatenate(scores, axis=1), xbd) + ecol * yraw
                                  + dskx_ref[:, cols] * xpb.astype(F32))
                st_ref[k] = elast * st + _dot(jnp.concatenate(bws, axis=1), xbd)
        _gated_group_norm(lambda cols: y_buf[:, cols], z_ref, nw_ref, y_ref, rows)

    @pl.when(c == N_CHUNKS // CHUNKS_PER_STEP - 1)
    def _():
        for k in range(HEADS // 2):
            ssm_ref[0, k] = st_ref[k].T


def _ssds_prep_step(tt, x_ref, bc_ref, dt_ref, hp_ref, dskx_ref, ex_ref, seg_ref,
                    ypre_ref, efull_ref, xw_ref, bb_ref, cc_ref, e3_ref):
    B = DEC_BATCH
    cdim = GROUPS * STATE
    blk = lambda t: slice(t * B, (t + 1) * B)
    x_of = lambda t: x_ref[blk(t), :].astype(F32)
    b_of = lambda t: bc_ref[blk(t), 0:cdim].astype(F32)
    c_of = lambda t: bc_ref[blk(t), cdim:2 * cdim].astype(F32)

    bb_ref[...] = b_of(tt)
    cc_ref[...] = c_of(tt)

    hp = hp_ref[...]
    a_neg = -jnp.exp(hp[1:2, :])
    dts, css = [], []
    run = None
    for t in range(DEC_SEQ):
        dt = _softplus(dt_ref[blk(t), :] + hp[0:1, :])
        run = dt * a_neg if run is None else run + dt * a_neg
        dts.append(dt)
        css.append(run)

    ex = ex_ref[...]
    e3_ref[...] = jnp.exp(css[-1])
    efull_ref[...] = _dot_exact_rhs(jnp.exp(css[tt]), ex)
    w_t = jnp.exp(css[-1] - css[tt]) * dts[tt]
    xw_ref[...] = x_of(tt) * _dot_exact_rhs(w_t, ex)

    seg = seg_ref[...]
    acc = dskx_ref[...] * x_of(tt)
    c_t = c_of(tt)
    for s in range(tt + 1):
        cbh = _dot_exact_rhs(c_t * b_of(s), seg)
        g_ts = cbh * jnp.exp(css[tt] - css[s]) * dts[s]
        acc = acc + _dot_exact_rhs(g_ts, ex) * x_of(s)
    ypre_ref[...] = acc


def _ssds_prep_kernel(*refs):
    t = pl.program_id(0)
    for tt in range(DEC_SEQ):
        pl.when(t == tt)(functools.partial(_ssds_prep_step, tt, *refs))


def _ssds_prep_call(proj, dt_raw, hp, dskx, ex, seg):
    full = lambda shape: pl.BlockSpec(shape, lambda t: (0,) * len(shape))
    step = lambda width: pl.BlockSpec((DEC_BATCH, width), lambda t: (t, 0))
    cdim = GROUPS * STATE
    return pl.pallas_call(
        _ssds_prep_kernel,
        grid=(DEC_SEQ,),
        in_specs=[pl.BlockSpec((NS_ROWS, D), lambda t: (NP_TILES, PX)),
                  pl.BlockSpec((NS_ROWS, 2 * cdim), lambda t: (NP_TILES, PBC_1024)),
                  pl.BlockSpec((NS_ROWS, LANE), lambda t: (NP_TILES, 0)),
                  full((8, LANE)), full((1, D)), full((LANE, D)), full((cdim, LANE))],
        out_specs=[step(D), step(D), step(D), step(cdim), step(cdim),
                   full((DEC_BATCH, LANE))],
        out_shape=[jax.ShapeDtypeStruct((NS_ROWS, D), F32),
                   jax.ShapeDtypeStruct((NS_ROWS, D), F32),
                   jax.ShapeDtypeStruct((NS_ROWS, D), F32),
                   jax.ShapeDtypeStruct((NS_ROWS, cdim), F32),
                   jax.ShapeDtypeStruct((NS_ROWS, cdim), F32),
                   jax.ShapeDtypeStruct((DEC_BATCH, LANE), F32)],
        compiler_params=_params("arbitrary"),
        name="ssd_sample_prep",
    )(proj, proj, dt_raw, hp, dskx, ex, seg)


SEQ_PER_STEP = 8


def _ssds_state_body(blk, e3_ref, st_ref, cc_ref, bb_ref, xw_ref, yo_ref, so_ref):
    nt = (((1,), (1,)), ((), ()))
    tn = (((0,), (0,)), ((), ()))
    hpg = HEADS // GROUPS
    gw = hpg * HEADDIM

    def rows_of(ref, s, cols):
        v = ref[:, s, cols]
        return jnp.concatenate([v, jnp.zeros((8 - DEC_SEQ, v.shape[-1]), F32)], axis=0).astype(BF16)

    for s in range(SEQ_PER_STEP):
        b = blk * SEQ_PER_STEP + s
        for g in range(GROUPS):
            h0 = st_ref[s, g]
            c_g = rows_of(cc_ref, s, slice(g * STATE, (g + 1) * STATE))
            yraw = lax.dot_general(c_g, h0.astype(BF16), nt, preferred_element_type=F32)
            yo_ref[:, s, g * gw:(g + 1) * gw] = yraw[0:DEC_SEQ]
            x_g = rows_of(xw_ref, s, slice(g * gw, (g + 1) * gw))
            b_g = rows_of(bb_ref, s, slice(g * STATE, (g + 1) * STATE))
            dh = lax.dot_general(x_g, b_g, tn, preferred_element_type=F32)
            for hh in range(hpg):
                rows = slice(hh * HEADDIM, (hh + 1) * HEADDIM)
                so_ref[s, g, rows, :] = e3_ref[b, g * hpg + hh] * h0[rows] + dh[rows]


PROMPT_STEPS = N_CHUNKS // CHUNKS_PER_STEP
assert BATCH * PROMPT_STEPS * SEQ_PER_STEP == DEC_BATCH


def _ssd_fused_kernel(z_ref, x_ref, bc_ref, dt_ref, hp_ref, hpc_ref, dskx_ref, nw_ref, triu_ref,
                      e3_ref, sti_ref, cc_ref, bb_ref, xw_ref,
                      y_ref, ssm_ref, yo_ref, so_ref, st_ref, yscr_ref):
    _ssd_prompt_kernel(z_ref, x_ref, bc_ref, dt_ref, hp_ref, hpc_ref, dskx_ref, nw_ref, triu_ref,
                       y_ref, ssm_ref, st_ref, yscr_ref)
    blk = pl.program_id(0) * PROMPT_STEPS + pl.program_id(1)
    _ssds_state_body(blk, e3_ref, sti_ref, cc_ref, bb_ref, xw_ref, yo_ref, so_ref)


def _ssd_fused_call(proj, dt_raw, hp, hpc, dskx, nw, triu, e3, state, cc_t, bb_t, xw_t):
    rows = CHUNK * CHUNKS_PER_STEP
    sb = SEQ_PER_STEP
    gw = (HEADS // GROUPS) * HEADDIM
    cdim = GROUPS * STATE
    step = lambda b, c: b * PROMPT_STEPS + c
    const = lambda b, c: (0, 0)
    tmajor = lambda width: pl.BlockSpec((DEC_SEQ, sb, width), lambda b, c: (0, step(b, c), 0))
    st_spec = pl.BlockSpec((sb, GROUPS, gw, STATE), lambda b, c: (step(b, c), 0, 0, 0))
    return pl.pallas_call(
        _ssd_fused_kernel,
        grid=(BATCH, PROMPT_STEPS),
        in_specs=[pl.BlockSpec((rows, D), lambda b, c: (step(b, c), PZ)),
                  pl.BlockSpec((rows, D), lambda b, c: (step(b, c), PX)),
                  pl.BlockSpec((rows, 2 * cdim), lambda b, c: (step(b, c), PBC_1024)),
                  pl.BlockSpec((rows, LANE), lambda b, c: (step(b, c), 0)),
                  pl.BlockSpec((8, LANE), const),
                  pl.BlockSpec((LANE, 8), const),
                  pl.BlockSpec((1, D), const),
                  pl.BlockSpec((1, D), const),
                  pl.BlockSpec((CHUNK, CHUNK), const),
                  pl.BlockSpec(memory_space=pltpu.SMEM),
                  st_spec, tmajor(cdim), tmajor(cdim), tmajor(D)],
        out_specs=[pl.BlockSpec((rows, D), lambda b, c: (step(b, c), 0)),
                   pl.BlockSpec((1, HEADS // 2, LANE, STATE), lambda b, c: (b, 0, 0, 0)),
                   tmajor(D), st_spec],
        out_shape=[jax.ShapeDtypeStruct((NP_ROWS, D), BF16),
                   jax.ShapeDtypeStruct((BATCH, HEADS // 2, LANE, STATE), F32),
                   jax.ShapeDtypeStruct((DEC_SEQ, DEC_BATCH, D), F32),
                   jax.ShapeDtypeStruct((DEC_BATCH, GROUPS, gw, STATE), F32)],
        scratch_shapes=[pltpu.VMEM((HEADS // 2, STATE, LANE), F32),
                        pltpu.VMEM((2, CHUNK, D), F32)],
        compiler_params=_params("arbitrary", "arbitrary"),
        name="ssd_prompt_and_sample_state",
    )(proj, proj, proj, dt_raw, hp, hpc, dskx, nw, triu,
      e3, state, cc_t.reshape(DEC_SEQ, DEC_BATCH, cdim), bb_t.reshape(DEC_SEQ, DEC_BATCH, cdim),
      xw_t.reshape(DEC_SEQ, DEC_BATCH, D))


def _ssds_post_kernel(ypre_ref, efull_ref, yo_ref, z_ref, nw_ref, o_ref):
    get_y = lambda cols: ypre_ref[:, cols] + efull_ref[:, cols] * yo_ref[:, cols]
    _gated_group_norm(get_y, z_ref, nw_ref, o_ref)


def _ssds_post_call(ypre, efull, yo_t, proj, nw):
    step = pl.BlockSpec((DEC_BATCH, D), lambda t: (t, 0))
    return pl.pallas_call(
        _ssds_post_kernel,
        grid=(DEC_SEQ,),
        in_specs=[step, step, step,
                  pl.BlockSpec((DEC_BATCH, D), lambda t: (NP_ROWS // DEC_BATCH + t, PZ)),
                  pl.BlockSpec((1, D), lambda t: (0, 0))],
        out_specs=step,
        out_shape=jax.ShapeDtypeStruct((NS_ROWS, D), BF16),
        compiler_params=_params("arbitrary"),
        name="ssd_sample_post",
    )(ypre, efull, yo_t, proj, nw)


def _mlp_kernel(u_ref, v_ref, lnw_ref, lnb_ref, ws_ref, bsx_ref, wsx_ref, bsx4_ref,
                y_ref, cv_ref, wm_ref, cvs_ref):
    i = pl.program_id(0)
    T = CHUNK

    @pl.when(i == 0)
    def _():
        row = lax.broadcasted_iota(jnp.int32, (T, T), 0)
        lane = lax.broadcasted_iota(jnp.int32, (T, T), 1)
        for g in range(MLP_GROUPS):
            wm_ref[g] = jnp.where(row >= lane, ws_ref[g], 0.0).astype(BF16)

    def vnorm(rows):
        vg = v_ref[rows, :].astype(F32)
        xc = vg - jnp.mean(vg, axis=-1, keepdims=True)
        y = xc * lax.rsqrt(jnp.mean(xc * xc, axis=-1, keepdims=True) + EPS)
        return y * lnw_ref[...] + lnb_ref[...]

    @pl.when(i < NP_MT)
    def _():
        for cc in range(TMM // T):
            rows = slice(cc * T, (cc + 1) * T)
            vnb = vnorm(rows).astype(BF16)
            for g in range(MLP_GROUPS):
                cols = slice(g * MLP_GROUP_DIM, (g + 1) * MLP_GROUP_DIM)
                sv = _dot(wm_ref[g], vnb[:, cols]) + bsx_ref[:, cols]
                y_ref[rows, cols] = (u_ref[rows, cols].astype(F32) * sv).astype(BF16)

    @pl.when(i == NP_MT)
    def _():
        B = DEC_BATCH
        for t in range(DEC_SEQ):
            rows = slice(t * B, (t + 1) * B)
            vn = vnorm(rows)
            cvs_ref[rows, :] = vn
            cv_ref[:, t, :] = vn
        for t in range(DEC_SEQ):
            rows = slice(t * B, (t + 1) * B)
            acc = bsx4_ref[t:t + 1, :]
            for s in range(t + 1):
                acc = acc + wsx_ref[4 * t + s:4 * t + s + 1, :] * cvs_ref[s * B:(s + 1) * B, :]
            y_ref[rows, :] = (u_ref[rows, :].astype(F32) * acc).astype(BF16)


def _mlp_call(proj, lnw, lnb, ws, bsx, wsx, bsx4):
    full = lambda shape: pl.BlockSpec(shape, lambda i: (0,) * len(shape))
    return pl.pallas_call(
        _mlp_kernel,
        grid=(N_MT,),
        in_specs=[pl.BlockSpec((TMM, D), lambda i: (i, PU)),
                  pl.BlockSpec((TMM, D), lambda i: (i, PV)),
                  full((1, D)), full((1, D)),
                  full((MLP_GROUPS, CHUNK, CHUNK)),
                  full((CHUNK, D)), full((16, D)), full((8, D))],
        out_specs=[pl.BlockSpec((TMM, D), lambda i: (i, 0)),
                   full((DEC_BATCH, DEC_SEQ, D))],
        out_shape=[jax.ShapeDtypeStruct((M_ROWS, D), BF16),
                   jax.ShapeDtypeStruct((DEC_BATCH, DEC_SEQ, D), F32)],
        scratch_shapes=[pltpu.VMEM((MLP_GROUPS, CHUNK, CHUNK), BF16),
                        pltpu.VMEM((NS_ROWS, D), F32)],
        compiler_params=_params("arbitrary"),
        name="gmlp",
    )(proj, proj, lnw, lnb, ws, bsx, wsx, bsx4)


def _cast_rows(src_ref, dst_ref, chunk=256):
    def body(r, carry):
        rows = pl.ds(pl.multiple_of(r * chunk, chunk), chunk)
        dst_ref[rows, :] = src_ref[rows, :].astype(dst_ref.dtype)
        return carry
    lax.fori_loop(0, src_ref.shape[0] // chunk, body, 0)


def _per_tile(i, prompt_fn, sample_fn):
    pl.when(i < NP_MT)(prompt_fn)
    pl.when(i == NP_MT)(sample_fn)


def _merge_kernel(ysp_ref, yss_ref, ym_ref, ga_ref, gb_ref, w1_ref, w2_ref, o_ref,
                  w1b, w2b, w1s, w2s, wsem):
    j = pl.program_id(0)
    i = pl.program_id(1)
    tn = w1b.shape[1]

    def consume():
        _cast_rows(w1s, w1b)
        _cast_rows(w2s, w2b)

    _fetch_ahead(j, i, pl.num_programs(0),
                 lambda jj: (pltpu.make_async_copy(_col_block(w1_ref, jj, tn), w1s, wsem.at[0]),
                             pltpu.make_async_copy(_col_block(w2_ref, jj, tn), w2s, wsem.at[1])),
                 consume)

    def emit(ys, rows):
        a1 = _dot(ys, w1b[...])
        a2 = _dot(ym_ref[rows, :], w2b[...])
        o_ref[rows, :] = (jax.nn.sigmoid(ga_ref[rows, :].astype(F32)) * a1
                          + jax.nn.sigmoid(gb_ref[rows, :].astype(F32)) * a2).astype(BF16)

    _per_tile(i,
              lambda: emit(ysp_ref[...], slice(None)),
              lambda: emit(yss_ref[...], slice(0, NS_ROWS)))


def _merge_call(ysp, yss, ym, proj, w1, w2):
    tn = 512
    nb = D // tn
    return pl.pallas_call(
        _merge_kernel,
        grid=(nb, N_MT),
        in_specs=[pl.BlockSpec((TMM, D), lambda j, i: (jnp.minimum(i, NP_MT - 1), 0)),
                  pl.BlockSpec((NS_ROWS, D), lambda j, i: (0, 0)),
                  pl.BlockSpec((TMM, D), lambda j, i: (i, 0)),
                  pl.BlockSpec((TMM, tn), lambda j, i: (i, PGA * nb + j)),
                  pl.BlockSpec((TMM, tn), lambda j, i: (i, PGB * nb + j)),
                  pl.BlockSpec(memory_space=pl.ANY),
                  pl.BlockSpec(memory_space=pl.ANY)],
        out_specs=pl.BlockSpec((TMM, tn), lambda j, i: (i, j)),
        out_shape=jax.ShapeDtypeStruct((M_ROWS, D), BF16),
        scratch_shapes=[pltpu.VMEM((D, tn), BF16), pltpu.VMEM((D, tn), BF16),
                        pltpu.VMEM((D, tn), F32), pltpu.VMEM((D, tn), F32),
                        pltpu.SemaphoreType.DMA((2,))],
        compiler_params=_params("arbitrary", "arbitrary"),
        name="branch_merge",
    )(ysp, yss, ym, proj, proj, w1, w2)


def _fetch_ahead(j, i, n_blocks, copies_of, consume):
    @pl.when(i == 0)
    def _():
        @pl.when(j == 0)
        def _():
            for c in copies_of(0):
                c.start()

        for c in copies_of(j):
            c.wait()
        consume()

        @pl.when(j + 1 < n_blocks)
        def _():
            for c in copies_of(j + 1):
                c.start()


def _col_block(w_ref, jj, tn):
    return w_ref.at[:, pl.ds(pl.multiple_of(jj * tn, tn), tn)]


def _resid_kernel(a_ref, w_ref, rp_ref, rs_ref, gp_ref, gs_ref, o_ref, wb, ws, wsem):
    j = pl.program_id(0)
    i = pl.program_id(1)
    tn = wb.shape[1]
    _fetch_ahead(j, i, pl.num_programs(0),
                 lambda jj: (pltpu.make_async_copy(_col_block(w_ref, jj, tn), ws, wsem),),
                 lambda: _cast_rows(ws, wb))

    def prompt():
        o_ref[...] = rp_ref[...] + gp_ref[0] * _dot(a_ref[...], wb[...])

    def sample():
        acc = _dot(a_ref[0:NS_ROWS, :], wb[...])
        tn = acc.shape[-1]
        acc3 = acc.reshape(DEC_SEQ, DEC_BATCH, tn) * gs_ref[...][None]
        o_ref[0:NS_ROWS, :] = _sample_rows(rs_ref) + acc3.reshape(NS_ROWS, tn)

    _per_tile(i, prompt, sample)


def _resid_call(a, w, rp, rs, rs_block, mod_p, mod_s, k_gate, tn, name):
    kdim = a.shape[1]
    nb = D // tn
    if rs.ndim == 3:
        rs_spec = pl.BlockSpec((DEC_BATCH, DEC_SEQ, tn), lambda j, i: (0, 0, j))
    else:
        rs_spec = pl.BlockSpec((NS_ROWS, tn), lambda j, i: (rs_block, j))
    return pl.pallas_call(
        _resid_kernel,
        grid=(nb, N_MT),
        in_specs=[pl.BlockSpec((TMM, kdim), lambda j, i: (i, 0)),
                  pl.BlockSpec(memory_space=pl.ANY),
                  pl.BlockSpec((TMM, tn), lambda j, i: (jnp.minimum(i, NP_MT - 1), j)),
                  rs_spec,
                  pl.BlockSpec((1, 1, tn), lambda j, i: (_seq_of_mtile(i), 0, k_gate * nb + j)),
                  pl.BlockSpec((DEC_BATCH, tn), lambda j, i: (0, k_gate * nb + j))],
        out_specs=pl.BlockSpec((TMM, tn), lambda j, i: (i, j)),
        out_shape=jax.ShapeDtypeStruct((M_ROWS, D), F32),
        scratch_shapes=[pltpu.VMEM((kdim, tn), BF16), pltpu.VMEM((kdim, tn), F32),
                        pltpu.SemaphoreType.DMA(())],
        compiler_params=_params("arbitrary", "arbitrary"),
        name=name,
    )(a, w, rp, rs, mod_p, mod_s)


TN_FF = 512
N_FF_BLOCKS = D_FF // TN_FF


def _up_kernel(a_ref, w_ref, cw_ref, cb_ref, fst_ref,
               h_ref, fcp_ref, fcs_ref, wab, wvb, acc_a, acc_v, wsa, wsv, wsem):
    j = pl.program_id(0)
    i = pl.program_id(1)
    rc = 64

    def w_fetch(jj):
        col = lambda blk: pl.ds(pl.multiple_of(blk * TN_FF, TN_FF), TN_FF)
        return (pltpu.make_async_copy(w_ref.at[:, col(jj)], wsa, wsem.at[0]),
                pltpu.make_async_copy(w_ref.at[:, col(N_FF_BLOCKS + jj)], wsv, wsem.at[1]))

    @pl.when(i == 0)
    def _():
        @pl.when(j == 0)
        def _():
            for c in w_fetch(0):
                c.start()

        for c in w_fetch(j):
            c.wait()
        _cast_rows(wsa, wab)
        _cast_rows(wsv, wvb)

        @pl.when(j + 1 < N_FF_BLOCKS)
        def _():
            for c in w_fetch(j + 1):
                c.start()

    def prompt():
        starts_sequence = i % MT_PER_SEQ == 0
        acc_a[0:8, :] = jnp.where(starts_sequence, 0.0, acc_a[TMM:TMM + 8, :])
        x = a_ref[...]
        acc_a[8:8 + TMM, :] = _dot(x, wab[...])
        acc_v[...] = _dot(x, wvb[...])
        cw, cb = cw_ref[...], cb_ref[...]
        for r in range(0, TMM, rc):
            conv = (cb + cw[2:3] * acc_a[8 + r:8 + r + rc, :]
                    + cw[1:2] * acc_a[7 + r:7 + r + rc, :]
                    + cw[0:1] * acc_a[6 + r:6 + r + rc, :])
            h_ref[r:r + rc, :] = (jax.nn.gelu(conv) * acc_v[r:r + rc, :]).astype(BF16)
        fcp_ref[0] = acc_a[TMM + 6:TMM + 8, :]

    def sample():
        B = DEC_BATCH
        x = a_ref[0:NS_ROWS, :]
        acc_a[8:8 + NS_ROWS, :] = _dot(x, wab[...])
        acc_v[0:NS_ROWS, :] = _dot(x, wvb[...])
        cw, cb = cw_ref[...], cb_ref[...]

        def pre(t, r):
            if t < 0:
                return fst_ref[r:r + rc, t + 2, :]
            return acc_a[8 + t * B + r:8 + t * B + r + rc, :]

        for t in range(DEC_SEQ):
            for r in range(0, B, rc):
                conv = cb + cw[2:3] * pre(t, r) + cw[1:2] * pre(t - 1, r) + cw[0:1] * pre(t - 2, r)
                h_ref[t * B + r:t * B + r + rc, :] = (
                    jax.nn.gelu(conv) * acc_v[t * B + r:t * B + r + rc, :]).astype(BF16)
        fcs_ref[:, 0, :] = acc_a[8 + 2 * B:8 + 3 * B, :]
        fcs_ref[:, 1, :] = acc_a[8 + 3 * B:8 + 4 * B, :]

    _per_tile(i, prompt, sample)


def _up_call(n2, w_up, cw, cb, fst):
    return pl.pallas_call(
        _up_kernel,
        grid=(N_FF_BLOCKS, N_MT),
        in_specs=[pl.BlockSpec((TMM, D), lambda j, i: (i, 0)),
                  pl.BlockSpec(memory_space=pl.ANY),
                  pl.BlockSpec((3, TN_FF), lambda j, i: (0, j)),
                  pl.BlockSpec((1, TN_FF), lambda j, i: (0, j)),
                  pl.BlockSpec((DEC_BATCH, 2, TN_FF), lambda j, i: (0, 0, j))],
        out_specs=[pl.BlockSpec((TMM, TN_FF), lambda j, i: (i, j)),
                   pl.BlockSpec((1, 2, TN_FF), lambda j, i: (_seq_of_mtile(i), 0, j)),
                   pl.BlockSpec((DEC_BATCH, 2, TN_FF), lambda j, i: (0, 0, j))],
        out_shape=[jax.ShapeDtypeStruct((M_ROWS, D_FF), BF16),
                   jax.ShapeDtypeStruct((BATCH, 2, D_FF), F32),
                   jax.ShapeDtypeStruct((DEC_BATCH, 2, D_FF), F32)],
        scratch_shapes=[pltpu.VMEM((D, TN_FF), BF16), pltpu.VMEM((D, TN_FF), BF16),
                        pltpu.VMEM((TMM + 8, TN_FF), F32), pltpu.VMEM((TMM, TN_FF), F32),
                        pltpu.VMEM((D, TN_FF), F32), pltpu.VMEM((D, TN_FF), F32),
                        pltpu.SemaphoreType.DMA((2,))],
        compiler_params=_params("arbitrary", "arbitrary"),
        name="ffn_up",
    )(n2, w_up, cw, cb, fst)


def _final_kernel(x_ref, w_ref, yp_ref, ys_ref):
    i = pl.program_id(0)

    @pl.when(i < NP_MT)
    def _():
        for r in range(0, TMM, TM):
            yp_ref[r:r + TM, :] = _rms(x_ref[r:r + TM, :]) * w_ref[...]

    @pl.when(i == NP_MT)
    def _():
        y = _rms(x_ref[0:NS_ROWS, :]) * w_ref[...]
        for t in range(DEC_SEQ):
            ys_ref[:, t, :] = y[t * DEC_BATCH:(t + 1) * DEC_BATCH]


def _final_call(x3, w):
    return pl.pallas_call(
        _final_kernel,
        grid=(N_MT,),
        in_specs=[pl.BlockSpec((TMM, D), lambda i: (i, 0)),
                  pl.BlockSpec((1, D), lambda i: (0, 0))],
        out_specs=[pl.BlockSpec((TMM, D), lambda i: (jnp.minimum(i, NP_MT - 1), 0)),
                   pl.BlockSpec((DEC_BATCH, DEC_SEQ, D), lambda i: (0, 0, 0))],
        out_shape=[jax.ShapeDtypeStruct((NP_ROWS, D), F32),
                   jax.ShapeDtypeStruct((DEC_BATCH, DEC_SEQ, D), F32)],
        compiler_params=_params("arbitrary"),
        name="final_norm",
    )(x3, w)


def _to_time_major(a):
    return jnp.transpose(a, (1, 0, 2))


def kernel(x_prompt, x_sample, state_ssm, state_ssd_conv, state_ffn_conv, c_prompt, c_sample,
           norm1_w, w_ada, b_ada, w_in, ssd_conv_w, ssd_conv_b, dt_bias, a_log, d_skip,
           ssd_norm_w, mlp_ln_w, mlp_ln_b, w_spatial, b_spatial, w_ssd_o, w_mlp_o, w_out,
           norm2_w, w_up, ffn_conv_w, ffn_conv_b, w_down, final_norm_w):
    assert w_in.shape[0] == 1, "single-layer trunk"
    row = lambda v: v.reshape(1, -1)

    xp = x_prompt.reshape(NP_ROWS, D)
    xs = x_sample

    mod_p, mod_s = _ada_call(c_prompt, c_sample, w_ada[0], row(b_ada[0]))
    K_SHIFT1, K_SCALE1, K_GATE1, K_SHIFT2, K_SCALE2, K_GATE2 = range(6)

    xp_spec = pl.BlockSpec((TMM, D), lambda i: (jnp.minimum(i, NP_MT - 1), 0))
    xs_spec = pl.BlockSpec((DEC_BATCH, DEC_SEQ, D), lambda i: (0, 0, 0))
    w_in_t = w_in[0].T
    n1, dt_raw = _norm_call(xp, xp_spec, xs, xs_spec, row(norm1_w[0]), mod_p, mod_s,
                            K_SCALE1, K_SHIFT1, w_in_t)
    cst_t = _to_time_major(state_ssd_conv[0])
    proj, cst_p, ncs_t = _inproj_call(n1, w_in_t, ssd_conv_w[0], row(ssd_conv_b[0]), cst_t)

    hp = jnp.pad(jnp.stack([dt_bias[0], a_log[0], d_skip[0]]), ((0, 5), (0, LANE - HEADS)))
    hpc = hp.T
    triu = jnp.asarray(np.triu(np.ones((CHUNK, CHUNK), np.float32)), dtype=BF16)
    dskx = row(jnp.repeat(d_skip[0], HEADDIM))
    nw = row(ssd_norm_w[0])

    head_of_col = np.arange(D) // HEADDIM
    ex = jnp.asarray(np.arange(LANE)[:, None] == head_of_col[None, :], dtype=BF16)
    grp_of_head = np.arange(LANE) // (HEADS // GROUPS)
    seg = jnp.asarray((np.arange(GROUPS * STATE)[:, None] // STATE == grp_of_head[None, :])
                      & (np.arange(LANE)[None, :] < HEADS), dtype=BF16)
    ypre, efull, xw_t, bb_t, cc_t, e3 = _ssds_prep_call(proj, dt_raw, hp, dskx, ex, seg)
    gw = (HEADS // GROUPS) * HEADDIM
    y_ssd_p, ssm_p, yo_t, ssm_s = _ssd_fused_call(
        proj, dt_raw, hp, hpc, dskx, nw, triu,
        e3[:, :HEADS], state_ssm[0].reshape(DEC_BATCH, GROUPS, gw, STATE), cc_t, bb_t, xw_t)
    y_ssd_s = _ssds_post_call(ypre, efull, yo_t.reshape(NS_ROWS, D), proj, nw)

    per_col = lambda a: jnp.repeat(a, MLP_GROUP_DIM, axis=1)
    bsx = per_col(b_spatial[0][:, :CHUNK].T)
    ws4 = w_spatial[0][:, :DEC_SEQ, :DEC_SEQ]
    wsx = per_col(jnp.transpose(ws4, (1, 2, 0)).reshape(DEC_SEQ * DEC_SEQ, MLP_GROUPS))
    bsx4 = jnp.pad(bsx[:DEC_SEQ], ((0, 8 - DEC_SEQ), (0, 0)))
    y_mlp, cv_s = _mlp_call(proj, row(mlp_ln_w[0]), row(mlp_ln_b[0]), w_spatial[0], bsx, wsx, bsx4)

    mixed = _merge_call(y_ssd_p, y_ssd_s, y_mlp, proj, w_ssd_o[0], w_mlp_o[0])
    x2 = _resid_call(mixed, w_out[0], xp, xs, 0, mod_p, mod_s, K_GATE1, 1024, "out_proj")

    x2p_spec = pl.BlockSpec((TMM, D), lambda i: (jnp.minimum(i, NP_MT - 1), 0))
    x2s_spec = pl.BlockSpec((NS_ROWS, D), lambda i: (NP_TILES, 0))
    (n2,) = _norm_call(x2, x2p_spec, x2, x2s_spec, row(norm2_w[0]), mod_p, mod_s,
                       K_SCALE2, K_SHIFT2, None)
    h, ffn_p, ffn_s = _up_call(n2, w_up[0], ffn_conv_w[0], row(ffn_conv_b[0]), state_ffn_conv[0])
    x3 = _resid_call(h, w_down[0], x2, x2, NP_TILES, mod_p, mod_s, K_GATE2, 512, "ffn_down")

    y_p, y_s = _final_call(x3, row(final_norm_w))

    from_t = lambda a, t, c: jnp.transpose(a.reshape(t, DEC_BATCH, c), (1, 0, 2))
    return (y_p.reshape(BATCH, SEQ, D),
            y_s,
            ssm_p.reshape(1, BATCH, HEADS, HEADDIM, STATE),
            ssm_s.reshape(1, DEC_BATCH, HEADS, HEADDIM, STATE),
            cst_p[None],
            from_t(ncs_t, 3, CONV_DIM)[None],
            ffn_p[None],
            ffn_s[None],
            cv_s[None])
```

```python
import functools

import jax
import jax.numpy as jnp
import numpy as np
from jax import lax
from jax.experimental import pallas as pl
from jax.experimental.pallas import tpu as pltpu

F32 = jnp.float32
BF16 = jnp.bfloat16

D = 2048
BATCH, SEQ = 4, 2048
DEC_BATCH, DEC_SEQ = 128, 4
NP_ROWS = BATCH * SEQ
NS_ROWS = DEC_BATCH * DEC_SEQ
M_ROWS = NP_ROWS + NS_ROWS
TM = 512
NP_TILES = NP_ROWS // TM
TMM = 1024
NP_MT = NP_ROWS // TMM
N_MT = NP_MT + 1
MT_PER_SEQ = SEQ // TMM
HEADS, HEADDIM, GROUPS, STATE = 32, 64, 4, 128
CHUNK = 128
N_CHUNKS = SEQ // CHUNK
CONV_DIM = D + 2 * GROUPS * STATE
MLP_GROUPS = 8
MLP_GROUP_DIM = D // MLP_GROUPS
D_FF = 5632
EPS = 1e-6
DT_COL = D + CONV_DIM
TN_IN = 1024
LANE = 128
VMEM_LIMIT = 56 * 1024 * 1024


def _params(*sem, flags=None):
    return pltpu.CompilerParams(dimension_semantics=sem, vmem_limit_bytes=VMEM_LIMIT, flags=flags)


def _dot(a, b):
    return jnp.dot(a, b, preferred_element_type=F32)


def _dot_nt(a, b_t):
    return lax.dot_general(a, b_t, (((1,), (1,)), ((), ())), preferred_element_type=F32)


def _split_bf16(v, terms):
    out = []
    r = v
    for _ in range(terms):
        p = r.astype(BF16)
        out.append(p)
        r = r - p.astype(F32)
    return out


def _dot_exact_rhs(v, e, terms=3):
    acc = None
    for p in _split_bf16(v, terms):
        d = _dot(p, e)
        acc = d if acc is None else acc + d
    return acc


def _silu(x):
    return x * jax.nn.sigmoid(x)


def _softplus(x):
    return jnp.maximum(x, 0.0) + jnp.log1p(jnp.exp(-jnp.abs(x)))


def _rms(x):
    return x * lax.rsqrt(jnp.mean(x * x, axis=-1, keepdims=True) + EPS)


ADA_TN = 1024
ADA_SLOTS = 3


def _ada_kernel(cp_ref, cs_ref, w_ref, b_ref, op_ref, os_ref, ws, wsem):
    s = pl.program_id(0)
    n = pl.num_programs(0)

    def w_fetch(ss):
        slot = ss % ADA_SLOTS
        return pltpu.make_async_copy(_col_block(w_ref, ss, ADA_TN), ws.at[slot], wsem.at[slot])

    @pl.when(s == 0)
    def _():
        for ss in range(ADA_SLOTS - 1):
            w_fetch(ss).start()

    w_fetch(s).wait()

    @pl.when(s + ADA_SLOTS - 1 < n)
    def _():
        w_fetch(s + ADA_SLOTS - 1).start()

    w = ws[s % ADA_SLOTS].astype(BF16)
    cp8 = jnp.concatenate([cp_ref[...], jnp.zeros((8 - BATCH, D), F32)], axis=0)
    op_ref[:, 0, :] = (_dot(_silu(cp8).astype(BF16), w) + b_ref[...])[0:BATCH]
    os_ref[...] = _dot(_silu(cs_ref[...]).astype(BF16), w) + b_ref[...]


def _ada_call(c_prompt, c_sample, w, b):
    tn = ADA_TN
    return pl.pallas_call(
        _ada_kernel,
        grid=(6 * D // tn,),
        in_specs=[pl.BlockSpec((BATCH, D), lambda j: (0, 0)),
                  pl.BlockSpec((DEC_BATCH, D), lambda j: (0, 0)),
                  pl.BlockSpec(memory_space=pl.ANY),
                  pl.BlockSpec((1, tn), lambda j: (0, j))],
        out_specs=[pl.BlockSpec((BATCH, 1, tn), lambda j: (0, 0, j)),
                   pl.BlockSpec((DEC_BATCH, tn), lambda j: (0, j))],
        out_shape=[jax.ShapeDtypeStruct((BATCH, 1, 6 * D), F32),
                   jax.ShapeDtypeStruct((DEC_BATCH, 6 * D), F32)],
        scratch_shapes=[pltpu.VMEM((ADA_SLOTS, D, ADA_TN), F32), pltpu.SemaphoreType.DMA((ADA_SLOTS,))],
        compiler_params=_params("arbitrary"),
        name="ada_mod",
    )(c_prompt, c_sample, w, b)


def _sample_rows(ref):
    if len(ref.shape) == 2:
        return ref[...]
    return jnp.concatenate([ref[:, t, :] for t in range(DEC_SEQ)], axis=0)


def _norm_kernel(with_dt, xp_ref, xs_ref, nw_ref, scp_ref, shp_ref, scs_ref, shs_ref, *rest):
    if with_dt:
        wdt_ref, n_ref, dt_ref = rest
    else:
        (n_ref,) = rest
    i = pl.program_id(0)

    def emit(n, rows):
        nb = n.astype(BF16)
        n_ref[rows, :] = nb
        if with_dt:
            dt_ref[rows, :] = lax.dot_general(nb, wdt_ref[...].astype(BF16),
                                              (((1,), (1,)), ((), ())),
                                              preferred_element_type=F32)

    @pl.when(i < NP_MT)
    def _():
        for r in range(0, TMM, TM):
            rows = slice(r, r + TM)
            y = _rms(xp_ref[rows, :]) * nw_ref[...]
            emit(y * (1.0 + scp_ref[0]) + shp_ref[0], rows)

    @pl.when(i == NP_MT)
    def _():
        y = _rms(_sample_rows(xs_ref)) * nw_ref[...]
        y3 = y.reshape(DEC_SEQ, DEC_BATCH, D)
        emit((y3 * (1.0 + scs_ref[...])[None] + shs_ref[...][None]).reshape(NS_ROWS, D),
             slice(0, NS_ROWS))


def _seq_of_mtile(i):
    return jnp.minimum(i // MT_PER_SEQ, BATCH - 1)


def _norm_call(xp, xp_spec, xs, xs_spec, nw, mod_p, mod_s, k_scale, k_shift, w_in_t):
    with_dt = w_in_t is not None
    in_specs = [
        xp_spec, xs_spec,
        pl.BlockSpec((1, D), lambda i: (0, 0)),
        pl.BlockSpec((1, 1, D), lambda i: (_seq_of_mtile(i), 0, k_scale)),
        pl.BlockSpec((1, 1, D), lambda i: (_seq_of_mtile(i), 0, k_shift)),
        pl.BlockSpec((DEC_BATCH, D), lambda i: (0, k_scale)),
        pl.BlockSpec((DEC_BATCH, D), lambda i: (0, k_shift)),
    ]
    args = [xp, xs, nw, mod_p, mod_p, mod_s, mod_s]
    out_specs = [pl.BlockSpec((TMM, D), lambda i: (i, 0))]
    out_shape = [jax.ShapeDtypeStruct((M_ROWS, D), BF16)]
    if with_dt:
        in_specs.append(pl.BlockSpec((pl.Element(LANE), pl.Element(D)), lambda i: (DT_COL, 0)))
        args.append(w_in_t)
        out_specs.append(pl.BlockSpec((TMM, LANE), lambda i: (i, 0)))
        out_shape.append(jax.ShapeDtypeStruct((M_ROWS, LANE), F32))
    return pl.pallas_call(
        functools.partial(_norm_kernel, with_dt),
        grid=(N_MT,),
        in_specs=in_specs,
        out_specs=out_specs,
        out_shape=out_shape,
        compiler_params=_params("arbitrary"),
        name="norm_mod_dt" if with_dt else "norm_mod",
    )(*args)


N_IN_BLOCKS = 13
UVG_ROW = DT_COL + HEADS
FIRST_CONV_BLOCK = 10


def _in_src_row(j):
    row = jnp.where(j < 2, j * TN_IN,
                    jnp.where(j < FIRST_CONV_BLOCK, UVG_ROW + (j - 2) * TN_IN,
                              D + (j - FIRST_CONV_BLOCK) * TN_IN))
    return pl.multiple_of(row, HEADS)


def _in_conv_block(j):
    return jnp.maximum(j - FIRST_CONV_BLOCK, 0)


def _inproj_kernel(a_ref, wt_ref, cw_ref, cb_ref, cst_ref, o_ref, csp_ref, css_ref,
                   wbf_ref, acc_ref, wst_ref, wsem):
    j = pl.program_id(0)
    i = pl.program_id(1)
    rc = 64
    B = DEC_BATCH

    def w_fetch(jj):
        return pltpu.make_async_copy(wt_ref.at[pl.ds(_in_src_row(jj), TN_IN), :], wst_ref, wsem)

    @pl.when(i == 0)
    def _():
        @pl.when(j == 0)
        def _():
            w_fetch(0).start()

        w_fetch(j).wait()
        for r in range(TN_IN // LANE):
            rows = slice(r * LANE, (r + 1) * LANE)
            wbf_ref[rows, :] = wst_ref[rows, :].astype(BF16)

        @pl.when(j + 1 < N_IN_BLOCKS)
        def _():
            w_fetch(j + 1).start()

    def elementwise(fn, rows):
        def body():
            acc_ref[8:8 + rows, :] = _dot_nt(a_ref[0:rows, :], wbf_ref[...])
            for r in range(0, rows, rc):
                val = acc_ref[8 + r:8 + r + rc, :]
                o_ref[r:r + rc, :] = (val if fn is None else fn(val)).astype(o_ref.dtype)
        return body

    def conv_prompt():
        starts_sequence = i % MT_PER_SEQ == 0
        acc_ref[0:8, :] = jnp.where(starts_sequence, 0.0, acc_ref[TMM:TMM + 8, :])
        acc_ref[8:8 + TMM, :] = _dot_nt(a_ref[...], wbf_ref[...])
        cw, cb = cw_ref[...], cb_ref[...]
        for r in range(0, TMM, rc):
            conv = cb + cw[3:4] * acc_ref[8 + r:8 + r + rc, :]
            for k in range(3):
                conv = conv + cw[k:k + 1] * acc_ref[5 + k + r:5 + k + r + rc, :]
            o_ref[r:r + rc, :] = _silu(conv).astype(o_ref.dtype)
        csp_ref[0] = acc_ref[TMM + 5:TMM + 8, :]

    def conv_sample():
        acc_ref[8:8 + NS_ROWS, :] = _dot_nt(a_ref[0:NS_ROWS, :], wbf_ref[...])
        cw, cb = cw_ref[...], cb_ref[...]

        def pre(t, r):
            if t < 0:
                return cst_ref[t + 3, r:r + rc, :]
            return acc_ref[8 + t * B + r:8 + t * B + r + rc, :]

        for t in range(DEC_SEQ):
            for r in range(0, B, rc):
                conv = cb + cw[3:4] * pre(t, r)
                for k in range(3):
                    conv = conv + cw[k:k + 1] * pre(t - 3 + k, r)
                o_ref[t * B + r:t * B + r + rc, :] = _silu(conv).astype(o_ref.dtype)
        for t in range(1, DEC_SEQ):
            css_ref[t - 1] = acc_ref[8 + t * B:8 + (t + 1) * B, :]

    is_conv = j >= FIRST_CONV_BLOCK
    kinds = ((j < 2, _silu),
             (jnp.logical_and(j >= 2, j < 6), jax.nn.gelu),
             (jnp.logical_and(j >= 6, j < FIRST_CONV_BLOCK), None))
    for cond, fn in kinds:
        pl.when(jnp.logical_and(cond, i < NP_MT))(elementwise(fn, TMM))
        pl.when(jnp.logical_and(cond, i == NP_MT))(elementwise(fn, NS_ROWS))
    pl.when(jnp.logical_and(is_conv, i < NP_MT))(conv_prompt)
    pl.when(jnp.logical_and(is_conv, i == NP_MT))(conv_sample)


def _inproj_call(n1, w_in_t, cw, cb, cst_t):
    cblk = _in_conv_block
    seq = lambda j, i: jnp.where(j < FIRST_CONV_BLOCK, 0, _seq_of_mtile(i))
    return pl.pallas_call(
        _inproj_kernel,
        grid=(N_IN_BLOCKS, N_MT),
        in_specs=[pl.BlockSpec((TMM, D), lambda j, i: (i, 0)),
                  pl.BlockSpec(memory_space=pl.ANY),
                  pl.BlockSpec((4, TN_IN), lambda j, i: (0, cblk(j))),
                  pl.BlockSpec((1, TN_IN), lambda j, i: (0, cblk(j))),
                  pl.BlockSpec((3, DEC_BATCH, TN_IN), lambda j, i: (0, 0, cblk(j)))],
        out_specs=[pl.BlockSpec((TMM, TN_IN), lambda j, i: (i, j)),
                   pl.BlockSpec((1, 3, TN_IN), lambda j, i: (seq(j, i), 0, cblk(j))),
                   pl.BlockSpec((3, DEC_BATCH, TN_IN), lambda j, i: (0, 0, cblk(j)))],
        out_shape=[jax.ShapeDtypeStruct((M_ROWS, N_IN_BLOCKS * TN_IN), BF16),
                   jax.ShapeDtypeStruct((BATCH, 3, CONV_DIM), F32),
                   jax.ShapeDtypeStruct((3, DEC_BATCH, CONV_DIM), F32)],
        scratch_shapes=[pltpu.VMEM((TN_IN, D), BF16), pltpu.VMEM((TMM + 8, TN_IN), F32),
                        pltpu.VMEM((TN_IN, D), F32), pltpu.SemaphoreType.DMA(())],
        compiler_params=_params("arbitrary", "arbitrary"),
        name="in_proj",
    )(n1, w_in_t, cw, cb, cst_t)


PZ, PU, PV, PGA, PGB, PX = 0, 1, 2, 3, 4, 5
PBC_1024 = 12


def _gated_group_norm(get_y, zact_ref, nw_ref, o_ref, rows=slice(None)):
    gw = D // GROUPS
    for g in range(GROUPS):
        cols = slice(g * gw, (g + 1) * gw)
        gg = _rms(get_y(cols) * zact_ref[rows, cols].astype(F32))
        o_ref[rows, cols] = (gg * nw_ref[:, cols]).astype(o_ref.dtype)


CHUNKS_PER_STEP = 4


def _ssd_prompt_kernel(z_ref, x_ref, bc_ref, dt_ref, hp_ref, hpc_ref, dskx_ref, nw_ref, triu_ref,
                       y_ref, ssm_ref, st_ref, yscr_ref):
    c = pl.program_id(1)
    T = CHUNK
    cdim = GROUPS * STATE

    @pl.when(c == 0)
    def _():
        st_ref[...] = jnp.zeros(st_ref.shape, F32)

    hp = hp_ref[...]
    a_col = -jnp.exp(hpc_ref[0:HEADS, 1:2])
    triu = triu_ref[...]
    row = lax.broadcasted_iota(jnp.int32, (T, T), 0)
    lane = lax.broadcasted_iota(jnp.int32, (T, T), 1)
    causal = row >= lane
    left = lane < HEADDIM
    mask_l = jnp.where(left, 1.0, 0.0).astype(BF16)
    nt = (((1,), (1,)), ((), ()))
    pairs_per_group = HEADS // GROUPS // 2

    for sub in range(CHUNKS_PER_STEP):
        rows = slice(sub * T, (sub + 1) * T)
        y_buf = yscr_ref.at[sub % 2]
        dt_t = _softplus((dt_ref[rows, :] + hp[0:1, :]).T[0:HEADS, :])
        cs_t = _dot_exact_rhs(dt_t * a_col, triu)
        rsub_t = cs_t - jnp.log(dt_t)
        cs = jnp.concatenate([cs_t, jnp.zeros((LANE - HEADS, T), F32)], axis=0).T

        for g in range(GROUPS):
            c_b = bc_ref[rows, cdim + g * STATE:cdim + (g + 1) * STATE]
            b_b = bc_ref[rows, g * STATE:(g + 1) * STATE]
            cb = lax.dot_general(c_b, b_b, nt, preferred_element_type=F32)
            b_t = b_b.astype(F32).T
            for k4 in range(pairs_per_group):
                k = g * pairs_per_group + k4
                cols = slice(k * LANE, (k + 1) * LANE)
                xpb = x_ref[rows, cols]
                x_lo = xpb * mask_l
                xbd = jnp.concatenate([x_lo, xpb - x_lo], axis=0)
                st = st_ref[k]
                yraw = _dot(c_b, st.astype(BF16))
                scores, bws, colbs, alasts = [], [], [], []
                for h in (2 * k, 2 * k + 1):
                    colb = jnp.broadcast_to(cs[:, h:h + 1], (T, T))
                    alast = cs_t[h:h + 1, T - 1:T]
                    decay_dt = jnp.exp(jnp.where(causal, colb - rsub_t[h:h + 1, :], -jnp.inf))
                    scores.append((cb * decay_dt).astype(BF16))
                    wrow = jnp.exp(alast - cs_t[h:h + 1, :]) * dt_t[h:h + 1, :]
                    bws.append((b_t * wrow).astype(BF16))
                    colbs.append(colb)
                    alasts.append(alast)
                ecol = jnp.exp(jnp.where(left, colbs[0], colbs[1]))
                elast = jnp.exp(jnp.where(left[0:1], alasts[0], alasts[1]))
                y_buf[:, cols] = (_dot(jnp.concatenate(scores, axis=1), xbd) + ecol * yraw
                                  + dskx_ref[:, cols] * xpb.astype(F32))
                st_ref[k] = elast * st + _dot(jnp.concatenate(bws, axis=1), xbd)
        _gated_group_norm(lambda cols: y_buf[:, cols], z_ref, nw_ref, y_ref, rows)

    @pl.when(c == N_CHUNKS // CHUNKS_PER_STEP - 1)
    def _():
        for k in range(HEADS // 2):
            ssm_ref[0, k] = st_ref[k].T


def _ssds_prep_step(tt, x_ref, bc_ref, dt_ref, hp_ref, dskx_ref, ex_ref, seg_ref,
                    ypre_ref, efull_ref, xw_ref, bb_ref, cc_ref, e3_ref):
    B = DEC_BATCH
    cdim = GROUPS * STATE
    blk = lambda t: slice(t * B, (t + 1) * B)
    x_of = lambda t: x_ref[blk(t), :].astype(F32)
    b_of = lambda t: bc_ref[blk(t), 0:cdim].astype(F32)
    c_of = lambda t: bc_ref[blk(t), cdim:2 * cdim].astype(F32)

    bb_ref[...] = b_of(tt)
    cc_ref[...] = c_of(tt)

    hp = hp_ref[...]
    a_neg = -jnp.exp(hp[1:2, :])
    dts, css = [], []
    run = None
    for t in range(DEC_SEQ):
        dt = _softplus(dt_ref[blk(t), :] + hp[0:1, :])
        run = dt * a_neg if run is None else run + dt * a_neg
        dts.append(dt)
        css.append(run)

    ex = ex_ref[...]
    e3_ref[...] = jnp.exp(css[-1])
    efull_ref[...] = _dot_exact_rhs(jnp.exp(css[tt]), ex)
    w_t = jnp.exp(css[-1] - css[tt]) * dts[tt]
    xw_ref[...] = x_of(tt) * _dot_exact_rhs(w_t, ex)

    seg = seg_ref[...]
    acc = dskx_ref[...] * x_of(tt)
    c_t = c_of(tt)
    for s in range(tt + 1):
        cbh = _dot_exact_rhs(c_t * b_of(s), seg)
        g_ts = cbh * jnp.exp(css[tt] - css[s]) * dts[s]
        acc = acc + _dot_exact_rhs(g_ts, ex) * x_of(s)
    ypre_ref[...] = acc


def _ssds_prep_kernel(*refs):
    t = pl.program_id(0)
    for tt in range(DEC_SEQ):
        pl.when(t == tt)(functools.partial(_ssds_prep_step, tt, *refs))


def _ssds_prep_call(proj, dt_raw, hp, dskx, ex, seg):
    full = lambda shape: pl.BlockSpec(shape, lambda t: (0,) * len(shape))
    step = lambda width: pl.BlockSpec((DEC_BATCH, width), lambda t: (t, 0))
    cdim = GROUPS * STATE
    return pl.pallas_call(
        _ssds_prep_kernel,
        grid=(DEC_SEQ,),
        in_specs=[pl.BlockSpec((NS_ROWS, D), lambda t: (NP_TILES, PX)),
                  pl.BlockSpec((NS_ROWS, 2 * cdim), lambda t: (NP_TILES, PBC_1024)),
                  pl.BlockSpec((NS_ROWS, LANE), lambda t: (NP_TILES, 0)),
                  full((8, LANE)), full((1, D)), full((LANE, D)), full((cdim, LANE))],
        out_specs=[step(D), step(D), step(D), step(cdim), step(cdim),
                   full((DEC_BATCH, LANE))],
        out_shape=[jax.ShapeDtypeStruct((NS_ROWS, D), F32),
                   jax.ShapeDtypeStruct((NS_ROWS, D), F32),
                   jax.ShapeDtypeStruct((NS_ROWS, D), F32),
                   jax.ShapeDtypeStruct((NS_ROWS, cdim), F32),
                   jax.ShapeDtypeStruct((NS_ROWS, cdim), F32),
                   jax.ShapeDtypeStruct((DEC_BATCH, LANE), F32)],
        compiler_params=_params("arbitrary"),
        name="ssd_sample_prep",
    )(proj, proj, dt_raw, hp, dskx, ex, seg)


SEQ_PER_STEP = 8


def _ssds_state_body(blk, e3_ref, st_ref, cc_ref, bb_ref, xw_ref, yo_ref, so_ref):
    nt = (((1,), (1,)), ((), ()))
    tn = (((0,), (0,)), ((), ()))
    hpg = HEADS // GROUPS
    gw = hpg * HEADDIM

    def rows_of(ref, s, cols):
        v = ref[:, s, cols]
        return jnp.concatenate([v, jnp.zeros((8 - DEC_SEQ, v.shape[-1]), F32)], axis=0).astype(BF16)

    for s in range(SEQ_PER_STEP):
        b = blk * SEQ_PER_STEP + s
        for g in range(GROUPS):
            h0 = st_ref[s, g]
            c_g = rows_of(cc_ref, s, slice(g * STATE, (g + 1) * STATE))
            yraw = lax.dot_general(c_g, h0.astype(BF16), nt, preferred_element_type=F32)
            yo_ref[:, s, g * gw:(g + 1) * gw] = yraw[0:DEC_SEQ]
            x_g = rows_of(xw_ref, s, slice(g * gw, (g + 1) * gw))
            b_g = rows_of(bb_ref, s, slice(g * STATE, (g + 1) * STATE))
            dh = lax.dot_general(x_g, b_g, tn, preferred_element_type=F32)
            for hh in range(hpg):
                rows = slice(hh * HEADDIM, (hh + 1) * HEADDIM)
                so_ref[s, g, rows, :] = e3_ref[b, g * hpg + hh] * h0[rows] + dh[rows]


PROMPT_STEPS = N_CHUNKS // CHUNKS_PER_STEP
assert BATCH * PROMPT_STEPS * SEQ_PER_STEP == DEC_BATCH


def _ssd_fused_kernel(z_ref, x_ref, bc_ref, dt_ref, hp_ref, hpc_ref, dskx_ref, nw_ref, triu_ref,
                      e3_ref, sti_ref, cc_ref, bb_ref, xw_ref,
                      y_ref, ssm_ref, yo_ref, so_ref, st_ref, yscr_ref):
    _ssd_prompt_kernel(z_ref, x_ref, bc_ref, dt_ref, hp_ref, hpc_ref, dskx_ref, nw_ref, triu_ref,
                       y_ref, ssm_ref, st_ref, yscr_ref)
    blk = pl.program_id(0) * PROMPT_STEPS + pl.program_id(1)
    _ssds_state_body(blk, e3_ref, sti_ref, cc_ref, bb_ref, xw_ref, yo_ref, so_ref)


def _ssd_fused_call(proj, dt_raw, hp, hpc, dskx, nw, triu, e3, state, cc_t, bb_t, xw_t):
    rows = CHUNK * CHUNKS_PER_STEP
    sb = SEQ_PER_STEP
    gw = (HEADS // GROUPS) * HEADDIM
    cdim = GROUPS * STATE
    step = lambda b, c: b * PROMPT_STEPS + c
    const = lambda b, c: (0, 0)
    tmajor = lambda width: pl.BlockSpec((DEC_SEQ, sb, width), lambda b, c: (0, step(b, c), 0))
    st_spec = pl.BlockSpec((sb, GROUPS, gw, STATE), lambda b, c: (step(b, c), 0, 0, 0))
    return pl.pallas_call(
        _ssd_fused_kernel,
        grid=(BATCH, PROMPT_STEPS),
        in_specs=[pl.BlockSpec((rows, D), lambda b, c: (step(b, c), PZ)),
                  pl.BlockSpec((rows, D), lambda b, c: (step(b, c), PX)),
                  pl.BlockSpec((rows, 2 * cdim), lambda b, c: (step(b, c), PBC_1024)),
                  pl.BlockSpec((rows, LANE), lambda b, c: (step(b, c), 0)),
                  pl.BlockSpec((8, LANE), const),
                  pl.BlockSpec((LANE, 8), const),
                  pl.BlockSpec((1, D), const),
                  pl.BlockSpec((1, D), const),
                  pl.BlockSpec((CHUNK, CHUNK), const),
                  pl.BlockSpec(memory_space=pltpu.SMEM),
                  st_spec, tmajor(cdim), tmajor(cdim), tmajor(D)],
        out_specs=[pl.BlockSpec((rows, D), lambda b, c: (step(b, c), 0)),
                   pl.BlockSpec((1, HEADS // 2, LANE, STATE), lambda b, c: (b, 0, 0, 0)),
                   tmajor(D), st_spec],
        out_shape=[jax.ShapeDtypeStruct((NP_ROWS, D), BF16),
                   jax.ShapeDtypeStruct((BATCH, HEADS // 2, LANE, STATE), F32),
                   jax.ShapeDtypeStruct((DEC_SEQ, DEC_BATCH, D), F32),
                   jax.ShapeDtypeStruct((DEC_BATCH, GROUPS, gw, STATE), F32)],
        scratch_shapes=[pltpu.VMEM((HEADS // 2, STATE, LANE), F32),
                        pltpu.VMEM((2, CHUNK, D), F32)],
        compiler_params=_params("arbitrary", "arbitrary"),
        name="ssd_prompt_and_sample_state",
    )(proj, proj, proj, dt_raw, hp, hpc, dskx, nw, triu,
      e3, state, cc_t.reshape(DEC_SEQ, DEC_BATCH, cdim), bb_t.reshape(DEC_SEQ, DEC_BATCH, cdim),
      xw_t.reshape(DEC_SEQ, DEC_BATCH, D))


def _ssds_post_kernel(ypre_ref, efull_ref, yo_ref, z_ref, nw_ref, o_ref):
    get_y = lambda cols: ypre_ref[:, cols] + efull_ref[:, cols] * yo_ref[:, cols]
    _gated_group_norm(get_y, z_ref, nw_ref, o_ref)


def _ssds_post_call(ypre, efull, yo_t, proj, nw):
    step = pl.BlockSpec((DEC_BATCH, D), lambda t: (t, 0))
    return pl.pallas_call(
        _ssds_post_kernel,
        grid=(DEC_SEQ,),
        in_specs=[step, step, step,
                  pl.BlockSpec((DEC_BATCH, D), lambda t: (NP_ROWS // DEC_BATCH + t, PZ)),
                  pl.BlockSpec((1, D), lambda t: (0, 0))],
        out_specs=step,
        out_shape=jax.ShapeDtypeStruct((NS_ROWS, D), BF16),
        compiler_params=_params("arbitrary"),
        name="ssd_sample_post",
    )(ypre, efull, yo_t, proj, nw)


def _mlp_kernel(u_ref, v_ref, lnw_ref, lnb_ref, ws_ref, bsx_ref, wsx_ref, bsx4_ref,
                y_ref, cv_ref, wm_ref, cvs_ref):
    i = pl.program_id(0)
    T = CHUNK

    @pl.when(i == 0)
    def _():
        row = lax.broadcasted_iota(jnp.int32, (T, T), 0)
        lane = lax.broadcasted_iota(jnp.int32, (T, T), 1)
        for g in range(MLP_GROUPS):
            wm_ref[g] = jnp.where(row >= lane, ws_ref[g], 0.0).astype(BF16)

    def vnorm(rows):
        vg = v_ref[rows, :].astype(F32)
        xc = vg - jnp.mean(vg, axis=-1, keepdims=True)
        y = xc * lax.rsqrt(jnp.mean(xc * xc, axis=-1, keepdims=True) + EPS)
        return y * lnw_ref[...] + lnb_ref[...]

    @pl.when(i < NP_MT)
    def _():
        for cc in range(TMM // T):
            rows = slice(cc * T, (cc + 1) * T)
            vnb = vnorm(rows).astype(BF16)
            for g in range(MLP_GROUPS):
                cols = slice(g * MLP_GROUP_DIM, (g + 1) * MLP_GROUP_DIM)
                sv = _dot(wm_ref[g], vnb[:, cols]) + bsx_ref[:, cols]
                y_ref[rows, cols] = (u_ref[rows, cols].astype(F32) * sv).astype(BF16)

    @pl.when(i == NP_MT)
    def _():
        B = DEC_BATCH
        for t in range(DEC_SEQ):
            rows = slice(t * B, (t + 1) * B)
            vn = vnorm(rows)
            cvs_ref[rows, :] = vn
            cv_ref[:, t, :] = vn
        for t in range(DEC_SEQ):
            rows = slice(t * B, (t + 1) * B)
            acc = bsx4_ref[t:t + 1, :]
            for s in range(t + 1):
                acc = acc + wsx_ref[4 * t + s:4 * t + s + 1, :] * cvs_ref[s * B:(s + 1) * B, :]
            y_ref[rows, :] = (u_ref[rows, :].astype(F32) * acc).astype(BF16)


def _mlp_call(proj, lnw, lnb, ws, bsx, wsx, bsx4):
    full = lambda shape: pl.BlockSpec(shape, lambda i: (0,) * len(shape))
    return pl.pallas_call(
        _mlp_kernel,
        grid=(N_MT,),
        in_specs=[pl.BlockSpec((TMM, D), lambda i: (i, PU)),
                  pl.BlockSpec((TMM, D), lambda i: (i, PV)),
                  full((1, D)), full((1, D)),
                  full((MLP_GROUPS, CHUNK, CHUNK)),
                  full((CHUNK, D)), full((16, D)), full((8, D))],
        out_specs=[pl.BlockSpec((TMM, D), lambda i: (i, 0)),
                   full((DEC_BATCH, DEC_SEQ, D))],
        out_shape=[jax.ShapeDtypeStruct((M_ROWS, D), BF16),
                   jax.ShapeDtypeStruct((DEC_BATCH, DEC_SEQ, D), F32)],
        scratch_shapes=[pltpu.VMEM((MLP_GROUPS, CHUNK, CHUNK), BF16),
                        pltpu.VMEM((NS_ROWS, D), F32)],
        compiler_params=_params("arbitrary"),
        name="gmlp",
    )(proj, proj, lnw, lnb, ws, bsx, wsx, bsx4)


def _cast_rows(src_ref, dst_ref, chunk=256):
    def body(r, carry):
        rows = pl.ds(pl.multiple_of(r * chunk, chunk), chunk)
        dst_ref[rows, :] = src_ref[rows, :].astype(dst_ref.dtype)
        return carry
    lax.fori_loop(0, src_ref.shape[0] // chunk, body, 0)


def _per_tile(i, prompt_fn, sample_fn):
    pl.when(i < NP_MT)(prompt_fn)
    pl.when(i == NP_MT)(sample_fn)


def _merge_kernel(ysp_ref, yss_ref, ym_ref, ga_ref, gb_ref, w1_ref, w2_ref, o_ref,
                  w1b, w2b, w1s, w2s, wsem):
    j = pl.program_id(0)
    i = pl.program_id(1)
    tn = w1b.shape[1]

    def consume():
        _cast_rows(w1s, w1b)
        _cast_rows(w2s, w2b)

    _fetch_ahead(j, i, pl.num_programs(0),
                 lambda jj: (pltpu.make_async_copy(_col_block(w1_ref, jj, tn), w1s, wsem.at[0]),
                             pltpu.make_async_copy(_col_block(w2_ref, jj, tn), w2s, wsem.at[1])),
                 consume)

    def emit(ys, rows):
        a1 = _dot(ys, w1b[...])
        a2 = _dot(ym_ref[rows, :], w2b[...])
        o_ref[rows, :] = (jax.nn.sigmoid(ga_ref[rows, :].astype(F32)) * a1
                          + jax.nn.sigmoid(gb_ref[rows, :].astype(F32)) * a2).astype(BF16)

    _per_tile(i,
              lambda: emit(ysp_ref[...], slice(None)),
              lambda: emit(yss_ref[...], slice(0, NS_ROWS)))


def _merge_call(ysp, yss, ym, proj, w1, w2):
    tn = 512
    nb = D // tn
    return pl.pallas_call(
        _merge_kernel,
        grid=(nb, N_MT),
        in_specs=[pl.BlockSpec((TMM, D), lambda j, i: (jnp.minimum(i, NP_MT - 1), 0)),
                  pl.BlockSpec((NS_ROWS, D), lambda j, i: (0, 0)),
                  pl.BlockSpec((TMM, D), lambda j, i: (i, 0)),
                  pl.BlockSpec((TMM, tn), lambda j, i: (i, PGA * nb + j)),
                  pl.BlockSpec((TMM, tn), lambda j, i: (i, PGB * nb + j)),
                  pl.BlockSpec(memory_space=pl.ANY),
                  pl.BlockSpec(memory_space=pl.ANY)],
        out_specs=pl.BlockSpec((TMM, tn), lambda j, i: (i, j)),
        out_shape=jax.ShapeDtypeStruct((M_ROWS, D), BF16),
        scratch_shapes=[pltpu.VMEM((D, tn), BF16), pltpu.VMEM((D, tn), BF16),
                        pltpu.VMEM((D, tn), F32), pltpu.VMEM((D, tn), F32),
                        pltpu.SemaphoreType.DMA((2,))],
        compiler_params=_params("arbitrary", "arbitrary"),
        name="branch_merge",
    )(ysp, yss, ym, proj, proj, w1, w2)


def _fetch_ahead(j, i, n_blocks, copies_of, consume):
    @pl.when(i == 0)
    def _():
        @pl.when(j == 0)
        def _():
            for c in copies_of(0):
                c.start()

        for c in copies_of(j):
            c.wait()
        consume()

        @pl.when(j + 1 < n_blocks)
        def _():
            for c in copies_of(j + 1):
                c.start()


def _col_block(w_ref, jj, tn):
    return w_ref.at[:, pl.ds(pl.multiple_of(jj * tn, tn), tn)]


def _resid_kernel(a_ref, w_ref, rp_ref, rs_ref, gp_ref, gs_ref, o_ref, wb, ws, wsem):
    j = pl.program_id(0)
    i = pl.program_id(1)
    tn = wb.shape[1]
    _fetch_ahead(j, i, pl.num_programs(0),
                 lambda jj: (pltpu.make_async_copy(_col_block(w_ref, jj, tn), ws, wsem),),
                 lambda: _cast_rows(ws, wb))

    def prompt():
        o_ref[...] = rp_ref[...] + gp_ref[0] * _dot(a_ref[...], wb[...])

    def sample():
        acc = _dot(a_ref[0:NS_ROWS, :], wb[...])
        tn = acc.shape[-1]
        acc3 = acc.reshape(DEC_SEQ, DEC_BATCH, tn) * gs_ref[...][None]
        o_ref[0:NS_ROWS, :] = _sample_rows(rs_ref) + acc3.reshape(NS_ROWS, tn)

    _per_tile(i, prompt, sample)


def _resid_call(a, w, rp, rs, rs_block, mod_p, mod_s, k_gate, tn, name):
    kdim = a.shape[1]
    nb = D // tn
    if rs.ndim == 3:
        rs_spec = pl.BlockSpec((DEC_BATCH, DEC_SEQ, tn), lambda j, i: (0, 0, j))
    else:
        rs_spec = pl.BlockSpec((NS_ROWS, tn), lambda j, i: (rs_block, j))
    return pl.pallas_call(
        _resid_kernel,
        grid=(nb, N_MT),
        in_specs=[pl.BlockSpec((TMM, kdim), lambda j, i: (i, 0)),
                  pl.BlockSpec(memory_space=pl.ANY),
                  pl.BlockSpec((TMM, tn), lambda j, i: (jnp.minimum(i, NP_MT - 1), j)),
                  rs_spec,
                  pl.BlockSpec((1, 1, tn), lambda j, i: (_seq_of_mtile(i), 0, k_gate * nb + j)),
                  pl.BlockSpec((DEC_BATCH, tn), lambda j, i: (0, k_gate * nb + j))],
        out_specs=pl.BlockSpec((TMM, tn), lambda j, i: (i, j)),
        out_shape=jax.ShapeDtypeStruct((M_ROWS, D), F32),
        scratch_shapes=[pltpu.VMEM((kdim, tn), BF16), pltpu.VMEM((kdim, tn), F32),
                        pltpu.SemaphoreType.DMA(())],
        compiler_params=_params("arbitrary", "arbitrary"),
        name=name,
    )(a, w, rp, rs, mod_p, mod_s)


TN_FF = 512
N_FF_BLOCKS = D_FF // TN_FF


def _up_kernel(a_ref, w_ref, cw_ref, cb_ref, fst_ref,
               h_ref, fcp_ref, fcs_ref, wab, wvb, acc_a, acc_v, wsa, wsv, wsem):
    j = pl.program_id(0)
    i = pl.program_id(1)
    rc = 64

    def w_fetch(jj):
        col = lambda blk: pl.ds(pl.multiple_of(blk * TN_FF, TN_FF), TN_FF)
        return (pltpu.make_async_copy(w_ref.at[:, col(jj)], wsa, wsem.at[0]),
                pltpu.make_async_copy(w_ref.at[:, col(N_FF_BLOCKS + jj)], wsv, wsem.at[1]))

    @pl.when(i == 0)
    def _():
        @pl.when(j == 0)
        def _():
            for c in w_fetch(0):
                c.start()

        for c in w_fetch(j):
            c.wait()
        _cast_rows(wsa, wab)
        _cast_rows(wsv, wvb)

        @pl.when(j + 1 < N_FF_BLOCKS)
        def _():
            for c in w_fetch(j + 1):
                c.start()

    def prompt():
        starts_sequence = i % MT_PER_SEQ == 0
        acc_a[0:8, :] = jnp.where(starts_sequence, 0.0, acc_a[TMM:TMM + 8, :])
        x = a_ref[...]
        acc_a[8:8 + TMM, :] = _dot(x, wab[...])
        acc_v[...] = _dot(x, wvb[...])
        cw, cb = cw_ref[...], cb_ref[...]
        for r in range(0, TMM, rc):
            conv = (cb + cw[2:3] * acc_a[8 + r:8 + r + rc, :]
                    + cw[1:2] * acc_a[7 + r:7 + r + rc, :]
                    + cw[0:1] * acc_a[6 + r:6 + r + rc, :])
            h_ref[r:r + rc, :] = (jax.nn.gelu(conv) * acc_v[r:r + rc, :]).astype(BF16)
        fcp_ref[0] = acc_a[TMM + 6:TMM + 8, :]

    def sample():
        B = DEC_BATCH
        x = a_ref[0:NS_ROWS, :]
        acc_a[8:8 + NS_ROWS, :] = _dot(x, wab[...])
        acc_v[0:NS_ROWS, :] = _dot(x, wvb[...])
        cw, cb = cw_ref[...], cb_ref[...]

        def pre(t, r):
            if t < 0:
                return fst_ref[r:r + rc, t + 2, :]
            return acc_a[8 + t * B + r:8 + t * B + r + rc, :]

        for t in range(DEC_SEQ):
            for r in range(0, B, rc):
                conv = cb + cw[2:3] * pre(t, r) + cw[1:2] * pre(t - 1, r) + cw[0:1] * pre(t - 2, r)
                h_ref[t * B + r:t * B + r + rc, :] = (
                    jax.nn.gelu(conv) * acc_v[t * B + r:t * B + r + rc, :]).astype(BF16)
        fcs_ref[:, 0, :] = acc_a[8 + 2 * B:8 + 3 * B, :]
        fcs_ref[:, 1, :] = acc_a[8 + 3 * B:8 + 4 * B, :]

    _per_tile(i, prompt, sample)


def _up_call(n2, w_up, cw, cb, fst):
    return pl.pallas_call(
        _up_kernel,
        grid=(N_FF_BLOCKS, N_MT),
        in_specs=[pl.BlockSpec((TMM, D), lambda j, i: (i, 0)),
                  pl.BlockSpec(memory_space=pl.ANY),
                  pl.BlockSpec((3, TN_FF), lambda j, i: (0, j)),
                  pl.BlockSpec((1, TN_FF), lambda j, i: (0, j)),
                  pl.BlockSpec((DEC_BATCH, 2, TN_FF), lambda j, i: (0, 0, j))],
        out_specs=[pl.BlockSpec((TMM, TN_FF), lambda j, i: (i, j)),
                   pl.BlockSpec((1, 2, TN_FF), lambda j, i: (_seq_of_mtile(i), 0, j)),
                   pl.BlockSpec((DEC_BATCH, 2, TN_FF), lambda j, i: (0, 0, j))],
        out_shape=[jax.ShapeDtypeStruct((M_ROWS, D_FF), BF16),
                   jax.ShapeDtypeStruct((BATCH, 2, D_FF), F32),
                   jax.ShapeDtypeStruct((DEC_BATCH, 2, D_FF), F32)],
        scratch_shapes=[pltpu.VMEM((D, TN_FF), BF16), pltpu.VMEM((D, TN_FF), BF16),
                        pltpu.VMEM((TMM + 8, TN_FF), F32), pltpu.VMEM((TMM, TN_FF), F32),
                        pltpu.VMEM((D, TN_FF), F32), pltpu.VMEM((D, TN_FF), F32),
                        pltpu.SemaphoreType.DMA((2,))],
        compiler_params=_params("arbitrary", "arbitrary"),
        name="ffn_up",
    )(n2, w_up, cw, cb, fst)


def _final_kernel(x_ref, w_ref, yp_ref, ys_ref):
    i = pl.program_id(0)

    @pl.when(i < NP_MT)
    def _():
        for r in range(0, TMM, TM):
            yp_ref[r:r + TM, :] = _rms(x_ref[r:r + TM, :]) * w_ref[...]

    @pl.when(i == NP_MT)
    def _():
        y = _rms(x_ref[0:NS_ROWS, :]) * w_ref[...]
        for t in range(DEC_SEQ):
            ys_ref[:, t, :] = y[t * DEC_BATCH:(t + 1) * DEC_BATCH]


def _final_call(x3, w):
    return pl.pallas_call(
        _final_kernel,
        grid=(N_MT,),
        in_specs=[pl.BlockSpec((TMM, D), lambda i: (i, 0)),
                  pl.BlockSpec((1, D), lambda i: (0, 0))],
        out_specs=[pl.BlockSpec((TMM, D), lambda i: (jnp.minimum(i, NP_MT - 1), 0)),
                   pl.BlockSpec((DEC_BATCH, DEC_SEQ, D), lambda i: (0, 0, 0))],
        out_shape=[jax.ShapeDtypeStruct((NP_ROWS, D), F32),
                   jax.ShapeDtypeStruct((DEC_BATCH, DEC_SEQ, D), F32)],
        compiler_params=_params("arbitrary"),
        name="final_norm",
    )(x3, w)


def _to_time_major(a):
    return jnp.transpose(a, (1, 0, 2))


def kernel(x_prompt, x_sample, state_ssm, state_ssd_conv, state_ffn_conv, c_prompt, c_sample,
           norm1_w, w_ada, b_ada, w_in, ssd_conv_w, ssd_conv_b, dt_bias, a_log, d_skip,
           ssd_norm_w, mlp_ln_w, mlp_ln_b, w_spatial, b_spatial, w_ssd_o, w_mlp_o, w_out,
           norm2_w, w_up, ffn_conv_w, ffn_conv_b, w_down, final_norm_w):
    assert w_in.shape[0] == 1, "single-layer trunk"
    row = lambda v: v.reshape(1, -1)

    xp = x_prompt.reshape(NP_ROWS, D)
    xs = x_sample

    mod_p, mod_s = _ada_call(c_prompt, c_sample, w_ada[0], row(b_ada[0]))
    K_SHIFT1, K_SCALE1, K_GATE1, K_SHIFT2, K_SCALE2, K_GATE2 = range(6)

    xp_spec = pl.BlockSpec((TMM, D), lambda i: (jnp.minimum(i, NP_MT - 1), 0))
    xs_spec = pl.BlockSpec((DEC_BATCH, DEC_SEQ, D), lambda i: (0, 0, 0))
    w_in_t = w_in[0].T
    n1, dt_raw = _norm_call(xp, xp_spec, xs, xs_spec, row(norm1_w[0]), mod_p, mod_s,
                            K_SCALE1, K_SHIFT1, w_in_t)
    cst_t = _to_time_major(state_ssd_conv[0])
    proj, cst_p, ncs_t = _inproj_call(n1, w_in_t, ssd_conv_w[0], row(ssd_conv_b[0]), cst_t)

    hp = jnp.pad(jnp.stack([dt_bias[0], a_log[0], d_skip[0]]), ((0, 5), (0, LANE - HEADS)))
    hpc = hp.T
    triu = jnp.asarray(np.triu(np.ones((CHUNK, CHUNK), np.float32)), dtype=BF16)
    dskx = row(jnp.repeat(d_skip[0], HEADDIM))
    nw = row(ssd_norm_w[0])

    head_of_col = np.arange(D) // HEADDIM
    ex = jnp.asarray(np.arange(LANE)[:, None] == head_of_col[None, :], dtype=BF16)
    grp_of_head = np.arange(LANE) // (HEADS // GROUPS)
    seg = jnp.asarray((np.arange(GROUPS * STATE)[:, None] // STATE == grp_of_head[None, :])
                      & (np.arange(LANE)[None, :] < HEADS), dtype=BF16)
    ypre, efull, xw_t, bb_t, cc_t, e3 = _ssds_prep_call(proj, dt_raw, hp, dskx, ex, seg)
    gw = (HEADS // GROUPS) * HEADDIM
    y_ssd_p, ssm_p, yo_t, ssm_s = _ssd_fused_call(
        proj, dt_raw, hp, hpc, dskx, nw, triu,
        e3[:, :HEADS], state_ssm[0].reshape(DEC_BATCH, GROUPS, gw, STATE), cc_t, bb_t, xw_t)
    y_ssd_s = _ssds_post_call(ypre, efull, yo_t.reshape(NS_ROWS, D), proj, nw)

    per_col = lambda a: jnp.repeat(a, MLP_GROUP_DIM, axis=1)
    bsx = per_col(b_spatial[0][:, :CHUNK].T)
    ws4 = w_spatial[0][:, :DEC_SEQ, :DEC_SEQ]
    wsx = per_col(jnp.transpose(ws4, (1, 2, 0)).reshape(DEC_SEQ * DEC_SEQ, MLP_GROUPS))
    bsx4 = jnp.pad(bsx[:DEC_SEQ], ((0, 8 - DEC_SEQ), (0, 0)))
    y_mlp, cv_s = _mlp_call(proj, row(mlp_ln_w[0]), row(mlp_ln_b[0]), w_spatial[0], bsx, wsx, bsx4)

    mixed = _merge_call(y_ssd_p, y_ssd_s, y_mlp, proj, w_ssd_o[0], w_mlp_o[0])
    x2 = _resid_call(mixed, w_out[0], xp, xs, 0, mod_p, mod_s, K_GATE1, 1024, "out_proj")

    x2p_spec = pl.BlockSpec((TMM, D), lambda i: (jnp.minimum(i, NP_MT - 1), 0))
    x2s_spec = pl.BlockSpec((NS_ROWS, D), lambda i: (NP_TILES, 0))
    (n2,) = _norm_call(x2, x2p_spec, x2, x2s_spec, row(norm2_w[0]), mod_p, mod_s,
                       K_SCALE2, K_SHIFT2, None)
    h, ffn_p, ffn_s = _up_call(n2, w_up[0], ffn_conv_w[0], row(ffn_conv_b[0]), state_ffn_conv[0])
    x3 = _resid_call(h, w_down[0], x2, x2, NP_TILES, mod_p, mod_s, K_GATE2, 512, "ffn_down")

    y_p, y_s = _final_call(x3, row(final_norm_w))

    from_t = lambda a, t, c: jnp.transpose(a.reshape(t, DEC_BATCH, c), (1, 0, 2))
    return (y_p.reshape(BATCH, SEQ, D),
            y_s,
            ssm_p.reshape(1, BATCH, HEADS, HEADDIM, STATE),
            ssm_s.reshape(1, DEC_BATCH, HEADS, HEADDIM, STATE),
            cst_p[None],
            from_t(ncs_t, 3, CONV_DIM)[None],
            ffn_p[None],
            ffn_s[None],
            cv_s[None])
```
